```python
import math
import jax, jax.numpy as jnp
from jax import lax
import numpy as np

D_MODEL = 2048
BATCH = 8
SEQ = 2048
DEPTH = 4

N_MIXERS = 4
MIX_WIDTH = D_MODEL
GROUP_W = MIX_WIDTH // N_MIXERS
S5_CH_PER_GROUP = 16
S5_GROUPS = GROUP_W // S5_CH_PER_GROUP
S5_STATE = 64
S5_DT_MIN = 0.001
S5_DT_MAX = 0.1
POOL_WINDOWS = (2, 4, 8, 16)
POOL_CH = GROUP_W // len(POOL_WINDOWS)
CONV_WIDTH = 31
ATT_HEADS = 8
ATT_HEAD_DIM = GROUP_W // ATT_HEADS
DILATED_PATTERNS = ((128, 1), (512, 4), (2048, 16))
ATT_BLOCK = 128
REL_BUCKETS = 32
REL_MAX_DIST = 2048
MEM_LEN = 256
X_HEADS = 4
X_HEAD_DIM = 128
X_WIDTH = X_HEADS * X_HEAD_DIM
D_FF = 4 * D_MODEL
NORM_EPS = 1e-6
NEG_INF = -1e30
IN_WIDTH = GROUP_W + GROUP_W + 2 * GROUP_W + 3 * GROUP_W

kernel_name = 'hymba_style_multimixer_trunk'

F32 = jnp.float32


def rmsnorm(x, g):
    xf = x.astype(F32)
    y = xf * lax.rsqrt(jnp.mean(xf * xf, axis=-1, keepdims=True) + NORM_EPS)
    return (y * g.astype(F32)).astype(x.dtype)


def layernorm(x, g, b):
    xf = x.astype(F32)
    xc = xf - jnp.mean(xf, axis=-1, keepdims=True)
    y = xc * lax.rsqrt(jnp.mean(xc * xc, axis=-1, keepdims=True) + NORM_EPS)
    return (y * g.astype(F32) + b.astype(F32)).astype(x.dtype)


def group_rmsnorm(y, g, out_dtype):
    Bsz, L, _ = y.shape
    yf = y.astype(F32).reshape(Bsz, L, N_MIXERS, GROUP_W)
    yf = yf * lax.rsqrt(jnp.mean(yf * yf, axis=-1, keepdims=True) + NORM_EPS)
    return (yf.reshape(Bsz, L, MIX_WIDTH) * g.astype(F32)).astype(out_dtype)


def _cmul(ar, ai, br, bi):
    return ar * br - ai * bi, ar * bi + ai * br


def s5_mixer(u, lam_re, lam_im, log_dt, b_re, b_im, c_re, c_im, d_skip, w_glu):
    Bsz, L, _ = u.shape
    uf = u.astype(F32).reshape(Bsz, L, S5_GROUPS, S5_CH_PER_GROUP)
    lr = lam_re.astype(F32)
    li = lam_im.astype(F32)
    dt = jnp.exp(log_dt.astype(F32))[:, None]
    mag = jnp.exp(lr * dt)
    ab_r, ab_i = mag * jnp.cos(li * dt), mag * jnp.sin(li * dt)
    den = lr * lr + li * li
    nr, ni = ab_r - 1.0, ab_i
    f_r = (nr * lr + ni * li) / den
    f_i = (ni * lr - nr * li) / den
    bb_r, bb_i = _cmul(f_r[..., None], f_i[..., None], b_re.astype(F32), b_im.astype(F32))
    bu_r = jnp.einsum('gnc,blgc->blgn', bb_r, uf)
    bu_i = jnp.einsum('gnc,blgc->blgn', bb_i, uf)
    a_r = jnp.broadcast_to(ab_r, bu_r.shape)
    a_i = jnp.broadcast_to(ab_i, bu_i.shape)

    def combine(e1, e2):
        a1r, a1i, b1r, b1i = e1
        a2r, a2i, b2r, b2i = e2
        ar, ai = _cmul(a2r, a2i, a1r, a1i)
        br, bi = _cmul(a2r, a2i, b1r, b1i)
        return ar, ai, br + b2r, bi + b2i

    _, _, xr, xi = lax.associative_scan(combine, (a_r, a_i, bu_r, bu_i), axis=1)
    y = jnp.einsum('gcn,blgn->blgc', c_re.astype(F32), xr) - jnp.einsum('gcn,blgn->blgc', c_im.astype(F32), xi)
    y = y.reshape(Bsz, L, GROUP_W) + d_skip.astype(F32) * uf.reshape(Bsz, L, GROUP_W)
    g = jax.nn.gelu(y)
    out = g * jax.nn.sigmoid(jnp.einsum('blc,cd->bld', g, w_glu.astype(F32)))
    return out.astype(u.dtype)


def pool_mixer(u, pool_w, pool_scale):
    Bsz, L, _ = u.shape
    uf = u.astype(F32).reshape(Bsz, L, len(POOL_WINDOWS), POOL_CH)
    cs = jnp.cumsum(uf, axis=1)
    t = jnp.arange(L)
    pooled = []
    for gi, w in enumerate(POOL_WINDOWS):
        c = cs[:, :, gi]
        shifted = jnp.pad(c, ((0, 0), (w, 0), (0, 0)))[:, :L]
        cnt = jnp.minimum(t + 1, w).astype(F32)[None, :, None]
        pooled.append((c - shifted) / cnt - uf[:, :, gi])
    p = jnp.stack(pooled, axis=2)
    y = jnp.einsum('blgc,gcd->blgd', p, pool_w.astype(F32)).reshape(Bsz, L, GROUP_W)
    return (y * pool_scale.astype(F32)).astype(u.dtype)


def conv_mixer(u, w_dw, b_dw, ln_g, ln_b, w_pw):
    val, gate = jnp.split(u, 2, axis=-1)
    h = val * jax.nn.sigmoid(gate)
    h = lax.conv_general_dilated(h, w_dw[:, None, :], window_strides=(1,),
                                 padding=((CONV_WIDTH - 1, 0),),
                                 dimension_numbers=('NWC', 'WIO', 'NWC'),
                                 feature_group_count=GROUP_W) + b_dw
    h = jax.nn.silu(layernorm(h, ln_g, ln_b))
    return jnp.einsum('blc,cd->bld', h, w_pw)


def _t5_bucket(dist):
    n = np.maximum(dist, 0)
    max_exact = REL_BUCKETS // 2
    large = max_exact + (np.log(np.maximum(n, 1) / max_exact) / np.log(REL_MAX_DIST / max_exact)
                         * (REL_BUCKETS - max_exact)).astype(np.int64)
    large = np.minimum(large, REL_BUCKETS - 1)
    return np.where(n < max_exact, n, large).astype(np.int32)


def _dilated_branch(q, k, v, rel_bias, window, dilation):
    Bsz, L, H, E = q.shape
    Ls = L // dilation
    nb = -(-Ls // ATT_BLOCK)
    pad = nb * ATT_BLOCK - Ls

    def to_sub(t):
        t = t.reshape(Bsz, Ls, dilation, H, E).transpose(0, 2, 3, 1, 4)
        return jnp.pad(t, ((0, 0), (0, 0), (0, 0), (0, pad), (0, 0)))

    def band(t):
        t = jnp.pad(t, ((0, 0), (0, 0), (0, 0), (ATT_BLOCK, 0), (0, 0)))
        t = t.reshape(Bsz, dilation, H, nb + 1, ATT_BLOCK, E)
        return jnp.concatenate([t[:, :, :, :-1], t[:, :, :, 1:]], axis=4)

    qb = to_sub(q).reshape(Bsz, dilation, H, nb, ATT_BLOCK, E)
    kb = band(to_sub(k))
    vb = band(to_sub(v))
    s = jnp.einsum('bdhnqe,bdhnke->bdhnqk', qb, kb, preferred_element_type=F32) * (E ** -0.5)
    a_idx = np.arange(ATT_BLOCK)[:, None]
    b_idx = np.arange(2 * ATT_BLOCK)[None, :]
    sub_dist = a_idx + ATT_BLOCK - b_idx
    key_idx = np.arange(nb)[:, None, None] * ATT_BLOCK - ATT_BLOCK + b_idx[None]
    valid = (sub_dist >= 0) & (sub_dist <= window // dilation) & (key_idx >= 0)
    bucket = _t5_bucket(sub_dist * dilation)
    bias = jnp.transpose(rel_bias[jnp.asarray(bucket)], (2, 0, 1)).astype(F32)
    s = s + bias[None, None, :, None]
    s = jnp.where(jnp.asarray(valid)[None, None, None], s, NEG_INF)
    m = jnp.max(s, axis=-1, keepdims=True)
    p = jnp.exp(s - m)
    den = jnp.sum(p, axis=-1, keepdims=True)
    o = jnp.einsum('bdhnqk,bdhnke->bdhnqe', p, vb.astype(F32)) / den
    lse = (m + jnp.log(den))[..., 0]
    o = o.reshape(Bsz, dilation, H, nb * ATT_BLOCK, E)[:, :, :, :Ls]
    o = o.transpose(0, 3, 1, 2, 4).reshape(Bsz, L, H, E)
    lse = lse.reshape(Bsz, dilation, H, nb * ATT_BLOCK)[:, :, :, :Ls]
    lse = lse.transpose(0, 3, 1, 2).reshape(Bsz, L, H)
    return o, lse


def dilated_attention(qkv, rel_bias):
    Bsz, L, _ = qkv.shape
    q, k, v = [t.reshape(Bsz, L, ATT_HEADS, ATT_HEAD_DIM) for t in jnp.split(qkv, 3, axis=-1)]
    outs, lses = [], []
    for window, dilation in DILATED_PATTERNS:
        o, lse = _dilated_branch(q, k, v, rel_bias, window, dilation)
        outs.append(o)
        lses.append(lse)
    wts = jax.nn.softmax(jnp.stack(lses, axis=0), axis=0)
    o = jnp.einsum('pblh,pblhe->blhe', wts, jnp.stack(outs, axis=0))
    return o.reshape(Bsz, L, GROUP_W).astype(qkv.dtype)


def cross_attention(h, mem_n, w_xq, w_xk, w_xv, w_xo):
    Bsz, L, _ = h.shape
    M = mem_n.shape[1]
    q = jnp.einsum('bld,de->ble', h, w_xq).reshape(Bsz, L, X_HEADS, X_HEAD_DIM)
    k = jnp.einsum('bmd,de->bme', mem_n, w_xk).reshape(Bsz, M, X_HEADS, X_HEAD_DIM)
    v = jnp.einsum('bmd,de->bme', mem_n, w_xv).reshape(Bsz, M, X_HEADS, X_HEAD_DIM)
    s = jnp.einsum('blhe,bmhe->bhlm', q, k, preferred_element_type=F32) * (X_HEAD_DIM ** -0.5)
    p = jax.nn.softmax(s, axis=-1)
    o = jnp.einsum('bhlm,bmhe->blhe', p, v.astype(F32)).reshape(Bsz, L, X_WIDTH).astype(h.dtype)
    return jnp.einsum('ble,ed->bld', o, w_xo)


def _fwd_setup_inputs(seed: int = 0) -> dict:
    key = jax.random.key(seed)
    ks = iter(jax.random.split(key, 48))

    def nrm(shape, scale):
        return jax.random.normal(next(ks), shape, F32) * scale

    def gain(shape):
        return 1.0 + nrm(shape, 0.02)

    x = nrm((BATCH, SEQ, D_MODEL), 1.0)
    mem = nrm((BATCH, MEM_LEN, D_MODEL), 1.0)
    rel_bias = nrm((REL_BUCKETS, ATT_HEADS), 0.2)
    mem_norm_g = gain((D_MODEL,))
    norm_mix_g = gain((DEPTH, D_MODEL))
    w_in = nrm((DEPTH, D_MODEL, IN_WIDTH), D_MODEL ** -0.5)
    s5_lam_re = -0.5 * jnp.exp(nrm((DEPTH, S5_GROUPS, S5_STATE), 0.02))
    s5_lam_im = math.pi * jnp.arange(S5_STATE, dtype=F32) + nrm((DEPTH, S5_GROUPS, S5_STATE), 0.02)
    s5_log_dt = jax.random.uniform(next(ks), (DEPTH, S5_GROUPS), F32,
                                   math.log(S5_DT_MIN), math.log(S5_DT_MAX))
    s5_b_re = nrm((DEPTH, S5_GROUPS, S5_STATE, S5_CH_PER_GROUP), (2 * S5_CH_PER_GROUP) ** -0.5)
    s5_b_im = nrm((DEPTH, S5_GROUPS, S5_STATE, S5_CH_PER_GROUP), (2 * S5_CH_PER_GROUP) ** -0.5)
    s5_c_re = nrm((DEPTH, S5_GROUPS, S5_CH_PER_GROUP, S5_STATE), 0.5)
    s5_c_im = nrm((DEPTH, S5_GROUPS, S5_CH_PER_GROUP, S5_STATE), 0.5)
    s5_d = nrm((DEPTH, GROUP_W), 1.0)
    s5_w_glu = nrm((DEPTH, GROUP_W, GROUP_W), GROUP_W ** -0.5)
    pool_w = nrm((DEPTH, len(POOL_WINDOWS), POOL_CH, POOL_CH), POOL_CH ** -0.5)
    pool_scale = gain((DEPTH, GROUP_W))
    conv_w_dw = nrm((DEPTH, CONV_WIDTH, GROUP_W), CONV_WIDTH ** -0.5)
    conv_b_dw = nrm((DEPTH, GROUP_W), 0.02)
    conv_ln_g = gain((DEPTH, GROUP_W))
    conv_ln_b = nrm((DEPTH, GROUP_W), 0.02)
    conv_w_pw = nrm((DEPTH, GROUP_W, GROUP_W), GROUP_W ** -0.5)
    grp_norm_g = gain((DEPTH, MIX_WIDTH))
    w_out = nrm((DEPTH, MIX_WIDTH, D_MODEL), MIX_WIDTH ** -0.5)
    norm_x_g = gain((DEPTH, D_MODEL))
    w_xq = nrm((DEPTH, D_MODEL, X_WIDTH), D_MODEL ** -0.5)
    w_xk = nrm((DEPTH, D_MODEL, X_WIDTH), D_MODEL ** -0.5)
    w_xv = nrm((DEPTH, D_MODEL, X_WIDTH), D_MODEL ** -0.5)
    w_xo = nrm((DEPTH, X_WIDTH, D_MODEL), X_WIDTH ** -0.5)
    norm_mlp_g = gain((DEPTH, D_MODEL))
    w_up = nrm((DEPTH, D_MODEL, D_FF), D_MODEL ** -0.5)
    w_down = nrm((DEPTH, D_FF, D_MODEL), D_FF ** -0.5)
    norm_final_g = gain((D_MODEL,))
    return {'x': x, 'mem': mem, 'rel_bias': rel_bias, 'mem_norm_g': mem_norm_g,
            'norm_mix_g': norm_mix_g, 'w_in': w_in,
            's5_lam_re': s5_lam_re, 's5_lam_im': s5_lam_im, 's5_log_dt': s5_log_dt,
            's5_b_re': s5_b_re, 's5_b_im': s5_b_im, 's5_c_re': s5_c_re, 's5_c_im': s5_c_im,
            's5_d': s5_d, 's5_w_glu': s5_w_glu,
            'pool_w': pool_w, 'pool_scale': pool_scale,
            'conv_w_dw': conv_w_dw, 'conv_b_dw': conv_b_dw, 'conv_ln_g': conv_ln_g,
            'conv_ln_b': conv_ln_b, 'conv_w_pw': conv_w_pw,
            'grp_norm_g': grp_norm_g, 'w_out': w_out,
            'norm_x_g': norm_x_g, 'w_xq': w_xq, 'w_xk': w_xk, 'w_xv': w_xv, 'w_xo': w_xo,
            'norm_mlp_g': norm_mlp_g, 'w_up': w_up, 'w_down': w_down,
            'norm_final_g': norm_final_g}


def _fwd_reference(x, mem, rel_bias, mem_norm_g, norm_mix_g, w_in,
              s5_lam_re, s5_lam_im, s5_log_dt, s5_b_re, s5_b_im, s5_c_re, s5_c_im, s5_d, s5_w_glu,
              pool_w, pool_scale, conv_w_dw, conv_b_dw, conv_ln_g, conv_ln_b, conv_w_pw,
              grp_norm_g, w_out, norm_x_g, w_xq, w_xk, w_xv, w_xo,
              norm_mlp_g, w_up, w_down, norm_final_g):
    mem_n = rmsnorm(mem, mem_norm_g)
    h = x
    for l in range(DEPTH):
        xn = rmsnorm(h, norm_mix_g[l])
        proj = jnp.einsum('bld,dc->blc', xn, w_in[l])
        u_a, u_b, u_c, qkv = jnp.split(proj, [GROUP_W, 2 * GROUP_W, 4 * GROUP_W], axis=-1)
        y_a = s5_mixer(u_a, s5_lam_re[l], s5_lam_im[l], s5_log_dt[l], s5_b_re[l], s5_b_im[l],
                       s5_c_re[l], s5_c_im[l], s5_d[l], s5_w_glu[l])
        y_b = pool_mixer(u_b, pool_w[l], pool_scale[l])
        y_c = conv_mixer(u_c, conv_w_dw[l], conv_b_dw[l], conv_ln_g[l], conv_ln_b[l], conv_w_pw[l])
        y_d = dilated_attention(qkv, rel_bias)
        y = group_rmsnorm(jnp.concatenate([y_a, y_b, y_c, y_d], axis=-1), grp_norm_g[l], h.dtype)
        h = h + jnp.einsum('blc,cd->bld', y, w_out[l])
        h = h + cross_attention(rmsnorm(h, norm_x_g[l]), mem_n, w_xq[l], w_xk[l], w_xv[l], w_xo[l])
        hn = rmsnorm(h, norm_mlp_g[l])
        h = h + jnp.einsum('blf,fd->bld', jnp.square(jax.nn.relu(jnp.einsum('bld,df->blf', hn, w_up[l]))), w_down[l])
    return rmsnorm(h, norm_final_g)


import jax as _jax
import jax.numpy as _jnp

TWIN_FORMAT = 'train_step'
FWD_PARAMS = ['x', 'mem', 'rel_bias', 'mem_norm_g', 'norm_mix_g', 'w_in', 's5_lam_re', 's5_lam_im', 's5_log_dt', 's5_b_re', 's5_b_im', 's5_c_re', 's5_c_im', 's5_d', 's5_w_glu', 'pool_w', 'pool_scale', 'conv_w_dw', 'conv_b_dw', 'conv_ln_g', 'conv_ln_b', 'conv_w_pw', 'grp_norm_g', 'w_out', 'norm_x_g', 'w_xq', 'w_xk', 'w_xv', 'w_xo', 'norm_mlp_g', 'w_up', 'w_down', 'norm_final_g']
TWIN_WEIGHTS = ['rel_bias', 'mem_norm_g', 'norm_mix_g', 'w_in', 's5_lam_re', 's5_lam_im', 's5_log_dt', 's5_b_re', 's5_b_im', 's5_c_re', 's5_c_im', 's5_d', 's5_w_glu', 'pool_w', 'pool_scale', 'conv_w_dw', 'conv_b_dw', 'conv_ln_g', 'conv_ln_b', 'conv_w_pw', 'grp_norm_g', 'w_out', 'norm_x_g', 'w_xq', 'w_xk', 'w_xv', 'w_xo', 'norm_mlp_g', 'w_up', 'w_down', 'norm_final_g']
TWIN_DIFF_INPUT = 'x'
TWIN_INPUTS = ['x', 'mem', 'rel_bias', 'mem_norm_g', 'norm_mix_g', 'w_in', 's5_lam_re', 's5_lam_im', 's5_log_dt', 's5_b_re', 's5_b_im', 's5_c_re', 's5_c_im', 's5_d', 's5_w_glu', 'pool_w', 'pool_scale', 'conv_w_dw', 'conv_b_dw', 'conv_ln_g', 'conv_ln_b', 'conv_w_pw', 'grp_norm_g', 'w_out', 'norm_x_g', 'w_xq', 'w_xk', 'w_xv', 'w_xo', 'norm_mlp_g', 'w_up', 'w_down', 'norm_final_g', 'loss_target', 'm_rel_bias', 'm_mem_norm_g', 'm_norm_mix_g', 'm_w_in', 'm_s5_lam_re', 'm_s5_lam_im', 'm_s5_log_dt', 'm_s5_b_re', 'm_s5_b_im', 'm_s5_c_re', 'm_s5_c_im', 'm_s5_d', 'm_s5_w_glu', 'm_pool_w', 'm_pool_scale', 'm_conv_w_dw', 'm_conv_b_dw', 'm_conv_ln_g', 'm_conv_ln_b', 'm_conv_w_pw', 'm_grp_norm_g', 'm_w_out', 'm_norm_x_g', 'm_w_xq', 'm_w_xk', 'm_w_xv', 'm_w_xo', 'm_norm_mlp_g', 'm_w_up', 'm_w_down', 'm_norm_final_g', 'v_rel_bias', 'v_mem_norm_g', 'v_norm_mix_g', 'v_w_in', 'v_s5_lam_re', 'v_s5_lam_im', 'v_s5_log_dt', 'v_s5_b_re', 'v_s5_b_im', 'v_s5_c_re', 'v_s5_c_im', 'v_s5_d', 'v_s5_w_glu', 'v_pool_w', 'v_pool_scale', 'v_conv_w_dw', 'v_conv_b_dw', 'v_conv_ln_g', 'v_conv_ln_b', 'v_conv_w_pw', 'v_grp_norm_g', 'v_w_out', 'v_norm_x_g', 'v_w_xq', 'v_w_xk', 'v_w_xv', 'v_w_xo', 'v_norm_mlp_g', 'v_w_up', 'v_w_down', 'v_norm_final_g']
TWIN_OUTPUTS = ['loss', 'grad_x', 'grad_rel_bias', 'grad_mem_norm_g', 'grad_norm_mix_g', 'grad_w_in', 'grad_s5_lam_re', 'grad_s5_lam_im', 'grad_s5_log_dt', 'grad_s5_b_re', 'grad_s5_b_im', 'grad_s5_c_re', 'grad_s5_c_im', 'grad_s5_d', 'grad_s5_w_glu', 'grad_pool_w', 'grad_pool_scale', 'grad_conv_w_dw', 'grad_conv_b_dw', 'grad_conv_ln_g', 'grad_conv_ln_b', 'grad_conv_w_pw', 'grad_grp_norm_g', 'grad_w_out', 'grad_norm_x_g', 'grad_w_xq', 'grad_w_xk', 'grad_w_xv', 'grad_w_xo', 'grad_norm_mlp_g', 'grad_w_up', 'grad_w_down', 'grad_norm_final_g', 'delta_rel_bias', 'delta_mem_norm_g', 'delta_norm_mix_g', 'delta_w_in', 'delta_s5_lam_re', 'delta_s5_lam_im', 'delta_s5_log_dt', 'delta_s5_b_re', 'delta_s5_b_im', 'delta_s5_c_re', 'delta_s5_c_im', 'delta_s5_d', 'delta_s5_w_glu', 'delta_pool_w', 'delta_pool_scale', 'delta_conv_w_dw', 'delta_conv_b_dw', 'delta_conv_ln_g', 'delta_conv_ln_b', 'delta_conv_w_pw', 'delta_grp_norm_g', 'delta_w_out', 'delta_norm_x_g', 'delta_w_xq', 'delta_w_xk', 'delta_w_xv', 'delta_w_xo', 'delta_norm_mlp_g', 'delta_w_up', 'delta_w_down', 'delta_norm_final_g', 'new_m_rel_bias', 'new_m_mem_norm_g', 'new_m_norm_mix_g', 'new_m_w_in', 'new_m_s5_lam_re', 'new_m_s5_lam_im', 'new_m_s5_log_dt', 'new_m_s5_b_re', 'new_m_s5_b_im', 'new_m_s5_c_re', 'new_m_s5_c_im', 'new_m_s5_d', 'new_m_s5_w_glu', 'new_m_pool_w', 'new_m_pool_scale', 'new_m_conv_w_dw', 'new_m_conv_b_dw', 'new_m_conv_ln_g', 'new_m_conv_ln_b', 'new_m_conv_w_pw', 'new_m_grp_norm_g', 'new_m_w_out', 'new_m_norm_x_g', 'new_m_w_xq', 'new_m_w_xk', 'new_m_w_xv', 'new_m_w_xo', 'new_m_norm_mlp_g', 'new_m_w_up', 'new_m_w_down', 'new_m_norm_final_g', 'new_v_rel_bias', 'new_v_mem_norm_g', 'new_v_norm_mix_g', 'new_v_w_in', 'new_v_s5_lam_re', 'new_v_s5_lam_im', 'new_v_s5_log_dt', 'new_v_s5_b_re', 'new_v_s5_b_im', 'new_v_s5_c_re', 'new_v_s5_c_im', 'new_v_s5_d', 'new_v_s5_w_glu', 'new_v_pool_w', 'new_v_pool_scale', 'new_v_conv_w_dw', 'new_v_conv_b_dw', 'new_v_conv_ln_g', 'new_v_conv_ln_b', 'new_v_conv_w_pw', 'new_v_grp_norm_g', 'new_v_w_out', 'new_v_norm_x_g', 'new_v_w_xq', 'new_v_w_xk', 'new_v_w_xv', 'new_v_w_xo', 'new_v_norm_mlp_g', 'new_v_w_up', 'new_v_w_down', 'new_v_norm_final_g']
TWIN_LEAF_KINDS = {'loss': 'loss', 'grad_x': 'grad_x', 'grad_rel_bias': 'grad_w', 'grad_mem_norm_g': 'grad_w', 'grad_norm_mix_g': 'grad_w', 'grad_w_in': 'grad_w', 'grad_s5_lam_re': 'grad_w', 'grad_s5_lam_im': 'grad_w', 'grad_s5_log_dt': 'grad_w', 'grad_s5_b_re': 'grad_w', 'grad_s5_b_im': 'grad_w', 'grad_s5_c_re': 'grad_w', 'grad_s5_c_im': 'grad_w', 'grad_s5_d': 'grad_w', 'grad_s5_w_glu': 'grad_w', 'grad_pool_w': 'grad_w', 'grad_pool_scale': 'grad_w', 'grad_conv_w_dw': 'grad_w', 'grad_conv_b_dw': 'grad_w', 'grad_conv_ln_g': 'grad_w', 'grad_conv_ln_b': 'grad_w', 'grad_conv_w_pw': 'grad_w', 'grad_grp_norm_g': 'grad_w', 'grad_w_out': 'grad_w', 'grad_norm_x_g': 'grad_w', 'grad_w_xq': 'grad_w', 'grad_w_xk': 'grad_w', 'grad_w_xv': 'grad_w', 'grad_w_xo': 'grad_w', 'grad_norm_mlp_g': 'grad_w', 'grad_w_up': 'grad_w', 'grad_w_down': 'grad_w', 'grad_norm_final_g': 'grad_w', 'delta_rel_bias': 'delta_w', 'delta_mem_norm_g': 'delta_w', 'delta_norm_mix_g': 'delta_w', 'delta_w_in': 'delta_w', 'delta_s5_lam_re': 'delta_w', 'delta_s5_lam_im': 'delta_w', 'delta_s5_log_dt': 'delta_w', 'delta_s5_b_re': 'delta_w', 'delta_s5_b_im': 'delta_w', 'delta_s5_c_re': 'delta_w', 'delta_s5_c_im': 'delta_w', 'delta_s5_d': 'delta_w', 'delta_s5_w_glu': 'delta_w', 'delta_pool_w': 'delta_w', 'delta_pool_scale': 'delta_w', 'delta_conv_w_dw': 'delta_w', 'delta_conv_b_dw': 'delta_w', 'delta_conv_ln_g': 'delta_w', 'delta_conv_ln_b': 'delta_w', 'delta_conv_w_pw': 'delta_w', 'delta_grp_norm_g': 'delta_w', 'delta_w_out': 'delta_w', 'delta_norm_x_g': 'delta_w', 'delta_w_xq': 'delta_w', 'delta_w_xk': 'delta_w', 'delta_w_xv': 'delta_w', 'delta_w_xo': 'delta_w', 'delta_norm_mlp_g': 'delta_w', 'delta_w_up': 'delta_w', 'delta_w_down': 'delta_w', 'delta_norm_final_g': 'delta_w', 'new_m_rel_bias': 'new_m', 'new_m_mem_norm_g': 'new_m', 'new_m_norm_mix_g': 'new_m', 'new_m_w_in': 'new_m', 'new_m_s5_lam_re': 'new_m', 'new_m_s5_lam_im': 'new_m', 'new_m_s5_log_dt': 'new_m', 'new_m_s5_b_re': 'new_m', 'new_m_s5_b_im': 'new_m', 'new_m_s5_c_re': 'new_m', 'new_m_s5_c_im': 'new_m', 'new_m_s5_d': 'new_m', 'new_m_s5_w_glu': 'new_m', 'new_m_pool_w': 'new_m', 'new_m_pool_scale': 'new_m', 'new_m_conv_w_dw': 'new_m', 'new_m_conv_b_dw': 'new_m', 'new_m_conv_ln_g': 'new_m', 'new_m_conv_ln_b': 'new_m', 'new_m_conv_w_pw': 'new_m', 'new_m_grp_norm_g': 'new_m', 'new_m_w_out': 'new_m', 'new_m_norm_x_g': 'new_m', 'new_m_w_xq': 'new_m', 'new_m_w_xk': 'new_m', 'new_m_w_xv': 'new_m', 'new_m_w_xo': 'new_m', 'new_m_norm_mlp_g': 'new_m', 'new_m_w_up': 'new_m', 'new_m_w_down': 'new_m', 'new_m_norm_final_g': 'new_m', 'new_v_rel_bias': 'new_v', 'new_v_mem_norm_g': 'new_v', 'new_v_norm_mix_g': 'new_v', 'new_v_w_in': 'new_v', 'new_v_s5_lam_re': 'new_v', 'new_v_s5_lam_im': 'new_v', 'new_v_s5_log_dt': 'new_v', 'new_v_s5_b_re': 'new_v', 'new_v_s5_b_im': 'new_v', 'new_v_s5_c_re': 'new_v', 'new_v_s5_c_im': 'new_v', 'new_v_s5_d': 'new_v', 'new_v_s5_w_glu': 'new_v', 'new_v_pool_w': 'new_v', 'new_v_pool_scale': 'new_v', 'new_v_conv_w_dw': 'new_v', 'new_v_conv_b_dw': 'new_v', 'new_v_conv_ln_g': 'new_v', 'new_v_conv_ln_b': 'new_v', 'new_v_conv_w_pw': 'new_v', 'new_v_grp_norm_g': 'new_v', 'new_v_w_out': 'new_v', 'new_v_norm_x_g': 'new_v', 'new_v_w_xq': 'new_v', 'new_v_w_xk': 'new_v', 'new_v_w_xv': 'new_v', 'new_v_w_xo': 'new_v', 'new_v_norm_mlp_g': 'new_v', 'new_v_w_up': 'new_v', 'new_v_w_down': 'new_v', 'new_v_norm_final_g': 'new_v'}


def _forward(args):
    return _fwd_reference(*[args[k] for k in FWD_PARAMS])


def _output_shape():
    out = _jax.eval_shape(lambda: _forward(_fwd_setup_inputs(0)))
    return out.shape, out.dtype

N_MICROBATCH = 1
ADAM_LR = 0.001
ADAM_B1 = 0.9
ADAM_B2 = 0.999
ADAM_EPS = 1e-08
ADAM_WD = 0.01
ADAM_STEP = 10
PER_EXAMPLE_BATCH_AXIS = {'x': 0, 'mem': 0, 'loss_target': 0}
SHARED_INPUTS = []
_WEIGHT_DTYPES = {'rel_bias': _jnp.float32, 'mem_norm_g': _jnp.float32, 'norm_mix_g': _jnp.float32, 'w_in': _jnp.float32, 's5_lam_re': _jnp.float32, 's5_lam_im': _jnp.float32, 's5_log_dt': _jnp.float32, 's5_b_re': _jnp.float32, 's5_b_im': _jnp.float32, 's5_c_re': _jnp.float32, 's5_c_im': _jnp.float32, 's5_d': _jnp.float32, 's5_w_glu': _jnp.float32, 'pool_w': _jnp.float32, 'pool_scale': _jnp.float32, 'conv_w_dw': _jnp.float32, 'conv_b_dw': _jnp.float32, 'conv_ln_g': _jnp.float32, 'conv_ln_b': _jnp.float32, 'conv_w_pw': _jnp.float32, 'grp_norm_g': _jnp.float32, 'w_out': _jnp.float32, 'norm_x_g': _jnp.float32, 'w_xq': _jnp.float32, 'w_xk': _jnp.float32, 'w_xv': _jnp.float32, 'w_xo': _jnp.float32, 'norm_mlp_g': _jnp.float32, 'w_up': _jnp.float32, 'w_down': _jnp.float32, 'norm_final_g': _jnp.float32}
MOMENT_SCALE = {'rel_bias': 6.952880e-02, 'mem_norm_g': 1.128086e-02, 'norm_mix_g': 4.553329e-02, 'w_in': 3.381071e-02, 's5_lam_re': 1.505261e-02, 's5_lam_im': 1.255113e-02, 's5_log_dt': 6.353655e+00, 's5_b_re': 8.528314e-03, 's5_b_im': 9.431437e-03, 's5_c_re': 2.756916e-03, 's5_c_im': 3.138197e-03, 's5_d': 4.358091e-02, 's5_w_glu': 1.244594e-02, 'pool_w': 3.443530e-02, 'pool_scale': 3.553494e-02, 'conv_w_dw': 3.721712e-02, 'conv_b_dw': 9.458170e-02, 'conv_ln_g': 4.821860e-02, 'conv_ln_b': 5.559717e-02, 'conv_w_pw': 4.047714e-02, 'grp_norm_g': 4.266292e-02, 'w_out': 4.259436e-02, 'norm_x_g': 3.503303e-03, 'w_xq': 7.107555e-03, 'w_xk': 7.108733e-03, 'w_xv': 8.154675e-03, 'w_xo': 4.099327e-03, 'norm_mlp_g': 3.976314e-02, 'w_up': 2.013798e-02, 'w_down': 4.385656e-02, 'norm_final_g': 8.428546e+00}


def _to_microbatches(a, axis):
    t = _jnp.moveaxis(a, axis, 0)
    t = t.reshape((N_MICROBATCH, t.shape[0] // N_MICROBATCH) + t.shape[1:])
    return _jnp.moveaxis(t, 1, axis + 1)


def setup_inputs(seed: int = 0) -> dict:
    inp = _fwd_setup_inputs(seed)
    key = _jax.random.fold_in(_jax.random.key(seed), 7919)
    shape, _ = _output_shape()
    out = dict(inp)
    out["loss_target"] = _jax.random.normal(_jax.random.fold_in(key, 0), shape, _jnp.float32)
    for i, name in enumerate(TWIN_WEIGHTS):
        w = inp[name].astype(_jnp.float32)
        if MOMENT_SCALE is None:
            s = _jnp.sqrt(_jnp.mean(_jnp.square(w)) + 1e-30)
        else:
            s = MOMENT_SCALE[name]
        km, kv = _jax.random.split(_jax.random.fold_in(key, i + 1))
        out[name] = w
        out["m_" + name] = s * _jax.random.normal(km, w.shape, _jnp.float32)
        out["v_" + name] = (s * s) * _jax.random.uniform(kv, w.shape, _jnp.float32, 0.5, 1.5)
    if N_MICROBATCH > 1:
        for name, axis in PER_EXAMPLE_BATCH_AXIS.items():
            out[name] = _to_microbatches(out[name], axis)
    return {'x': out['x'], 'mem': out['mem'], 'rel_bias': out['rel_bias'], 'mem_norm_g': out['mem_norm_g'], 'norm_mix_g': out['norm_mix_g'], 'w_in': out['w_in'], 's5_lam_re': out['s5_lam_re'], 's5_lam_im': out['s5_lam_im'], 's5_log_dt': out['s5_log_dt'], 's5_b_re': out['s5_b_re'], 's5_b_im': out['s5_b_im'], 's5_c_re': out['s5_c_re'], 's5_c_im': out['s5_c_im'], 's5_d': out['s5_d'], 's5_w_glu': out['s5_w_glu'], 'pool_w': out['pool_w'], 'pool_scale': out['pool_scale'], 'conv_w_dw': out['conv_w_dw'], 'conv_b_dw': out['conv_b_dw'], 'conv_ln_g': out['conv_ln_g'], 'conv_ln_b': out['conv_ln_b'], 'conv_w_pw': out['conv_w_pw'], 'grp_norm_g': out['grp_norm_g'], 'w_out': out['w_out'], 'norm_x_g': out['norm_x_g'], 'w_xq': out['w_xq'], 'w_xk': out['w_xk'], 'w_xv': out['w_xv'], 'w_xo': out['w_xo'], 'norm_mlp_g': out['norm_mlp_g'], 'w_up': out['w_up'], 'w_down': out['w_down'], 'norm_final_g': out['norm_final_g'], 'loss_target': out['loss_target'], 'm_rel_bias': out['m_rel_bias'], 'm_mem_norm_g': out['m_mem_norm_g'], 'm_norm_mix_g': out['m_norm_mix_g'], 'm_w_in': out['m_w_in'], 'm_s5_lam_re': out['m_s5_lam_re'], 'm_s5_lam_im': out['m_s5_lam_im'], 'm_s5_log_dt': out['m_s5_log_dt'], 'm_s5_b_re': out['m_s5_b_re'], 'm_s5_b_im': out['m_s5_b_im'], 'm_s5_c_re': out['m_s5_c_re'], 'm_s5_c_im': out['m_s5_c_im'], 'm_s5_d': out['m_s5_d'], 'm_s5_w_glu': out['m_s5_w_glu'], 'm_pool_w': out['m_pool_w'], 'm_pool_scale': out['m_pool_scale'], 'm_conv_w_dw': out['m_conv_w_dw'], 'm_conv_b_dw': out['m_conv_b_dw'], 'm_conv_ln_g': out['m_conv_ln_g'], 'm_conv_ln_b': out['m_conv_ln_b'], 'm_conv_w_pw': out['m_conv_w_pw'], 'm_grp_norm_g': out['m_grp_norm_g'], 'm_w_out': out['m_w_out'], 'm_norm_x_g': out['m_norm_x_g'], 'm_w_xq': out['m_w_xq'], 'm_w_xk': out['m_w_xk'], 'm_w_xv': out['m_w_xv'], 'm_w_xo': out['m_w_xo'], 'm_norm_mlp_g': out['m_norm_mlp_g'], 'm_w_up': out['m_w_up'], 'm_w_down': out['m_w_down'], 'm_norm_final_g': out['m_norm_final_g'], 'v_rel_bias': out['v_rel_bias'], 'v_mem_norm_g': out['v_mem_norm_g'], 'v_norm_mix_g': out['v_norm_mix_g'], 'v_w_in': out['v_w_in'], 'v_s5_lam_re': out['v_s5_lam_re'], 'v_s5_lam_im': out['v_s5_lam_im'], 'v_s5_log_dt': out['v_s5_log_dt'], 'v_s5_b_re': out['v_s5_b_re'], 'v_s5_b_im': out['v_s5_b_im'], 'v_s5_c_re': out['v_s5_c_re'], 'v_s5_c_im': out['v_s5_c_im'], 'v_s5_d': out['v_s5_d'], 'v_s5_w_glu': out['v_s5_w_glu'], 'v_pool_w': out['v_pool_w'], 'v_pool_scale': out['v_pool_scale'], 'v_conv_w_dw': out['v_conv_w_dw'], 'v_conv_b_dw': out['v_conv_b_dw'], 'v_conv_ln_g': out['v_conv_ln_g'], 'v_conv_ln_b': out['v_conv_ln_b'], 'v_conv_w_pw': out['v_conv_w_pw'], 'v_grp_norm_g': out['v_grp_norm_g'], 'v_w_out': out['v_w_out'], 'v_norm_x_g': out['v_norm_x_g'], 'v_w_xq': out['v_w_xq'], 'v_w_xk': out['v_w_xk'], 'v_w_xv': out['v_w_xv'], 'v_w_xo': out['v_w_xo'], 'v_norm_mlp_g': out['v_norm_mlp_g'], 'v_w_up': out['v_w_up'], 'v_w_down': out['v_w_down'], 'v_norm_final_g': out['v_norm_final_g']}


def _loss(weights, diff, rest, loss_target):
    with _jax.named_scope("forward"):
        args = {**rest, TWIN_DIFF_INPUT: diff, **{k: w.astype(_WEIGHT_DTYPES[k]) for k, w in weights.items()}}
        y = _forward(args)
    with _jax.named_scope("loss_head"):
        err = _jnp.square(y.astype(_jnp.float32) - loss_target)
        return 0.5 * _jnp.sum(_jnp.mean(err, axis=-1)) if err.ndim else 0.5 * err


def _adamw(w, g, m, v):
    m = ADAM_B1 * m + (1.0 - ADAM_B1) * g
    v = ADAM_B2 * v + (1.0 - ADAM_B2) * _jnp.square(g)
    m_hat = m / (1.0 - ADAM_B1 ** ADAM_STEP)
    v_hat = v / (1.0 - ADAM_B2 ** ADAM_STEP)
    delta = -ADAM_LR * (m_hat / (_jnp.sqrt(v_hat) + ADAM_EPS) + ADAM_WD * w)
    return delta, m, v


def reference(x, mem, rel_bias, mem_norm_g, norm_mix_g, w_in, s5_lam_re, s5_lam_im, s5_log_dt, s5_b_re, s5_b_im, s5_c_re, s5_c_im, s5_d, s5_w_glu, pool_w, pool_scale, conv_w_dw, conv_b_dw, conv_ln_g, conv_ln_b, conv_w_pw, grp_norm_g, w_out, norm_x_g, w_xq, w_xk, w_xv, w_xo, norm_mlp_g, w_up, w_down, norm_final_g, loss_target, m_rel_bias, m_mem_norm_g, m_norm_mix_g, m_w_in, m_s5_lam_re, m_s5_lam_im, m_s5_log_dt, m_s5_b_re, m_s5_b_im, m_s5_c_re, m_s5_c_im, m_s5_d, m_s5_w_glu, m_pool_w, m_pool_scale, m_conv_w_dw, m_conv_b_dw, m_conv_ln_g, m_conv_ln_b, m_conv_w_pw, m_grp_norm_g, m_w_out, m_norm_x_g, m_w_xq, m_w_xk, m_w_xv, m_w_xo, m_norm_mlp_g, m_w_up, m_w_down, m_norm_final_g, v_rel_bias, v_mem_norm_g, v_norm_mix_g, v_w_in, v_s5_lam_re, v_s5_lam_im, v_s5_log_dt, v_s5_b_re, v_s5_b_im, v_s5_c_re, v_s5_c_im, v_s5_d, v_s5_w_glu, v_pool_w, v_pool_scale, v_conv_w_dw, v_conv_b_dw, v_conv_ln_g, v_conv_ln_b, v_conv_w_pw, v_grp_norm_g, v_w_out, v_norm_x_g, v_w_xq, v_w_xk, v_w_xv, v_w_xo, v_norm_mlp_g, v_w_up, v_w_down, v_norm_final_g):
    given = dict(x=x, mem=mem, rel_bias=rel_bias, mem_norm_g=mem_norm_g, norm_mix_g=norm_mix_g, w_in=w_in, s5_lam_re=s5_lam_re, s5_lam_im=s5_lam_im, s5_log_dt=s5_log_dt, s5_b_re=s5_b_re, s5_b_im=s5_b_im, s5_c_re=s5_c_re, s5_c_im=s5_c_im, s5_d=s5_d, s5_w_glu=s5_w_glu, pool_w=pool_w, pool_scale=pool_scale, conv_w_dw=conv_w_dw, conv_b_dw=conv_b_dw, conv_ln_g=conv_ln_g, conv_ln_b=conv_ln_b, conv_w_pw=conv_w_pw, grp_norm_g=grp_norm_g, w_out=w_out, norm_x_g=norm_x_g, w_xq=w_xq, w_xk=w_xk, w_xv=w_xv, w_xo=w_xo, norm_mlp_g=norm_mlp_g, w_up=w_up, w_down=w_down, norm_final_g=norm_final_g, loss_target=loss_target, m_rel_bias=m_rel_bias, m_mem_norm_g=m_mem_norm_g, m_norm_mix_g=m_norm_mix_g, m_w_in=m_w_in, m_s5_lam_re=m_s5_lam_re, m_s5_lam_im=m_s5_lam_im, m_s5_log_dt=m_s5_log_dt, m_s5_b_re=m_s5_b_re, m_s5_b_im=m_s5_b_im, m_s5_c_re=m_s5_c_re, m_s5_c_im=m_s5_c_im, m_s5_d=m_s5_d, m_s5_w_glu=m_s5_w_glu, m_pool_w=m_pool_w, m_pool_scale=m_pool_scale, m_conv_w_dw=m_conv_w_dw, m_conv_b_dw=m_conv_b_dw, m_conv_ln_g=m_conv_ln_g, m_conv_ln_b=m_conv_ln_b, m_conv_w_pw=m_conv_w_pw, m_grp_norm_g=m_grp_norm_g, m_w_out=m_w_out, m_norm_x_g=m_norm_x_g, m_w_xq=m_w_xq, m_w_xk=m_w_xk, m_w_xv=m_w_xv, m_w_xo=m_w_xo, m_norm_mlp_g=m_norm_mlp_g, m_w_up=m_w_up, m_w_down=m_w_down, m_norm_final_g=m_norm_final_g, v_rel_bias=v_rel_bias, v_mem_norm_g=v_mem_norm_g, v_norm_mix_g=v_norm_mix_g, v_w_in=v_w_in, v_s5_lam_re=v_s5_lam_re, v_s5_lam_im=v_s5_lam_im, v_s5_log_dt=v_s5_log_dt, v_s5_b_re=v_s5_b_re, v_s5_b_im=v_s5_b_im, v_s5_c_re=v_s5_c_re, v_s5_c_im=v_s5_c_im, v_s5_d=v_s5_d, v_s5_w_glu=v_s5_w_glu, v_pool_w=v_pool_w, v_pool_scale=v_pool_scale, v_conv_w_dw=v_conv_w_dw, v_conv_b_dw=v_conv_b_dw, v_conv_ln_g=v_conv_ln_g, v_conv_ln_b=v_conv_ln_b, v_conv_w_pw=v_conv_w_pw, v_grp_norm_g=v_grp_norm_g, v_w_out=v_w_out, v_norm_x_g=v_norm_x_g, v_w_xq=v_w_xq, v_w_xk=v_w_xk, v_w_xv=v_w_xv, v_w_xo=v_w_xo, v_norm_mlp_g=v_norm_mlp_g, v_w_up=v_w_up, v_w_down=v_w_down, v_norm_final_g=v_norm_final_g)
    weights = {n: given[n] for n in TWIN_WEIGHTS}
    shared = {n: given[n] for n in SHARED_INPUTS}
    per_example = {n: given[n] for n in ['x', 'mem']}
    grad_fn = _jax.value_and_grad(_loss, argnums=(0, 1))

    def one_microbatch(ex, loss_target):
        ex = dict(ex)
        diff = ex.pop(TWIN_DIFF_INPUT)
        return grad_fn(weights, diff, {**shared, **ex}, loss_target)

    if N_MICROBATCH == 1:
        loss, (grad_w, grad_x) = one_microbatch(per_example, given["loss_target"])
    else:
        def body(carry, xs):
            loss_sum, grad_sum = carry
            l_k, (gw_k, gx_k) = one_microbatch(xs[0], xs[1])
            with _jax.named_scope("update"):
                return (loss_sum + l_k, _jax.tree.map(_jnp.add, grad_sum, gw_k)), gx_k

        init = (_jnp.zeros((), _jnp.float32), _jax.tree.map(_jnp.zeros_like, weights))
        (loss, grad_w), grad_x = _jax.lax.scan(body, init, (per_example, given["loss_target"]))
    with _jax.named_scope("update"):
        delta_w, new_m, new_v = {}, {}, {}
        for n in TWIN_WEIGHTS:
            delta_w[n], new_m[n], new_v[n] = _adamw(weights[n], grad_w[n], given["m_" + n], given["v_" + n])
    return (loss, grad_x, *[grad_w[n] for n in TWIN_WEIGHTS], *[delta_w[n] for n in TWIN_WEIGHTS],
            *[new_m[n] for n in TWIN_WEIGHTS], *[new_v[n] for n in TWIN_WEIGHTS])
```

```python
import functools
import math

import numpy as np
import jax
import jax.numpy as jnp
from jax import lax
from jax.experimental import pallas as pl
from jax.experimental.pallas import tpu as pltpu

F32 = jnp.float32
BF16 = jnp.bfloat16

DEPTH = 4
N_MIXERS = 4
S5_CH_PER_GROUP = 16
S5_STATE = 64
POOL_WINDOWS = (2, 4, 8, 16)
CONV_WIDTH = 31
ATT_HEADS = 8
DILATED_PATTERNS = ((128, 1), (512, 4), (2048, 16))
ATT_BLOCK = 128
REL_BUCKETS = 32
REL_MAX_DIST = 2048
X_HEADS = 4
X_HEAD_DIM = 128
NORM_EPS = 1e-6
NEG_INF = -1e30
ADAM_LR = 0.001
ADAM_B1 = 0.9
ADAM_B2 = 0.999
ADAM_EPS = 1e-08
ADAM_WD = 0.01
ADAM_STEP = 10

LANES = 128
SUBLANES = 8
VMEM_BYTES = 64 * 1024 * 1024
VMEM_LIMIT = (VMEM_BYTES * 3) // 4
VMEM_LIMIT_BIG = (VMEM_BYTES * 7) // 8
ADAM_BLOCK_BYTES = 1024 * 1024
N_DEV = 8
MESH_AXES = ("x", "y", "c")

WEIGHTS = ['rel_bias', 'mem_norm_g', 'norm_mix_g', 'w_in', 's5_lam_re', 's5_lam_im', 's5_log_dt', 's5_b_re',
           's5_b_im', 's5_c_re', 's5_c_im', 's5_d', 's5_w_glu', 'pool_w', 'pool_scale', 'conv_w_dw', 'conv_b_dw',
           'conv_ln_g', 'conv_ln_b', 'conv_w_pw', 'grp_norm_g', 'w_out', 'norm_x_g', 'w_xq', 'w_xk', 'w_xv', 'w_xo',
           'norm_mlp_g', 'w_up', 'w_down', 'norm_final_g']
SHARDED = {'w_in': 2, 's5_w_glu': 1, 'conv_w_dw': 2, 'conv_w_pw': 1, 'w_out': 1, 'w_xq': 1, 'w_xk': 1, 'w_xv': 1,
           'w_xo': 2, 'w_up': 2, 'w_down': 1}
GATHER_BF16 = ('w_in', 'w_out', 'w_xq', 'w_xk', 'w_xv', 'w_xo', 'w_up', 'w_down')
SMALL = [n for n in WEIGHTS if n not in SHARDED]


def _cp(n_axes, vmem=VMEM_LIMIT):
    return pltpu.CompilerParams(dimension_semantics=("arbitrary",) * n_axes, vmem_limit_bytes=vmem)


def _tile(n, prefs):
    for t in prefs:
        if n % t == 0:
            return t
    return n


def _mm(a, b, *, ta=False, tb=False, res=None, out_dtype=F32, name):
    if ta:
        K, M = a.shape
    else:
        M, K = a.shape
    if tb:
        N, K2 = b.shape
    else:
        K2, N = b.shape
    assert K == K2, (a.shape, b.shape, ta, tb)
    tm = _tile(M, (1024, 512, 256, 128))
    tn = _tile(N, (1024, 512, 256, 128))
    tk = _tile(K, (512, 256, 128))
    nk = K // tk
    a_spec = pl.BlockSpec((tk, tm), lambda i, j, k: (k, i)) if ta else pl.BlockSpec((tm, tk), lambda i, j, k: (i, k))
    b_spec = pl.BlockSpec((tn, tk), lambda i, j, k: (j, k)) if tb else pl.BlockSpec((tk, tn), lambda i, j, k: (k, j))
    o_spec = pl.BlockSpec((tm, tn), lambda i, j, k: (i, j))
    dn = (((0 if ta else 1,), (1 if tb else 0,)), ((), ()))
    has_res = res is not None

    def body(*refs):
        if has_res:
            a_ref, b_ref, r_ref, o_ref, acc = refs
        else:
            a_ref, b_ref, o_ref, acc = refs
        k = pl.program_id(2)

        @pl.when(k == 0)
        def _():
            acc[...] = jnp.zeros_like(acc)

        acc[...] += lax.dot_general(a_ref[...].astype(BF16), b_ref[...].astype(BF16), dn,
                                    preferred_element_type=F32)

        @pl.when(k == nk - 1)
        def _():
            r = acc[...]
            if has_res:
                r = r + r_ref[...]
            o_ref[...] = r.astype(out_dtype)

    in_specs = [a_spec, b_spec] + ([o_spec] if has_res else [])
    args = (a, b) + ((res,) if has_res else ())
    return pl.pallas_call(
        body, grid=(M // tm, N // tn, nk), in_specs=in_specs, out_specs=o_spec,
        out_shape=jax.ShapeDtypeStruct((M, N), out_dtype),
        scratch_shapes=[pltpu.VMEM((tm, tn), F32)], name=name, compiler_params=_cp(3))(*args)


def _linear(name):
    @jax.custom_vjp
    def lin(a, w):
        return _mm(a, w, name=name + "_fwd")

    def fwd(a, w):
        return _mm(a, w, name=name + "_fwd"), (a, w)

    def bwd(r, dy):
        a, w = r
        da = _mm(dy, w, tb=True, name=name + "_dx")
        dw = _mm(a, dy, ta=True, out_dtype=w.dtype, name=name + "_dw")
        return da, dw

    lin.defvjp(fwd, bwd)
    return lin


def _linear_res(name):
    @jax.custom_vjp
    def lin(a, w, res):
        return _mm(a, w, res=res, name=name + "_fwd")

    def fwd(a, w, res):
        return _mm(a, w, res=res, name=name + "_fwd"), (a, w)

    def bwd(r, dy):
        a, w = r
        da = _mm(dy, w, tb=True, name=name + "_dx")
        dw = _mm(a, dy, ta=True, out_dtype=w.dtype, name=name + "_dw")
        return da, dw, dy

    lin.defvjp(fwd, bwd)
    return lin


def _block_op(name, f, grid, ins, outs, vmem=VMEM_LIMIT):
    n_in, n_out = len(ins), len(outs)
    in_specs = [pl.BlockSpec(bs, im) for bs, im, _, _ in ins]
    out_specs = [pl.BlockSpec(bs, im) for _, _, bs, im in outs]
    out_shape = [jax.ShapeDtypeStruct(s, d) for s, d, _, _ in outs]
    didx = [i for i in range(n_in) if ins[i][3]]

    def fwd_call(*args):
        def body(*refs):
            res = f(*[r[...] for r in refs[:n_in]])
            for r, o in zip(refs[n_in:], res):
                r[...] = o

        return pl.pallas_call(body, grid=grid, in_specs=in_specs, out_specs=out_specs, out_shape=out_shape,
                              name=name + "_fwd", compiler_params=_cp(len(grid), vmem))(*args)

    def bwd_call(args, cts):
        def body(*refs):
            vals = [r[...] for r in refs[:n_in]]
            ct_refs = refs[n_in:n_in + n_out]
            g_refs = refs[n_in + n_out:]

            def fd(*dv):
                full = list(vals)
                for i, v in zip(didx, dv):
                    full[i] = v
                return f(*full)

            _, vjp = jax.vjp(fd, *[vals[i] for i in didx])
            grads = vjp(tuple(r[...] for r in ct_refs))
            for gref, i, g in zip(g_refs, didx, grads):
                acc = ins[i][2]
                if acc:
                    first = functools.reduce(jnp.logical_and, [pl.program_id(ax) == 0 for ax in acc])

                    @pl.when(first)
                    def _(gref=gref):
                        gref[...] = jnp.zeros_like(gref)

                    gref[...] += g.astype(gref.dtype)
                else:
                    gref[...] = g.astype(gref.dtype)

        g_specs = [pl.BlockSpec(ins[i][0], ins[i][1]) for i in didx]
        g_shape = [jax.ShapeDtypeStruct(args[i].shape, args[i].dtype) for i in didx]
        return pl.pallas_call(body, grid=grid, in_specs=in_specs + out_specs, out_specs=g_specs, out_shape=g_shape,
                              name=name + "_bwd", compiler_params=_cp(len(grid), vmem))(*args, *cts)

    @jax.custom_vjp
    def op(*args):
        return tuple(fwd_call(*args))

    def op_fwd(*args):
        return tuple(fwd_call(*args)), args

    def op_bwd(args, cts):
        it = iter(bwd_call(args, cts))
        return tuple(next(it) if ins[i][3] else jnp.zeros_like(args[i]) for i in range(n_in))

    op.defvjp(op_fwd, op_bwd)
    return op


def _row(tr, c):
    return ((tr, c), lambda i: (i, 0), None, True)


def _par(shape):
    nd = len(shape)
    return (shape, lambda i: (0,) * nd, (0,), True)


def _bdot(a, w):
    return jnp.dot(a.astype(BF16), w.astype(BF16), preferred_element_type=F32)


def _rms_f(x, g):
    return (x * lax.rsqrt(jnp.mean(x * x, axis=-1, keepdims=True) + NORM_EPS) * g,)


def rmsnorm(x, g, name):
    R, D = x.shape
    tr = _tile(R, (256,))
    op = _block_op(name, _rms_f, (R // tr,), [_row(tr, D), _par((1, D))], [((R, D), F32, (tr, D), lambda i: (i, 0))])
    return op(x, g.reshape(1, D))[0]


def s5_epilogue(yc, u, d, w_glu, name):
    R, C = yc.shape
    tr = _tile(R, (256,))

    def f(yc, u, d, w):
        g = jax.nn.gelu(yc + d * u)
        return (g * jax.nn.sigmoid(_bdot(g, w)),)

    op = _block_op(name, f, (R // tr,), [_row(tr, C), _row(tr, C), _par((1, C)), _par((C, C))],
                   [((R, C), F32, (tr, C), lambda i: (i, 0))])
    return op(yc, u, d.reshape(1, C), w_glu)[0]


def pool_proj(p, w, scale, name):
    R, C = p.shape
    ng, pc, _ = w.shape
    tr = _tile(R, (256,))

    def f(p, w, s):
        ys = [_bdot(p[:, g * pc:(g + 1) * pc], w[g]) for g in range(ng)]
        return (jnp.concatenate(ys, axis=-1) * s,)

    op = _block_op(name, f, (R // tr,), [_row(tr, C), _par((ng, pc, pc)), _par((1, C))],
                   [((R, C), F32, (tr, C), lambda i: (i, 0))])
    return op(p, w, scale.reshape(1, C))[0]


def conv_post(h, ln_g, ln_b, w_pw, name):
    R, C = h.shape
    tr = _tile(R, (256,))

    def f(h, g, b, w):
        hc = h - jnp.mean(h, axis=-1, keepdims=True)
        y = hc * lax.rsqrt(jnp.mean(hc * hc, axis=-1, keepdims=True) + NORM_EPS) * g + b
        return (_bdot(jax.nn.silu(y), w),)

    op = _block_op(name, f, (R // tr,), [_row(tr, C), _par((1, C)), _par((1, C)), _par((C, C))],
                   [((R, C), F32, (tr, C), lambda i: (i, 0))])
    return op(h, ln_g.reshape(1, C), ln_b.reshape(1, C), w_pw)[0]


def group_norm(ys, g, name):
    R, C = ys[0].shape
    n = len(ys)
    tr = _tile(R, (256,))

    def f(*a):
        g = a[n]
        parts = [y * lax.rsqrt(jnp.mean(y * y, axis=-1, keepdims=True) + NORM_EPS) for y in a[:n]]
        return (jnp.concatenate(parts, axis=-1) * g,)

    op = _block_op(name, f, (R // tr,), [_row(tr, C)] * n + [_par((1, n * C))],
                   [((R, n * C), F32, (tr, n * C), lambda i: (i, 0))])
    return op(*ys, g.reshape(1, n * C))[0]


def relu_sq(a, name):
    R, C = a.shape
    tr = _tile(R, (128,))
    op = _block_op(name, lambda a: (jnp.square(jnp.maximum(a, 0.0)),), (R // tr,), [_row(tr, C)],
                   [((R, C), F32, (tr, C), lambda i: (i, 0))])
    return op(a)[0]


def att_combine(os_, ls, name):
    R, C = os_[0].shape
    n = len(os_)
    tr = _tile(R, (256,))

    def f(*a):
        o, l = a[:n], a[n:]
        m = functools.reduce(jnp.maximum, l)
        e = [jnp.exp(li - m) for li in l]
        return (sum(ei * oi for ei, oi in zip(e, o)) / sum(e),)

    op = _block_op(name, f, (R // tr,), [_row(tr, C)] * (2 * n), [((R, C), F32, (tr, C), lambda i: (i, 0))])
    return op(*os_, *ls)[0]


def cross_attention(q, k, v, name):
    L, W = q.shape
    M = k.shape[0]
    E = X_HEAD_DIM
    tq = _tile(L, (512,))

    def f(q, k, v):
        s = lax.dot_general(q.astype(BF16), k.astype(BF16), (((1,), (1,)), ((), ())),
                            preferred_element_type=F32) * (E ** -0.5)
        p = jax.nn.softmax(s, axis=-1)
        return (_bdot(p, v),)

    qspec = ((tq, E), lambda h, i: (i, h), None, True)
    kspec = ((M, E), lambda h, i: (0, h), (1,), True)
    op = _block_op(name, f, (W // E, L // tq), [qspec, kspec, kspec], [((L, W), F32, (tq, E), lambda h, i: (i, h))])
    return op(q, k, v)[0]


def s5_discretise(lam_re, lam_im, log_dt, b_re_t, b_im_t, name):
    G, _, N = lam_re.shape
    C = b_re_t.shape[1]

    def f(lr, li, ldt, br, bi):
        dt = jnp.exp(ldt)
        mag = jnp.exp(lr * dt)
        ab_r, ab_i = mag * jnp.cos(li * dt), mag * jnp.sin(li * dt)
        den = lr * lr + li * li
        nr, ni = ab_r - 1.0, ab_i
        f_r = (nr * lr + ni * li) / den
        f_i = (ni * lr - nr * li) / den
        return ab_r, ab_i, f_r * br - f_i * bi, f_r * bi + f_i * br

    vec = ((G, 1, N), lambda i: (0, 0, 0), None, True)
    mat = ((G, C, N), lambda i: (0, 0, 0), None, True)
    ov = ((G, 1, N), F32, (G, 1, N), lambda i: (0, 0, 0))
    om = ((G, C, N), F32, (G, C, N), lambda i: (0, 0, 0))
    op = _block_op(name, f, (1,), [vec, vec, vec, mat, mat], [ov, ov, om, om])
    return op(lam_re, lam_im, log_dt, b_re_t, b_im_t)


def rel_bias_tables(rel_bias, onehot, name):
    B, H = rel_bias.shape
    P, _, Q = onehot.shape

    def f(rbt, oh):
        return (jnp.dot(rbt, oh, precision=lax.Precision.HIGHEST, preferred_element_type=F32),)

    op = _block_op(name, f, (P,), [((H, B), lambda p: (0, 0), (0,), True), ((None, B, Q), lambda p: (p, 0, 0), None, False)],
                   [((P, H, Q), F32, (None, H, Q), lambda p: (p, 0, 0))])
    return op(rel_bias.T, onehot)[0]


def _shift_down(x, s, row):
    return jnp.where(row >= s, pltpu.roll(x, s, 0), 0.0)


def _shift_up(x, s, row):
    n = x.shape[0]
    return jnp.where(row < n - s, pltpu.roll(x, n - s, 0), 0.0)


def _window_sum(x, w, row, shift):
    span = 1
    while span < w:
        x = x + shift(x, span, row)
        span *= 2
    return x


def _pool_call(u, d_out, name):
    L, C = u.shape
    pc = C // len(POOL_WINDOWS)
    assert pc % LANES == 0

    def body(x_ref, o_ref):
        row = lax.broadcasted_iota(jnp.int32, (L, pc), 0)
        for g, w in enumerate(POOL_WINDOWS):
            sl = slice(g * pc, (g + 1) * pc)
            x = x_ref[:, sl]
            cnt = jnp.minimum(row + 1, w).astype(F32)
            if d_out is None:
                o_ref[:, sl] = _window_sum(x, w, row, _shift_down) / cnt - x
            else:
                o_ref[:, sl] = _window_sum(x / cnt, w, row, _shift_up) - x

    src = u if d_out is None else d_out
    return pl.pallas_call(body, out_shape=jax.ShapeDtypeStruct((L, C), F32), name=name,
                          compiler_params=pltpu.CompilerParams(vmem_limit_bytes=VMEM_LIMIT))(src)


def _pool_mix(name):
    @jax.custom_vjp
    def op(u):
        return _pool_call(u, None, name + "_fwd")

    def fwd(u):
        return _pool_call(u, None, name + "_fwd"), u

    def bwd(u, dp):
        return (_pool_call(u, dp, name + "_bwd"),)

    op.defvjp(fwd, bwd)
    return op


def _conv_fwd(u, w, b, name):
    L, C2 = u.shape
    C = C2 // 2
    K = w.shape[0]
    nb = C // LANES

    def body(val_ref, gate_ref, w_ref, b_ref, o_ref):
        row = lax.broadcasted_iota(jnp.int32, (L, LANES), 0)
        h = val_ref[...] * jax.nn.sigmoid(gate_ref[...])
        acc = jnp.broadcast_to(b_ref[...], (L, LANES))
        for k in range(K):
            acc = acc + w_ref[k:k + 1, :] * _shift_down(h, K - 1 - k, row)
        o_ref[...] = acc

    blk = lambda off: pl.BlockSpec((L, LANES), lambda j: (0, j + off))
    return pl.pallas_call(
        body, grid=(nb,), in_specs=[blk(0), blk(nb), pl.BlockSpec((K, LANES), lambda j: (0, j)),
                                    pl.BlockSpec((1, LANES), lambda j: (0, j))],
        out_specs=blk(0), out_shape=jax.ShapeDtypeStruct((L, C), F32), name=name, compiler_params=_cp(1))(u, u, w, b)


def _conv_bwd(u, w, dh, name):
    L, C2 = u.shape
    C = C2 // 2
    K = w.shape[0]
    nb = C // LANES

    def body(val_ref, gate_ref, w_ref, dh_ref, dval_ref, dgate_ref, dw_ref, db_ref):
        row = lax.broadcasted_iota(jnp.int32, (L, LANES), 0)
        val = val_ref[...]
        sig = jax.nn.sigmoid(gate_ref[...])
        h = val * sig
        d = dh_ref[...]
        dh0 = jnp.zeros((L, LANES), F32)
        for k in range(K):
            s = K - 1 - k
            dh0 = dh0 + w_ref[k:k + 1, :] * _shift_up(d, s, row)
            dw_ref[k:k + 1, :] = jnp.sum(d * _shift_down(h, s, row), axis=0, keepdims=True)
        db_ref[...] = jnp.sum(d, axis=0, keepdims=True)
        dval_ref[...] = dh0 * sig
        dgate_ref[...] = dh0 * val * sig * (1.0 - sig)

    blk = lambda off: pl.BlockSpec((L, LANES), lambda j: (0, j + off))
    return pl.pallas_call(
        body, grid=(nb,), in_specs=[blk(0), blk(nb), pl.BlockSpec((K, LANES), lambda j: (0, j)), blk(0)],
        out_specs=[blk(0), blk(0), pl.BlockSpec((K, LANES), lambda j: (0, j)), pl.BlockSpec((1, LANES), lambda j: (0, j))],
        out_shape=[jax.ShapeDtypeStruct((L, C), F32), jax.ShapeDtypeStruct((L, C), F32),
                   jax.ShapeDtypeStruct((K, C), F32), jax.ShapeDtypeStruct((1, C), F32)],
        name=name, compiler_params=_cp(1))(u, u, w, dh)


def _glu_conv(name):
    @jax.custom_vjp
    def op(u, w, b):
        return _conv_fwd(u, w, b, name + "_fwd")

    def fwd(u, w, b):
        return _conv_fwd(u, w, b, name + "_fwd"), (u, w)

    def bwd(r, dh):
        u, w = r
        dval, dgate, dw, db = _conv_bwd(u, w, dh, name + "_bwd")
        return jnp.concatenate([dval, dgate], axis=-1), dw, db

    op.defvjp(fwd, bwd)
    return op


S5_BLOCK_CH = LANES
S5_BLOCK_ST = S5_BLOCK_CH // S5_CH_PER_GROUP * S5_STATE


def _s5_scan(br_ref, bi_ref, ar, ai, reverse):
    L, C = br_ref.shape
    T = SUBLANES
    row = lax.broadcasted_iota(jnp.int32, (T, C), 0)
    pw = [(ar, ai)]
    for _ in range(T - 1):
        pr, pi = pw[-1]
        pw.append((pr * ar - pi * ai, pr * ai + pi * ar))
    cr = jnp.zeros((T, C), F32)
    ci = jnp.zeros((T, C), F32)
    for r in range(T):
        e = (T - r) if reverse else (r + 1)
        cr = jnp.where(row == r, pw[e - 1][0], cr)
        ci = jnp.where(row == r, pw[e - 1][1], ci)
    steps = []
    s = 1
    while s < T:
        mask = (row < T - s) if reverse else (row >= s)
        steps.append((T - s if reverse else s, mask, pw[s - 1][0], pw[s - 1][1]))
        s *= 2
    nt = L // T
    last = 0 if reverse else T - 1

    def body(i, carry):
        kr, ki = carry
        t = (nt - 1 - i) if reverse else i
        off = pl.multiple_of(t * T, T)
        xr = br_ref[pl.ds(off, T), :]
        xi = bi_ref[pl.ds(off, T), :]
        for sh, mask, mr, mi in steps:
            sr = jnp.where(mask, pltpu.roll(xr, sh, 0), 0.0)
            si = jnp.where(mask, pltpu.roll(xi, sh, 0), 0.0)
            xr, xi = xr + mr * sr - mi * si, xi + mr * si + mi * sr
        xr, xi = xr + cr * kr - ci * ki, xi + cr * ki + ci * kr
        br_ref[pl.ds(off, T), :] = xr
        bi_ref[pl.ds(off, T), :] = xi
        return (jnp.broadcast_to(xr[last:last + 1, :], (T, C)), jnp.broadcast_to(xi[last:last + 1, :], (T, C)))

    z = jnp.zeros((T, C), F32)
    lax.fori_loop(0, nt, body, (z, z))


def _s5_specs(L):
    nb_axis = lambda j: (j, 0, 0)
    u = pl.BlockSpec((L, S5_BLOCK_CH), lambda j: (0, j))
    wb = pl.BlockSpec((None, S5_BLOCK_CH, S5_BLOCK_ST), nb_axis)
    a = pl.BlockSpec((1, S5_BLOCK_ST), lambda j: (0, j))
    wc = pl.BlockSpec((None, S5_BLOCK_ST, S5_BLOCK_CH), nb_axis)
    return u, wb, a, wc


def _s5_fwd(u, wbr, wbi, ar, ai, wcr, wci, name):
    L, C = u.shape
    nb = C // S5_BLOCK_CH
    us, wbs, as_, wcs = _s5_specs(L)

    def body(u_ref, wbr_ref, wbi_ref, ar_ref, ai_ref, wcr_ref, wci_ref, y_ref, xr, xi):
        ub = u_ref[...]
        xr[...] = _bdot(ub, wbr_ref[...])
        xi[...] = _bdot(ub, wbi_ref[...])
        _s5_scan(xr, xi, ar_ref[...], ai_ref[...], False)
        y_ref[...] = _bdot(xr[...], wcr_ref[...]) - _bdot(xi[...], wci_ref[...])

    return pl.pallas_call(
        body, grid=(nb,), in_specs=[us, wbs, wbs, as_, as_, wcs, wcs], out_specs=us,
        out_shape=jax.ShapeDtypeStruct((L, C), F32),
        scratch_shapes=[pltpu.VMEM((L, S5_BLOCK_ST), F32)] * 2, name=name, compiler_params=_cp(1))(
            u, wbr, wbi, ar, ai, wcr, wci)


def _dot_t(a, b):
    return lax.dot_general(a.astype(BF16), b.astype(BF16), (((0,), (0,)), ((), ())), preferred_element_type=F32)


def _dot_nt(a, b):
    return lax.dot_general(a.astype(BF16), b.astype(BF16), (((1,), (1,)), ((), ())), preferred_element_type=F32)


def _s5_bwd(u, wbr, wbi, ar, ai, wcr, wci, dy, name):
    L, C = u.shape
    nb = C // S5_BLOCK_CH
    us, wbs, as_, wcs = _s5_specs(L)
    T = SUBLANES

    def body(u_ref, wbr_ref, wbi_ref, ar_ref, ai_ref, wcr_ref, wci_ref, dy_ref,
             du_ref, dwbr_ref, dwbi_ref, dar_ref, dai_ref, dwcr_ref, dwci_ref, xr, xi, gr, gi):
        ub = u_ref[...]
        a_r, a_i = ar_ref[...], ai_ref[...]
        xr[...] = _bdot(ub, wbr_ref[...])
        xi[...] = _bdot(ub, wbi_ref[...])
        _s5_scan(xr, xi, a_r, a_i, False)
        d = dy_ref[...]
        dwcr_ref[...] = _dot_t(xr[...], d)
        dwci_ref[...] = -_dot_t(xi[...], d)
        gr[...] = _dot_nt(d, wcr_ref[...])
        gi[...] = -_dot_nt(d, wci_ref[...])
        _s5_scan(gr, gi, a_r, -a_i, True)

        row = lax.broadcasted_iota(jnp.int32, (T, S5_BLOCK_ST), 0)

        def da_body(i, carry):
            pr, pi, sr, si = carry
            off = pl.multiple_of(i * T, T)
            xr_t, xi_t = xr[pl.ds(off, T), :], xi[pl.ds(off, T), :]
            lr_t, li_t = gr[pl.ds(off, T), :], gi[pl.ds(off, T), :]
            qr = jnp.where(row == 0, pr, pltpu.roll(xr_t, 1, 0))
            qi = jnp.where(row == 0, pi, pltpu.roll(xi_t, 1, 0))
            sr = sr + qr * lr_t + qi * li_t
            si = si + qr * li_t - qi * lr_t
            return (jnp.broadcast_to(xr_t[T - 1:T, :], (T, S5_BLOCK_ST)),
                    jnp.broadcast_to(xi_t[T - 1:T, :], (T, S5_BLOCK_ST)), sr, si)

        z = jnp.zeros((T, S5_BLOCK_ST), F32)
        _, _, sr, si = lax.fori_loop(0, L // T, da_body, (z, z, z, z))
        dar_ref[...] = jnp.sum(sr, axis=0, keepdims=True)
        dai_ref[...] = jnp.sum(si, axis=0, keepdims=True)
        lr, li = gr[...], gi[...]
        dwbr_ref[...] = _dot_t(ub, lr)
        dwbi_ref[...] = _dot_t(ub, li)
        du_ref[...] = _dot_nt(lr, wbr_ref[...]) + _dot_nt(li, wbi_ref[...])

    sds = jax.ShapeDtypeStruct
    return pl.pallas_call(
        body, grid=(nb,), in_specs=[us, wbs, wbs, as_, as_, wcs, wcs, us],
        out_specs=[us, wbs, wbs, as_, as_, wcs, wcs],
        out_shape=[sds(u.shape, F32), sds(wbr.shape, F32), sds(wbi.shape, F32), sds(ar.shape, F32),
                   sds(ai.shape, F32), sds(wcr.shape, F32), sds(wci.shape, F32)],
        scratch_shapes=[pltpu.VMEM((L, S5_BLOCK_ST), F32)] * 4, name=name,
        compiler_params=_cp(1, VMEM_LIMIT_BIG))(u, wbr, wbi, ar, ai, wcr, wci, dy)


def _s5_core(name):
    @jax.custom_vjp
    def op(u, wbr, wbi, ar, ai, wcr, wci):
        return _s5_fwd(u, wbr, wbi, ar, ai, wcr, wci, name + "_fwd")

    def fwd(*a):
        return _s5_fwd(*a, name + "_fwd"), a

    def bwd(a, dy):
        return tuple(_s5_bwd(*a, dy, name + "_bwd"))

    op.defvjp(fwd, bwd)
    return op


def _band_attn_f(first, q, kp, kc, vp, vc, bias):
    nq, E = q.shape
    k = jnp.concatenate([kp, kc], axis=0)
    v = jnp.concatenate([vp, vc], axis=0)
    s = _dot_nt(q, k) * (E ** -0.5) + bias
    r = lax.broadcasted_iota(jnp.int32, (nq, 2 * nq), 0)
    c = lax.broadcasted_iota(jnp.int32, (nq, 2 * nq), 1)
    prev_ok = jnp.logical_and(jnp.logical_and(c < nq, c >= r), jnp.logical_not(first))
    valid = jnp.logical_or(prev_ok, jnp.logical_and(c >= nq, c - nq <= r))
    s = jnp.where(valid, s, NEG_INF)
    m = jnp.max(s, axis=-1, keepdims=True)
    p = jnp.exp(s - m)
    den = jnp.sum(p, axis=-1, keepdims=True)
    o = _bdot(p, v) / den
    return o, jnp.broadcast_to(m + jnp.log(den), (nq, E))


def _band_specs(Ls, E):
    blk = lambda f: pl.BlockSpec((None, None, ATT_BLOCK, E), f)
    q = blk(lambda h, r, n: (h, r, n, 0))
    kprev = blk(lambda h, r, n: (h, r, jnp.maximum(n - 1, 0), 0))
    bias = pl.BlockSpec((None, ATT_BLOCK, 2 * ATT_BLOCK), lambda h, r, n: (h, 0, 0))
    whole = pl.BlockSpec((None, None, Ls, E), lambda h, r, n: (h, r, 0, 0))
    return q, kprev, bias, whole


def _band_fwd(q, k, v, bias, name):
    H, d, Ls, E = q.shape
    nb = Ls // ATT_BLOCK
    qs, kps, bs, _ = _band_specs(Ls, E)

    def body(q_ref, kp_ref, kc_ref, vp_ref, vc_ref, b_ref, o_ref, l_ref):
        o, l = _band_attn_f(pl.program_id(2) == 0, q_ref[...], kp_ref[...], kc_ref[...], vp_ref[...], vc_ref[...],
                            b_ref[...])
        o_ref[...] = o
        l_ref[...] = l

    sds = jax.ShapeDtypeStruct(q.shape, F32)
    return pl.pallas_call(body, grid=(H, d, nb), in_specs=[qs, kps, qs, kps, qs, bs], out_specs=[qs, qs],
                          out_shape=[sds, sds], name=name, compiler_params=_cp(3))(q, k, k, v, v, bias)


def _band_bwd(q, k, v, bias, do, dl, name):
    H, d, Ls, E = q.shape
    nb = Ls // ATT_BLOCK
    qs, kps, bs, whole = _band_specs(Ls, E)

    def body(q_ref, kp_ref, kc_ref, vp_ref, vc_ref, b_ref, do_ref, dl_ref, dq_ref, dk_ref, dv_ref, db_ref):
        r, n = pl.program_id(1), pl.program_id(2)
        first = n == 0
        _, vjp = jax.vjp(functools.partial(_band_attn_f, first), q_ref[...], kp_ref[...], kc_ref[...], vp_ref[...],
                         vc_ref[...], b_ref[...])
        dq, dkp, dkc, dvp, dvc, db = vjp((do_ref[...], dl_ref[...]))

        @pl.when(first)
        def _():
            dk_ref[...] = jnp.zeros_like(dk_ref)
            dv_ref[...] = jnp.zeros_like(dv_ref)

        @pl.when(jnp.logical_and(first, r == 0))
        def _():
            db_ref[...] = jnp.zeros_like(db_ref)

        dq_ref[...] = dq
        db_ref[...] += db
        cur = pl.multiple_of(n * ATT_BLOCK, ATT_BLOCK)
        dk_ref[pl.ds(cur, ATT_BLOCK), :] += dkc
        dv_ref[pl.ds(cur, ATT_BLOCK), :] += dvc

        @pl.when(jnp.logical_not(first))
        def _():
            prev = pl.multiple_of((n - 1) * ATT_BLOCK, ATT_BLOCK)
            dk_ref[pl.ds(prev, ATT_BLOCK), :] += dkp
            dv_ref[pl.ds(prev, ATT_BLOCK), :] += dvp

    sds = jax.ShapeDtypeStruct(q.shape, F32)
    return pl.pallas_call(
        body, grid=(H, d, nb), in_specs=[qs, kps, qs, kps, qs, bs, qs, qs], out_specs=[qs, whole, whole, bs],
        out_shape=[sds, sds, sds, jax.ShapeDtypeStruct(bias.shape, F32)], name=name, compiler_params=_cp(3))(
            q, k, k, v, v, bias, do, dl)


def _band_attention(name):
    @jax.custom_vjp
    def op(q, k, v, bias):
        return tuple(_band_fwd(q, k, v, bias, name + "_fwd"))

    def fwd(q, k, v, bias):
        return tuple(_band_fwd(q, k, v, bias, name + "_fwd")), (q, k, v, bias)

    def bwd(r, ct):
        return tuple(_band_bwd(*r, ct[0], ct[1], name + "_bwd"))

    op.defvjp(fwd, bwd)
    return op


def _t5_bucket(dist):
    n = np.maximum(dist, 0)
    max_exact = REL_BUCKETS // 2
    large = max_exact + (np.log(np.maximum(n, 1) / max_exact) / np.log(REL_MAX_DIST / max_exact)
                         * (REL_BUCKETS - max_exact)).astype(np.int64)
    large = np.minimum(large, REL_BUCKETS - 1)
    return np.where(n < max_exact, n, large).astype(np.int32)


def _bucket_onehot():
    a = np.arange(ATT_BLOCK)[:, None]
    b = np.arange(2 * ATT_BLOCK)[None, :]
    sub = a + ATT_BLOCK - b
    bucket = jnp.asarray(np.stack([_t5_bucket(sub * dil).reshape(-1) for _, dil in DILATED_PATTERNS]))
    ids = jnp.arange(REL_BUCKETS, dtype=jnp.int32)
    return (bucket[:, None, :] == ids[None, :, None]).astype(F32)


def loss_head(h, target, g, name):
    R, D = h.shape
    tr = _tile(R, (256,))

    def body(h_ref, t_ref, g_ref, l_ref, dh_ref, dg_ref):
        def lf(hv, gv):
            y = _rms_f(hv, gv)[0]
            return 0.5 * jnp.sum(jnp.mean(jnp.square(y - t_ref[...]), axis=-1))

        l, (dh, dg) = jax.value_and_grad(lf, argnums=(0, 1))(h_ref[...], g_ref[...])

        @pl.when(pl.program_id(0) == 0)
        def _():
            l_ref[...] = jnp.zeros_like(l_ref)
            dg_ref[...] = jnp.zeros_like(dg_ref)

        dh_ref[...] = dh
        dg_ref[...] += dg
        l_ref[...] += l

    rows = pl.BlockSpec((tr, D), lambda i: (i, 0))
    vec = pl.BlockSpec((1, D), lambda i: (0, 0))
    l, dh, dg = pl.pallas_call(
        body, grid=(R // tr,), in_specs=[rows, rows, vec],
        out_specs=[pl.BlockSpec((SUBLANES, LANES), lambda i: (0, 0)), rows, vec],
        out_shape=[jax.ShapeDtypeStruct((SUBLANES, LANES), F32), jax.ShapeDtypeStruct((R, D), F32),
                   jax.ShapeDtypeStruct((1, D), F32)], name=name, compiler_params=_cp(1))(h, target, g.reshape(1, D))
    return l[0, 0], dh, dg.reshape(D)


def adamw(w, parts, m, v, name):
    R, C = w.shape
    cap = max(SUBLANES, ADAM_BLOCK_BYTES // (4 * C))
    tr = _tile(R, [t for t in (512, 256, 128, 64, 32, 16, 8) if t <= cap])
    c1 = 1.0 - ADAM_B1 ** ADAM_STEP
    c2 = 1.0 - ADAM_B2 ** ADAM_STEP

    def body(w_ref, p_ref, m_ref, v_ref, g_ref, d_ref, nm_ref, nv_ref):
        g = p_ref[0].astype(F32)
        for i in range(1, N_DEV):
            g = g + p_ref[i].astype(F32)
        nm = ADAM_B1 * m_ref[...] + (1.0 - ADAM_B1) * g
        nv = ADAM_B2 * v_ref[...] + (1.0 - ADAM_B2) * jnp.square(g)
        d_ref[...] = -ADAM_LR * ((nm / c1) / (jnp.sqrt(nv / c2) + ADAM_EPS) + ADAM_WD * w_ref[...])
        g_ref[...] = g
        nm_ref[...] = nm
        nv_ref[...] = nv

    rows = pl.BlockSpec((tr, C), lambda i: (i, 0))
    sds = jax.ShapeDtypeStruct((R, C), F32)
    return pl.pallas_call(body, grid=(R // tr,),
                          in_specs=[rows, pl.BlockSpec((N_DEV, tr, C), lambda i: (0, i, 0)), rows, rows],
                          out_specs=[rows] * 4, out_shape=[sds] * 4, name=name, compiler_params=_cp(1))(w, parts, m, v)


HBM_SPEC = pl.BlockSpec(memory_space=pltpu.HBM)
MESH_ID = pl.DeviceIdType.MESH


def _place():
    return lax.axis_index("x"), lax.axis_index("y"), lax.axis_index("c")


def _index(x, y, c):
    return 4 * x + 2 * y + c


def all_gather(x, name):
    def body(x_ref, out_ref, send_sems, recv_sems, local_sem):
        x_, y_, c_ = _place()
        me, sibling = (x_, y_, c_), (x_, y_, 1 - c_)
        chips = [(1 - x_, y_), (x_, 1 - y_), (1 - x_, 1 - y_)]

        def slot(px, py, pc):
            return out_ref.at[_index(px, py, pc)]

        def copy(k, block, to, src=None):
            return pltpu.make_async_remote_copy(
                src_ref=slot(*block) if src is None else src, dst_ref=slot(*block),
                send_sem=send_sems.at[k], recv_sem=recv_sems.at[k], device_id=to, device_id_type=MESH_ID)

        mine = pltpu.make_async_copy(x_ref, slot(*me), local_sem)
        mine.start()
        first = [copy(0, me, sibling, src=x_ref)]
        first += [copy(1 + j, me, (*chip, c_), src=x_ref) for j, chip in enumerate(chips)]
        for cp in first:
            cp.start()
        passed = [copy(4 + j, (*chip, c_), sibling) for j, chip in enumerate(chips)]
        for j, chip in enumerate(chips):
            copy(1 + j, (*chip, c_), me).wait_recv()
            passed[j].start()
        copy(0, sibling, me).wait_recv()
        for j, chip in enumerate(chips):
            copy(4 + j, (*chip, 1 - c_), me).wait_recv()
        for cp in first + passed:
            cp.wait_send()
        mine.wait()

    return pl.pallas_call(
        body, out_shape=jax.ShapeDtypeStruct((N_DEV,) + x.shape, x.dtype), in_specs=[HBM_SPEC], out_specs=HBM_SPEC,
        scratch_shapes=[pltpu.SemaphoreType.DMA((N_DEV - 1,)), pltpu.SemaphoreType.DMA((N_DEV - 1,)),
                        pltpu.SemaphoreType.DMA], name=name)(x)


def all_to_all(x, name):
    def body(x_ref, out_ref, send_sems, recv_sems, local_sem):
        x_, y_, c_ = _place()
        me = _index(x_, y_, c_)
        local = pltpu.make_async_copy(x_ref.at[me], out_ref.at[me], local_sem)
        local.start()
        copies = []
        for k in range(1, N_DEV):
            px = 1 - x_ if k & 4 else x_
            py = 1 - y_ if k & 2 else y_
            pc = 1 - c_ if k & 1 else c_
            cp = pltpu.make_async_remote_copy(
                src_ref=x_ref.at[_index(px, py, pc)], dst_ref=out_ref.at[me],
                send_sem=send_sems.at[k - 1], recv_sem=recv_sems.at[k - 1], device_id=(px, py, pc),
                device_id_type=MESH_ID)
            cp.start()
            copies.append(cp)
        for cp in copies:
            cp.wait()
        local.wait()

    return pl.pallas_call(
        body, out_shape=jax.ShapeDtypeStruct(x.shape, x.dtype), in_specs=[HBM_SPEC], out_specs=HBM_SPEC,
        scratch_shapes=[pltpu.SemaphoreType.DMA((N_DEV - 1,)), pltpu.SemaphoreType.DMA((N_DEV - 1,)),
                        pltpu.SemaphoreType.DMA], name=name)(x)


def _to_sub(t, dil):
    L, W = t.shape
    E = W // ATT_HEADS
    return t.reshape(L // dil, dil, ATT_HEADS, E).transpose(2, 1, 0, 3)


def _from_sub(t):
    H, dil, Ls, E = t.shape
    return t.transpose(2, 1, 0, 3).reshape(Ls * dil, H * E)


def _block_diag(w, nb):
    G, a, b = w.shape
    gp = G // nb
    eye = jnp.eye(gp, dtype=w.dtype)
    return jnp.einsum('jgab,gh->jgahb', w.reshape(nb, gp, a, b), eye).reshape(nb, gp * a, gp * b)


def _layer(l, h, memn, bias_tabs, P):
    nm = lambda s: f"l{l}_{s}"
    L, D = h.shape
    GW = D // N_MIXERS
    G = GW // S5_CH_PER_GROUP

    xn = rmsnorm(h, P['norm_mix_g'], nm("norm_mix"))
    proj = _linear(nm("w_in"))(xn, P['w_in'])
    u_a, u_b, u_c, qkv = proj[:, :GW], proj[:, GW:2 * GW], proj[:, 2 * GW:4 * GW], proj[:, 4 * GW:]

    v3 = lambda a: a.reshape(G, 1, S5_STATE)
    log_dt = jnp.broadcast_to(P['s5_log_dt'][:, None, None], (G, 1, S5_STATE))
    a_r, a_i, bb_r, bb_i = s5_discretise(v3(P['s5_lam_re']), v3(P['s5_lam_im']), log_dt,
                                         P['s5_b_re'].transpose(0, 2, 1), P['s5_b_im'].transpose(0, 2, 1), nm("s5_disc"))
    nblk = GW // S5_BLOCK_CH
    y_s5 = _s5_core(nm("s5_core"))(
        u_a, _block_diag(bb_r, nblk), _block_diag(bb_i, nblk), a_r.reshape(1, G * S5_STATE), a_i.reshape(1, G * S5_STATE),
        _block_diag(P['s5_c_re'].transpose(0, 2, 1), nblk), _block_diag(P['s5_c_im'].transpose(0, 2, 1), nblk))
    y_a = s5_epilogue(y_s5, u_a, P['s5_d'], P['s5_w_glu'], nm("s5_glu"))

    y_b = pool_proj(_pool_mix(nm("pool_mix"))(u_b), P['pool_w'], P['pool_scale'], nm("pool_proj"))

    hc = _glu_conv(nm("conv_dw"))(u_c, P['conv_w_dw'], P['conv_b_dw'].reshape(1, GW))
    y_c = conv_post(hc, P['conv_ln_g'], P['conv_ln_b'], P['conv_w_pw'], nm("conv_post"))

    q, k, v = qkv[:, :GW], qkv[:, GW:2 * GW], qkv[:, 2 * GW:]
    outs, lses = [], []
    for p, (_, dil) in enumerate(DILATED_PATTERNS):
        o, lse = _band_attention(nm(f"att{p}"))(_to_sub(q, dil), _to_sub(k, dil), _to_sub(v, dil), bias_tabs[p])
        outs.append(_from_sub(o))
        lses.append(_from_sub(lse))
    y_d = att_combine(outs, lses, nm("att_mix"))

    y = group_norm([y_a, y_b, y_c, y_d], P['grp_norm_g'], nm("grp_norm"))
    h = _linear_res(nm("w_out"))(y, P['w_out'], h)

    hn = rmsnorm(h, P['norm_x_g'], nm("norm_x"))
    xq = _linear(nm("w_xq"))(hn, P['w_xq'])
    xk = _linear(nm("w_xk"))(memn, P['w_xk'])
    xv = _linear(nm("w_xv"))(memn, P['w_xv'])
    xo = cross_attention(xq, xk, xv, nm("xattn"))
    h = _linear_res(nm("w_xo"))(xo, P['w_xo'], h)

    hn = rmsnorm(h, P['norm_mlp_g'], nm("norm_mlp"))
    r = relu_sq(_linear(nm("w_up"))(hn, P['w_up']), nm("relu_sq"))
    return _linear_res(nm("w_down"))(r, P['w_down'], h)


def _trunk(x, mem, W):
    memn = rmsnorm(mem, W['mem_norm_g'], "mem_norm")
    tabs = rel_bias_tables(W['rel_bias'], _bucket_onehot(), "rel_bias")
    bias_tabs = [tabs[p].reshape(ATT_HEADS, ATT_BLOCK, 2 * ATT_BLOCK) for p in range(len(DILATED_PATTERNS))]
    h = x
    for l in range(DEPTH):
        P = {n: W[n][l] for n in WEIGHTS if n not in ('rel_bias', 'mem_norm_g', 'norm_final_g')}
        h = _layer(l, h, memn, bias_tabs, P)
    return h


def _gather_weight(name, w):
    ax = SHARDED[name]
    dt = BF16 if name in GATHER_BF16 else F32
    nl, a, b = w.shape
    g = all_gather(w.astype(dt).reshape(nl * a, b), "ag_" + name).reshape(N_DEV, nl, a, b)
    if ax == 1:
        return g.transpose(1, 0, 2, 3).reshape(nl, N_DEV * a, b)
    return g.transpose(1, 2, 0, 3).reshape(nl, a, N_DEV * b)


def _scatter_grad(name, g):
    ax = SHARDED[name]
    nl = g.shape[0]
    if ax == 1:
        a, b = g.shape[1] // N_DEV, g.shape[2]
        s = g.reshape(nl, N_DEV, a, b).transpose(1, 0, 2, 3)
    else:
        a, b = g.shape[1], g.shape[2] // N_DEV
        s = g.reshape(nl, a, N_DEV, b).transpose(2, 0, 1, 3)
    return all_to_all(s.reshape(N_DEV, nl * a, b), "a2a_" + name)


def _flatten_small(d):
    flat = jnp.concatenate([d[n].reshape(-1).astype(F32) for n in SMALL])
    pad = (-flat.shape[0]) % (LANES * SUBLANES)
    return jnp.pad(flat, (0, pad)).reshape(-1, LANES)


def _split_small(flat, like):
    flat = flat.reshape(-1)
    out, off = {}, 0
    for n in SMALL:
        sz = math.prod(like[n].shape)
        out[n] = flat[off:off + sz].reshape(like[n].shape)
        off += sz
    return out


def kernel(x, mem, rel_bias, mem_norm_g, norm_mix_g, w_in, s5_lam_re, s5_lam_im, s5_log_dt, s5_b_re, s5_b_im, s5_c_re, s5_c_im, s5_d, s5_w_glu, pool_w, pool_scale, conv_w_dw, conv_b_dw, conv_ln_g, conv_ln_b, conv_w_pw, grp_norm_g, w_out, norm_x_g, w_xq, w_xk, w_xv, w_xo, norm_mlp_g, w_up, w_down, norm_final_g, loss_target, m_rel_bias, m_mem_norm_g, m_norm_mix_g, m_w_in, m_s5_lam_re, m_s5_lam_im, m_s5_log_dt, m_s5_b_re, m_s5_b_im, m_s5_c_re, m_s5_c_im, m_s5_d, m_s5_w_glu, m_pool_w, m_pool_scale, m_conv_w_dw, m_conv_b_dw, m_conv_ln_g, m_conv_ln_b, m_conv_w_pw, m_grp_norm_g, m_w_out, m_norm_x_g, m_w_xq, m_w_xk, m_w_xv, m_w_xo, m_norm_mlp_g, m_w_up, m_w_down, m_norm_final_g, v_rel_bias, v_mem_norm_g, v_norm_mix_g, v_w_in, v_s5_lam_re, v_s5_lam_im, v_s5_log_dt, v_s5_b_re, v_s5_b_im, v_s5_c_re, v_s5_c_im, v_s5_d, v_s5_w_glu, v_pool_w, v_pool_scale, v_conv_w_dw, v_conv_b_dw, v_conv_ln_g, v_conv_ln_b, v_conv_w_pw, v_grp_norm_g, v_w_out, v_norm_x_g, v_w_xq, v_w_xk, v_w_xv, v_w_xo, v_norm_mlp_g, v_w_up, v_w_down, v_norm_final_g):
    w = dict(zip(WEIGHTS, (rel_bias, mem_norm_g, norm_mix_g, w_in, s5_lam_re, s5_lam_im, s5_log_dt, s5_b_re, s5_b_im, s5_c_re, s5_c_im, s5_d, s5_w_glu, pool_w, pool_scale, conv_w_dw, conv_b_dw, conv_ln_g, conv_ln_b, conv_w_pw, grp_norm_g, w_out, norm_x_g, w_xq, w_xk, w_xv, w_xo, norm_mlp_g, w_up, w_down, norm_final_g)))
    m = dict(zip(WEIGHTS, (m_rel_bias, m_mem_norm_g, m_norm_mix_g, m_w_in, m_s5_lam_re, m_s5_lam_im, m_s5_log_dt, m_s5_b_re, m_s5_b_im, m_s5_c_re, m_s5_c_im, m_s5_d, m_s5_w_glu, m_pool_w, m_pool_scale, m_conv_w_dw, m_conv_b_dw, m_conv_ln_g, m_conv_ln_b, m_conv_w_pw, m_grp_norm_g, m_w_out, m_norm_x_g, m_w_xq, m_w_xk, m_w_xv, m_w_xo, m_norm_mlp_g, m_w_up, m_w_down, m_norm_final_g)))
    v = dict(zip(WEIGHTS, (v_rel_bias, v_mem_norm_g, v_norm_mix_g, v_w_in, v_s5_lam_re, v_s5_lam_im, v_s5_log_dt, v_s5_b_re, v_s5_b_im, v_s5_c_re, v_s5_c_im, v_s5_d, v_s5_w_glu, v_pool_w, v_pool_scale, v_conv_w_dw, v_conv_b_dw, v_conv_ln_g, v_conv_ln_b, v_conv_w_pw, v_grp_norm_g, v_w_out, v_norm_x_g, v_w_xq, v_w_xk, v_w_xv, v_w_xo, v_norm_mlp_g, v_w_up, v_w_down, v_norm_final_g)))

    full = {n: (_gather_weight(n, w[n]) if n in SHARDED else w[n]) for n in WEIGHTS if n != 'norm_final_g'}

    h, trunk_vjp = jax.vjp(_trunk, x[0], mem[0], full)
    loss_local, dh, d_final_g = loss_head(h, loss_target[0], w['norm_final_g'], "loss_head")
    dx, _, dfull = trunk_vjp(dh)
    dfull['norm_final_g'] = d_final_g
    loss = lax.psum(loss_local, MESH_AXES)

    grads, deltas, new_m, new_v = {}, {}, {}, {}
    for n in SHARDED:
        parts = _scatter_grad(n, dfull[n])
        shp = w[n].shape
        two_d = lambda a: a.reshape(shp[0] * shp[1], shp[2])
        res = adamw(two_d(w[n]), parts, two_d(m[n]), two_d(v[n]), "adamw_" + n)
        grads[n], deltas[n], new_m[n], new_v[n] = (r.reshape(shp) for r in res)

    parts = all_gather(_flatten_small(dfull), "ag_small_grads")
    res = adamw(_flatten_small(w), parts, _flatten_small(m), _flatten_small(v), "adamw_small")
    for dst, r in zip((grads, deltas, new_m, new_v), res):
        dst.update(_split_small(r, w))

    return (loss, dx[None], *[grads[n] for n in WEIGHTS], *[deltas[n] for n in WEIGHTS],
            *[new_m[n] for n in WEIGHTS], *[new_v[n] for n in WEIGHTS])
```

```python
import functools
import math

import numpy as np
import jax
import jax.numpy as jnp
from jax import lax
from jax.experimental import pallas as pl
from jax.experimental.pallas import tpu as pltpu

F32 = jnp.float32
BF16 = jnp.bfloat16

DEPTH = 4
N_MIXERS = 4
S5_CH_PER_GROUP = 16
S5_STATE = 64
POOL_WINDOWS = (2, 4, 8, 16)
CONV_WIDTH = 31
ATT_HEADS = 8
DILATED_PATTERNS = ((128, 1), (512, 4), (2048, 16))
ATT_BLOCK = 128
REL_BUCKETS = 32
REL_MAX_DIST = 2048
X_HEADS = 4
X_HEAD_DIM = 128
NORM_EPS = 1e-6
NEG_INF = -1e30
ADAM_LR = 0.001
ADAM_B1 = 0.9
ADAM_B2 = 0.999
ADAM_EPS = 1e-08
ADAM_WD = 0.01
ADAM_STEP = 10

LANES = 128
SUBLANES = 8
VMEM_BYTES = 64 * 1024 * 1024
VMEM_LIMIT = (VMEM_BYTES * 3) // 4
VMEM_LIMIT_BIG = (VMEM_BYTES * 7) // 8
STREAM_BLOCK_BYTES = 1024 * 1024
SMALL_ROW_TILE = 512
N_DEV = 8
N_CHIP = 4
MESH_AXES = ("x", "y", "c")

WEIGHTS = ['rel_bias', 'mem_norm_g', 'norm_mix_g', 'w_in', 's5_lam_re', 's5_lam_im', 's5_log_dt', 's5_b_re',
           's5_b_im', 's5_c_re', 's5_c_im', 's5_d', 's5_w_glu', 'pool_w', 'pool_scale', 'conv_w_dw', 'conv_b_dw',
           'conv_ln_g', 'conv_ln_b', 'conv_w_pw', 'grp_norm_g', 'w_out', 'norm_x_g', 'w_xq', 'w_xk', 'w_xv', 'w_xo',
           'norm_mlp_g', 'w_up', 'w_down', 'norm_final_g']
SHARDED = {'w_in': 2, 's5_w_glu': 1, 'conv_w_dw': 2, 'conv_w_pw': 1, 'w_out': 1, 'w_xq': 1, 'w_xk': 1, 'w_xv': 1,
           'w_xo': 2, 'w_up': 2, 'w_down': 1}
GATHER_BF16 = ('w_in', 'w_out', 'w_xq', 'w_xk', 'w_xv', 'w_xo', 'w_up', 'w_down')
SMALL = [n for n in WEIGHTS if n not in SHARDED]


def _cp(n_axes, vmem=VMEM_LIMIT):
    return pltpu.CompilerParams(dimension_semantics=("arbitrary",) * n_axes, vmem_limit_bytes=vmem)


def _tile(n, prefs):
    for t in prefs:
        if n % t == 0:
            return t
    return n


def _mm(a, b, *, ta=False, tb=False, res=None, out_dtype=F32, name):
    if ta:
        K, M = a.shape
    else:
        M, K = a.shape
    if tb:
        N, K2 = b.shape
    else:
        K2, N = b.shape
    assert K == K2, (a.shape, b.shape, ta, tb)
    tm = _tile(M, (1024, 512, 256, 128))
    tn = _tile(N, (1024, 512, 256, 128))
    tk = _tile(K, (512, 256, 128))
    nk = K // tk
    a_spec = pl.BlockSpec((tk, tm), lambda i, j, k: (k, i)) if ta else pl.BlockSpec((tm, tk), lambda i, j, k: (i, k))
    b_spec = pl.BlockSpec((tn, tk), lambda i, j, k: (j, k)) if tb else pl.BlockSpec((tk, tn), lambda i, j, k: (k, j))
    o_spec = pl.BlockSpec((tm, tn), lambda i, j, k: (i, j))
    dn = (((0 if ta else 1,), (1 if tb else 0,)), ((), ()))
    has_res = res is not None

    def body(*refs):
        if has_res:
            a_ref, b_ref, r_ref, o_ref, acc = refs
        else:
            a_ref, b_ref, o_ref, acc = refs
        k = pl.program_id(2)

        @pl.when(k == 0)
        def _():
            acc[...] = jnp.zeros_like(acc)

        acc[...] += lax.dot_general(a_ref[...].astype(BF16), b_ref[...].astype(BF16), dn,
                                    preferred_element_type=F32)

        @pl.when(k == nk - 1)
        def _():
            r = acc[...]
            if has_res:
                r = r + r_ref[...]
            o_ref[...] = r.astype(out_dtype)

    in_specs = [a_spec, b_spec] + ([o_spec] if has_res else [])
    args = (a, b) + ((res,) if has_res else ())
    return pl.pallas_call(
        body, grid=(M // tm, N // tn, nk), in_specs=in_specs, out_specs=o_spec,
        out_shape=jax.ShapeDtypeStruct((M, N), out_dtype),
        scratch_shapes=[pltpu.VMEM((tm, tn), F32)], name=name, compiler_params=_cp(3))(*args)


def _linear(name):
    @jax.custom_vjp
    def lin(a, w):
        return _mm(a, w, name=name + "_fwd")

    def fwd(a, w):
        return _mm(a, w, name=name + "_fwd"), (a, w)

    def bwd(r, dy):
        a, w = r
        da = _mm(dy, w, tb=True, name=name + "_dx")
        dw = _mm(a, dy, ta=True, out_dtype=w.dtype, name=name + "_dw")
        return da, dw

    lin.defvjp(fwd, bwd)
    return lin


def _linear_res(name):
    @jax.custom_vjp
    def lin(a, w, res):
        return _mm(a, w, res=res, name=name + "_fwd")

    def fwd(a, w, res):
        return _mm(a, w, res=res, name=name + "_fwd"), (a, w)

    def bwd(r, dy):
        a, w = r
        da = _mm(dy, w, tb=True, name=name + "_dx")
        dw = _mm(a, dy, ta=True, out_dtype=w.dtype, name=name + "_dw")
        return da, dw, dy

    lin.defvjp(fwd, bwd)
    return lin


def _block_op(name, f, grid, ins, outs, vmem=VMEM_LIMIT):
    n_in, n_out = len(ins), len(outs)
    in_specs = [pl.BlockSpec(bs, im) for bs, im, _, _ in ins]
    out_specs = [pl.BlockSpec(bs, im) for _, _, bs, im in outs]
    out_shape = [jax.ShapeDtypeStruct(s, d) for s, d, _, _ in outs]
    didx = [i for i in range(n_in) if ins[i][3]]

    def fwd_call(*args):
        def body(*refs):
            res = f(*[r[...] for r in refs[:n_in]])
            for r, o in zip(refs[n_in:], res):
                r[...] = o

        return pl.pallas_call(body, grid=grid, in_specs=in_specs, out_specs=out_specs, out_shape=out_shape,
                              name=name + "_fwd", compiler_params=_cp(len(grid), vmem))(*args)

    def bwd_call(args, cts):
        def body(*refs):
            vals = [r[...] for r in refs[:n_in]]
            ct_refs = refs[n_in:n_in + n_out]
            g_refs = refs[n_in + n_out:]

            def fd(*dv):
                full = list(vals)
                for i, v in zip(didx, dv):
                    full[i] = v
                return f(*full)

            _, vjp = jax.vjp(fd, *[vals[i] for i in didx])
            grads = vjp(tuple(r[...] for r in ct_refs))
            for gref, i, g in zip(g_refs, didx, grads):
                acc = ins[i][2]
                if acc:
                    first = functools.reduce(jnp.logical_and, [pl.program_id(ax) == 0 for ax in acc])

                    @pl.when(first)
                    def _(gref=gref):
                        gref[...] = jnp.zeros_like(gref)

                    gref[...] += g.astype(gref.dtype)
                else:
                    gref[...] = g.astype(gref.dtype)

        g_specs = [pl.BlockSpec(ins[i][0], ins[i][1]) for i in didx]
        g_shape = [jax.ShapeDtypeStruct(args[i].shape, args[i].dtype) for i in didx]
        return pl.pallas_call(body, grid=grid, in_specs=in_specs + out_specs, out_specs=g_specs, out_shape=g_shape,
                              name=name + "_bwd", compiler_params=_cp(len(grid), vmem))(*args, *cts)

    @jax.custom_vjp
    def op(*args):
        return tuple(fwd_call(*args))

    def op_fwd(*args):
        return tuple(fwd_call(*args)), args

    def op_bwd(args, cts):
        it = iter(bwd_call(args, cts))
        return tuple(next(it) if ins[i][3] else jnp.zeros_like(args[i]) for i in range(n_in))

    op.defvjp(op_fwd, op_bwd)
    return op


def _row(tr, c):
    return ((tr, c), lambda i: (i, 0), None, True)


def _par(shape):
    nd = len(shape)
    return (shape, lambda i: (0,) * nd, (0,), True)


def _bdot(a, w):
    return jnp.dot(a.astype(BF16), w.astype(BF16), preferred_element_type=F32)


def _rms_f(x, g):
    return (x * lax.rsqrt(jnp.mean(x * x, axis=-1, keepdims=True) + NORM_EPS) * g,)


def rmsnorm(x, g, name):
    R, D = x.shape
    tr = _tile(R, (256,))
    op = _block_op(name, _rms_f, (R // tr,), [_row(tr, D), _par((1, D))], [((R, D), F32, (tr, D), lambda i: (i, 0))])
    return op(x, g.reshape(1, D))[0]


def s5_epilogue(yc, u, d, w_glu, name):
    R, C = yc.shape
    tr = _tile(R, (256,))

    def f(yc, u, d, w):
        g = jax.nn.gelu(yc + d * u)
        return (g * jax.nn.sigmoid(_bdot(g, w)),)

    op = _block_op(name, f, (R // tr,), [_row(tr, C), _row(tr, C), _par((1, C)), _par((C, C))],
                   [((R, C), F32, (tr, C), lambda i: (i, 0))])
    return op(yc, u, d.reshape(1, C), w_glu)[0]


def pool_proj(p, w, scale, name):
    R, C = p.shape
    ng, pc, _ = w.shape
    tr = _tile(R, (256,))

    def f(p, w, s):
        ys = [_bdot(p[:, g * pc:(g + 1) * pc], w[g]) for g in range(ng)]
        return (jnp.concatenate(ys, axis=-1) * s,)

    op = _block_op(name, f, (R // tr,), [_row(tr, C), _par((ng, pc, pc)), _par((1, C))],
                   [((R, C), F32, (tr, C), lambda i: (i, 0))])
    return op(p, w, scale.reshape(1, C))[0]


def conv_post(h, ln_g, ln_b, w_pw, name):
    R, C = h.shape
    tr = _tile(R, (256,))

    def f(h, g, b, w):
        hc = h - jnp.mean(h, axis=-1, keepdims=True)
        y = hc * lax.rsqrt(jnp.mean(hc * hc, axis=-1, keepdims=True) + NORM_EPS) * g + b
        return (_bdot(jax.nn.silu(y), w),)

    op = _block_op(name, f, (R // tr,), [_row(tr, C), _par((1, C)), _par((1, C)), _par((C, C))],
                   [((R, C), F32, (tr, C), lambda i: (i, 0))])
    return op(h, ln_g.reshape(1, C), ln_b.reshape(1, C), w_pw)[0]


def group_norm(ys, g, name):
    R, C = ys[0].shape
    n = len(ys)
    tr = _tile(R, (256,))

    def f(*a):
        g = a[n]
        parts = [y * lax.rsqrt(jnp.mean(y * y, axis=-1, keepdims=True) + NORM_EPS) for y in a[:n]]
        return (jnp.concatenate(parts, axis=-1) * g,)

    op = _block_op(name, f, (R // tr,), [_row(tr, C)] * n + [_par((1, n * C))],
                   [((R, n * C), F32, (tr, n * C), lambda i: (i, 0))])
    return op(*ys, g.reshape(1, n * C))[0]


def relu_sq(a, name):
    R, C = a.shape
    tr = _tile(R, (128,))
    op = _block_op(name, lambda a: (jnp.square(jnp.maximum(a, 0.0)),), (R // tr,), [_row(tr, C)],
                   [((R, C), F32, (tr, C), lambda i: (i, 0))])
    return op(a)[0]


def att_combine(os_, ls, name):
    R, C = os_[0].shape
    n = len(os_)
    tr = _tile(R, (256,))

    def f(*a):
        o, l = a[:n], a[n:]
        m = functools.reduce(jnp.maximum, l)
        e = [jnp.exp(li - m) for li in l]
        return (sum(ei * oi for ei, oi in zip(e, o)) / sum(e),)

    op = _block_op(name, f, (R // tr,), [_row(tr, C)] * (2 * n), [((R, C), F32, (tr, C), lambda i: (i, 0))])
    return op(*os_, *ls)[0]


def cross_attention(q, k, v, name):
    L, W = q.shape
    M = k.shape[0]
    E = X_HEAD_DIM
    tq = _tile(L, (512,))

    def f(q, k, v):
        s = lax.dot_general(q.astype(BF16), k.astype(BF16), (((1,), (1,)), ((), ())),
                            preferred_element_type=F32) * (E ** -0.5)
        p = jax.nn.softmax(s, axis=-1)
        return (_bdot(p, v),)

    qspec = ((tq, E), lambda h, i: (i, h), None, True)
    kspec = ((M, E), lambda h, i: (0, h), (1,), True)
    op = _block_op(name, f, (W // E, L // tq), [qspec, kspec, kspec], [((L, W), F32, (tq, E), lambda h, i: (i, h))])
    return op(q, k, v)[0]


def s5_discretise(lam_re, lam_im, log_dt, b_re_t, b_im_t, name):
    G, _, N = lam_re.shape
    C = b_re_t.shape[1]

    def f(lr, li, ldt, br, bi):
        dt = jnp.exp(ldt)
        mag = jnp.exp(lr * dt)
        ab_r, ab_i = mag * jnp.cos(li * dt), mag * jnp.sin(li * dt)
        den = lr * lr + li * li
        nr, ni = ab_r - 1.0, ab_i
        f_r = (nr * lr + ni * li) / den
        f_i = (ni * lr - nr * li) / den
        return ab_r, ab_i, f_r * br - f_i * bi, f_r * bi + f_i * br

    vec = ((G, 1, N), lambda i: (0, 0, 0), None, True)
    mat = ((G, C, N), lambda i: (0, 0, 0), None, True)
    ov = ((G, 1, N), F32, (G, 1, N), lambda i: (0, 0, 0))
    om = ((G, C, N), F32, (G, C, N), lambda i: (0, 0, 0))
    op = _block_op(name, f, (1,), [vec, vec, vec, mat, mat], [ov, ov, om, om])
    return op(lam_re, lam_im, log_dt, b_re_t, b_im_t)


def rel_bias_tables(rel_bias, onehot, name):
    B, H = rel_bias.shape
    P, _, Q = onehot.shape

    def f(rbt, oh):
        return (jnp.dot(rbt, oh, precision=lax.Precision.HIGHEST, preferred_element_type=F32),)

    op = _block_op(name, f, (P,), [((H, B), lambda p: (0, 0), (0,), True), ((None, B, Q), lambda p: (p, 0, 0), None, False)],
                   [((P, H, Q), F32, (None, H, Q), lambda p: (p, 0, 0))])
    return op(rel_bias.T, onehot)[0]


def _shift_down(x, s, row):
    return jnp.where(row >= s, pltpu.roll(x, s, 0), 0.0)


def _shift_up(x, s, row):
    n = x.shape[0]
    return jnp.where(row < n - s, pltpu.roll(x, n - s, 0), 0.0)


def _window_sum(x, w, row, shift):
    span = 1
    while span < w:
        x = x + shift(x, span, row)
        span *= 2
    return x


def _pool_call(u, d_out, name):
    L, C = u.shape
    pc = C // len(POOL_WINDOWS)
    assert pc % LANES == 0

    def body(x_ref, o_ref):
        row = lax.broadcasted_iota(jnp.int32, (L, pc), 0)
        for g, w in enumerate(POOL_WINDOWS):
            sl = slice(g * pc, (g + 1) * pc)
            x = x_ref[:, sl]
            cnt = jnp.minimum(row + 1, w).astype(F32)
            if d_out is None:
                o_ref[:, sl] = _window_sum(x, w, row, _shift_down) / cnt - x
            else:
                o_ref[:, sl] = _window_sum(x / cnt, w, row, _shift_up) - x

    src = u if d_out is None else d_out
    return pl.pallas_call(body, out_shape=jax.ShapeDtypeStruct((L, C), F32), name=name,
                          compiler_params=pltpu.CompilerParams(vmem_limit_bytes=VMEM_LIMIT))(src)


def _pool_mix(name):
    @jax.custom_vjp
    def op(u):
        return _pool_call(u, None, name + "_fwd")

    def fwd(u):
        return _pool_call(u, None, name + "_fwd"), u

    def bwd(u, dp):
        return (_pool_call(u, dp, name + "_bwd"),)

    op.defvjp(fwd, bwd)
    return op


def _conv_fwd(u, w, b, name):
    L, C2 = u.shape
    C = C2 // 2
    K = w.shape[0]
    nb = C // LANES

    def body(val_ref, gate_ref, w_ref, b_ref, o_ref):
        row = lax.broadcasted_iota(jnp.int32, (L, LANES), 0)
        h = val_ref[...] * jax.nn.sigmoid(gate_ref[...])
        acc = jnp.broadcast_to(b_ref[...], (L, LANES))
        for k in range(K):
            acc = acc + w_ref[k:k + 1, :] * _shift_down(h, K - 1 - k, row)
        o_ref[...] = acc

    blk = lambda off: pl.BlockSpec((L, LANES), lambda j: (0, j + off))
    return pl.pallas_call(
        body, grid=(nb,), in_specs=[blk(0), blk(nb), pl.BlockSpec((K, LANES), lambda j: (0, j)),
                                    pl.BlockSpec((1, LANES), lambda j: (0, j))],
        out_specs=blk(0), out_shape=jax.ShapeDtypeStruct((L, C), F32), name=name, compiler_params=_cp(1))(u, u, w, b)


def _conv_bwd(u, w, dh, name):
    L, C2 = u.shape
    C = C2 // 2
    K = w.shape[0]
    nb = C // LANES

    def body(val_ref, gate_ref, w_ref, dh_ref, dval_ref, dgate_ref, dw_ref, db_ref):
        row = lax.broadcasted_iota(jnp.int32, (L, LANES), 0)
        val = val_ref[...]
        sig = jax.nn.sigmoid(gate_ref[...])
        h = val * sig
        d = dh_ref[...]
        dh0 = jnp.zeros((L, LANES), F32)
        for k in range(K):
            s = K - 1 - k
            dh0 = dh0 + w_ref[k:k + 1, :] * _shift_up(d, s, row)
            dw_ref[k:k + 1, :] = jnp.sum(d * _shift_down(h, s, row), axis=0, keepdims=True)
        db_ref[...] = jnp.sum(d, axis=0, keepdims=True)
        dval_ref[...] = dh0 * sig
        dgate_ref[...] = dh0 * val * sig * (1.0 - sig)

    blk = lambda off: pl.BlockSpec((L, LANES), lambda j: (0, j + off))
    return pl.pallas_call(
        body, grid=(nb,), in_specs=[blk(0), blk(nb), pl.BlockSpec((K, LANES), lambda j: (0, j)), blk(0)],
        out_specs=[blk(0), blk(0), pl.BlockSpec((K, LANES), lambda j: (0, j)), pl.BlockSpec((1, LANES), lambda j: (0, j))],
        out_shape=[jax.ShapeDtypeStruct((L, C), F32), jax.ShapeDtypeStruct((L, C), F32),
                   jax.ShapeDtypeStruct((K, C), F32), jax.ShapeDtypeStruct((1, C), F32)],
        name=name, compiler_params=_cp(1))(u, u, w, dh)


def _glu_conv(name):
    @jax.custom_vjp
    def op(u, w, b):
        return _conv_fwd(u, w, b, name + "_fwd")

    def fwd(u, w, b):
        return _conv_fwd(u, w, b, name + "_fwd"), (u, w)

    def bwd(r, dh):
        u, w = r
        dval, dgate, dw, db = _conv_bwd(u, w, dh, name + "_bwd")
        return jnp.concatenate([dval, dgate], axis=-1), dw, db

    op.defvjp(fwd, bwd)
    return op


S5_BLOCK_CH = LANES
S5_BLOCK_ST = S5_BLOCK_CH // S5_CH_PER_GROUP * S5_STATE


def _s5_scan(br_ref, bi_ref, ar, ai, reverse):
    L, C = br_ref.shape
    T = SUBLANES
    row = lax.broadcasted_iota(jnp.int32, (T, C), 0)
    pw = [(ar, ai)]
    for _ in range(T - 1):
        pr, pi = pw[-1]
        pw.append((pr * ar - pi * ai, pr * ai + pi * ar))
    cr = jnp.zeros((T, C), F32)
    ci = jnp.zeros((T, C), F32)
    for r in range(T):
        e = (T - r) if reverse else (r + 1)
        cr = jnp.where(row == r, pw[e - 1][0], cr)
        ci = jnp.where(row == r, pw[e - 1][1], ci)
    steps = []
    s = 1
    while s < T:
        mask = (row < T - s) if reverse else (row >= s)
        steps.append((T - s if reverse else s, mask, pw[s - 1][0], pw[s - 1][1]))
        s *= 2
    nt = L // T
    last = 0 if reverse else T - 1

    def body(i, carry):
        kr, ki = carry
        t = (nt - 1 - i) if reverse else i
        off = pl.multiple_of(t * T, T)
        xr = br_ref[pl.ds(off, T), :]
        xi = bi_ref[pl.ds(off, T), :]
        for sh, mask, mr, mi in steps:
            sr = jnp.where(mask, pltpu.roll(xr, sh, 0), 0.0)
            si = jnp.where(mask, pltpu.roll(xi, sh, 0), 0.0)
            xr, xi = xr + mr * sr - mi * si, xi + mr * si + mi * sr
        xr, xi = xr + cr * kr - ci * ki, xi + cr * ki + ci * kr
        br_ref[pl.ds(off, T), :] = xr
        bi_ref[pl.ds(off, T), :] = xi
        return (jnp.broadcast_to(xr[last:last + 1, :], (T, C)), jnp.broadcast_to(xi[last:last + 1, :], (T, C)))

    z = jnp.zeros((T, C), F32)
    lax.fori_loop(0, nt, body, (z, z))


def _s5_specs(L):
    nb_axis = lambda j: (j, 0, 0)
    u = pl.BlockSpec((L, S5_BLOCK_CH), lambda j: (0, j))
    wb = pl.BlockSpec((None, S5_BLOCK_CH, S5_BLOCK_ST), nb_axis)
    a = pl.BlockSpec((1, S5_BLOCK_ST), lambda j: (0, j))
    wc = pl.BlockSpec((None, S5_BLOCK_ST, S5_BLOCK_CH), nb_axis)
    return u, wb, a, wc


def _s5_fwd(u, wbr, wbi, ar, ai, wcr, wci, name):
    L, C = u.shape
    nb = C // S5_BLOCK_CH
    us, wbs, as_, wcs = _s5_specs(L)

    def body(u_ref, wbr_ref, wbi_ref, ar_ref, ai_ref, wcr_ref, wci_ref, y_ref, xr, xi):
        ub = u_ref[...]
        xr[...] = _bdot(ub, wbr_ref[...])
        xi[...] = _bdot(ub, wbi_ref[...])
        _s5_scan(xr, xi, ar_ref[...], ai_ref[...], False)
        y_ref[...] = _bdot(xr[...], wcr_ref[...]) - _bdot(xi[...], wci_ref[...])

    return pl.pallas_call(
        body, grid=(nb,), in_specs=[us, wbs, wbs, as_, as_, wcs, wcs], out_specs=us,
        out_shape=jax.ShapeDtypeStruct((L, C), F32),
        scratch_shapes=[pltpu.VMEM((L, S5_BLOCK_ST), F32)] * 2, name=name, compiler_params=_cp(1))(
            u, wbr, wbi, ar, ai, wcr, wci)


def _dot_t(a, b):
    return lax.dot_general(a.astype(BF16), b.astype(BF16), (((0,), (0,)), ((), ())), preferred_element_type=F32)


def _dot_nt(a, b):
    return lax.dot_general(a.astype(BF16), b.astype(BF16), (((1,), (1,)), ((), ())), preferred_element_type=F32)


def _s5_bwd(u, wbr, wbi, ar, ai, wcr, wci, dy, name):
    L, C = u.shape
    nb = C // S5_BLOCK_CH
    us, wbs, as_, wcs = _s5_specs(L)
    T = SUBLANES

    def body(u_ref, wbr_ref, wbi_ref, ar_ref, ai_ref, wcr_ref, wci_ref, dy_ref,
             du_ref, dwbr_ref, dwbi_ref, dar_ref, dai_ref, dwcr_ref, dwci_ref, xr, xi, gr, gi):
        ub = u_ref[...]
        a_r, a_i = ar_ref[...], ai_ref[...]
        xr[...] = _bdot(ub, wbr_ref[...])
        xi[...] = _bdot(ub, wbi_ref[...])
        _s5_scan(xr, xi, a_r, a_i, False)
        d = dy_ref[...]
        dwcr_ref[...] = _dot_t(xr[...], d)
        dwci_ref[...] = -_dot_t(xi[...], d)
        gr[...] = _dot_nt(d, wcr_ref[...])
        gi[...] = -_dot_nt(d, wci_ref[...])
        _s5_scan(gr, gi, a_r, -a_i, True)

        row = lax.broadcasted_iota(jnp.int32, (T, S5_BLOCK_ST), 0)

        def da_body(i, carry):
            pr, pi, sr, si = carry
            off = pl.multiple_of(i * T, T)
            xr_t, xi_t = xr[pl.ds(off, T), :], xi[pl.ds(off, T), :]
            lr_t, li_t = gr[pl.ds(off, T), :], gi[pl.ds(off, T), :]
            qr = jnp.where(row == 0, pr, pltpu.roll(xr_t, 1, 0))
            qi = jnp.where(row == 0, pi, pltpu.roll(xi_t, 1, 0))
            sr = sr + qr * lr_t + qi * li_t
            si = si + qr * li_t - qi * lr_t
            return (jnp.broadcast_to(xr_t[T - 1:T, :], (T, S5_BLOCK_ST)),
                    jnp.broadcast_to(xi_t[T - 1:T, :], (T, S5_BLOCK_ST)), sr, si)

        z = jnp.zeros((T, S5_BLOCK_ST), F32)
        _, _, sr, si = lax.fori_loop(0, L // T, da_body, (z, z, z, z))
        dar_ref[...] = jnp.sum(sr, axis=0, keepdims=True)
        dai_ref[...] = jnp.sum(si, axis=0, keepdims=True)
        lr, li = gr[...], gi[...]
        dwbr_ref[...] = _dot_t(ub, lr)
        dwbi_ref[...] = _dot_t(ub, li)
        du_ref[...] = _dot_nt(lr, wbr_ref[...]) + _dot_nt(li, wbi_ref[...])

    sds = jax.ShapeDtypeStruct
    return pl.pallas_call(
        body, grid=(nb,), in_specs=[us, wbs, wbs, as_, as_, wcs, wcs, us],
        out_specs=[us, wbs, wbs, as_, as_, wcs, wcs],
        out_shape=[sds(u.shape, F32), sds(wbr.shape, F32), sds(wbi.shape, F32), sds(ar.shape, F32),
                   sds(ai.shape, F32), sds(wcr.shape, F32), sds(wci.shape, F32)],
        scratch_shapes=[pltpu.VMEM((L, S5_BLOCK_ST), F32)] * 4, name=name,
        compiler_params=_cp(1, VMEM_LIMIT_BIG))(u, wbr, wbi, ar, ai, wcr, wci, dy)


def _s5_core(name):
    @jax.custom_vjp
    def op(u, wbr, wbi, ar, ai, wcr, wci):
        return _s5_fwd(u, wbr, wbi, ar, ai, wcr, wci, name + "_fwd")

    def fwd(*a):
        return _s5_fwd(*a, name + "_fwd"), a

    def bwd(a, dy):
        return tuple(_s5_bwd(*a, dy, name + "_bwd"))

    op.defvjp(fwd, bwd)
    return op


def _band_attn_f(first, q, kp, kc, vp, vc, bias):
    H, nq, E = q.shape
    bmm = lambda a, b, ca, cb: lax.dot_general(a.astype(BF16), b.astype(BF16), (((ca,), (cb,)), ((0,), (0,))),
                                               preferred_element_type=F32)
    k = jnp.concatenate([kp, kc], axis=1)
    v = jnp.concatenate([vp, vc], axis=1)
    s = bmm(q, k, 2, 2) * (E ** -0.5) + bias
    r = lax.broadcasted_iota(jnp.int32, (nq, 2 * nq), 0)
    c = lax.broadcasted_iota(jnp.int32, (nq, 2 * nq), 1)
    prev_ok = jnp.logical_and(jnp.logical_and(c < nq, c >= r), jnp.logical_not(first))
    valid = jnp.logical_or(prev_ok, jnp.logical_and(c >= nq, c - nq <= r))
    s = jnp.where(valid[None], s, NEG_INF)
    m = jnp.max(s, axis=-1, keepdims=True)
    p = jnp.exp(s - m)
    den = jnp.sum(p, axis=-1, keepdims=True)
    o = bmm(p, v, 2, 1) / den
    return o, jnp.broadcast_to(m + jnp.log(den), (H, nq, E))


def _band_specs(H, Ls, E):
    blk = lambda f: pl.BlockSpec((H, None, ATT_BLOCK, E), f)
    q = blk(lambda r, n: (0, r, n, 0))
    kprev = blk(lambda r, n: (0, r, jnp.maximum(n - 1, 0), 0))
    bias = pl.BlockSpec((H, ATT_BLOCK, 2 * ATT_BLOCK), lambda r, n: (0, 0, 0))
    whole = pl.BlockSpec((H, None, Ls, E), lambda r, n: (0, r, 0, 0))
    return q, kprev, bias, whole


def _band_fwd(q, k, v, bias, name):
    H, d, Ls, E = q.shape
    nb = Ls // ATT_BLOCK
    qs, kps, bs, _ = _band_specs(H, Ls, E)

    def body(q_ref, kp_ref, kc_ref, vp_ref, vc_ref, b_ref, o_ref, l_ref):
        o, l = _band_attn_f(pl.program_id(1) == 0, q_ref[...], kp_ref[...], kc_ref[...], vp_ref[...], vc_ref[...],
                            b_ref[...])
        o_ref[...] = o
        l_ref[...] = l

    sds = jax.ShapeDtypeStruct(q.shape, F32)
    return pl.pallas_call(body, grid=(d, nb), in_specs=[qs, kps, qs, kps, qs, bs], out_specs=[qs, qs],
                          out_shape=[sds, sds], name=name, compiler_params=_cp(2))(q, k, k, v, v, bias)


def _band_bwd(q, k, v, bias, do, dl, name):
    H, d, Ls, E = q.shape
    nb = Ls // ATT_BLOCK
    qs, kps, bs, whole = _band_specs(H, Ls, E)

    def body(q_ref, kp_ref, kc_ref, vp_ref, vc_ref, b_ref, do_ref, dl_ref, dq_ref, dk_ref, dv_ref, db_ref):
        r, n = pl.program_id(0), pl.program_id(1)
        first = n == 0
        _, vjp = jax.vjp(functools.partial(_band_attn_f, first), q_ref[...], kp_ref[...], kc_ref[...], vp_ref[...],
                         vc_ref[...], b_ref[...])
        dq, dkp, dkc, dvp, dvc, db = vjp((do_ref[...], dl_ref[...]))

        @pl.when(first)
        def _():
            dk_ref[...] = jnp.zeros_like(dk_ref)
            dv_ref[...] = jnp.zeros_like(dv_ref)

        @pl.when(jnp.logical_and(first, r == 0))
        def _():
            db_ref[...] = jnp.zeros_like(db_ref)

        dq_ref[...] = dq
        db_ref[...] += db
        cur = pl.multiple_of(n * ATT_BLOCK, ATT_BLOCK)
        dk_ref[:, pl.ds(cur, ATT_BLOCK), :] += dkc
        dv_ref[:, pl.ds(cur, ATT_BLOCK), :] += dvc

        @pl.when(jnp.logical_not(first))
        def _():
            prev = pl.multiple_of((n - 1) * ATT_BLOCK, ATT_BLOCK)
            dk_ref[:, pl.ds(prev, ATT_BLOCK), :] += dkp
            dv_ref[:, pl.ds(prev, ATT_BLOCK), :] += dvp

    sds = jax.ShapeDtypeStruct(q.shape, F32)
    return pl.pallas_call(
        body, grid=(d, nb), in_specs=[qs, kps, qs, kps, qs, bs, qs, qs], out_specs=[qs, whole, whole, bs],
        out_shape=[sds, sds, sds, jax.ShapeDtypeStruct(bias.shape, F32)], name=name, compiler_params=_cp(2))(
            q, k, k, v, v, bias, do, dl)


def _band_attention(name):
    @jax.custom_vjp
    def op(q, k, v, bias):
        return tuple(_band_fwd(q, k, v, bias, name + "_fwd"))

    def fwd(q, k, v, bias):
        return tuple(_band_fwd(q, k, v, bias, name + "_fwd")), (q, k, v, bias)

    def bwd(r, ct):
        return tuple(_band_bwd(*r, ct[0], ct[1], name + "_bwd"))

    op.defvjp(fwd, bwd)
    return op


def _t5_bucket(dist):
    n = np.maximum(dist, 0)
    max_exact = REL_BUCKETS // 2
    large = max_exact + (np.log(np.maximum(n, 1) / max_exact) / np.log(REL_MAX_DIST / max_exact)
                         * (REL_BUCKETS - max_exact)).astype(np.int64)
    large = np.minimum(large, REL_BUCKETS - 1)
    return np.where(n < max_exact, n, large).astype(np.int32)


def _bucket_onehot():
    a = np.arange(ATT_BLOCK)[:, None]
    b = np.arange(2 * ATT_BLOCK)[None, :]
    sub = a + ATT_BLOCK - b
    bucket = jnp.asarray(np.stack([_t5_bucket(sub * dil).reshape(-1) for _, dil in DILATED_PATTERNS]))
    ids = jnp.arange(REL_BUCKETS, dtype=jnp.int32)
    return (bucket[:, None, :] == ids[None, :, None]).astype(F32)


def loss_head(h, target, g, name):
    R, D = h.shape
    tr = _tile(R, (256,))

    def body(h_ref, t_ref, g_ref, l_ref, dh_ref, dg_ref):
        def lf(hv, gv):
            y = _rms_f(hv, gv)[0]
            return 0.5 * jnp.sum(jnp.mean(jnp.square(y - t_ref[...]), axis=-1))

        l, (dh, dg) = jax.value_and_grad(lf, argnums=(0, 1))(h_ref[...], g_ref[...])

        @pl.when(pl.program_id(0) == 0)
        def _():
            l_ref[...] = jnp.zeros_like(l_ref)
            dg_ref[...] = jnp.zeros_like(dg_ref)

        dh_ref[...] = dh
        dg_ref[...] += dg
        l_ref[...] += l

    rows = pl.BlockSpec((tr, D), lambda i: (i, 0))
    vec = pl.BlockSpec((1, D), lambda i: (0, 0))
    l, dh, dg = pl.pallas_call(
        body, grid=(R // tr,), in_specs=[rows, rows, vec],
        out_specs=[pl.BlockSpec((SUBLANES, LANES), lambda i: (0, 0)), rows, vec],
        out_shape=[jax.ShapeDtypeStruct((SUBLANES, LANES), F32), jax.ShapeDtypeStruct((R, D), F32),
                   jax.ShapeDtypeStruct((1, D), F32)], name=name, compiler_params=_cp(1))(h, target, g.reshape(1, D))
    return l[0, 0], dh, dg.reshape(D)


def adamw(w, parts, m, v, name):
    R, C = w.shape
    n_parts = parts.shape[0]
    tr = _row_tile(R, C)
    c1 = 1.0 - ADAM_B1 ** ADAM_STEP
    c2 = 1.0 - ADAM_B2 ** ADAM_STEP

    def body(w_ref, p_ref, m_ref, v_ref, g_ref, d_ref, nm_ref, nv_ref):
        g = p_ref[0].astype(F32)
        for i in range(1, n_parts):
            g = g + p_ref[i].astype(F32)
        nm = ADAM_B1 * m_ref[...] + (1.0 - ADAM_B1) * g
        nv = ADAM_B2 * v_ref[...] + (1.0 - ADAM_B2) * jnp.square(g)
        d_ref[...] = -ADAM_LR * ((nm / c1) / (jnp.sqrt(nv / c2) + ADAM_EPS) + ADAM_WD * w_ref[...])
        g_ref[...] = g
        nm_ref[...] = nm
        nv_ref[...] = nv

    rows = pl.BlockSpec((tr, C), lambda i: (i, 0))
    sds = jax.ShapeDtypeStruct((R, C), F32)
    return pl.pallas_call(body, grid=(R // tr,),
                          in_specs=[rows, pl.BlockSpec((n_parts, tr, C), lambda i: (0, i, 0)), rows, rows],
                          out_specs=[rows] * 4, out_shape=[sds] * 4, name=name, compiler_params=_cp(1))(w, parts, m, v)


HBM_SPEC = pl.BlockSpec(memory_space=pltpu.HBM)
MESH_ID = pl.DeviceIdType.MESH


def _place():
    return lax.axis_index("x"), lax.axis_index("y"), lax.axis_index("c")


def _index(x, y, c):
    return 4 * x + 2 * y + c


def all_gather(x, name):
    def body(x_ref, out_ref, send_sems, recv_sems, local_sem):
        x_, y_, c_ = _place()
        me, sibling = (x_, y_, c_), (x_, y_, 1 - c_)
        chips = [(1 - x_, y_), (x_, 1 - y_), (1 - x_, 1 - y_)]

        def slot(px, py, pc):
            return out_ref.at[_index(px, py, pc)]

        def copy(k, block, to, src=None):
            return pltpu.make_async_remote_copy(
                src_ref=slot(*block) if src is None else src, dst_ref=slot(*block),
                send_sem=send_sems.at[k], recv_sem=recv_sems.at[k], device_id=to, device_id_type=MESH_ID)

        mine = pltpu.make_async_copy(x_ref, slot(*me), local_sem)
        mine.start()
        first = [copy(0, me, sibling, src=x_ref)]
        first += [copy(1 + j, me, (*chip, c_), src=x_ref) for j, chip in enumerate(chips)]
        for cp in first:
            cp.start()
        passed = [copy(4 + j, (*chip, c_), sibling) for j, chip in enumerate(chips)]
        for j, chip in enumerate(chips):
            copy(1 + j, (*chip, c_), me).wait_recv()
            passed[j].start()
        copy(0, sibling, me).wait_recv()
        for j, chip in enumerate(chips):
            copy(4 + j, (*chip, 1 - c_), me).wait_recv()
        for cp in first + passed:
            cp.wait_send()
        mine.wait()

    return pl.pallas_call(
        body, out_shape=jax.ShapeDtypeStruct((N_DEV,) + x.shape, x.dtype), in_specs=[HBM_SPEC], out_specs=HBM_SPEC,
        scratch_shapes=[pltpu.SemaphoreType.DMA((N_DEV - 1,)), pltpu.SemaphoreType.DMA((N_DEV - 1,)),
                        pltpu.SemaphoreType.DMA], name=name)(x)


def _chip(x, y):
    return 2 * x + y


def sibling_exchange(x, name):
    nc, _, R, C = x.shape

    def body(x_ref, out_ref, send_sem, recv_sem):
        x_, y_, c_ = _place()
        cp = pltpu.make_async_remote_copy(src_ref=x_ref.at[:, 1 - c_], dst_ref=out_ref, send_sem=send_sem,
                                          recv_sem=recv_sem, device_id=(x_, y_, 1 - c_), device_id_type=MESH_ID)
        cp.start()
        cp.wait()

    return pl.pallas_call(
        body, out_shape=jax.ShapeDtypeStruct((nc, R, C), x.dtype), in_specs=[HBM_SPEC], out_specs=HBM_SPEC,
        scratch_shapes=[pltpu.SemaphoreType.DMA, pltpu.SemaphoreType.DMA], name=name)(x)


def _row_tile(R, C):
    cap = max(SUBLANES, STREAM_BLOCK_BYTES // (4 * C))
    return _tile(R, [t for t in (512, 256, 128, 64, 32, 16, 8) if t <= cap])


def pair_sum(x, recv, name):
    nc, _, R, C = x.shape
    tr = _row_tile(R, C)
    core = lax.axis_index("c").astype(jnp.int32).reshape(1)

    def body(c_ref, a_ref, b_ref, o_ref):
        o_ref[...] = (a_ref[...].astype(F32) + b_ref[...].astype(F32)).astype(o_ref.dtype)

    blk = pl.BlockSpec((None, tr, C), lambda k, i, c_ref: (k, i, 0))
    grid_spec = pltpu.PrefetchScalarGridSpec(
        num_scalar_prefetch=1, grid=(nc, R // tr),
        in_specs=[pl.BlockSpec((None, None, tr, C), lambda k, i, c_ref: (k, c_ref[0], i, 0)), blk], out_specs=blk)
    return pl.pallas_call(body, grid_spec=grid_spec, out_shape=jax.ShapeDtypeStruct((nc, R, C), x.dtype), name=name,
                          compiler_params=_cp(2))(core, x, recv)


def chip_exchange(s, name):
    def body(s_ref, out_ref, send_sems, recv_sems, local_sem):
        x_, y_, c_ = _place()
        me = _chip(x_, y_)
        local = pltpu.make_async_copy(s_ref.at[me], out_ref.at[me], local_sem)
        local.start()
        copies = []
        for k in range(1, N_CHIP):
            px = 1 - x_ if k & 2 else x_
            py = 1 - y_ if k & 1 else y_
            cp = pltpu.make_async_remote_copy(
                src_ref=s_ref.at[_chip(px, py)], dst_ref=out_ref.at[me], send_sem=send_sems.at[k - 1],
                recv_sem=recv_sems.at[k - 1], device_id=(px, py, c_), device_id_type=MESH_ID)
            cp.start()
            copies.append(cp)
        for cp in copies:
            cp.wait()
        local.wait()

    return pl.pallas_call(
        body, out_shape=jax.ShapeDtypeStruct(s.shape, s.dtype), in_specs=[HBM_SPEC], out_specs=HBM_SPEC,
        scratch_shapes=[pltpu.SemaphoreType.DMA((N_CHIP - 1,)), pltpu.SemaphoreType.DMA((N_CHIP - 1,)),
                        pltpu.SemaphoreType.DMA], name=name)(s)


def _to_sub(t, dil):
    L, W = t.shape
    E = W // ATT_HEADS
    return t.reshape(L // dil, dil, ATT_HEADS, E).transpose(2, 1, 0, 3)


def _from_sub(t):
    H, dil, Ls, E = t.shape
    return t.transpose(2, 1, 0, 3).reshape(Ls * dil, H * E)


def _block_diag(w, nb):
    G, a, b = w.shape
    gp = G // nb
    eye = jnp.eye(gp, dtype=w.dtype)
    return jnp.einsum('jgab,gh->jgahb', w.reshape(nb, gp, a, b), eye).reshape(nb, gp * a, gp * b)


def _layer(l, h, memn, bias_tabs, P):
    nm = lambda s: f"l{l}_{s}"
    L, D = h.shape
    GW = D // N_MIXERS
    G = GW // S5_CH_PER_GROUP

    xn = rmsnorm(h, P['norm_mix_g'], nm("norm_mix"))
    proj = _linear(nm("w_in"))(xn, P['w_in'])
    u_a, u_b, u_c, qkv = proj[:, :GW], proj[:, GW:2 * GW], proj[:, 2 * GW:4 * GW], proj[:, 4 * GW:]

    v3 = lambda a: a.reshape(G, 1, S5_STATE)
    log_dt = jnp.broadcast_to(P['s5_log_dt'][:, None, None], (G, 1, S5_STATE))
    a_r, a_i, bb_r, bb_i = s5_discretise(v3(P['s5_lam_re']), v3(P['s5_lam_im']), log_dt,
                                         P['s5_b_re'].transpose(0, 2, 1), P['s5_b_im'].transpose(0, 2, 1), nm("s5_disc"))
    nblk = GW // S5_BLOCK_CH
    y_s5 = _s5_core(nm("s5_core"))(
        u_a, _block_diag(bb_r, nblk), _block_diag(bb_i, nblk), a_r.reshape(1, G * S5_STATE), a_i.reshape(1, G * S5_STATE),
        _block_diag(P['s5_c_re'].transpose(0, 2, 1), nblk), _block_diag(P['s5_c_im'].transpose(0, 2, 1), nblk))
    y_a = s5_epilogue(y_s5, u_a, P['s5_d'], P['s5_w_glu'], nm("s5_glu"))

    y_b = pool_proj(_pool_mix(nm("pool_mix"))(u_b), P['pool_w'], P['pool_scale'], nm("pool_proj"))

    hc = _glu_conv(nm("conv_dw"))(u_c, P['conv_w_dw'], P['conv_b_dw'].reshape(1, GW))
    y_c = conv_post(hc, P['conv_ln_g'], P['conv_ln_b'], P['conv_w_pw'], nm("conv_post"))

    q, k, v = qkv[:, :GW], qkv[:, GW:2 * GW], qkv[:, 2 * GW:]
    outs, lses = [], []
    for p, (_, dil) in enumerate(DILATED_PATTERNS):
        o, lse = _band_attention(nm(f"att{p}"))(_to_sub(q, dil), _to_sub(k, dil), _to_sub(v, dil), bias_tabs[p])
        outs.append(_from_sub(o))
        lses.append(_from_sub(lse))
    y_d = att_combine(outs, lses, nm("att_mix"))

    y = group_norm([y_a, y_b, y_c, y_d], P['grp_norm_g'], nm("grp_norm"))
    h = _linear_res(nm("w_out"))(y, P['w_out'], h)

    hn = rmsnorm(h, P['norm_x_g'], nm("norm_x"))
    xq = _linear(nm("w_xq"))(hn, P['w_xq'])
    xk = _linear(nm("w_xk"))(memn, P['w_xk'])
    xv = _linear(nm("w_xv"))(memn, P['w_xv'])
    xo = cross_attention(xq, xk, xv, nm("xattn"))
    h = _linear_res(nm("w_xo"))(xo, P['w_xo'], h)

    hn = rmsnorm(h, P['norm_mlp_g'], nm("norm_mlp"))
    r = relu_sq(_linear(nm("w_up"))(hn, P['w_up']), nm("relu_sq"))
    return _linear_res(nm("w_down"))(r, P['w_down'], h)


def _trunk(x, mem, W):
    memn = rmsnorm(mem, W['mem_norm_g'], "mem_norm")
    tabs = rel_bias_tables(W['rel_bias'], _bucket_onehot(), "rel_bias")
    bias_tabs = [tabs[p].reshape(ATT_HEADS, ATT_BLOCK, 2 * ATT_BLOCK) for p in range(len(DILATED_PATTERNS))]
    h = x
    for l in range(DEPTH):
        P = {n: W[n][l] for n in WEIGHTS if n not in ('rel_bias', 'mem_norm_g', 'norm_final_g')}
        h = _layer(l, h, memn, bias_tabs, P)
    return h


def _gather_weight(name, w):
    ax = SHARDED[name]
    dt = BF16 if name in GATHER_BF16 else F32
    nl, a, b = w.shape
    g = all_gather(w.astype(dt).reshape(nl * a, b), "ag_" + name).reshape(N_DEV, nl, a, b)
    if ax == 1:
        return g.transpose(1, 0, 2, 3).reshape(nl, N_DEV * a, b)
    return g.transpose(1, 2, 0, 3).reshape(nl, a, N_DEV * b)


def _scatter_grad(name, g):
    ax = SHARDED[name]
    nl = g.shape[0]
    if ax == 1:
        a, b = g.shape[1] // N_DEV, g.shape[2]
        s = g.reshape(nl, N_DEV, a, b).transpose(1, 0, 2, 3)
    else:
        a, b = g.shape[1], g.shape[2] // N_DEV
        s = g.reshape(nl, a, N_DEV, b).transpose(2, 0, 1, 3)
    s = s.reshape(N_CHIP, N_DEV // N_CHIP, nl * a, b)
    pair = pair_sum(s, sibling_exchange(s, "d2d_" + name), "pairsum_" + name)
    return chip_exchange(pair, "ici_" + name)


def _flatten_small(d):
    flat = jnp.concatenate([d[n].reshape(-1).astype(F32) for n in SMALL])
    pad = (-flat.shape[0]) % (LANES * SMALL_ROW_TILE)
    return jnp.pad(flat, (0, pad)).reshape(-1, LANES)


def _split_small(flat, like):
    flat = flat.reshape(-1)
    out, off = {}, 0
    for n in SMALL:
        sz = math.prod(like[n].shape)
        out[n] = flat[off:off + sz].reshape(like[n].shape)
        off += sz
    return out


def kernel(x, mem, rel_bias, mem_norm_g, norm_mix_g, w_in, s5_lam_re, s5_lam_im, s5_log_dt, s5_b_re, s5_b_im, s5_c_re, s5_c_im, s5_d, s5_w_glu, pool_w, pool_scale, conv_w_dw, conv_b_dw, conv_ln_g, conv_ln_b, conv_w_pw, grp_norm_g, w_out, norm_x_g, w_xq, w_xk, w_xv, w_xo, norm_mlp_g, w_up, w_down, norm_final_g, loss_target, m_rel_bias, m_mem_norm_g, m_norm_mix_g, m_w_in, m_s5_lam_re, m_s5_lam_im, m_s5_log_dt, m_s5_b_re, m_s5_b_im, m_s5_c_re, m_s5_c_im, m_s5_d, m_s5_w_glu, m_pool_w, m_pool_scale, m_conv_w_dw, m_conv_b_dw, m_conv_ln_g, m_conv_ln_b, m_conv_w_pw, m_grp_norm_g, m_w_out, m_norm_x_g, m_w_xq, m_w_xk, m_w_xv, m_w_xo, m_norm_mlp_g, m_w_up, m_w_down, m_norm_final_g, v_rel_bias, v_mem_norm_g, v_norm_mix_g, v_w_in, v_s5_lam_re, v_s5_lam_im, v_s5_log_dt, v_s5_b_re, v_s5_b_im, v_s5_c_re, v_s5_c_im, v_s5_d, v_s5_w_glu, v_pool_w, v_pool_scale, v_conv_w_dw, v_conv_b_dw, v_conv_ln_g, v_conv_ln_b, v_conv_w_pw, v_grp_norm_g, v_w_out, v_norm_x_g, v_w_xq, v_w_xk, v_w_xv, v_w_xo, v_norm_mlp_g, v_w_up, v_w_down, v_norm_final_g):
    w = dict(zip(WEIGHTS, (rel_bias, mem_norm_g, norm_mix_g, w_in, s5_lam_re, s5_lam_im, s5_log_dt, s5_b_re, s5_b_im, s5_c_re, s5_c_im, s5_d, s5_w_glu, pool_w, pool_scale, conv_w_dw, conv_b_dw, conv_ln_g, conv_ln_b, conv_w_pw, grp_norm_g, w_out, norm_x_g, w_xq, w_xk, w_xv, w_xo, norm_mlp_g, w_up, w_down, norm_final_g)))
    m = dict(zip(WEIGHTS, (m_rel_bias, m_mem_norm_g, m_norm_mix_g, m_w_in, m_s5_lam_re, m_s5_lam_im, m_s5_log_dt, m_s5_b_re, m_s5_b_im, m_s5_c_re, m_s5_c_im, m_s5_d, m_s5_w_glu, m_pool_w, m_pool_scale, m_conv_w_dw, m_conv_b_dw, m_conv_ln_g, m_conv_ln_b, m_conv_w_pw, m_grp_norm_g, m_w_out, m_norm_x_g, m_w_xq, m_w_xk, m_w_xv, m_w_xo, m_norm_mlp_g, m_w_up, m_w_down, m_norm_final_g)))
    v = dict(zip(WEIGHTS, (v_rel_bias, v_mem_norm_g, v_norm_mix_g, v_w_in, v_s5_lam_re, v_s5_lam_im, v_s5_log_dt, v_s5_b_re, v_s5_b_im, v_s5_c_re, v_s5_c_im, v_s5_d, v_s5_w_glu, v_pool_w, v_pool_scale, v_conv_w_dw, v_conv_b_dw, v_conv_ln_g, v_conv_ln_b, v_conv_w_pw, v_grp_norm_g, v_w_out, v_norm_x_g, v_w_xq, v_w_xk, v_w_xv, v_w_xo, v_norm_mlp_g, v_w_up, v_w_down, v_norm_final_g)))

    full = {n: (_gather_weight(n, w[n]) if n in SHARDED else w[n]) for n in WEIGHTS if n != 'norm_final_g'}

    h, trunk_vjp = jax.vjp(_trunk, x[0], mem[0], full)
    loss_local, dh, d_final_g = loss_head(h, loss_target[0], w['norm_final_g'], "loss_head")
    dx, _, dfull = trunk_vjp(dh)
    dfull['norm_final_g'] = d_final_g
    loss = lax.psum(loss_local, MESH_AXES)

    grads, deltas, new_m, new_v = {}, {}, {}, {}
    for n in SHARDED:
        parts = _scatter_grad(n, dfull[n])
        shp = w[n].shape
        two_d = lambda a: a.reshape(shp[0] * shp[1], shp[2])
        res = adamw(two_d(w[n]), parts, two_d(m[n]), two_d(v[n]), "adamw_" + n)
        grads[n], deltas[n], new_m[n], new_v[n] = (r.reshape(shp) for r in res)

    parts = all_gather(_flatten_small(dfull), "ag_small_grads")
    res = adamw(_flatten_small(w), parts, _flatten_small(m), _flatten_small(v), "adamw_small")
    for dst, r in zip((grads, deltas, new_m, new_v), res):
        dst.update(_split_small(r, w))

    return (loss, dx[None], *[grads[n] for n in WEIGHTS], *[deltas[n] for n in WEIGHTS],
            *[new_m[n] for n in WEIGHTS], *[new_v[n] for n in WEIGHTS])
```

```python
import functools
import math

import numpy as np
import jax
import jax.numpy as jnp
from jax import lax
from jax.experimental import pallas as pl
from jax.experimental.pallas import tpu as pltpu

F32 = jnp.float32
BF16 = jnp.bfloat16

DEPTH = 4
N_MIXERS = 4
S5_CH_PER_GROUP = 16
S5_STATE = 64
POOL_WINDOWS = (2, 4, 8, 16)
CONV_WIDTH = 31
ATT_HEADS = 8
DILATED_PATTERNS = ((128, 1), (512, 4), (2048, 16))
ATT_BLOCK = 128
ATT_MIX_ROWS = 256
REL_BUCKETS = 32
REL_MAX_DIST = 2048
X_HEADS = 4
X_HEAD_DIM = 128
NORM_EPS = 1e-6
NEG_INF = -1e30
ADAM_LR = 0.001
ADAM_B1 = 0.9
ADAM_B2 = 0.999
ADAM_EPS = 1e-08
ADAM_WD = 0.01
ADAM_STEP = 10

LANES = 128
SUBLANES = 8
VMEM_BYTES = 64 * 1024 * 1024
VMEM_LIMIT = (VMEM_BYTES * 3) // 4
VMEM_LIMIT_BIG = (VMEM_BYTES * 7) // 8
STREAM_BLOCK_BYTES = 1024 * 1024
SMALL_ROW_TILE = 512
N_DEV = 8
N_CHIP = 4
MESH_AXES = ("x", "y", "c")

WEIGHTS = ['rel_bias', 'mem_norm_g', 'norm_mix_g', 'w_in', 's5_lam_re', 's5_lam_im', 's5_log_dt', 's5_b_re',
           's5_b_im', 's5_c_re', 's5_c_im', 's5_d', 's5_w_glu', 'pool_w', 'pool_scale', 'conv_w_dw', 'conv_b_dw',
           'conv_ln_g', 'conv_ln_b', 'conv_w_pw', 'grp_norm_g', 'w_out', 'norm_x_g', 'w_xq', 'w_xk', 'w_xv', 'w_xo',
           'norm_mlp_g', 'w_up', 'w_down', 'norm_final_g']
SHARDED = {'w_in': 2, 's5_w_glu': 1, 'conv_w_dw': 2, 'conv_w_pw': 1, 'w_out': 1, 'w_xq': 1, 'w_xk': 1, 'w_xv': 1,
           'w_xo': 2, 'w_up': 2, 'w_down': 1}
GATHER_BF16 = ('w_in', 'w_out', 'w_xq', 'w_xk', 'w_xv', 'w_xo', 'w_up', 'w_down')
SMALL = [n for n in WEIGHTS if n not in SHARDED]


def _cp(n_axes, vmem=VMEM_LIMIT):
    return pltpu.CompilerParams(dimension_semantics=("arbitrary",) * n_axes, vmem_limit_bytes=vmem)


def _tile(n, prefs):
    for t in prefs:
        if n % t == 0:
            return t
    return n


MM_TILE = 1024
MM_TILE_K = 2048


def _mm(a, b, *, ta=False, tb=False, res=None, out_dtype=F32, epilogue=None, pre=None, name):
    if ta:
        K, M = a.shape
    else:
        M, K = a.shape
    if tb:
        N, K2 = b.shape
    else:
        K2, N = b.shape
    assert K == K2, (a.shape, b.shape, ta, tb)
    wide_f32 = K >= MM_TILE_K and F32 in (a.dtype, b.dtype)
    tm = _tile(M, (MM_TILE // 2 if wide_f32 and not ta else MM_TILE, 512, 256, 128))
    tn = _tile(N, (MM_TILE, 512, 256, 128))
    tk = _tile(K, (MM_TILE_K, 1024, 512, 256, 128))
    nk = K // tk
    a_spec = pl.BlockSpec((tk, tm), lambda i, j, k: (k, i)) if ta else pl.BlockSpec((tm, tk), lambda i, j, k: (i, k))
    b_spec = pl.BlockSpec((tn, tk), lambda i, j, k: (j, k)) if tb else pl.BlockSpec((tk, tn), lambda i, j, k: (k, j))
    o_spec = pl.BlockSpec((tm, tn), lambda i, j, k: (i, j))
    dn = (((0 if ta else 1,), (1 if tb else 0,)), ((), ()))
    extra = [x for x in (res, pre) if x is not None]
    assert not (res is not None and pre is not None)
    n_out = 2 if epilogue == 'relu_sq' else 1

    def body(*refs):
        a_ref, b_ref = refs[:2]
        x_ref = refs[2] if extra else None
        o_refs = refs[2 + len(extra):2 + len(extra) + n_out]
        acc = refs[-1] if nk > 1 else None

        def finish(r):
            if res is not None:
                r = r + x_ref[...]
            if epilogue == 'relu_sq':
                o_refs[0][...] = r
                o_refs[1][...] = jnp.square(jnp.maximum(r, 0.0)).astype(out_dtype)
            elif epilogue == 'relu_sq_grad':
                o_refs[0][...] = (r * (2.0 * jnp.maximum(x_ref[...], 0.0))).astype(out_dtype)
            else:
                o_refs[0][...] = r.astype(out_dtype)

        part = lax.dot_general(a_ref[...].astype(BF16), b_ref[...].astype(BF16), dn, preferred_element_type=F32)
        if nk == 1:
            finish(part)
        else:
            k = pl.program_id(2)

            @pl.when(k == 0)
            def _():
                acc[...] = part

            @pl.when(k > 0)
            def _():
                acc[...] += part

            @pl.when(k == nk - 1)
            def _():
                finish(acc[...])

    out_shape = [jax.ShapeDtypeStruct((M, N), F32 if epilogue == 'relu_sq' else out_dtype)]
    if n_out == 2:
        out_shape.append(jax.ShapeDtypeStruct((M, N), out_dtype))
    outs = pl.pallas_call(
        body, grid=(M // tm, N // tn, nk), in_specs=[a_spec, b_spec] + [o_spec] * len(extra),
        out_specs=[o_spec] * n_out, out_shape=out_shape,
        scratch_shapes=[pltpu.VMEM((tm, tn), F32)] if nk > 1 else [], name=name,
        compiler_params=_cp(3, VMEM_LIMIT_BIG))(a, b, *extra)
    return outs[0] if n_out == 1 else tuple(outs)


def _linear(name):
    @jax.custom_vjp
    def lin(a, w):
        return _mm(a, w, name=name + "_fwd")

    def fwd(a, w):
        return _mm(a, w, name=name + "_fwd"), (a, w)

    def bwd(r, dy):
        a, w = r
        da = _mm(dy, w, tb=True, name=name + "_dx")
        dw = _mm(a, dy, ta=True, out_dtype=w.dtype, name=name + "_dw")
        return da, dw

    lin.defvjp(fwd, bwd)
    return lin


def _act_linear(name, act, n_in, with_res):
    def run(*a):
        ins, w = a[:n_in], a[n_in]
        x = act.fwd_call(*ins)[0]
        return _mm(x, w, res=a[n_in + 1] if with_res else None, name=name + "_fwd"), (ins, x, w)

    @jax.custom_vjp
    def op(*a):
        return run(*a)[0]

    def bwd(r, dy):
        ins, x, w = r
        dx = _mm(dy, w, tb=True, name=name + "_dx")
        dw = _mm(x, dy, ta=True, out_dtype=w.dtype, name=name + "_dw")
        return (*act.bwd_all(ins, (dx,)), dw) + ((dy,) if with_res else ())

    op.defvjp(run, bwd)
    return op


def _mlp(name, norm):
    def run(h, g, w_up, w_down):
        hn = norm.fwd_call(h, g)[0]
        a, r = _mm(hn, w_up, epilogue='relu_sq', out_dtype=BF16, name=name + "_up_fwd")
        return _mm(r, w_down, res=h, name=name + "_down_fwd"), (h, g, hn, a, r, w_up, w_down)

    @jax.custom_vjp
    def op(h, g, w_up, w_down):
        return run(h, g, w_up, w_down)[0]

    def bwd(res, dy):
        h, g, hn, a, r, w_up, w_down = res
        da = _mm(dy, w_down, tb=True, epilogue='relu_sq_grad', pre=a, out_dtype=BF16, name=name + "_down_dx")
        dw_down = _mm(r, dy, ta=True, out_dtype=w_down.dtype, name=name + "_down_dw")
        dhn = _mm(da, w_up, tb=True, name=name + "_up_dx")
        dw_up = _mm(hn, da, ta=True, out_dtype=w_up.dtype, name=name + "_up_dw")
        dh, dg = norm.bwd_all((h, g), (dhn,))
        return dh + dy, dg, dw_up, dw_down

    op.defvjp(run, bwd)
    return op


def _block_op(name, f, grid, ins, outs, vmem=VMEM_LIMIT):
    n_in, n_out = len(ins), len(outs)
    in_specs = [pl.BlockSpec(bs, im) for bs, im, _, _ in ins]
    out_specs = [pl.BlockSpec(bs, im) for _, _, bs, im in outs]
    out_shape = [jax.ShapeDtypeStruct(s, d) for s, d, _, _ in outs]
    didx = [i for i in range(n_in) if ins[i][3]]

    def fwd_call(*args):
        def body(*refs):
            res = f(*[r[...] for r in refs[:n_in]])
            for r, o in zip(refs[n_in:], res):
                r[...] = o.astype(r.dtype)

        return pl.pallas_call(body, grid=grid, in_specs=in_specs, out_specs=out_specs, out_shape=out_shape,
                              name=name + "_fwd", compiler_params=_cp(len(grid), vmem))(*args)

    def bwd_call(args, cts):
        def body(*refs):
            vals = [r[...] for r in refs[:n_in]]
            ct_refs = refs[n_in:n_in + n_out]
            g_refs = refs[n_in + n_out:]

            def fd(*dv):
                full = list(vals)
                for i, v in zip(didx, dv):
                    full[i] = v
                return f(*full)

            _, vjp = jax.vjp(fd, *[vals[i] for i in didx])
            grads = vjp(tuple(r[...] for r in ct_refs))
            for gref, i, g in zip(g_refs, didx, grads):
                acc = ins[i][2]
                if acc:
                    first = functools.reduce(jnp.logical_and, [pl.program_id(ax) == 0 for ax in acc])

                    @pl.when(first)
                    def _(gref=gref):
                        gref[...] = jnp.zeros_like(gref)

                    gref[...] += g.astype(gref.dtype)
                else:
                    gref[...] = g.astype(gref.dtype)

        g_specs = [pl.BlockSpec(ins[i][0], ins[i][1]) for i in didx]
        g_shape = [jax.ShapeDtypeStruct(args[i].shape, args[i].dtype) for i in didx]
        return pl.pallas_call(body, grid=grid, in_specs=in_specs + out_specs, out_specs=g_specs, out_shape=g_shape,
                              name=name + "_bwd", compiler_params=_cp(len(grid), vmem))(*args, *cts)

    @jax.custom_vjp
    def op(*args):
        return tuple(fwd_call(*args))

    def op_fwd(*args):
        return tuple(fwd_call(*args)), args

    def op_bwd(args, cts):
        it = iter(bwd_call(args, cts))
        return tuple(next(it) if ins[i][3] else jnp.zeros_like(args[i]) for i in range(n_in))

    op.defvjp(op_fwd, op_bwd)
    op.fwd_call = fwd_call
    op.bwd_all = op_bwd
    return op


def _row(tr, c):
    return ((tr, c), lambda i: (i, 0), None, True)


def _par(shape):
    nd = len(shape)
    return (shape, lambda i: (0,) * nd, (0,), True)


def _bdot(a, w):
    return jnp.dot(a.astype(BF16), w.astype(BF16), preferred_element_type=F32)


def _rms_f(x, g):
    return (x * lax.rsqrt(jnp.mean(x * x, axis=-1, keepdims=True) + NORM_EPS) * g,)


def _rms_op(name, R, D, out_dtype):
    tr = _tile(R, (256,))
    return _block_op(name, _rms_f, (R // tr,), [_row(tr, D), _par((1, D))],
                     [((R, D), out_dtype, (tr, D), lambda i: (i, 0))])


def rmsnorm(x, g, name):
    R, D = x.shape
    return _rms_op(name, R, D, F32)(x, g.reshape(1, D))[0]


def s5_epilogue(yc, u, d, w_glu, name):
    R, C = yc.shape
    tr = _tile(R, (256,))

    def f(yc, u, d, w):
        g = jax.nn.gelu(yc + d * u)
        return (g * jax.nn.sigmoid(_bdot(g, w)),)

    op = _block_op(name, f, (R // tr,), [_row(tr, C), _row(tr, C), _par((1, C)), _par((C, C))],
                   [((R, C), F32, (tr, C), lambda i: (i, 0))])
    return op(yc, u, d.reshape(1, C), w_glu)[0]


def pool_proj(p, w, scale, name):
    R, C = p.shape
    ng, pc, _ = w.shape
    tr = _tile(R, (256,))

    def f(p, w, s):
        ys = [_bdot(p[:, g * pc:(g + 1) * pc], w[g]) for g in range(ng)]
        return (jnp.concatenate(ys, axis=-1) * s,)

    op = _block_op(name, f, (R // tr,), [_row(tr, C), _par((ng, pc, pc)), _par((1, C))],
                   [((R, C), F32, (tr, C), lambda i: (i, 0))])
    return op(p, w, scale.reshape(1, C))[0]


def conv_post(h, ln_g, ln_b, w_pw, name):
    R, C = h.shape
    tr = _tile(R, (256,))

    def f(h, g, b, w):
        hc = h - jnp.mean(h, axis=-1, keepdims=True)
        y = hc * lax.rsqrt(jnp.mean(hc * hc, axis=-1, keepdims=True) + NORM_EPS) * g + b
        return (_bdot(jax.nn.silu(y), w),)

    op = _block_op(name, f, (R // tr,), [_row(tr, C), _par((1, C)), _par((1, C)), _par((C, C))],
                   [((R, C), F32, (tr, C), lambda i: (i, 0))])
    return op(h, ln_g.reshape(1, C), ln_b.reshape(1, C), w_pw)[0]


def _group_norm_op(name, R, C, n, out_dtype):
    tr = _tile(R, (256,))

    def f(*a):
        g = a[n]
        parts = [y * lax.rsqrt(jnp.mean(y * y, axis=-1, keepdims=True) + NORM_EPS) for y in a[:n]]
        return (jnp.concatenate(parts, axis=-1) * g,)

    return _block_op(name, f, (R // tr,), [_row(tr, C)] * n + [_par((1, n * C))],
                     [((R, n * C), out_dtype, (tr, n * C), lambda i: (i, 0))])


def _cross_attention_op(name, L, W, M, out_dtype):
    E = X_HEAD_DIM
    tq = _tile(L, (512,))

    def f(q, k, v):
        s = lax.dot_general(q.astype(BF16), k.astype(BF16), (((1,), (1,)), ((), ())),
                            preferred_element_type=F32) * (E ** -0.5)
        p = jax.nn.softmax(s, axis=-1)
        return (_bdot(p, v),)

    qspec = ((tq, E), lambda h, i: (i, h), None, True)
    kspec = ((M, E), lambda h, i: (0, h), (1,), True)
    return _block_op(name, f, (W // E, L // tq), [qspec, kspec, kspec],
                     [((L, W), out_dtype, (tq, E), lambda h, i: (i, h))])


def cross_attention(q, k, v, name):
    return _cross_attention_op(name, q.shape[0], q.shape[1], k.shape[0], F32)(q, k, v)[0]


def s5_discretise(lam_re, lam_im, log_dt, b_re_t, b_im_t, name):
    G, _, N = lam_re.shape
    C = b_re_t.shape[1]

    def f(lr, li, ldt, br, bi):
        dt = jnp.exp(ldt)
        mag = jnp.exp(lr * dt)
        ab_r, ab_i = mag * jnp.cos(li * dt), mag * jnp.sin(li * dt)
        den = lr * lr + li * li
        nr, ni = ab_r - 1.0, ab_i
        f_r = (nr * lr + ni * li) / den
        f_i = (ni * lr - nr * li) / den
        return ab_r, ab_i, f_r * br - f_i * bi, f_r * bi + f_i * br

    vec = ((G, 1, N), lambda i: (0, 0, 0), None, True)
    mat = ((G, C, N), lambda i: (0, 0, 0), None, True)
    ov = ((G, 1, N), F32, (G, 1, N), lambda i: (0, 0, 0))
    om = ((G, C, N), F32, (G, C, N), lambda i: (0, 0, 0))
    op = _block_op(name, f, (1,), [vec, vec, vec, mat, mat], [ov, ov, om, om])
    return op(lam_re, lam_im, log_dt, b_re_t, b_im_t)


def rel_bias_tables(rel_bias, onehot, name):
    B, H = rel_bias.shape
    P, _, Q = onehot.shape

    def f(rbt, oh):
        return (jnp.dot(rbt, oh, precision=lax.Precision.HIGHEST, preferred_element_type=F32),)

    op = _block_op(name, f, (P,), [((H, B), lambda p: (0, 0), (0,), True), ((None, B, Q), lambda p: (p, 0, 0), None, False)],
                   [((P, H, Q), F32, (None, H, Q), lambda p: (p, 0, 0))])
    return op(rel_bias.T, onehot)[0]


def _shift_down(x, s, row):
    return jnp.where(row >= s, pltpu.roll(x, s, 0), 0.0)


def _shift_up(x, s, row):
    n = x.shape[0]
    return jnp.where(row < n - s, pltpu.roll(x, n - s, 0), 0.0)


def _window_sum(x, w, row, shift):
    span = 1
    while span < w:
        x = x + shift(x, span, row)
        span *= 2
    return x


def _pool_call(u, d_out, name):
    L, C = u.shape
    pc = C // len(POOL_WINDOWS)
    assert pc % LANES == 0

    def body(x_ref, o_ref):
        row = lax.broadcasted_iota(jnp.int32, (L, pc), 0)
        for g, w in enumerate(POOL_WINDOWS):
            sl = slice(g * pc, (g + 1) * pc)
            x = x_ref[:, sl]
            cnt = jnp.minimum(row + 1, w).astype(F32)
            if d_out is None:
                o_ref[:, sl] = _window_sum(x, w, row, _shift_down) / cnt - x
            else:
                o_ref[:, sl] = _window_sum(x / cnt, w, row, _shift_up) - x

    src = u if d_out is None else d_out
    return pl.pallas_call(body, out_shape=jax.ShapeDtypeStruct((L, C), F32), name=name,
                          compiler_params=pltpu.CompilerParams(vmem_limit_bytes=VMEM_LIMIT))(src)


def _pool_mix(name):
    @jax.custom_vjp
    def op(u):
        return _pool_call(u, None, name + "_fwd")

    def fwd(u):
        return _pool_call(u, None, name + "_fwd"), u

    def bwd(u, dp):
        return (_pool_call(u, dp, name + "_bwd"),)

    op.defvjp(fwd, bwd)
    return op


def _conv_fwd(u, w, b, name):
    L, C2 = u.shape
    C = C2 // 2
    K = w.shape[0]
    nb = C // LANES

    def body(val_ref, gate_ref, w_ref, b_ref, o_ref):
        row = lax.broadcasted_iota(jnp.int32, (L, LANES), 0)
        h = val_ref[...] * jax.nn.sigmoid(gate_ref[...])
        acc = jnp.broadcast_to(b_ref[...], (L, LANES))
        for k in range(K):
            acc = acc + w_ref[k:k + 1, :] * _shift_down(h, K - 1 - k, row)
        o_ref[...] = acc

    blk = lambda off: pl.BlockSpec((L, LANES), lambda j: (0, j + off))
    return pl.pallas_call(
        body, grid=(nb,), in_specs=[blk(0), blk(nb), pl.BlockSpec((K, LANES), lambda j: (0, j)),
                                    pl.BlockSpec((1, LANES), lambda j: (0, j))],
        out_specs=blk(0), out_shape=jax.ShapeDtypeStruct((L, C), F32), name=name, compiler_params=_cp(1))(u, u, w, b)


def _conv_bwd(u, w, dh, name):
    L, C2 = u.shape
    C = C2 // 2
    K = w.shape[0]
    nb = C // LANES

    def body(val_ref, gate_ref, w_ref, dh_ref, dval_ref, dgate_ref, dw_ref, db_ref):
        row = lax.broadcasted_iota(jnp.int32, (L, LANES), 0)
        val = val_ref[...]
        sig = jax.nn.sigmoid(gate_ref[...])
        h = val * sig
        d = dh_ref[...]
        dh0 = jnp.zeros((L, LANES), F32)
        for k in range(K):
            s = K - 1 - k
            dh0 = dh0 + w_ref[k:k + 1, :] * _shift_up(d, s, row)
            dw_ref[k:k + 1, :] = jnp.sum(d * _shift_down(h, s, row), axis=0, keepdims=True)
        db_ref[...] = jnp.sum(d, axis=0, keepdims=True)
        dval_ref[...] = dh0 * sig
        dgate_ref[...] = dh0 * val * sig * (1.0 - sig)

    blk = lambda off: pl.BlockSpec((L, LANES), lambda j: (0, j + off))
    return pl.pallas_call(
        body, grid=(nb,), in_specs=[blk(0), blk(nb), pl.BlockSpec((K, LANES), lambda j: (0, j)), blk(0)],
        out_specs=[blk(0), blk(0), pl.BlockSpec((K, LANES), lambda j: (0, j)), pl.BlockSpec((1, LANES), lambda j: (0, j))],
        out_shape=[jax.ShapeDtypeStruct((L, C), F32), jax.ShapeDtypeStruct((L, C), F32),
                   jax.ShapeDtypeStruct((K, C), F32), jax.ShapeDtypeStruct((1, C), F32)],
        name=name, compiler_params=_cp(1))(u, u, w, dh)


def _glu_conv(name):
    @jax.custom_vjp
    def op(u, w, b):
        return _conv_fwd(u, w, b, name + "_fwd")

    def fwd(u, w, b):
        return _conv_fwd(u, w, b, name + "_fwd"), (u, w)

    def bwd(r, dh):
        u, w = r
        dval, dgate, dw, db = _conv_bwd(u, w, dh, name + "_bwd")
        return jnp.concatenate([dval, dgate], axis=-1), dw, db

    op.defvjp(fwd, bwd)
    return op


S5_BLOCK_CH = LANES
S5_BLOCK_ST = S5_BLOCK_CH // S5_CH_PER_GROUP * S5_STATE


def _s5_scan(br_ref, bi_ref, ar, ai, reverse):
    L, C = br_ref.shape
    T = SUBLANES
    row = lax.broadcasted_iota(jnp.int32, (T, C), 0)
    pw = [(ar, ai)]
    for _ in range(T - 1):
        pr, pi = pw[-1]
        pw.append((pr * ar - pi * ai, pr * ai + pi * ar))
    cr = jnp.zeros((T, C), F32)
    ci = jnp.zeros((T, C), F32)
    for r in range(T):
        e = (T - r) if reverse else (r + 1)
        cr = jnp.where(row == r, pw[e - 1][0], cr)
        ci = jnp.where(row == r, pw[e - 1][1], ci)
    steps = []
    s = 1
    while s < T:
        mask = (row < T - s) if reverse else (row >= s)
        steps.append((T - s if reverse else s, mask, pw[s - 1][0], pw[s - 1][1]))
        s *= 2
    nt = L // T
    last = 0 if reverse else T - 1

    def body(i, carry):
        kr, ki = carry
        t = (nt - 1 - i) if reverse else i
        off = pl.multiple_of(t * T, T)
        xr = br_ref[pl.ds(off, T), :]
        xi = bi_ref[pl.ds(off, T), :]
        for sh, mask, mr, mi in steps:
            sr = jnp.where(mask, pltpu.roll(xr, sh, 0), 0.0)
            si = jnp.where(mask, pltpu.roll(xi, sh, 0), 0.0)
            xr, xi = xr + mr * sr - mi * si, xi + mr * si + mi * sr
        xr, xi = xr + cr * kr - ci * ki, xi + cr * ki + ci * kr
        br_ref[pl.ds(off, T), :] = xr
        bi_ref[pl.ds(off, T), :] = xi
        return (jnp.broadcast_to(xr[last:last + 1, :], (T, C)), jnp.broadcast_to(xi[last:last + 1, :], (T, C)))

    z = jnp.zeros((T, C), F32)
    lax.fori_loop(0, nt, body, (z, z))


def _s5_specs(L):
    nb_axis = lambda j: (j, 0, 0)
    u = pl.BlockSpec((L, S5_BLOCK_CH), lambda j: (0, j))
    wb = pl.BlockSpec((None, S5_BLOCK_CH, S5_BLOCK_ST), nb_axis)
    a = pl.BlockSpec((1, S5_BLOCK_ST), lambda j: (0, j))
    wc = pl.BlockSpec((None, S5_BLOCK_ST, S5_BLOCK_CH), nb_axis)
    return u, wb, a, wc


def _s5_fwd(u, wbr, wbi, ar, ai, wcr, wci, name):
    L, C = u.shape
    nb = C // S5_BLOCK_CH
    us, wbs, as_, wcs = _s5_specs(L)

    def body(u_ref, wbr_ref, wbi_ref, ar_ref, ai_ref, wcr_ref, wci_ref, y_ref, xr, xi):
        ub = u_ref[...]
        xr[...] = _bdot(ub, wbr_ref[...])
        xi[...] = _bdot(ub, wbi_ref[...])
        _s5_scan(xr, xi, ar_ref[...], ai_ref[...], False)
        y_ref[...] = _bdot(xr[...], wcr_ref[...]) - _bdot(xi[...], wci_ref[...])

    return pl.pallas_call(
        body, grid=(nb,), in_specs=[us, wbs, wbs, as_, as_, wcs, wcs], out_specs=us,
        out_shape=jax.ShapeDtypeStruct((L, C), F32),
        scratch_shapes=[pltpu.VMEM((L, S5_BLOCK_ST), F32)] * 2, name=name, compiler_params=_cp(1))(
            u, wbr, wbi, ar, ai, wcr, wci)


def _dot_t(a, b):
    return lax.dot_general(a.astype(BF16), b.astype(BF16), (((0,), (0,)), ((), ())), preferred_element_type=F32)


def _dot_nt(a, b):
    return lax.dot_general(a.astype(BF16), b.astype(BF16), (((1,), (1,)), ((), ())), preferred_element_type=F32)


def _s5_bwd(u, wbr, wbi, ar, ai, wcr, wci, dy, name):
    L, C = u.shape
    nb = C // S5_BLOCK_CH
    us, wbs, as_, wcs = _s5_specs(L)
    T = SUBLANES

    def body(u_ref, wbr_ref, wbi_ref, ar_ref, ai_ref, wcr_ref, wci_ref, dy_ref,
             du_ref, dwbr_ref, dwbi_ref, dar_ref, dai_ref, dwcr_ref, dwci_ref, xr, xi, gr, gi):
        ub = u_ref[...]
        a_r, a_i = ar_ref[...], ai_ref[...]
        xr[...] = _bdot(ub, wbr_ref[...])
        xi[...] = _bdot(ub, wbi_ref[...])
        _s5_scan(xr, xi, a_r, a_i, False)
        d = dy_ref[...]
        dwcr_ref[...] = _dot_t(xr[...], d)
        dwci_ref[...] = -_dot_t(xi[...], d)
        gr[...] = _dot_nt(d, wcr_ref[...])
        gi[...] = -_dot_nt(d, wci_ref[...])
        _s5_scan(gr, gi, a_r, -a_i, True)

        row = lax.broadcasted_iota(jnp.int32, (T, S5_BLOCK_ST), 0)

        def da_body(i, carry):
            pr, pi, sr, si = carry
            off = pl.multiple_of(i * T, T)
            xr_t, xi_t = xr[pl.ds(off, T), :], xi[pl.ds(off, T), :]
            lr_t, li_t = gr[pl.ds(off, T), :], gi[pl.ds(off, T), :]
            qr = jnp.where(row == 0, pr, pltpu.roll(xr_t, 1, 0))
            qi = jnp.where(row == 0, pi, pltpu.roll(xi_t, 1, 0))
            sr = sr + qr * lr_t + qi * li_t
            si = si + qr * li_t - qi * lr_t
            return (jnp.broadcast_to(xr_t[T - 1:T, :], (T, S5_BLOCK_ST)),
                    jnp.broadcast_to(xi_t[T - 1:T, :], (T, S5_BLOCK_ST)), sr, si)

        z = jnp.zeros((T, S5_BLOCK_ST), F32)
        _, _, sr, si = lax.fori_loop(0, L // T, da_body, (z, z, z, z))
        dar_ref[...] = jnp.sum(sr, axis=0, keepdims=True)
        dai_ref[...] = jnp.sum(si, axis=0, keepdims=True)
        lr, li = gr[...], gi[...]
        dwbr_ref[...] = _dot_t(ub, lr)
        dwbi_ref[...] = _dot_t(ub, li)
        du_ref[...] = _dot_nt(lr, wbr_ref[...]) + _dot_nt(li, wbi_ref[...])

    sds = jax.ShapeDtypeStruct
    return pl.pallas_call(
        body, grid=(nb,), in_specs=[us, wbs, wbs, as_, as_, wcs, wcs, us],
        out_specs=[us, wbs, wbs, as_, as_, wcs, wcs],
        out_shape=[sds(u.shape, F32), sds(wbr.shape, F32), sds(wbi.shape, F32), sds(ar.shape, F32),
                   sds(ai.shape, F32), sds(wcr.shape, F32), sds(wci.shape, F32)],
        scratch_shapes=[pltpu.VMEM((L, S5_BLOCK_ST), F32)] * 4, name=name,
        compiler_params=_cp(1, VMEM_LIMIT_BIG))(u, wbr, wbi, ar, ai, wcr, wci, dy)


def _s5_core(name):
    @jax.custom_vjp
    def op(u, wbr, wbi, ar, ai, wcr, wci):
        return _s5_fwd(u, wbr, wbi, ar, ai, wcr, wci, name + "_fwd")

    def fwd(*a):
        return _s5_fwd(*a, name + "_fwd"), a

    def bwd(a, dy):
        return tuple(_s5_bwd(*a, dy, name + "_bwd"))

    op.defvjp(fwd, bwd)
    return op


def _att_tile_f(first, q, kp, kc, vp, vc, bias):
    nq = q.shape[0]
    hb = bias.shape[0]
    E = q.shape[1] // hb
    r = lax.broadcasted_iota(jnp.int32, (nq, 2 * nq), 0)
    c = lax.broadcasted_iota(jnp.int32, (nq, 2 * nq), 1)
    prev_ok = jnp.logical_and(jnp.logical_and(c < nq, c >= r), jnp.logical_not(first))
    valid = jnp.logical_or(prev_ok, jnp.logical_and(c >= nq, c - nq <= r))
    outs, lses = [], []
    for h in range(hb):
        sl = slice(h * E, (h + 1) * E)
        k = jnp.concatenate([kp[:, sl], kc[:, sl]], axis=0)
        v = jnp.concatenate([vp[:, sl], vc[:, sl]], axis=0)
        s = jnp.where(valid, _dot_nt(q[:, sl], k) * (E ** -0.5) + bias[h], NEG_INF)
        m = jnp.max(s, axis=-1, keepdims=True)
        p = jnp.exp(s - m)
        den = jnp.sum(p, axis=-1, keepdims=True)
        outs.append(_bdot(p, v) / den)
        lses.append(jnp.broadcast_to(m + jnp.log(den), (nq, E)))
    return jnp.concatenate(outs, axis=-1), jnp.concatenate(lses, axis=-1)


def _att_mix_f(*a):
    n = len(a) // 2
    o, l = a[:n], a[n:]
    m = functools.reduce(jnp.maximum, l)
    e = [jnp.exp(li - m) for li in l]
    return sum(ei * oi for ei, oi in zip(e, o)) / sum(e)


def _att_rows(start, dil):
    if dil == 1:
        return pl.ds(pl.multiple_of(start, ATT_BLOCK), ATT_BLOCK)
    return pl.ds(start, ATT_BLOCK, stride=dil)


def _att_blocks(L, dil):
    nb = L // dil // ATT_BLOCK
    return dil * nb, nb


def _att_specs(L, W):
    nblk = W // LANES
    col = lambda off: pl.BlockSpec((L, LANES), lambda j: (0, j + off))
    per_pattern = pl.BlockSpec((len(DILATED_PATTERNS), L, LANES), lambda j: (0, 0, j))
    hb = ATT_HEADS // nblk
    bias = pl.BlockSpec((len(DILATED_PATTERNS), hb, ATT_BLOCK, 2 * ATT_BLOCK), lambda j: (0, j, 0, 0))
    return nblk, col, per_pattern, bias


def _att_fwd(qkv, bias, name):
    L, W3 = qkv.shape
    W = W3 // 3
    nblk, col, per_pattern, bias_spec = _att_specs(L, W)
    P = len(DILATED_PATTERNS)

    def body(q_ref, k_ref, v_ref, b_ref, y_ref, o_ref, l_ref):
        for p, (_, dil) in enumerate(DILATED_PATTERNS):
            n_it, nb = _att_blocks(L, dil)

            def step(i, carry, p=p, dil=dil, nb=nb):
                n = i % nb
                cur = i // nb + n * (ATT_BLOCK * dil)
                prev = i // nb + jnp.maximum(n - 1, 0) * (ATT_BLOCK * dil)
                rc, rp = _att_rows(cur, dil), _att_rows(prev, dil)
                o, l = _att_tile_f(n == 0, q_ref[rc, :], k_ref[rp, :], k_ref[rc, :], v_ref[rp, :], v_ref[rc, :],
                                   b_ref[p])
                o_ref[p, rc, :] = o
                l_ref[p, rc, :] = l
                return carry

            lax.fori_loop(0, n_it, step, 0)

        def mix(i, carry):
            rows = pl.ds(pl.multiple_of(i * ATT_MIX_ROWS, ATT_MIX_ROWS), ATT_MIX_ROWS)
            y_ref[rows, :] = _att_mix_f(*[o_ref[p, rows, :] for p in range(P)], *[l_ref[p, rows, :] for p in range(P)])
            return carry

        lax.fori_loop(0, L // ATT_MIX_ROWS, mix, 0)

    sds = jax.ShapeDtypeStruct
    return pl.pallas_call(
        body, grid=(nblk,), in_specs=[col(0), col(nblk), col(2 * nblk), bias_spec],
        out_specs=[col(0), per_pattern, per_pattern],
        out_shape=[sds((L, W), F32), sds((P, L, W), F32), sds((P, L, W), F32)], name=name,
        compiler_params=_cp(1))(qkv, qkv, qkv, bias)


def _att_bwd(qkv, bias, o_all, l_all, dy, name):
    L, W3 = qkv.shape
    W = W3 // 3
    nblk, col, per_pattern, bias_spec = _att_specs(L, W)
    P = len(DILATED_PATTERNS)

    def body(q_ref, k_ref, v_ref, b_ref, o_ref, l_ref, dy_ref, dq_ref, dk_ref, dv_ref, db_ref, do_s, dl_s):
        def mix(i, carry):
            rows = pl.ds(pl.multiple_of(i * ATT_MIX_ROWS, ATT_MIX_ROWS), ATT_MIX_ROWS)
            _, mix_vjp = jax.vjp(_att_mix_f, *[o_ref[p, rows, :] for p in range(P)],
                                 *[l_ref[p, rows, :] for p in range(P)])
            g = mix_vjp(dy_ref[rows, :])
            for p in range(P):
                do_s[p, rows, :] = g[p]
                dl_s[p, rows, :] = g[P + p]
            return carry

        lax.fori_loop(0, L // ATT_MIX_ROWS, mix, 0)
        for ref in (dq_ref, dk_ref, dv_ref, db_ref):
            ref[...] = jnp.zeros_like(ref)

        def add(ref, rows, val):
            ref[rows, :] = ref[rows, :] + val

        for p, (_, dil) in enumerate(DILATED_PATTERNS):
            n_it, nb = _att_blocks(L, dil)

            def step(i, carry, p=p, dil=dil, nb=nb):
                n = i % nb
                first = n == 0
                cur = i // nb + n * (ATT_BLOCK * dil)
                prev = i // nb + jnp.maximum(n - 1, 0) * (ATT_BLOCK * dil)
                rc, rp = _att_rows(cur, dil), _att_rows(prev, dil)
                _, vjp = jax.vjp(functools.partial(_att_tile_f, first), q_ref[rc, :], k_ref[rp, :], k_ref[rc, :],
                                 v_ref[rp, :], v_ref[rc, :], b_ref[p])
                dq, dkp, dkc, dvp, dvc, db = vjp((do_s[p, rc, :], dl_s[p, rc, :]))
                add(dq_ref, rc, dq)
                add(dk_ref, rc, dkc)
                add(dv_ref, rc, dvc)
                db_ref[p] = db_ref[p] + db

                @pl.when(jnp.logical_not(first))
                def _():
                    add(dk_ref, rp, dkp)
                    add(dv_ref, rp, dvp)

                return carry

            lax.fori_loop(0, n_it, step, 0)

    sds = jax.ShapeDtypeStruct((L, W), F32)
    return pl.pallas_call(
        body, grid=(nblk,),
        in_specs=[col(0), col(nblk), col(2 * nblk), bias_spec, per_pattern, per_pattern, col(0)],
        out_specs=[col(0), col(0), col(0), bias_spec],
        out_shape=[sds, sds, sds, jax.ShapeDtypeStruct(bias.shape, F32)],
        scratch_shapes=[pltpu.VMEM((P, L, LANES), F32)] * 2, name=name, compiler_params=_cp(1, VMEM_LIMIT_BIG))(
            qkv, qkv, qkv, bias, o_all, l_all, dy)


def _dilated_attention(name):
    @jax.custom_vjp
    def op(qkv, bias):
        return _att_fwd(qkv, bias, name + "_fwd")[0]

    def fwd(qkv, bias):
        y, o_all, l_all = _att_fwd(qkv, bias, name + "_fwd")
        return y, (qkv, bias, o_all, l_all)

    def bwd(r, dy):
        dq, dk, dv, db = _att_bwd(*r, dy, name + "_bwd")
        return jnp.concatenate([dq, dk, dv], axis=-1), db

    op.defvjp(fwd, bwd)
    return op


def _t5_bucket(dist):
    n = np.maximum(dist, 0)
    max_exact = REL_BUCKETS // 2
    large = max_exact + (np.log(np.maximum(n, 1) / max_exact) / np.log(REL_MAX_DIST / max_exact)
                         * (REL_BUCKETS - max_exact)).astype(np.int64)
    large = np.minimum(large, REL_BUCKETS - 1)
    return np.where(n < max_exact, n, large).astype(np.int32)


def _bucket_onehot():
    a = np.arange(ATT_BLOCK)[:, None]
    b = np.arange(2 * ATT_BLOCK)[None, :]
    sub = a + ATT_BLOCK - b
    bucket = jnp.asarray(np.stack([_t5_bucket(sub * dil).reshape(-1) for _, dil in DILATED_PATTERNS]))
    ids = jnp.arange(REL_BUCKETS, dtype=jnp.int32)
    return (bucket[:, None, :] == ids[None, :, None]).astype(F32)


def loss_head(h, target, g, name):
    R, D = h.shape
    tr = _tile(R, (256,))

    def body(h_ref, t_ref, g_ref, l_ref, dh_ref, dg_ref):
        def lf(hv, gv):
            y = _rms_f(hv, gv)[0]
            return 0.5 * jnp.sum(jnp.mean(jnp.square(y - t_ref[...]), axis=-1))

        l, (dh, dg) = jax.value_and_grad(lf, argnums=(0, 1))(h_ref[...], g_ref[...])

        @pl.when(pl.program_id(0) == 0)
        def _():
            l_ref[...] = jnp.zeros_like(l_ref)
            dg_ref[...] = jnp.zeros_like(dg_ref)

        dh_ref[...] = dh
        dg_ref[...] += dg
        l_ref[...] += l

    rows = pl.BlockSpec((tr, D), lambda i: (i, 0))
    vec = pl.BlockSpec((1, D), lambda i: (0, 0))
    l, dh, dg = pl.pallas_call(
        body, grid=(R // tr,), in_specs=[rows, rows, vec],
        out_specs=[pl.BlockSpec((SUBLANES, LANES), lambda i: (0, 0)), rows, vec],
        out_shape=[jax.ShapeDtypeStruct((SUBLANES, LANES), F32), jax.ShapeDtypeStruct((R, D), F32),
                   jax.ShapeDtypeStruct((1, D), F32)], name=name, compiler_params=_cp(1))(h, target, g.reshape(1, D))
    return l[0, 0], dh, dg.reshape(D)


def adamw(w, parts, m, v, name):
    R, C = w.shape
    n_parts = parts.shape[0]
    tr = _row_tile(R, C)
    c1 = 1.0 - ADAM_B1 ** ADAM_STEP
    c2 = 1.0 - ADAM_B2 ** ADAM_STEP

    def body(w_ref, p_ref, m_ref, v_ref, g_ref, d_ref, nm_ref, nv_ref):
        g = p_ref[0].astype(F32)
        for i in range(1, n_parts):
            g = g + p_ref[i].astype(F32)
        nm = ADAM_B1 * m_ref[...] + (1.0 - ADAM_B1) * g
        nv = ADAM_B2 * v_ref[...] + (1.0 - ADAM_B2) * jnp.square(g)
        d_ref[...] = -ADAM_LR * ((nm / c1) / (jnp.sqrt(nv / c2) + ADAM_EPS) + ADAM_WD * w_ref[...])
        g_ref[...] = g
        nm_ref[...] = nm
        nv_ref[...] = nv

    rows = pl.BlockSpec((tr, C), lambda i: (i, 0))
    sds = jax.ShapeDtypeStruct((R, C), F32)
    return pl.pallas_call(body, grid=(R // tr,),
                          in_specs=[rows, pl.BlockSpec((n_parts, tr, C), lambda i: (0, i, 0)), rows, rows],
                          out_specs=[rows] * 4, out_shape=[sds] * 4, name=name, compiler_params=_cp(1))(w, parts, m, v)


HBM_SPEC = pl.BlockSpec(memory_space=pltpu.HBM)
MESH_ID = pl.DeviceIdType.MESH


def _place():
    return lax.axis_index("x"), lax.axis_index("y"), lax.axis_index("c")


def _index(x, y, c):
    return 4 * x + 2 * y + c


def all_gather(x, name):
    def body(x_ref, out_ref, send_sems, recv_sems, local_sem):
        x_, y_, c_ = _place()
        me, sibling = (x_, y_, c_), (x_, y_, 1 - c_)
        chips = [(1 - x_, y_), (x_, 1 - y_), (1 - x_, 1 - y_)]

        def slot(px, py, pc):
            return out_ref.at[_index(px, py, pc)]

        def copy(k, block, to, src=None):
            return pltpu.make_async_remote_copy(
                src_ref=slot(*block) if src is None else src, dst_ref=slot(*block),
                send_sem=send_sems.at[k], recv_sem=recv_sems.at[k], device_id=to, device_id_type=MESH_ID)

        mine = pltpu.make_async_copy(x_ref, slot(*me), local_sem)
        mine.start()
        first = [copy(0, me, sibling, src=x_ref)]
        first += [copy(1 + j, me, (*chip, c_), src=x_ref) for j, chip in enumerate(chips)]
        for cp in first:
            cp.start()
        passed = [copy(4 + j, (*chip, c_), sibling) for j, chip in enumerate(chips)]
        for j, chip in enumerate(chips):
            copy(1 + j, (*chip, c_), me).wait_recv()
            passed[j].start()
        copy(0, sibling, me).wait_recv()
        for j, chip in enumerate(chips):
            copy(4 + j, (*chip, 1 - c_), me).wait_recv()
        for cp in first + passed:
            cp.wait_send()
        mine.wait()

    return pl.pallas_call(
        body, out_shape=jax.ShapeDtypeStruct((N_DEV,) + x.shape, x.dtype), in_specs=[HBM_SPEC], out_specs=HBM_SPEC,
        scratch_shapes=[pltpu.SemaphoreType.DMA((N_DEV - 1,)), pltpu.SemaphoreType.DMA((N_DEV - 1,)),
                        pltpu.SemaphoreType.DMA], name=name)(x)


def _chip(x, y):
    return 2 * x + y


def sibling_exchange(x, name):
    nc, _, R, C = x.shape

    def body(x_ref, out_ref, send_sem, recv_sem):
        x_, y_, c_ = _place()
        cp = pltpu.make_async_remote_copy(src_ref=x_ref.at[:, 1 - c_], dst_ref=out_ref, send_sem=send_sem,
                                          recv_sem=recv_sem, device_id=(x_, y_, 1 - c_), device_id_type=MESH_ID)
        cp.start()
        cp.wait()

    return pl.pallas_call(
        body, out_shape=jax.ShapeDtypeStruct((nc, R, C), x.dtype), in_specs=[HBM_SPEC], out_specs=HBM_SPEC,
        scratch_shapes=[pltpu.SemaphoreType.DMA, pltpu.SemaphoreType.DMA], name=name)(x)


def _row_tile(R, C):
    cap = max(SUBLANES, STREAM_BLOCK_BYTES // (4 * C))
    return _tile(R, [t for t in (512, 256, 128, 64, 32, 16, 8) if t <= cap])


def pair_sum(x, recv, name):
    nc, _, R, C = x.shape
    tr = _row_tile(R, C)
    core = lax.axis_index("c").astype(jnp.int32).reshape(1)

    def body(c_ref, a_ref, b_ref, o_ref):
        o_ref[...] = (a_ref[...].astype(F32) + b_ref[...].astype(F32)).astype(o_ref.dtype)

    blk = pl.BlockSpec((None, tr, C), lambda k, i, c_ref: (k, i, 0))
    grid_spec = pltpu.PrefetchScalarGridSpec(
        num_scalar_prefetch=1, grid=(nc, R // tr),
        in_specs=[pl.BlockSpec((None, None, tr, C), lambda k, i, c_ref: (k, c_ref[0], i, 0)), blk], out_specs=blk)
    return pl.pallas_call(body, grid_spec=grid_spec, out_shape=jax.ShapeDtypeStruct((nc, R, C), x.dtype), name=name,
                          compiler_params=_cp(2))(core, x, recv)


def chip_exchange(s, name):
    def body(s_ref, out_ref, send_sems, recv_sems, local_sem):
        x_, y_, c_ = _place()
        me = _chip(x_, y_)
        local = pltpu.make_async_copy(s_ref.at[me], out_ref.at[me], local_sem)
        local.start()
        copies = []
        for k in range(1, N_CHIP):
            px = 1 - x_ if k & 2 else x_
            py = 1 - y_ if k & 1 else y_
            cp = pltpu.make_async_remote_copy(
                src_ref=s_ref.at[_chip(px, py)], dst_ref=out_ref.at[me], send_sem=send_sems.at[k - 1],
                recv_sem=recv_sems.at[k - 1], device_id=(px, py, c_), device_id_type=MESH_ID)
            cp.start()
            copies.append(cp)
        for cp in copies:
            cp.wait()
        local.wait()

    return pl.pallas_call(
        body, out_shape=jax.ShapeDtypeStruct(s.shape, s.dtype), in_specs=[HBM_SPEC], out_specs=HBM_SPEC,
        scratch_shapes=[pltpu.SemaphoreType.DMA((N_CHIP - 1,)), pltpu.SemaphoreType.DMA((N_CHIP - 1,)),
                        pltpu.SemaphoreType.DMA], name=name)(s)


def _block_diag(w, nb):
    G, a, b = w.shape
    gp = G // nb
    eye = jnp.eye(gp, dtype=w.dtype)
    return jnp.einsum('jgab,gh->jgahb', w.reshape(nb, gp, a, b), eye).reshape(nb, gp * a, gp * b)


def _split_columns(x, cuts):
    edges = (0,) + tuple(cuts) + (x.shape[1],)

    def split(x):
        return tuple(x[:, a:b] for a, b in zip(edges[:-1], edges[1:]))

    op = jax.custom_vjp(split)
    op.defvjp(lambda x: (split(x), None), lambda _, cts: (jnp.concatenate(cts, axis=-1),))
    return op(x)


def _layer(l, h, memn, bias_tabs, P):
    nm = lambda s: f"l{l}_{s}"
    L, D = h.shape
    GW = D // N_MIXERS
    G = GW // S5_CH_PER_GROUP

    row = lambda g: g.reshape(1, D)

    proj = _act_linear(nm("w_in"), _rms_op(nm("norm_mix"), L, D, BF16), 2, False)(h, row(P['norm_mix_g']), P['w_in'])
    u_a, u_b, u_c, qkv = _split_columns(proj, (GW, 2 * GW, 4 * GW))

    v3 = lambda a: a.reshape(G, 1, S5_STATE)
    log_dt = jnp.broadcast_to(P['s5_log_dt'][:, None, None], (G, 1, S5_STATE))
    a_r, a_i, bb_r, bb_i = s5_discretise(v3(P['s5_lam_re']), v3(P['s5_lam_im']), log_dt,
                                         P['s5_b_re'].transpose(0, 2, 1), P['s5_b_im'].transpose(0, 2, 1), nm("s5_disc"))
    nblk = GW // S5_BLOCK_CH
    y_s5 = _s5_core(nm("s5_core"))(
        u_a, _block_diag(bb_r, nblk), _block_diag(bb_i, nblk), a_r.reshape(1, G * S5_STATE), a_i.reshape(1, G * S5_STATE),
        _block_diag(P['s5_c_re'].transpose(0, 2, 1), nblk), _block_diag(P['s5_c_im'].transpose(0, 2, 1), nblk))
    y_a = s5_epilogue(y_s5, u_a, P['s5_d'], P['s5_w_glu'], nm("s5_glu"))

    y_b = pool_proj(_pool_mix(nm("pool_mix"))(u_b), P['pool_w'], P['pool_scale'], nm("pool_proj"))

    hc = _glu_conv(nm("conv_dw"))(u_c, P['conv_w_dw'], P['conv_b_dw'].reshape(1, GW))
    y_c = conv_post(hc, P['conv_ln_g'], P['conv_ln_b'], P['conv_w_pw'], nm("conv_post"))

    y_d = _dilated_attention(nm("att"))(qkv, bias_tabs)

    grp = _group_norm_op(nm("grp_norm"), L, GW, N_MIXERS, BF16)
    h = _act_linear(nm("w_out"), grp, N_MIXERS + 1, True)(y_a, y_b, y_c, y_d, row(P['grp_norm_g']), P['w_out'], h)

    xq = _act_linear(nm("w_xq"), _rms_op(nm("norm_x"), L, D, BF16), 2, False)(h, row(P['norm_x_g']), P['w_xq'])
    xk = _linear(nm("w_xk"))(memn, P['w_xk'])
    xv = _linear(nm("w_xv"))(memn, P['w_xv'])
    xat = _cross_attention_op(nm("xattn"), L, xq.shape[1], memn.shape[0], BF16)
    h = _act_linear(nm("w_xo"), xat, 3, True)(xq, xk, xv, P['w_xo'], h)

    return _mlp(nm("mlp"), _rms_op(nm("norm_mlp"), L, D, BF16))(h, row(P['norm_mlp_g']), P['w_up'], P['w_down'])


def _trunk(x, mem, W):
    memn = rmsnorm(mem, W['mem_norm_g'], "mem_norm")
    tabs = rel_bias_tables(W['rel_bias'], _bucket_onehot(), "rel_bias")
    bias_tabs = tabs.reshape(len(DILATED_PATTERNS), ATT_HEADS, ATT_BLOCK, 2 * ATT_BLOCK)
    h = x
    for l in range(DEPTH):
        P = {n: W[n][l] for n in WEIGHTS if n not in ('rel_bias', 'mem_norm_g', 'norm_final_g')}
        h = _layer(l, h, memn, bias_tabs, P)
    return h


def _gather_weight(name, w):
    ax = SHARDED[name]
    dt = BF16 if name in GATHER_BF16 else F32
    nl, a, b = w.shape
    g = all_gather(w.astype(dt).reshape(nl * a, b), "ag_" + name).reshape(N_DEV, nl, a, b)
    if ax == 1:
        return g.transpose(1, 0, 2, 3).reshape(nl, N_DEV * a, b)
    return g.transpose(1, 2, 0, 3).reshape(nl, a, N_DEV * b)


def _scatter_grad(name, g):
    ax = SHARDED[name]
    nl = g.shape[0]
    if ax == 1:
        a, b = g.shape[1] // N_DEV, g.shape[2]
        s = g.reshape(nl, N_DEV, a, b).transpose(1, 0, 2, 3)
    else:
        a, b = g.shape[1], g.shape[2] // N_DEV
        s = g.reshape(nl, a, N_DEV, b).transpose(2, 0, 1, 3)
    s = s.reshape(N_CHIP, N_DEV // N_CHIP, nl * a, b)
    pair = pair_sum(s, sibling_exchange(s, "d2d_" + name), "pairsum_" + name)
    return chip_exchange(pair, "ici_" + name)


def _flatten_small(d):
    flat = jnp.concatenate([d[n].reshape(-1).astype(F32) for n in SMALL])
    pad = (-flat.shape[0]) % (LANES * SMALL_ROW_TILE)
    return jnp.pad(flat, (0, pad)).reshape(-1, LANES)


def _split_small(flat, like):
    flat = flat.reshape(-1)
    out, off = {}, 0
    for n in SMALL:
        sz = math.prod(like[n].shape)
        out[n] = flat[off:off + sz].reshape(like[n].shape)
        off += sz
    return out


def kernel(x, mem, rel_bias, mem_norm_g, norm_mix_g, w_in, s5_lam_re, s5_lam_im, s5_log_dt, s5_b_re, s5_b_im, s5_c_re, s5_c_im, s5_d, s5_w_glu, pool_w, pool_scale, conv_w_dw, conv_b_dw, conv_ln_g, conv_ln_b, conv_w_pw, grp_norm_g, w_out, norm_x_g, w_xq, w_xk, w_xv, w_xo, norm_mlp_g, w_up, w_down, norm_final_g, loss_target, m_rel_bias, m_mem_norm_g, m_norm_mix_g, m_w_in, m_s5_lam_re, m_s5_lam_im, m_s5_log_dt, m_s5_b_re, m_s5_b_im, m_s5_c_re, m_s5_c_im, m_s5_d, m_s5_w_glu, m_pool_w, m_pool_scale, m_conv_w_dw, m_conv_b_dw, m_conv_ln_g, m_conv_ln_b, m_conv_w_pw, m_grp_norm_g, m_w_out, m_norm_x_g, m_w_xq, m_w_xk, m_w_xv, m_w_xo, m_norm_mlp_g, m_w_up, m_w_down, m_norm_final_g, v_rel_bias, v_mem_norm_g, v_norm_mix_g, v_w_in, v_s5_lam_re, v_s5_lam_im, v_s5_log_dt, v_s5_b_re, v_s5_b_im, v_s5_c_re, v_s5_c_im, v_s5_d, v_s5_w_glu, v_pool_w, v_pool_scale, v_conv_w_dw, v_conv_b_dw, v_conv_ln_g, v_conv_ln_b, v_conv_w_pw, v_grp_norm_g, v_w_out, v_norm_x_g, v_w_xq, v_w_xk, v_w_xv, v_w_xo, v_norm_mlp_g, v_w_up, v_w_down, v_norm_final_g):
    w = dict(zip(WEIGHTS, (rel_bias, mem_norm_g, norm_mix_g, w_in, s5_lam_re, s5_lam_im, s5_log_dt, s5_b_re, s5_b_im, s5_c_re, s5_c_im, s5_d, s5_w_glu, pool_w, pool_scale, conv_w_dw, conv_b_dw, conv_ln_g, conv_ln_b, conv_w_pw, grp_norm_g, w_out, norm_x_g, w_xq, w_xk, w_xv, w_xo, norm_mlp_g, w_up, w_down, norm_final_g)))
    m = dict(zip(WEIGHTS, (m_rel_bias, m_mem_norm_g, m_norm_mix_g, m_w_in, m_s5_lam_re, m_s5_lam_im, m_s5_log_dt, m_s5_b_re, m_s5_b_im, m_s5_c_re, m_s5_c_im, m_s5_d, m_s5_w_glu, m_pool_w, m_pool_scale, m_conv_w_dw, m_conv_b_dw, m_conv_ln_g, m_conv_ln_b, m_conv_w_pw, m_grp_norm_g, m_w_out, m_norm_x_g, m_w_xq, m_w_xk, m_w_xv, m_w_xo, m_norm_mlp_g, m_w_up, m_w_down, m_norm_final_g)))
    v = dict(zip(WEIGHTS, (v_rel_bias, v_mem_norm_g, v_norm_mix_g, v_w_in, v_s5_lam_re, v_s5_lam_im, v_s5_log_dt, v_s5_b_re, v_s5_b_im, v_s5_c_re, v_s5_c_im, v_s5_d, v_s5_w_glu, v_pool_w, v_pool_scale, v_conv_w_dw, v_conv_b_dw, v_conv_ln_g, v_conv_ln_b, v_conv_w_pw, v_grp_norm_g, v_w_out, v_norm_x_g, v_w_xq, v_w_xk, v_w_xv, v_w_xo, v_norm_mlp_g, v_w_up, v_w_down, v_norm_final_g)))

    full = {n: (_gather_weight(n, w[n]) if n in SHARDED else w[n]) for n in WEIGHTS if n != 'norm_final_g'}

    h, trunk_vjp = jax.vjp(_trunk, x[0], mem[0], full)
    loss_local, dh, d_final_g = loss_head(h, loss_target[0], w['norm_final_g'], "loss_head")
    dx, _, dfull = trunk_vjp(dh)
    dfull['norm_final_g'] = d_final_g
    loss = lax.psum(loss_local, MESH_AXES)

    grads, deltas, new_m, new_v = {}, {}, {}, {}
    for n in SHARDED:
        parts = _scatter_grad(n, dfull[n])
        shp = w[n].shape
        two_d = lambda a: a.reshape(shp[0] * shp[1], shp[2])
        res = adamw(two_d(w[n]), parts, two_d(m[n]), two_d(v[n]), "adamw_" + n)
        grads[n], deltas[n], new_m[n], new_v[n] = (r.reshape(shp) for r in res)

    parts = all_gather(_flatten_small(dfull), "ag_small_grads")
    res = adamw(_flatten_small(w), parts, _flatten_small(m), _flatten_small(v), "adamw_small")
    for dst, r in zip((grads, deltas, new_m, new_v), res):
        dst.update(_split_small(r, w))

    return (loss, dx[None], *[grads[n] for n in WEIGHTS], *[deltas[n] for n in WEIGHTS],
            *[new_m[n] for n in WEIGHTS], *[new_v[n] for n in WEIGHTS])
```

```python
import functools
import math

import numpy as np
import jax
import jax.numpy as jnp
from jax import lax
from jax.experimental import pallas as pl
from jax.experimental.pallas import tpu as pltpu

F32 = jnp.float32
BF16 = jnp.bfloat16

DEPTH = 4
N_MIXERS = 4
S5_CH_PER_GROUP = 16
S5_STATE = 64
POOL_WINDOWS = (2, 4, 8, 16)
CONV_WIDTH = 31
ATT_HEADS = 8
DILATED_PATTERNS = ((128, 1), (512, 4), (2048, 16))
ATT_BLOCK = 128
ATT_MIX_ROWS = 256
REL_BUCKETS = 32
REL_MAX_DIST = 2048
X_HEADS = 4
X_HEAD_DIM = 128
NORM_EPS = 1e-6
NEG_INF = -1e30
ADAM_LR = 0.001
ADAM_B1 = 0.9
ADAM_B2 = 0.999
ADAM_EPS = 1e-08
ADAM_WD = 0.01
ADAM_STEP = 10

LANES = 128
SUBLANES = 8
VMEM_BYTES = 64 * 1024 * 1024
VMEM_LIMIT = (VMEM_BYTES * 3) // 4
VMEM_LIMIT_BIG = (VMEM_BYTES * 7) // 8
STREAM_BLOCK_BYTES = 1024 * 1024
SMALL_ROW_TILE = 512
N_DEV = 8
N_CHIP = 4
MESH_AXES = ("x", "y", "c")

WEIGHTS = ['rel_bias', 'mem_norm_g', 'norm_mix_g', 'w_in', 's5_lam_re', 's5_lam_im', 's5_log_dt', 's5_b_re',
           's5_b_im', 's5_c_re', 's5_c_im', 's5_d', 's5_w_glu', 'pool_w', 'pool_scale', 'conv_w_dw', 'conv_b_dw',
           'conv_ln_g', 'conv_ln_b', 'conv_w_pw', 'grp_norm_g', 'w_out', 'norm_x_g', 'w_xq', 'w_xk', 'w_xv', 'w_xo',
           'norm_mlp_g', 'w_up', 'w_down', 'norm_final_g']
SHARDED = {'w_in': 2, 's5_w_glu': 1, 'conv_w_dw': 2, 'conv_w_pw': 1, 'w_out': 1, 'w_xq': 1, 'w_xk': 1, 'w_xv': 1,
           'w_xo': 2, 'w_up': 2, 'w_down': 1}
GATHER_BF16 = ('w_in', 'w_out', 'w_xq', 'w_xk', 'w_xv', 'w_xo', 'w_up', 'w_down')
SMALL = [n for n in WEIGHTS if n not in SHARDED]
LAYER_WEIGHTS = [n for n in WEIGHTS if n not in ('rel_bias', 'mem_norm_g', 'norm_final_g')]
FRONT_WEIGHTS = [n for n in LAYER_WEIGHTS if n not in ('norm_mlp_g', 'w_up', 'w_down')]
PACK_COLS = 512


def _cp(n_axes, vmem=VMEM_LIMIT):
    return pltpu.CompilerParams(dimension_semantics=("arbitrary",) * n_axes, vmem_limit_bytes=vmem)


def _tile(n, prefs):
    for t in prefs:
        if n % t == 0:
            return t
    return n


MM_TILE = 1024
MM_TILE_K = 2048
MM_FULL_K = 4096


def _chip_exchange_copies(src, dst, rows, send_sems, recv_sems, local_sem):
    x_, y_, c_ = _place()
    me = _chip(x_, y_)
    copies = [pltpu.make_async_copy(src.at[me, rows], dst.at[me, rows], local_sem)]
    for k in range(1, N_CHIP):
        px = 1 - x_ if k & 2 else x_
        py = 1 - y_ if k & 1 else y_
        copies.append(pltpu.make_async_remote_copy(
            src_ref=src.at[_chip(px, py), rows], dst_ref=dst.at[me, rows], send_sem=send_sems.at[k - 1],
            recv_sem=recv_sems.at[k - 1], device_id=(px, py, c_), device_id_type=MESH_ID))
    return copies


def _mm(a, b, *, ta=False, tb=False, res=None, out_dtype=F32, epilogue=None, pre=None, rider=None, name):
    if ta:
        K, M = a.shape
    else:
        M, K = a.shape
    if tb:
        N, K2 = b.shape
    else:
        K2, N = b.shape
    assert K == K2, (a.shape, b.shape, ta, tb)
    wide_f32 = K >= MM_TILE_K and F32 in (a.dtype, b.dtype)
    tm = _tile(M, (MM_TILE // 2 if wide_f32 and not ta else MM_TILE, 512, 256, 128))
    tn = _tile(N, (MM_TILE, 512, 256, 128))
    tk = K if K <= MM_FULL_K else _tile(K, (MM_TILE_K, 1024, 512, 256, 128))
    nk = K // tk
    a_spec = pl.BlockSpec((tk, tm), lambda i, j, k: (k, i)) if ta else pl.BlockSpec((tm, tk), lambda i, j, k: (i, k))
    b_spec = pl.BlockSpec((tn, tk), lambda i, j, k: (j, k)) if tb else pl.BlockSpec((tk, tn), lambda i, j, k: (k, j))
    o_spec = pl.BlockSpec((tm, tn), lambda i, j, k: (i, j))
    dn = (((0 if ta else 1,), (1 if tb else 0,)), ((), ()))
    extra = [x for x in (res, pre) if x is not None]
    assert not (res is not None and pre is not None)
    n_out = 2 if epilogue == 'relu_sq' else 1
    n_ride = 2 if rider is not None else 0
    grid = (M // tm, N // tn, nk)

    def body(*refs):
        a_ref, b_ref = refs[:2]
        x_ref = refs[2] if extra else None
        n_in = 2 + len(extra) + n_ride
        o_refs = refs[n_in:n_in + n_out]
        scratch = refs[n_in + n_out + n_ride // 2:]
        acc = scratch[0] if nk > 1 else None
        if rider is not None:
            rows = pl.ds(rider[2], rider[3])
            ride = lambda: _chip_exchange_copies(refs[n_in - 2], refs[n_in + n_out], rows, *scratch[-3:])
            ids = [pl.program_id(ax) for ax in range(3)]
            at_start = functools.reduce(jnp.logical_and, [i == 0 for i in ids])
            at_end = functools.reduce(jnp.logical_and, [i == g - 1 for i, g in zip(ids, grid)])

            @pl.when(at_start)
            def _():
                for cp in ride():
                    cp.start()

        def finish(r):
            if res is not None:
                r = r + x_ref[...]
            if epilogue == 'relu_sq':
                o_refs[0][...] = r
                o_refs[1][...] = jnp.square(jnp.maximum(r, 0.0)).astype(out_dtype)
            elif epilogue == 'relu_sq_grad':
                o_refs[0][...] = (r * (2.0 * jnp.maximum(x_ref[...], 0.0))).astype(out_dtype)
            else:
                o_refs[0][...] = r.astype(out_dtype)

        part = lax.dot_general(a_ref[...].astype(BF16), b_ref[...].astype(BF16), dn, preferred_element_type=F32)
        if nk == 1:
            finish(part)
        else:
            k = pl.program_id(2)

            @pl.when(k == 0)
            def _():
                acc[...] = part

            @pl.when(k > 0)
            def _():
                acc[...] += part

            @pl.when(k == nk - 1)
            def _():
                finish(acc[...])

        if rider is not None:
            @pl.when(at_end)
            def _():
                for cp in ride():
                    cp.wait()

    out_shape = [jax.ShapeDtypeStruct((M, N), F32 if epilogue == 'relu_sq' else out_dtype)]
    if n_out == 2:
        out_shape.append(jax.ShapeDtypeStruct((M, N), out_dtype))
    in_specs = [a_spec, b_spec] + [o_spec] * len(extra)
    out_specs = [o_spec] * n_out
    scratch = [pltpu.VMEM((tm, tn), F32)] if nk > 1 else []
    args, aliases = (a, b, *extra), {}
    if rider is not None:
        src, dst = rider[0], rider[1]
        in_specs += [HBM_SPEC, HBM_SPEC]
        out_specs.append(HBM_SPEC)
        out_shape.append(jax.ShapeDtypeStruct(dst.shape, dst.dtype))
        scratch += [pltpu.SemaphoreType.DMA((N_CHIP - 1,)), pltpu.SemaphoreType.DMA((N_CHIP - 1,)),
                    pltpu.SemaphoreType.DMA]
        aliases = {len(args) + 1: n_out}
        args += (src, dst)
    outs = pl.pallas_call(
        body, grid=grid, in_specs=in_specs, out_specs=out_specs, out_shape=out_shape, scratch_shapes=scratch,
        input_output_aliases=aliases, name=name, compiler_params=_cp(3, VMEM_LIMIT_BIG))(*args)
    return outs[0] if len(outs) == 1 else tuple(outs)


def _linear(name):
    @jax.custom_vjp
    def lin(a, w):
        return _mm(a, w, name=name + "_fwd")

    def fwd(a, w):
        return _mm(a, w, name=name + "_fwd"), (a, w)

    def bwd(r, dy):
        a, w = r
        da = _mm(dy, w, tb=True, name=name + "_dx")
        dw = _mm(a, dy, ta=True, out_dtype=w.dtype, name=name + "_dw")
        return da, dw

    lin.defvjp(fwd, bwd)
    return lin


def _act_linear(name, act, n_in, with_res):
    def run(*a):
        ins, w = a[:n_in], a[n_in]
        x = act.fwd_call(*ins)[0]
        return _mm(x, w, res=a[n_in + 1] if with_res else None, name=name + "_fwd"), (ins, x, w)

    @jax.custom_vjp
    def op(*a):
        return run(*a)[0]

    def bwd(r, dy):
        ins, x, w = r
        dx = _mm(dy, w, tb=True, name=name + "_dx")
        dw = _mm(x, dy, ta=True, out_dtype=w.dtype, name=name + "_dw")
        return (*act.bwd_all(ins, (dx,)), dw) + ((dy,) if with_res else ())

    op.defvjp(run, bwd)
    return op


def _mlp_fwd(name, norm, h, g, w_up, w_down):
    hn = norm.fwd_call(h, g)[0]
    a, r = _mm(hn, w_up, epilogue='relu_sq', out_dtype=BF16, name=name + "_up_fwd")
    return _mm(r, w_down, res=h, name=name + "_down_fwd"), (h, g, hn, a, r, w_up, w_down)


MLP_BWD_MATMULS = 4


def _mlp_bwd(name, norm, saved, dy, pending):
    h, g, hn, a, r, w_up, w_down = saved
    land = None if pending is None else lax.empty(pending.shape, pending.dtype)
    quarter = None if pending is None else pending.shape[1] // MLP_BWD_MATMULS

    def mm(i, *args, **kw):
        nonlocal land
        if pending is None:
            return _mm(*args, **kw)
        *out, land = _mm(*args, rider=(pending, land, i * quarter, quarter), **kw)
        return out[0]

    da = mm(0, dy, w_down, tb=True, epilogue='relu_sq_grad', pre=a, out_dtype=BF16, name=name + "_down_dx")
    dw_down = mm(1, r, dy, ta=True, out_dtype=w_down.dtype, name=name + "_down_dw")
    dhn = mm(2, da, w_up, tb=True, name=name + "_up_dx")
    dw_up = mm(3, hn, da, ta=True, out_dtype=w_up.dtype, name=name + "_up_dw")
    dh, dg = norm.bwd_all((h, g), (dhn,))
    return dh + dy, dg, dw_up, dw_down, land


def _block_op(name, f, grid, ins, outs, vmem=VMEM_LIMIT):
    n_in, n_out = len(ins), len(outs)
    in_specs = [pl.BlockSpec(bs, im) for bs, im, _, _ in ins]
    out_specs = [pl.BlockSpec(bs, im) for _, _, bs, im in outs]
    out_shape = [jax.ShapeDtypeStruct(s, d) for s, d, _, _ in outs]
    didx = [i for i in range(n_in) if ins[i][3]]

    def fwd_call(*args):
        def body(*refs):
            res = f(*[r[...] for r in refs[:n_in]])
            for r, o in zip(refs[n_in:], res):
                r[...] = o.astype(r.dtype)

        return pl.pallas_call(body, grid=grid, in_specs=in_specs, out_specs=out_specs, out_shape=out_shape,
                              name=name + "_fwd", compiler_params=_cp(len(grid), vmem))(*args)

    def bwd_call(args, cts):
        def body(*refs):
            vals = [r[...] for r in refs[:n_in]]
            ct_refs = refs[n_in:n_in + n_out]
            g_refs = refs[n_in + n_out:]

            def fd(*dv):
                full = list(vals)
                for i, v in zip(didx, dv):
                    full[i] = v
                return f(*full)

            _, vjp = jax.vjp(fd, *[vals[i] for i in didx])
            grads = vjp(tuple(r[...] for r in ct_refs))
            for gref, i, g in zip(g_refs, didx, grads):
                acc = ins[i][2]
                if acc:
                    first = functools.reduce(jnp.logical_and, [pl.program_id(ax) == 0 for ax in acc])

                    @pl.when(first)
                    def _(gref=gref):
                        gref[...] = jnp.zeros_like(gref)

                    gref[...] += g.astype(gref.dtype)
                else:
                    gref[...] = g.astype(gref.dtype)

        g_specs = [pl.BlockSpec(ins[i][0], ins[i][1]) for i in didx]
        g_shape = [jax.ShapeDtypeStruct(args[i].shape, args[i].dtype) for i in didx]
        return pl.pallas_call(body, grid=grid, in_specs=in_specs + out_specs, out_specs=g_specs, out_shape=g_shape,
                              name=name + "_bwd", compiler_params=_cp(len(grid), vmem))(*args, *cts)

    @jax.custom_vjp
    def op(*args):
        return tuple(fwd_call(*args))

    def op_fwd(*args):
        return tuple(fwd_call(*args)), args

    def op_bwd(args, cts):
        it = iter(bwd_call(args, cts))
        return tuple(next(it) if ins[i][3] else jnp.zeros_like(args[i]) for i in range(n_in))

    op.defvjp(op_fwd, op_bwd)
    op.fwd_call = fwd_call
    op.bwd_all = op_bwd
    return op


def _row(tr, c):
    return ((tr, c), lambda i: (i, 0), None, True)


def _par(shape):
    nd = len(shape)
    return (shape, lambda i: (0,) * nd, (0,), True)


def _bdot(a, w):
    return jnp.dot(a.astype(BF16), w.astype(BF16), preferred_element_type=F32)


def _rms_f(x, g):
    return (x * lax.rsqrt(jnp.mean(x * x, axis=-1, keepdims=True) + NORM_EPS) * g,)


def _rms_op(name, R, D, out_dtype):
    tr = _tile(R, (256,))
    return _block_op(name, _rms_f, (R // tr,), [_row(tr, D), _par((1, D))],
                     [((R, D), out_dtype, (tr, D), lambda i: (i, 0))])


def rmsnorm(x, g, name):
    R, D = x.shape
    return _rms_op(name, R, D, F32)(x, g.reshape(1, D))[0]


def s5_epilogue(yc, u, d, w_glu, name):
    R, C = yc.shape
    tr = _tile(R, (256,))

    def f(yc, u, d, w):
        g = jax.nn.gelu(yc + d * u)
        return (g * jax.nn.sigmoid(_bdot(g, w)),)

    op = _block_op(name, f, (R // tr,), [_row(tr, C), _row(tr, C), _par((1, C)), _par((C, C))],
                   [((R, C), F32, (tr, C), lambda i: (i, 0))])
    return op(yc, u, d.reshape(1, C), w_glu)[0]


def pool_proj(p, w, scale, name):
    R, C = p.shape
    ng, pc, _ = w.shape
    tr = _tile(R, (256,))

    def f(p, w, s):
        ys = [_bdot(p[:, g * pc:(g + 1) * pc], w[g]) for g in range(ng)]
        return (jnp.concatenate(ys, axis=-1) * s,)

    op = _block_op(name, f, (R // tr,), [_row(tr, C), _par((ng, pc, pc)), _par((1, C))],
                   [((R, C), F32, (tr, C), lambda i: (i, 0))])
    return op(p, w, scale.reshape(1, C))[0]


def conv_post(h, ln_g, ln_b, w_pw, name):
    R, C = h.shape
    tr = _tile(R, (256,))

    def f(h, g, b, w):
        hc = h - jnp.mean(h, axis=-1, keepdims=True)
        y = hc * lax.rsqrt(jnp.mean(hc * hc, axis=-1, keepdims=True) + NORM_EPS) * g + b
        return (_bdot(jax.nn.silu(y), w),)

    op = _block_op(name, f, (R // tr,), [_row(tr, C), _par((1, C)), _par((1, C)), _par((C, C))],
                   [((R, C), F32, (tr, C), lambda i: (i, 0))])
    return op(h, ln_g.reshape(1, C), ln_b.reshape(1, C), w_pw)[0]


def _group_norm_op(name, R, C, n, out_dtype):
    tr = _tile(R, (256,))

    def f(*a):
        g = a[n]
        parts = [y * lax.rsqrt(jnp.mean(y * y, axis=-1, keepdims=True) + NORM_EPS) for y in a[:n]]
        return (jnp.concatenate(parts, axis=-1) * g,)

    return _block_op(name, f, (R // tr,), [_row(tr, C)] * n + [_par((1, n * C))],
                     [((R, n * C), out_dtype, (tr, n * C), lambda i: (i, 0))])


def _cross_attention_op(name, L, W, M, out_dtype):
    E = X_HEAD_DIM
    tq = _tile(L, (512,))

    def f(q, k, v):
        s = lax.dot_general(q.astype(BF16), k.astype(BF16), (((1,), (1,)), ((), ())),
                            preferred_element_type=F32) * (E ** -0.5)
        p = jax.nn.softmax(s, axis=-1)
        return (_bdot(p, v),)

    qspec = ((tq, E), lambda h, i: (i, h), None, True)
    kspec = ((M, E), lambda h, i: (0, h), (1,), True)
    return _block_op(name, f, (W // E, L // tq), [qspec, kspec, kspec],
                     [((L, W), out_dtype, (tq, E), lambda h, i: (i, h))])


def cross_attention(q, k, v, name):
    return _cross_attention_op(name, q.shape[0], q.shape[1], k.shape[0], F32)(q, k, v)[0]


def s5_discretise(lam_re, lam_im, log_dt, b_re_t, b_im_t, name):
    G, _, N = lam_re.shape
    C = b_re_t.shape[1]

    def f(lr, li, ldt, br, bi):
        dt = jnp.exp(ldt)
        mag = jnp.exp(lr * dt)
        ab_r, ab_i = mag * jnp.cos(li * dt), mag * jnp.sin(li * dt)
        den = lr * lr + li * li
        nr, ni = ab_r - 1.0, ab_i
        f_r = (nr * lr + ni * li) / den
        f_i = (ni * lr - nr * li) / den
        return ab_r, ab_i, f_r * br - f_i * bi, f_r * bi + f_i * br

    vec = ((G, 1, N), lambda i: (0, 0, 0), None, True)
    mat = ((G, C, N), lambda i: (0, 0, 0), None, True)
    ov = ((G, 1, N), F32, (G, 1, N), lambda i: (0, 0, 0))
    om = ((G, C, N), F32, (G, C, N), lambda i: (0, 0, 0))
    op = _block_op(name, f, (1,), [vec, vec, vec, mat, mat], [ov, ov, om, om])
    return op(lam_re, lam_im, log_dt, b_re_t, b_im_t)


def rel_bias_tables(rel_bias, onehot, name):
    B, H = rel_bias.shape
    P, _, Q = onehot.shape

    def f(rbt, oh):
        return (jnp.dot(rbt, oh, precision=lax.Precision.HIGHEST, preferred_element_type=F32),)

    op = _block_op(name, f, (P,), [((H, B), lambda p: (0, 0), (0,), True), ((None, B, Q), lambda p: (p, 0, 0), None, False)],
                   [((P, H, Q), F32, (None, H, Q), lambda p: (p, 0, 0))])
    return op(rel_bias.T, onehot)[0]


def _shift_down(x, s, row):
    return jnp.where(row >= s, pltpu.roll(x, s, 0), 0.0)


def _shift_up(x, s, row):
    n = x.shape[0]
    return jnp.where(row < n - s, pltpu.roll(x, n - s, 0), 0.0)


def _window_sum(x, w, row, shift):
    span = 1
    while span < w:
        x = x + shift(x, span, row)
        span *= 2
    return x


def _pool_call(u, d_out, name):
    L, C = u.shape
    pc = C // len(POOL_WINDOWS)
    assert pc % LANES == 0

    def body(x_ref, o_ref):
        row = lax.broadcasted_iota(jnp.int32, (L, pc), 0)
        for g, w in enumerate(POOL_WINDOWS):
            sl = slice(g * pc, (g + 1) * pc)
            x = x_ref[:, sl]
            cnt = jnp.minimum(row + 1, w).astype(F32)
            if d_out is None:
                o_ref[:, sl] = _window_sum(x, w, row, _shift_down) / cnt - x
            else:
                o_ref[:, sl] = _window_sum(x / cnt, w, row, _shift_up) - x

    src = u if d_out is None else d_out
    return pl.pallas_call(body, out_shape=jax.ShapeDtypeStruct((L, C), F32), name=name,
                          compiler_params=pltpu.CompilerParams(vmem_limit_bytes=VMEM_LIMIT))(src)


def _pool_mix(name):
    @jax.custom_vjp
    def op(u):
        return _pool_call(u, None, name + "_fwd")

    def fwd(u):
        return _pool_call(u, None, name + "_fwd"), u

    def bwd(u, dp):
        return (_pool_call(u, dp, name + "_bwd"),)

    op.defvjp(fwd, bwd)
    return op


def _conv_fwd(u, w, b, name):
    L, C2 = u.shape
    C = C2 // 2
    K = w.shape[0]
    nb = C // LANES

    def body(val_ref, gate_ref, w_ref, b_ref, o_ref):
        row = lax.broadcasted_iota(jnp.int32, (L, LANES), 0)
        h = val_ref[...] * jax.nn.sigmoid(gate_ref[...])
        acc = jnp.broadcast_to(b_ref[...], (L, LANES))
        for k in range(K):
            acc = acc + w_ref[k:k + 1, :] * _shift_down(h, K - 1 - k, row)
        o_ref[...] = acc

    blk = lambda off: pl.BlockSpec((L, LANES), lambda j: (0, j + off))
    return pl.pallas_call(
        body, grid=(nb,), in_specs=[blk(0), blk(nb), pl.BlockSpec((K, LANES), lambda j: (0, j)),
                                    pl.BlockSpec((1, LANES), lambda j: (0, j))],
        out_specs=blk(0), out_shape=jax.ShapeDtypeStruct((L, C), F32), name=name, compiler_params=_cp(1))(u, u, w, b)


def _conv_bwd(u, w, dh, name):
    L, C2 = u.shape
    C = C2 // 2
    K = w.shape[0]
    nb = C // LANES

    def body(val_ref, gate_ref, w_ref, dh_ref, dval_ref, dgate_ref, dw_ref, db_ref):
        row = lax.broadcasted_iota(jnp.int32, (L, LANES), 0)
        val = val_ref[...]
        sig = jax.nn.sigmoid(gate_ref[...])
        h = val * sig
        d = dh_ref[...]
        dh0 = jnp.zeros((L, LANES), F32)
        for k in range(K):
            s = K - 1 - k
            dh0 = dh0 + w_ref[k:k + 1, :] * _shift_up(d, s, row)
            dw_ref[k:k + 1, :] = jnp.sum(d * _shift_down(h, s, row), axis=0, keepdims=True)
        db_ref[...] = jnp.sum(d, axis=0, keepdims=True)
        dval_ref[...] = dh0 * sig
        dgate_ref[...] = dh0 * val * sig * (1.0 - sig)

    blk = lambda off: pl.BlockSpec((L, LANES), lambda j: (0, j + off))
    return pl.pallas_call(
        body, grid=(nb,), in_specs=[blk(0), blk(nb), pl.BlockSpec((K, LANES), lambda j: (0, j)), blk(0)],
        out_specs=[blk(0), blk(0), pl.BlockSpec((K, LANES), lambda j: (0, j)), pl.BlockSpec((1, LANES), lambda j: (0, j))],
        out_shape=[jax.ShapeDtypeStruct((L, C), F32), jax.ShapeDtypeStruct((L, C), F32),
                   jax.ShapeDtypeStruct((K, C), F32), jax.ShapeDtypeStruct((1, C), F32)],
        name=name, compiler_params=_cp(1))(u, u, w, dh)


def _glu_conv(name):
    @jax.custom_vjp
    def op(u, w, b):
        return _conv_fwd(u, w, b, name + "_fwd")

    def fwd(u, w, b):
        return _conv_fwd(u, w, b, name + "_fwd"), (u, w)

    def bwd(r, dh):
        u, w = r
        dval, dgate, dw, db = _conv_bwd(u, w, dh, name + "_bwd")
        return jnp.concatenate([dval, dgate], axis=-1), dw, db

    op.defvjp(fwd, bwd)
    return op


S5_BLOCK_CH = LANES
S5_BLOCK_ST = S5_BLOCK_CH // S5_CH_PER_GROUP * S5_STATE


def _s5_scan(br_ref, bi_ref, ar, ai, reverse):
    L, C = br_ref.shape
    T = SUBLANES
    row = lax.broadcasted_iota(jnp.int32, (T, C), 0)
    pw = [(ar, ai)]
    for _ in range(T - 1):
        pr, pi = pw[-1]
        pw.append((pr * ar - pi * ai, pr * ai + pi * ar))
    cr = jnp.zeros((T, C), F32)
    ci = jnp.zeros((T, C), F32)
    for r in range(T):
        e = (T - r) if reverse else (r + 1)
        cr = jnp.where(row == r, pw[e - 1][0], cr)
        ci = jnp.where(row == r, pw[e - 1][1], ci)
    steps = []
    s = 1
    while s < T:
        mask = (row < T - s) if reverse else (row >= s)
        steps.append((T - s if reverse else s, mask, pw[s - 1][0], pw[s - 1][1]))
        s *= 2
    nt = L // T
    last = 0 if reverse else T - 1

    def body(i, carry):
        kr, ki = carry
        t = (nt - 1 - i) if reverse else i
        off = pl.multiple_of(t * T, T)
        xr = br_ref[pl.ds(off, T), :]
        xi = bi_ref[pl.ds(off, T), :]
        for sh, mask, mr, mi in steps:
            sr = jnp.where(mask, pltpu.roll(xr, sh, 0), 0.0)
            si = jnp.where(mask, pltpu.roll(xi, sh, 0), 0.0)
            xr, xi = xr + mr * sr - mi * si, xi + mr * si + mi * sr
        xr, xi = xr + cr * kr - ci * ki, xi + cr * ki + ci * kr
        br_ref[pl.ds(off, T), :] = xr
        bi_ref[pl.ds(off, T), :] = xi
        return (jnp.broadcast_to(xr[last:last + 1, :], (T, C)), jnp.broadcast_to(xi[last:last + 1, :], (T, C)))

    z = jnp.zeros((T, C), F32)
    lax.fori_loop(0, nt, body, (z, z))


def _s5_specs(L):
    nb_axis = lambda j: (j, 0, 0)
    u = pl.BlockSpec((L, S5_BLOCK_CH), lambda j: (0, j))
    wb = pl.BlockSpec((None, S5_BLOCK_CH, S5_BLOCK_ST), nb_axis)
    a = pl.BlockSpec((1, S5_BLOCK_ST), lambda j: (0, j))
    wc = pl.BlockSpec((None, S5_BLOCK_ST, S5_BLOCK_CH), nb_axis)
    return u, wb, a, wc


def _s5_fwd(u, wbr, wbi, ar, ai, wcr, wci, name):
    L, C = u.shape
    nb = C // S5_BLOCK_CH
    us, wbs, as_, wcs = _s5_specs(L)

    def body(u_ref, wbr_ref, wbi_ref, ar_ref, ai_ref, wcr_ref, wci_ref, y_ref, xr, xi):
        ub = u_ref[...]
        xr[...] = _bdot(ub, wbr_ref[...])
        xi[...] = _bdot(ub, wbi_ref[...])
        _s5_scan(xr, xi, ar_ref[...], ai_ref[...], False)
        y_ref[...] = _bdot(xr[...], wcr_ref[...]) - _bdot(xi[...], wci_ref[...])

    return pl.pallas_call(
        body, grid=(nb,), in_specs=[us, wbs, wbs, as_, as_, wcs, wcs], out_specs=us,
        out_shape=jax.ShapeDtypeStruct((L, C), F32),
        scratch_shapes=[pltpu.VMEM((L, S5_BLOCK_ST), F32)] * 2, name=name, compiler_params=_cp(1))(
            u, wbr, wbi, ar, ai, wcr, wci)


def _dot_t(a, b):
    return lax.dot_general(a.astype(BF16), b.astype(BF16), (((0,), (0,)), ((), ())), preferred_element_type=F32)


def _dot_nt(a, b):
    return lax.dot_general(a.astype(BF16), b.astype(BF16), (((1,), (1,)), ((), ())), preferred_element_type=F32)


def _s5_bwd(u, wbr, wbi, ar, ai, wcr, wci, dy, name):
    L, C = u.shape
    nb = C // S5_BLOCK_CH
    us, wbs, as_, wcs = _s5_specs(L)
    T = SUBLANES

    def body(u_ref, wbr_ref, wbi_ref, ar_ref, ai_ref, wcr_ref, wci_ref, dy_ref,
             du_ref, dwbr_ref, dwbi_ref, dar_ref, dai_ref, dwcr_ref, dwci_ref, xr, xi, gr, gi):
        ub = u_ref[...]
        a_r, a_i = ar_ref[...], ai_ref[...]
        xr[...] = _bdot(ub, wbr_ref[...])
        xi[...] = _bdot(ub, wbi_ref[...])
        _s5_scan(xr, xi, a_r, a_i, False)
        d = dy_ref[...]
        dwcr_ref[...] = _dot_t(xr[...], d)
        dwci_ref[...] = -_dot_t(xi[...], d)
        gr[...] = _dot_nt(d, wcr_ref[...])
        gi[...] = -_dot_nt(d, wci_ref[...])
        _s5_scan(gr, gi, a_r, -a_i, True)

        row = lax.broadcasted_iota(jnp.int32, (T, S5_BLOCK_ST), 0)

        def da_body(i, carry):
            pr, pi, sr, si = carry
            off = pl.multiple_of(i * T, T)
            xr_t, xi_t = xr[pl.ds(off, T), :], xi[pl.ds(off, T), :]
            lr_t, li_t = gr[pl.ds(off, T), :], gi[pl.ds(off, T), :]
            qr = jnp.where(row == 0, pr, pltpu.roll(xr_t, 1, 0))
            qi = jnp.where(row == 0, pi, pltpu.roll(xi_t, 1, 0))
            sr = sr + qr * lr_t + qi * li_t
            si = si + qr * li_t - qi * lr_t
            return (jnp.broadcast_to(xr_t[T - 1:T, :], (T, S5_BLOCK_ST)),
                    jnp.broadcast_to(xi_t[T - 1:T, :], (T, S5_BLOCK_ST)), sr, si)

        z = jnp.zeros((T, S5_BLOCK_ST), F32)
        _, _, sr, si = lax.fori_loop(0, L // T, da_body, (z, z, z, z))
        dar_ref[...] = jnp.sum(sr, axis=0, keepdims=True)
        dai_ref[...] = jnp.sum(si, axis=0, keepdims=True)
        lr, li = gr[...], gi[...]
        dwbr_ref[...] = _dot_t(ub, lr)
        dwbi_ref[...] = _dot_t(ub, li)
        du_ref[...] = _dot_nt(lr, wbr_ref[...]) + _dot_nt(li, wbi_ref[...])

    sds = jax.ShapeDtypeStruct
    return pl.pallas_call(
        body, grid=(nb,), in_specs=[us, wbs, wbs, as_, as_, wcs, wcs, us],
        out_specs=[us, wbs, wbs, as_, as_, wcs, wcs],
        out_shape=[sds(u.shape, F32), sds(wbr.shape, F32), sds(wbi.shape, F32), sds(ar.shape, F32),
                   sds(ai.shape, F32), sds(wcr.shape, F32), sds(wci.shape, F32)],
        scratch_shapes=[pltpu.VMEM((L, S5_BLOCK_ST), F32)] * 4, name=name,
        compiler_params=_cp(1, VMEM_LIMIT_BIG))(u, wbr, wbi, ar, ai, wcr, wci, dy)


def _s5_core(name):
    @jax.custom_vjp
    def op(u, wbr, wbi, ar, ai, wcr, wci):
        return _s5_fwd(u, wbr, wbi, ar, ai, wcr, wci, name + "_fwd")

    def fwd(*a):
        return _s5_fwd(*a, name + "_fwd"), a

    def bwd(a, dy):
        return tuple(_s5_bwd(*a, dy, name + "_bwd"))

    op.defvjp(fwd, bwd)
    return op


def _att_tile_f(first, q, kp, kc, vp, vc, bias):
    nq = q.shape[0]
    hb = bias.shape[0]
    E = q.shape[1] // hb
    r = lax.broadcasted_iota(jnp.int32, (nq, 2 * nq), 0)
    c = lax.broadcasted_iota(jnp.int32, (nq, 2 * nq), 1)
    prev_ok = jnp.logical_and(jnp.logical_and(c < nq, c >= r), jnp.logical_not(first))
    valid = jnp.logical_or(prev_ok, jnp.logical_and(c >= nq, c - nq <= r))
    outs, lses = [], []
    for h in range(hb):
        sl = slice(h * E, (h + 1) * E)
        k = jnp.concatenate([kp[:, sl], kc[:, sl]], axis=0)
        v = jnp.concatenate([vp[:, sl], vc[:, sl]], axis=0)
        s = jnp.where(valid, _dot_nt(q[:, sl], k) * (E ** -0.5) + bias[h], NEG_INF)
        m = jnp.max(s, axis=-1, keepdims=True)
        p = jnp.exp(s - m)
        den = jnp.sum(p, axis=-1, keepdims=True)
        outs.append(_bdot(p, v) / den)
        lses.append(jnp.broadcast_to(m + jnp.log(den), (nq, E)))
    return jnp.concatenate(outs, axis=-1), jnp.concatenate(lses, axis=-1)


def _att_mix_f(*a):
    n = len(a) // 2
    o, l = a[:n], a[n:]
    m = functools.reduce(jnp.maximum, l)
    e = [jnp.exp(li - m) for li in l]
    return sum(ei * oi for ei, oi in zip(e, o)) / sum(e)


def _att_rows(start, dil):
    if dil == 1:
        return pl.ds(pl.multiple_of(start, ATT_BLOCK), ATT_BLOCK)
    return pl.ds(start, ATT_BLOCK, stride=dil)


def _att_blocks(L, dil):
    nb = L // dil // ATT_BLOCK
    return dil * nb, nb


def _att_specs(L, W):
    nblk = W // LANES
    col = lambda off: pl.BlockSpec((L, LANES), lambda j: (0, j + off))
    per_pattern = pl.BlockSpec((len(DILATED_PATTERNS), L, LANES), lambda j: (0, 0, j))
    hb = ATT_HEADS // nblk
    bias = pl.BlockSpec((len(DILATED_PATTERNS), hb, ATT_BLOCK, 2 * ATT_BLOCK), lambda j: (0, j, 0, 0))
    return nblk, col, per_pattern, bias


def _att_fwd(qkv, bias, name):
    L, W3 = qkv.shape
    W = W3 // 3
    nblk, col, per_pattern, bias_spec = _att_specs(L, W)
    P = len(DILATED_PATTERNS)

    def body(q_ref, k_ref, v_ref, b_ref, y_ref, o_ref, l_ref):
        for p, (_, dil) in enumerate(DILATED_PATTERNS):
            n_it, nb = _att_blocks(L, dil)

            def step(i, carry, p=p, dil=dil, nb=nb):
                n = i % nb
                cur = i // nb + n * (ATT_BLOCK * dil)
                prev = i // nb + jnp.maximum(n - 1, 0) * (ATT_BLOCK * dil)
                rc, rp = _att_rows(cur, dil), _att_rows(prev, dil)
                o, l = _att_tile_f(n == 0, q_ref[rc, :], k_ref[rp, :], k_ref[rc, :], v_ref[rp, :], v_ref[rc, :],
                                   b_ref[p])
                o_ref[p, rc, :] = o
                l_ref[p, rc, :] = l
                return carry

            lax.fori_loop(0, n_it, step, 0)

        def mix(i, carry):
            rows = pl.ds(pl.multiple_of(i * ATT_MIX_ROWS, ATT_MIX_ROWS), ATT_MIX_ROWS)
            y_ref[rows, :] = _att_mix_f(*[o_ref[p, rows, :] for p in range(P)], *[l_ref[p, rows, :] for p in range(P)])
            return carry

        lax.fori_loop(0, L // ATT_MIX_ROWS, mix, 0)

    sds = jax.ShapeDtypeStruct
    return pl.pallas_call(
        body, grid=(nblk,), in_specs=[col(0), col(nblk), col(2 * nblk), bias_spec],
        out_specs=[col(0), per_pattern, per_pattern],
        out_shape=[sds((L, W), F32), sds((P, L, W), F32), sds((P, L, W), F32)], name=name,
        compiler_params=_cp(1))(qkv, qkv, qkv, bias)


def _att_bwd(qkv, bias, o_all, l_all, dy, name):
    L, W3 = qkv.shape
    W = W3 // 3
    nblk, col, per_pattern, bias_spec = _att_specs(L, W)
    P = len(DILATED_PATTERNS)

    def body(q_ref, k_ref, v_ref, b_ref, o_ref, l_ref, dy_ref, dq_ref, dk_ref, dv_ref, db_ref, do_s, dl_s):
        def mix(i, carry):
            rows = pl.ds(pl.multiple_of(i * ATT_MIX_ROWS, ATT_MIX_ROWS), ATT_MIX_ROWS)
            _, mix_vjp = jax.vjp(_att_mix_f, *[o_ref[p, rows, :] for p in range(P)],
                                 *[l_ref[p, rows, :] for p in range(P)])
            g = mix_vjp(dy_ref[rows, :])
            for p in range(P):
                do_s[p, rows, :] = g[p]
                dl_s[p, rows, :] = g[P + p]
            return carry

        lax.fori_loop(0, L // ATT_MIX_ROWS, mix, 0)
        for ref in (dq_ref, dk_ref, dv_ref, db_ref):
            ref[...] = jnp.zeros_like(ref)

        def add(ref, rows, val):
            ref[rows, :] = ref[rows, :] + val

        for p, (_, dil) in enumerate(DILATED_PATTERNS):
            n_it, nb = _att_blocks(L, dil)

            def step(i, carry, p=p, dil=dil, nb=nb):
                n = i % nb
                first = n == 0
                cur = i // nb + n * (ATT_BLOCK * dil)
                prev = i // nb + jnp.maximum(n - 1, 0) * (ATT_BLOCK * dil)
                rc, rp = _att_rows(cur, dil), _att_rows(prev, dil)
                _, vjp = jax.vjp(functools.partial(_att_tile_f, first), q_ref[rc, :], k_ref[rp, :], k_ref[rc, :],
                                 v_ref[rp, :], v_ref[rc, :], b_ref[p])
                dq, dkp, dkc, dvp, dvc, db = vjp((do_s[p, rc, :], dl_s[p, rc, :]))
                add(dq_ref, rc, dq)
                add(dk_ref, rc, dkc)
                add(dv_ref, rc, dvc)
                db_ref[p] = db_ref[p] + db

                @pl.when(jnp.logical_not(first))
                def _():
                    add(dk_ref, rp, dkp)
                    add(dv_ref, rp, dvp)

                return carry

            lax.fori_loop(0, n_it, step, 0)

    sds = jax.ShapeDtypeStruct((L, W), F32)
    return pl.pallas_call(
        body, grid=(nblk,),
        in_specs=[col(0), col(nblk), col(2 * nblk), bias_spec, per_pattern, per_pattern, col(0)],
        out_specs=[col(0), col(0), col(0), bias_spec],
        out_shape=[sds, sds, sds, jax.ShapeDtypeStruct(bias.shape, F32)],
        scratch_shapes=[pltpu.VMEM((P, L, LANES), F32)] * 2, name=name, compiler_params=_cp(1, VMEM_LIMIT_BIG))(
            qkv, qkv, qkv, bias, o_all, l_all, dy)


def _dilated_attention(name):
    @jax.custom_vjp
    def op(qkv, bias):
        return _att_fwd(qkv, bias, name + "_fwd")[0]

    def fwd(qkv, bias):
        y, o_all, l_all = _att_fwd(qkv, bias, name + "_fwd")
        return y, (qkv, bias, o_all, l_all)

    def bwd(r, dy):
        dq, dk, dv, db = _att_bwd(*r, dy, name + "_bwd")
        return jnp.concatenate([dq, dk, dv], axis=-1), db

    op.defvjp(fwd, bwd)
    return op


def _t5_bucket(dist):
    n = np.maximum(dist, 0)
    max_exact = REL_BUCKETS // 2
    large = max_exact + (np.log(np.maximum(n, 1) / max_exact) / np.log(REL_MAX_DIST / max_exact)
                         * (REL_BUCKETS - max_exact)).astype(np.int64)
    large = np.minimum(large, REL_BUCKETS - 1)
    return np.where(n < max_exact, n, large).astype(np.int32)


def _bucket_onehot():
    a = np.arange(ATT_BLOCK)[:, None]
    b = np.arange(2 * ATT_BLOCK)[None, :]
    sub = a + ATT_BLOCK - b
    bucket = jnp.asarray(np.stack([_t5_bucket(sub * dil).reshape(-1) for _, dil in DILATED_PATTERNS]))
    ids = jnp.arange(REL_BUCKETS, dtype=jnp.int32)
    return (bucket[:, None, :] == ids[None, :, None]).astype(F32)


def loss_head(h, target, g, name):
    R, D = h.shape
    tr = _tile(R, (256,))

    def body(h_ref, t_ref, g_ref, l_ref, dh_ref, dg_ref):
        def lf(hv, gv):
            y = _rms_f(hv, gv)[0]
            return 0.5 * jnp.sum(jnp.mean(jnp.square(y - t_ref[...]), axis=-1))

        l, (dh, dg) = jax.value_and_grad(lf, argnums=(0, 1))(h_ref[...], g_ref[...])

        @pl.when(pl.program_id(0) == 0)
        def _():
            l_ref[...] = jnp.zeros_like(l_ref)
            dg_ref[...] = jnp.zeros_like(dg_ref)

        dh_ref[...] = dh
        dg_ref[...] += dg
        l_ref[...] += l

    rows = pl.BlockSpec((tr, D), lambda i: (i, 0))
    vec = pl.BlockSpec((1, D), lambda i: (0, 0))
    l, dh, dg = pl.pallas_call(
        body, grid=(R // tr,), in_specs=[rows, rows, vec],
        out_specs=[pl.BlockSpec((SUBLANES, LANES), lambda i: (0, 0)), rows, vec],
        out_shape=[jax.ShapeDtypeStruct((SUBLANES, LANES), F32), jax.ShapeDtypeStruct((R, D), F32),
                   jax.ShapeDtypeStruct((1, D), F32)], name=name, compiler_params=_cp(1))(h, target, g.reshape(1, D))
    return l[0, 0], dh, dg.reshape(D)


def adamw(w, parts, m, v, name):
    R, C = w.shape
    n_parts = parts.shape[0]
    tr = _row_tile(R, C)
    c1 = 1.0 - ADAM_B1 ** ADAM_STEP
    c2 = 1.0 - ADAM_B2 ** ADAM_STEP

    def body(w_ref, p_ref, m_ref, v_ref, g_ref, d_ref, nm_ref, nv_ref):
        g = p_ref[0].astype(F32)
        for i in range(1, n_parts):
            g = g + p_ref[i].astype(F32)
        nm = ADAM_B1 * m_ref[...] + (1.0 - ADAM_B1) * g
        nv = ADAM_B2 * v_ref[...] + (1.0 - ADAM_B2) * jnp.square(g)
        d_ref[...] = -ADAM_LR * ((nm / c1) / (jnp.sqrt(nv / c2) + ADAM_EPS) + ADAM_WD * w_ref[...])
        g_ref[...] = g
        nm_ref[...] = nm
        nv_ref[...] = nv

    rows = pl.BlockSpec((tr, C), lambda i: (i, 0))
    sds = jax.ShapeDtypeStruct((R, C), F32)
    return pl.pallas_call(body, grid=(R // tr,),
                          in_specs=[rows, pl.BlockSpec((n_parts, tr, C), lambda i: (0, i, 0)), rows, rows],
                          out_specs=[rows] * 4, out_shape=[sds] * 4, name=name, compiler_params=_cp(1))(w, parts, m, v)


HBM_SPEC = pl.BlockSpec(memory_space=pltpu.HBM)
MESH_ID = pl.DeviceIdType.MESH


def _place():
    return lax.axis_index("x"), lax.axis_index("y"), lax.axis_index("c")


def _index(x, y, c):
    return 4 * x + 2 * y + c


def all_gather(x, name):
    def body(x_ref, out_ref, send_sems, recv_sems, local_sem):
        x_, y_, c_ = _place()
        me, sibling = (x_, y_, c_), (x_, y_, 1 - c_)
        chips = [(1 - x_, y_), (x_, 1 - y_), (1 - x_, 1 - y_)]

        def slot(px, py, pc):
            return out_ref.at[_index(px, py, pc)]

        def copy(k, block, to, src=None):
            return pltpu.make_async_remote_copy(
                src_ref=slot(*block) if src is None else src, dst_ref=slot(*block),
                send_sem=send_sems.at[k], recv_sem=recv_sems.at[k], device_id=to, device_id_type=MESH_ID)

        mine = pltpu.make_async_copy(x_ref, slot(*me), local_sem)
        mine.start()
        first = [copy(0, me, sibling, src=x_ref)]
        first += [copy(1 + j, me, (*chip, c_), src=x_ref) for j, chip in enumerate(chips)]
        for cp in first:
            cp.start()
        passed = [copy(4 + j, (*chip, c_), sibling) for j, chip in enumerate(chips)]
        for j, chip in enumerate(chips):
            copy(1 + j, (*chip, c_), me).wait_recv()
            passed[j].start()
        copy(0, sibling, me).wait_recv()
        for j, chip in enumerate(chips):
            copy(4 + j, (*chip, 1 - c_), me).wait_recv()
        for cp in first + passed:
            cp.wait_send()
        mine.wait()

    return pl.pallas_call(
        body, out_shape=jax.ShapeDtypeStruct((N_DEV,) + x.shape, x.dtype), in_specs=[HBM_SPEC], out_specs=HBM_SPEC,
        scratch_shapes=[pltpu.SemaphoreType.DMA((N_DEV - 1,)), pltpu.SemaphoreType.DMA((N_DEV - 1,)),
                        pltpu.SemaphoreType.DMA], name=name)(x)


def _chip(x, y):
    return 2 * x + y


def sibling_exchange(x, name):
    nc, _, R, C = x.shape

    def body(x_ref, out_ref, send_sem, recv_sem):
        x_, y_, c_ = _place()
        cp = pltpu.make_async_remote_copy(src_ref=x_ref.at[:, 1 - c_], dst_ref=out_ref, send_sem=send_sem,
                                          recv_sem=recv_sem, device_id=(x_, y_, 1 - c_), device_id_type=MESH_ID)
        cp.start()
        cp.wait()

    return pl.pallas_call(
        body, out_shape=jax.ShapeDtypeStruct((nc, R, C), x.dtype), in_specs=[HBM_SPEC], out_specs=HBM_SPEC,
        scratch_shapes=[pltpu.SemaphoreType.DMA, pltpu.SemaphoreType.DMA], name=name)(x)


def _row_tile(R, C):
    cap = max(SUBLANES, STREAM_BLOCK_BYTES // (4 * C))
    return _tile(R, [t for t in (512, 256, 128, 64, 32, 16, 8) if t <= cap])


def pair_sum(x, recv, name):
    nc, _, R, C = x.shape
    tr = _row_tile(R, C)
    core = lax.axis_index("c").astype(jnp.int32).reshape(1)

    def body(c_ref, a_ref, b_ref, o_ref):
        o_ref[...] = (a_ref[...].astype(F32) + b_ref[...].astype(F32)).astype(o_ref.dtype)

    blk = pl.BlockSpec((None, tr, C), lambda k, i, c_ref: (k, i, 0))
    grid_spec = pltpu.PrefetchScalarGridSpec(
        num_scalar_prefetch=1, grid=(nc, R // tr),
        in_specs=[pl.BlockSpec((None, None, tr, C), lambda k, i, c_ref: (k, c_ref[0], i, 0)), blk], out_specs=blk)
    return pl.pallas_call(body, grid_spec=grid_spec, out_shape=jax.ShapeDtypeStruct((nc, R, C), x.dtype), name=name,
                          compiler_params=_cp(2))(core, x, recv)


def chip_exchange(s, name):
    def body(s_ref, out_ref, send_sems, recv_sems, local_sem):
        x_, y_, c_ = _place()
        me = _chip(x_, y_)
        local = pltpu.make_async_copy(s_ref.at[me], out_ref.at[me], local_sem)
        local.start()
        copies = []
        for k in range(1, N_CHIP):
            px = 1 - x_ if k & 2 else x_
            py = 1 - y_ if k & 1 else y_
            cp = pltpu.make_async_remote_copy(
                src_ref=s_ref.at[_chip(px, py)], dst_ref=out_ref.at[me], send_sem=send_sems.at[k - 1],
                recv_sem=recv_sems.at[k - 1], device_id=(px, py, c_), device_id_type=MESH_ID)
            cp.start()
            copies.append(cp)
        for cp in copies:
            cp.wait()
        local.wait()

    return pl.pallas_call(
        body, out_shape=jax.ShapeDtypeStruct(s.shape, s.dtype), in_specs=[HBM_SPEC], out_specs=HBM_SPEC,
        scratch_shapes=[pltpu.SemaphoreType.DMA((N_CHIP - 1,)), pltpu.SemaphoreType.DMA((N_CHIP - 1,)),
                        pltpu.SemaphoreType.DMA], name=name)(s)


def _block_diag(w, nb):
    G, a, b = w.shape
    gp = G // nb
    eye = jnp.eye(gp, dtype=w.dtype)
    return jnp.einsum('jgab,gh->jgahb', w.reshape(nb, gp, a, b), eye).reshape(nb, gp * a, gp * b)


def _split_columns(x, cuts):
    edges = (0,) + tuple(cuts) + (x.shape[1],)

    def split(x):
        return tuple(x[:, a:b] for a, b in zip(edges[:-1], edges[1:]))

    op = jax.custom_vjp(split)
    op.defvjp(lambda x: (split(x), None), lambda _, cts: (jnp.concatenate(cts, axis=-1),))
    return op(x)


def _mixers_and_memory(l, h, memn, bias_tabs, P):
    nm = lambda s: f"l{l}_{s}"
    L, D = h.shape
    GW = D // N_MIXERS
    G = GW // S5_CH_PER_GROUP

    row = lambda g: g.reshape(1, D)

    proj = _act_linear(nm("w_in"), _rms_op(nm("norm_mix"), L, D, BF16), 2, False)(h, row(P['norm_mix_g']), P['w_in'])
    u_a, u_b, u_c, qkv = _split_columns(proj, (GW, 2 * GW, 4 * GW))

    v3 = lambda a: a.reshape(G, 1, S5_STATE)
    log_dt = jnp.broadcast_to(P['s5_log_dt'][:, None, None], (G, 1, S5_STATE))
    a_r, a_i, bb_r, bb_i = s5_discretise(v3(P['s5_lam_re']), v3(P['s5_lam_im']), log_dt,
                                         P['s5_b_re'].transpose(0, 2, 1), P['s5_b_im'].transpose(0, 2, 1), nm("s5_disc"))
    nblk = GW // S5_BLOCK_CH
    y_s5 = _s5_core(nm("s5_core"))(
        u_a, _block_diag(bb_r, nblk), _block_diag(bb_i, nblk), a_r.reshape(1, G * S5_STATE), a_i.reshape(1, G * S5_STATE),
        _block_diag(P['s5_c_re'].transpose(0, 2, 1), nblk), _block_diag(P['s5_c_im'].transpose(0, 2, 1), nblk))
    y_a = s5_epilogue(y_s5, u_a, P['s5_d'], P['s5_w_glu'], nm("s5_glu"))

    y_b = pool_proj(_pool_mix(nm("pool_mix"))(u_b), P['pool_w'], P['pool_scale'], nm("pool_proj"))

    hc = _glu_conv(nm("conv_dw"))(u_c, P['conv_w_dw'], P['conv_b_dw'].reshape(1, GW))
    y_c = conv_post(hc, P['conv_ln_g'], P['conv_ln_b'], P['conv_w_pw'], nm("conv_post"))

    y_d = _dilated_attention(nm("att"))(qkv, bias_tabs)

    grp = _group_norm_op(nm("grp_norm"), L, GW, N_MIXERS, BF16)
    h = _act_linear(nm("w_out"), grp, N_MIXERS + 1, True)(y_a, y_b, y_c, y_d, row(P['grp_norm_g']), P['w_out'], h)

    xq = _act_linear(nm("w_xq"), _rms_op(nm("norm_x"), L, D, BF16), 2, False)(h, row(P['norm_x_g']), P['w_xq'])
    xk = _linear(nm("w_xk"))(memn, P['w_xk'])
    xv = _linear(nm("w_xv"))(memn, P['w_xv'])
    xat = _cross_attention_op(nm("xattn"), L, xq.shape[1], memn.shape[0], BF16)
    return _act_linear(nm("w_xo"), xat, 3, True)(xq, xk, xv, P['w_xo'], h)


def _bias_tables(rel_bias):
    tabs = rel_bias_tables(rel_bias, _bucket_onehot(), "rel_bias")
    return tabs.reshape(len(DILATED_PATTERNS), ATT_HEADS, ATT_BLOCK, 2 * ATT_BLOCK)


def _gather_weight(name, w):
    ax = SHARDED[name]
    dt = BF16 if name in GATHER_BF16 else F32
    nl, a, b = w.shape
    g = all_gather(w.astype(dt).reshape(nl * a, b), "ag_" + name).reshape(N_DEV, nl, a, b)
    if ax == 1:
        return g.transpose(1, 0, 2, 3).reshape(nl, N_DEV * a, b)
    return g.transpose(1, 2, 0, 3).reshape(nl, a, N_DEV * b)


def _scatter_grad(name, g):
    ax = SHARDED[name]
    nl = g.shape[0]
    if ax == 1:
        a, b = g.shape[1] // N_DEV, g.shape[2]
        s = g.reshape(nl, N_DEV, a, b).transpose(1, 0, 2, 3)
    else:
        a, b = g.shape[1], g.shape[2] // N_DEV
        s = g.reshape(nl, a, N_DEV, b).transpose(2, 0, 1, 3)
    s = s.reshape(N_CHIP, N_DEV // N_CHIP, nl * a, b)
    pair = pair_sum(s, sibling_exchange(s, "d2d_" + name), "pairsum_" + name)
    return chip_exchange(pair, "ici_" + name)


def _by_destination(name, g):
    A, B = g.shape
    if SHARDED[name] == 1:
        return g.reshape(N_DEV, A // N_DEV, B)
    return g.reshape(A, N_DEV, B // N_DEV).transpose(1, 0, 2)


def _pack_grads(d):
    parts = [_by_destination(n, d[n]).reshape(N_DEV, -1, PACK_COLS) for n in GATHER_BF16]
    return jnp.concatenate(parts, axis=1).reshape(N_CHIP, N_DEV // N_CHIP, -1, PACK_COLS)


def _unpack_grads(buf, block_shapes):
    out, off = {}, 0
    for n in GATHER_BF16:
        a, b = block_shapes[n]
        rows = a * b // PACK_COLS
        out[n] = buf[:, off:off + rows].reshape(N_CHIP, a, b)
        off += rows
    return out


def _flatten_small(d):
    flat = jnp.concatenate([d[n].reshape(-1).astype(F32) for n in SMALL])
    pad = (-flat.shape[0]) % (LANES * SMALL_ROW_TILE)
    return jnp.pad(flat, (0, pad)).reshape(-1, LANES)


def _split_small(flat, like):
    flat = flat.reshape(-1)
    out, off = {}, 0
    for n in SMALL:
        sz = math.prod(like[n].shape)
        out[n] = flat[off:off + sz].reshape(like[n].shape)
        off += sz
    return out


def kernel(x, mem, rel_bias, mem_norm_g, norm_mix_g, w_in, s5_lam_re, s5_lam_im, s5_log_dt, s5_b_re, s5_b_im, s5_c_re, s5_c_im, s5_d, s5_w_glu, pool_w, pool_scale, conv_w_dw, conv_b_dw, conv_ln_g, conv_ln_b, conv_w_pw, grp_norm_g, w_out, norm_x_g, w_xq, w_xk, w_xv, w_xo, norm_mlp_g, w_up, w_down, norm_final_g, loss_target, m_rel_bias, m_mem_norm_g, m_norm_mix_g, m_w_in, m_s5_lam_re, m_s5_lam_im, m_s5_log_dt, m_s5_b_re, m_s5_b_im, m_s5_c_re, m_s5_c_im, m_s5_d, m_s5_w_glu, m_pool_w, m_pool_scale, m_conv_w_dw, m_conv_b_dw, m_conv_ln_g, m_conv_ln_b, m_conv_w_pw, m_grp_norm_g, m_w_out, m_norm_x_g, m_w_xq, m_w_xk, m_w_xv, m_w_xo, m_norm_mlp_g, m_w_up, m_w_down, m_norm_final_g, v_rel_bias, v_mem_norm_g, v_norm_mix_g, v_w_in, v_s5_lam_re, v_s5_lam_im, v_s5_log_dt, v_s5_b_re, v_s5_b_im, v_s5_c_re, v_s5_c_im, v_s5_d, v_s5_w_glu, v_pool_w, v_pool_scale, v_conv_w_dw, v_conv_b_dw, v_conv_ln_g, v_conv_ln_b, v_conv_w_pw, v_grp_norm_g, v_w_out, v_norm_x_g, v_w_xq, v_w_xk, v_w_xv, v_w_xo, v_norm_mlp_g, v_w_up, v_w_down, v_norm_final_g):
    w = dict(zip(WEIGHTS, (rel_bias, mem_norm_g, norm_mix_g, w_in, s5_lam_re, s5_lam_im, s5_log_dt, s5_b_re, s5_b_im, s5_c_re, s5_c_im, s5_d, s5_w_glu, pool_w, pool_scale, conv_w_dw, conv_b_dw, conv_ln_g, conv_ln_b, conv_w_pw, grp_norm_g, w_out, norm_x_g, w_xq, w_xk, w_xv, w_xo, norm_mlp_g, w_up, w_down, norm_final_g)))
    m = dict(zip(WEIGHTS, (m_rel_bias, m_mem_norm_g, m_norm_mix_g, m_w_in, m_s5_lam_re, m_s5_lam_im, m_s5_log_dt, m_s5_b_re, m_s5_b_im, m_s5_c_re, m_s5_c_im, m_s5_d, m_s5_w_glu, m_pool_w, m_pool_scale, m_conv_w_dw, m_conv_b_dw, m_conv_ln_g, m_conv_ln_b, m_conv_w_pw, m_grp_norm_g, m_w_out, m_norm_x_g, m_w_xq, m_w_xk, m_w_xv, m_w_xo, m_norm_mlp_g, m_w_up, m_w_down, m_norm_final_g)))
    v = dict(zip(WEIGHTS, (v_rel_bias, v_mem_norm_g, v_norm_mix_g, v_w_in, v_s5_lam_re, v_s5_lam_im, v_s5_log_dt, v_s5_b_re, v_s5_b_im, v_s5_c_re, v_s5_c_im, v_s5_d, v_s5_w_glu, v_pool_w, v_pool_scale, v_conv_w_dw, v_conv_b_dw, v_conv_ln_g, v_conv_ln_b, v_conv_w_pw, v_grp_norm_g, v_w_out, v_norm_x_g, v_w_xq, v_w_xk, v_w_xv, v_w_xo, v_norm_mlp_g, v_w_up, v_w_down, v_norm_final_g)))

    full = {n: (_gather_weight(n, w[n]) if n in SHARDED else w[n]) for n in WEIGHTS if n != 'norm_final_g'}
    L, D = x.shape[1:]

    memn, mem_vjp = jax.vjp(lambda a, g: rmsnorm(a, g, "mem_norm"), mem[0], w['mem_norm_g'])
    tabs, tabs_vjp = jax.vjp(_bias_tables, w['rel_bias'])
    h = x[0]
    front_vjps, mlps = [], []
    for l in range(DEPTH):
        h, fv = jax.vjp(functools.partial(_mixers_and_memory, l), h, memn, tabs, {n: full[n][l] for n in FRONT_WEIGHTS})
        norm = _rms_op(f"l{l}_norm_mlp", L, D, BF16)
        h, saved = _mlp_fwd(f"l{l}_mlp", norm, h, full['norm_mlp_g'][l].reshape(1, D), full['w_up'][l], full['w_down'][l])
        front_vjps.append(fv)
        mlps.append((norm, saved))
    loss_local, dh, d_final_g = loss_head(h, loss_target[0], w['norm_final_g'], "loss_head")
    loss = lax.psum(loss_local, MESH_AXES)

    layer_grads, arrived = [None] * DEPTH, [None] * DEPTH
    pending, dmemn, dtabs = None, 0.0, 0.0
    for l in reversed(range(DEPTH)):
        norm, saved = mlps[l]
        dh, dg_mlp, dw_up, dw_down, land = _mlp_bwd(f"l{l}_mlp", norm, saved, dh, pending)
        if pending is not None:
            arrived[l + 1] = land
        dh, dmemn_l, dtabs_l, dP = front_vjps[l](dh)
        dmemn, dtabs = dmemn + dmemn_l, dtabs + dtabs_l
        layer_grads[l] = dict(dP, norm_mlp_g=dg_mlp.reshape(D), w_up=dw_up, w_down=dw_down)
        packed = _pack_grads(layer_grads[l])
        pending = pair_sum(packed, sibling_exchange(packed, f"d2d_l{l}"), f"pairsum_l{l}")
    arrived[0] = chip_exchange(pending, "ici_l0")
    dx = dh
    dfull = {n: jnp.stack([layer_grads[l][n] for l in range(DEPTH)]) for n in LAYER_WEIGHTS if n not in GATHER_BF16}
    dfull['mem_norm_g'] = mem_vjp(dmemn)[1]
    dfull['rel_bias'] = tabs_vjp(dtabs)[0]
    dfull['norm_final_g'] = d_final_g

    blocks = {n: w[n].shape[1:] for n in GATHER_BF16}
    by_layer = [_unpack_grads(arrived[l], blocks) for l in range(DEPTH)]
    grads, deltas, new_m, new_v = {}, {}, {}, {}
    for n in SHARDED:
        if n in GATHER_BF16:
            parts = jnp.concatenate([by_layer[l][n] for l in range(DEPTH)], axis=1)
        else:
            parts = _scatter_grad(n, dfull[n])
        shp = w[n].shape
        two_d = lambda a: a.reshape(shp[0] * shp[1], shp[2])
        res = adamw(two_d(w[n]), parts, two_d(m[n]), two_d(v[n]), "adamw_" + n)
        grads[n], deltas[n], new_m[n], new_v[n] = (r.reshape(shp) for r in res)

    parts = all_gather(_flatten_small(dfull), "ag_small_grads")
    res = adamw(_flatten_small(w), parts, _flatten_small(m), _flatten_small(v), "adamw_small")
    for dst, r in zip((grads, deltas, new_m, new_v), res):
        dst.update(_split_small(r, w))

    return (loss, dx[None], *[grads[n] for n in WEIGHTS], *[deltas[n] for n in WEIGHTS],
            *[new_m[n] for n in WEIGHTS], *[new_v[n] for n in WEIGHTS])
```

```python
import functools
import math

import numpy as np
import jax
import jax.numpy as jnp
from jax import lax
from jax.experimental import pallas as pl
from jax.experimental.pallas import tpu as pltpu

F32 = jnp.float32
BF16 = jnp.bfloat16

DEPTH = 4
N_MIXERS = 4
S5_CH_PER_GROUP = 16
S5_STATE = 64
POOL_WINDOWS = (2, 4, 8, 16)
CONV_WIDTH = 31
ATT_HEADS = 8
DILATED_PATTERNS = ((128, 1), (512, 4), (2048, 16))
ATT_BLOCK = 128
ATT_MIX_ROWS = 256
ATT_UNROLL = 2
REL_BUCKETS = 32
REL_MAX_DIST = 2048
X_HEADS = 4
X_HEAD_DIM = 128
NORM_EPS = 1e-6
NEG_INF = -1e30
ADAM_LR = 0.001
ADAM_B1 = 0.9
ADAM_B2 = 0.999
ADAM_EPS = 1e-08
ADAM_WD = 0.01
ADAM_STEP = 10

LANES = 128
SUBLANES = 8
VMEM_BYTES = 64 * 1024 * 1024
VMEM_LIMIT = (VMEM_BYTES * 3) // 4
VMEM_LIMIT_BIG = (VMEM_BYTES * 7) // 8
STREAM_BLOCK_BYTES = 1024 * 1024
SMALL_ROW_TILE = 512
N_DEV = 8
N_CHIP = 4
MESH_AXES = ("x", "y", "c")

WEIGHTS = ['rel_bias', 'mem_norm_g', 'norm_mix_g', 'w_in', 's5_lam_re', 's5_lam_im', 's5_log_dt', 's5_b_re',
           's5_b_im', 's5_c_re', 's5_c_im', 's5_d', 's5_w_glu', 'pool_w', 'pool_scale', 'conv_w_dw', 'conv_b_dw',
           'conv_ln_g', 'conv_ln_b', 'conv_w_pw', 'grp_norm_g', 'w_out', 'norm_x_g', 'w_xq', 'w_xk', 'w_xv', 'w_xo',
           'norm_mlp_g', 'w_up', 'w_down', 'norm_final_g']
SHARDED = {'w_in': 2, 's5_w_glu': 1, 'conv_w_dw': 2, 'conv_w_pw': 1, 'w_out': 1, 'w_xq': 1, 'w_xk': 1, 'w_xv': 1,
           'w_xo': 2, 'w_up': 2, 'w_down': 1}
GATHER_BF16 = ('w_in', 'w_out', 'w_xq', 'w_xk', 'w_xv', 'w_xo', 'w_up', 'w_down')
SMALL = [n for n in WEIGHTS if n not in SHARDED]
LAYER_WEIGHTS = [n for n in WEIGHTS if n not in ('rel_bias', 'mem_norm_g', 'norm_final_g')]
FRONT_WEIGHTS = [n for n in LAYER_WEIGHTS if n not in ('norm_mlp_g', 'w_up', 'w_down')]

def _cp(n_axes, vmem=VMEM_LIMIT):
    return pltpu.CompilerParams(dimension_semantics=("arbitrary",) * n_axes, vmem_limit_bytes=vmem)


def _tile(n, prefs):
    for t in prefs:
        if n % t == 0:
            return t
    return n


MM_TILE = 1024
MM_TILE_K = 2048
MM_FULL_K = 4096


def _chip_exchange_copies(srcs, dsts, part, n_parts, send_sems, recv_sems, local_sems):
    x_, y_, c_ = _place()
    me = _chip(x_, y_)
    copies = []
    for p, (src, dst) in enumerate(zip(srcs, dsts)):
        n = src.shape[1] // n_parts
        rows = pl.ds(part * n, n)
        copies.append(pltpu.make_async_copy(src.at[me, rows], dst.at[me, rows], local_sems.at[p]))
        for k in range(1, N_CHIP):
            px = 1 - x_ if k & 2 else x_
            py = 1 - y_ if k & 1 else y_
            s = p * (N_CHIP - 1) + k - 1
            copies.append(pltpu.make_async_remote_copy(
                src_ref=src.at[_chip(px, py), rows], dst_ref=dst.at[me, rows], send_sem=send_sems.at[s],
                recv_sem=recv_sems.at[s], device_id=(px, py, c_), device_id_type=MESH_ID))
    return copies


def _mm(a, b, *, ta=False, tb=False, res=None, out_dtype=F32, epilogue=None, pre=None, rider=None, dest_cols=None,
        name):
    if ta:
        K, M = a.shape
    else:
        M, K = a.shape
    if tb:
        N, K2 = b.shape
    else:
        K2, N = b.shape
    assert K == K2, (a.shape, b.shape, ta, tb)
    wide_f32 = K >= MM_TILE_K and F32 in (a.dtype, b.dtype)
    tm = _tile(M, (MM_TILE // 2 if wide_f32 and not ta else MM_TILE, 512, 256, 128))
    tn = _tile(N, (MM_TILE, 512, 256, 128)) if dest_cols is None else dest_cols
    tk = K if K <= MM_FULL_K else _tile(K, (MM_TILE_K, 1024, 512, 256, 128))
    nk = K // tk
    a_spec = pl.BlockSpec((tk, tm), lambda i, j, k: (k, i)) if ta else pl.BlockSpec((tm, tk), lambda i, j, k: (i, k))
    b_spec = pl.BlockSpec((tn, tk), lambda i, j, k: (j, k)) if tb else pl.BlockSpec((tk, tn), lambda i, j, k: (k, j))
    o_spec = pl.BlockSpec((tm, tn), lambda i, j, k: (i, j))
    dn = (((0 if ta else 1,), (1 if tb else 0,)), ((), ()))
    extra = [x for x in (res, pre) if x is not None]
    assert not (res is not None and pre is not None)
    assert dest_cols is None or (epilogue is None and not extra and tn <= MM_TILE)
    n_out = 2 if epilogue == 'relu_sq' else 1
    n_pairs = len(rider[0]) if rider is not None else 0
    grid = (M // tm, N // tn, nk)

    def body(*refs):
        a_ref, b_ref = refs[:2]
        x_ref = refs[2] if extra else None
        n_in = 2 + len(extra) + 2 * n_pairs
        o_refs = refs[n_in:n_in + n_out]
        scratch = refs[n_in + n_out + n_pairs:]
        acc = scratch[0] if nk > 1 else None
        if rider is not None:
            ride = lambda: _chip_exchange_copies(refs[n_in - 2 * n_pairs:n_in - n_pairs],
                                                 refs[n_in + n_out:n_in + n_out + n_pairs], rider[2], rider[3],
                                                 *scratch[-3:])
            ids = [pl.program_id(ax) for ax in range(3)]
            at_start = functools.reduce(jnp.logical_and, [i == 0 for i in ids])
            at_end = functools.reduce(jnp.logical_and, [i == g - 1 for i, g in zip(ids, grid)])

            @pl.when(at_start)
            def _():
                for cp in ride():
                    cp.start()

        def finish(r):
            if res is not None:
                r = r + x_ref[...]
            if epilogue == 'relu_sq':
                o_refs[0][...] = r
                o_refs[1][...] = jnp.square(jnp.maximum(r, 0.0)).astype(out_dtype)
            elif epilogue == 'relu_sq_grad':
                o_refs[0][...] = (r * (2.0 * jnp.maximum(x_ref[...], 0.0))).astype(out_dtype)
            else:
                o_refs[0][...] = r.astype(out_dtype)

        part = lax.dot_general(a_ref[...].astype(BF16), b_ref[...].astype(BF16), dn, preferred_element_type=F32)
        if nk == 1:
            finish(part)
        else:
            k = pl.program_id(2)

            @pl.when(k == 0)
            def _():
                acc[...] = part

            @pl.when(k > 0)
            def _():
                acc[...] += part

            @pl.when(k == nk - 1)
            def _():
                finish(acc[...])

        if rider is not None:
            @pl.when(at_end)
            def _():
                for cp in ride():
                    cp.wait()

    out_shape = [jax.ShapeDtypeStruct((M, N), F32 if epilogue == 'relu_sq' else out_dtype)]
    if n_out == 2:
        out_shape.append(jax.ShapeDtypeStruct((M, N), out_dtype))
    in_specs = [a_spec, b_spec] + [o_spec] * len(extra)
    out_specs = [o_spec] * n_out
    if dest_cols is not None:
        out_shape = [jax.ShapeDtypeStruct((N // tn, M, tn), out_dtype)]
        out_specs = [pl.BlockSpec((None, tm, tn), lambda i, j, k: (j, i, 0))]
    scratch = [pltpu.VMEM((tm, tn), F32)] if nk > 1 else []
    args, aliases = (a, b, *extra), {}
    if rider is not None:
        srcs, dsts = rider[0], rider[1]
        in_specs += [HBM_SPEC] * (2 * n_pairs)
        out_specs += [HBM_SPEC] * n_pairs
        out_shape += [jax.ShapeDtypeStruct(d.shape, d.dtype) for d in dsts]
        n_sem = n_pairs * (N_CHIP - 1)
        scratch += [pltpu.SemaphoreType.DMA((n_sem,)), pltpu.SemaphoreType.DMA((n_sem,)),
                    pltpu.SemaphoreType.DMA((n_pairs,))]
        aliases = {len(args) + n_pairs + p: n_out + p for p in range(n_pairs)}
        args += (*srcs, *dsts)
    outs = pl.pallas_call(
        body, grid=grid, in_specs=in_specs, out_specs=out_specs, out_shape=out_shape, scratch_shapes=scratch,
        input_output_aliases=aliases, name=name, compiler_params=_cp(3, VMEM_LIMIT_BIG))(*args)
    return outs[0] if len(outs) == 1 else tuple(outs)


def _linear(name):
    @jax.custom_vjp
    def lin(a, w):
        return _mm(a, w, name=name + "_fwd")

    def fwd(a, w):
        return _mm(a, w, name=name + "_fwd"), (a, w)

    def bwd(r, dy):
        a, w = r
        da = _mm(dy, w, tb=True, name=name + "_dx")
        dw = _mm(a, dy, ta=True, out_dtype=w.dtype, name=name + "_dw")
        return da, dw

    lin.defvjp(fwd, bwd)
    return lin


def _act_linear(name, act, n_in, with_res):
    def run(*a):
        ins, w = a[:n_in], a[n_in]
        x = act.fwd_call(*ins)[0]
        return _mm(x, w, res=a[n_in + 1] if with_res else None, name=name + "_fwd"), (ins, x, w)

    @jax.custom_vjp
    def op(*a):
        return run(*a)[0]

    def bwd(r, dy):
        ins, x, w = r
        dx = _mm(dy, w, tb=True, name=name + "_dx")
        dw = _mm(x, dy, ta=True, out_dtype=w.dtype, name=name + "_dw")
        return (*act.bwd_all(ins, (dx,)), dw) + ((dy,) if with_res else ())

    op.defvjp(run, bwd)
    return op


def _mlp_fwd(name, norm, h, g, w_up, w_down):
    hn = norm.fwd_call(h, g)[0]
    a, r = _mm(hn, w_up, epilogue='relu_sq', out_dtype=BF16, name=name + "_up_fwd")
    return _mm(r, w_down, res=h, name=name + "_down_fwd"), (h, g, hn, a, r, w_up, w_down)


MLP_BWD_MATMULS = 4


def _mlp_bwd(name, norm, saved, dy, pending):
    h, g, hn, a, r, w_up, w_down = saved
    lands = None if pending is None else [lax.empty(p.shape, p.dtype) for p in pending]

    def mm(i, *args, **kw):
        nonlocal lands
        if pending is None:
            return _mm(*args, **kw)
        out, *lands = _mm(*args, rider=(pending, lands, i, MLP_BWD_MATMULS), **kw)
        return out

    da = mm(0, dy, w_down, tb=True, epilogue='relu_sq_grad', pre=a, out_dtype=BF16, name=name + "_down_dx")
    dw_down = mm(1, r, dy, ta=True, out_dtype=w_down.dtype, name=name + "_down_dw")
    dhn = mm(2, da, w_up, tb=True, name=name + "_up_dx")
    dw_up = mm(3, hn, da, ta=True, out_dtype=w_up.dtype, dest_cols=w_up.shape[1] // N_DEV, name=name + "_up_dw")
    dh, dg = norm.bwd_all((h, g), (dhn,))
    return dh + dy, dg, dw_up, dw_down, lands


def _block_op(name, f, grid, ins, outs, vmem=VMEM_LIMIT):
    n_in, n_out = len(ins), len(outs)
    in_specs = [pl.BlockSpec(bs, im) for bs, im, _, _ in ins]
    out_specs = [pl.BlockSpec(bs, im) for _, _, bs, im in outs]
    out_shape = [jax.ShapeDtypeStruct(s, d) for s, d, _, _ in outs]
    didx = [i for i in range(n_in) if ins[i][3]]

    def fwd_call(*args):
        def body(*refs):
            res = f(*[r[...] for r in refs[:n_in]])
            for r, o in zip(refs[n_in:], res):
                r[...] = o.astype(r.dtype)

        return pl.pallas_call(body, grid=grid, in_specs=in_specs, out_specs=out_specs, out_shape=out_shape,
                              name=name + "_fwd", compiler_params=_cp(len(grid), vmem))(*args)

    def bwd_call(args, cts):
        def body(*refs):
            vals = [r[...] for r in refs[:n_in]]
            ct_refs = refs[n_in:n_in + n_out]
            g_refs = refs[n_in + n_out:]

            def fd(*dv):
                full = list(vals)
                for i, v in zip(didx, dv):
                    full[i] = v
                return f(*full)

            _, vjp = jax.vjp(fd, *[vals[i] for i in didx])
            grads = vjp(tuple(r[...] for r in ct_refs))
            for gref, i, g in zip(g_refs, didx, grads):
                acc = ins[i][2]
                if acc:
                    first = functools.reduce(jnp.logical_and, [pl.program_id(ax) == 0 for ax in acc])

                    @pl.when(first)
                    def _(gref=gref):
                        gref[...] = jnp.zeros_like(gref)

                    gref[...] += g.astype(gref.dtype)
                else:
                    gref[...] = g.astype(gref.dtype)

        g_specs = [pl.BlockSpec(ins[i][0], ins[i][1]) for i in didx]
        g_shape = [jax.ShapeDtypeStruct(args[i].shape, args[i].dtype) for i in didx]
        return pl.pallas_call(body, grid=grid, in_specs=in_specs + out_specs, out_specs=g_specs, out_shape=g_shape,
                              name=name + "_bwd", compiler_params=_cp(len(grid), vmem))(*args, *cts)

    @jax.custom_vjp
    def op(*args):
        return tuple(fwd_call(*args))

    def op_fwd(*args):
        return tuple(fwd_call(*args)), args

    def op_bwd(args, cts):
        it = iter(bwd_call(args, cts))
        return tuple(next(it) if ins[i][3] else jnp.zeros_like(args[i]) for i in range(n_in))

    op.defvjp(op_fwd, op_bwd)
    op.fwd_call = fwd_call
    op.bwd_all = op_bwd
    return op


def _row(tr, c):
    return ((tr, c), lambda i: (i, 0), None, True)


def _par(shape):
    nd = len(shape)
    return (shape, lambda i: (0,) * nd, (0,), True)


def _bdot(a, w):
    return jnp.dot(a.astype(BF16), w.astype(BF16), preferred_element_type=F32)


def _rms_f(x, g):
    return (x * lax.rsqrt(jnp.mean(x * x, axis=-1, keepdims=True) + NORM_EPS) * g,)


def _rms_op(name, R, D, out_dtype):
    tr = _tile(R, (256,))
    return _block_op(name, _rms_f, (R // tr,), [_row(tr, D), _par((1, D))],
                     [((R, D), out_dtype, (tr, D), lambda i: (i, 0))])


def rmsnorm(x, g, name):
    R, D = x.shape
    return _rms_op(name, R, D, F32)(x, g.reshape(1, D))[0]


def s5_epilogue(yc, u, d, w_glu, name):
    R, C = yc.shape
    tr = _tile(R, (256,))

    def f(yc, u, d, w):
        g = jax.nn.gelu(yc + d * u)
        return (g * jax.nn.sigmoid(_bdot(g, w)),)

    op = _block_op(name, f, (R // tr,), [_row(tr, C), _row(tr, C), _par((1, C)), _par((C, C))],
                   [((R, C), F32, (tr, C), lambda i: (i, 0))])
    return op(yc, u, d.reshape(1, C), w_glu)[0]


def pool_proj(p, w, scale, name):
    R, C = p.shape
    ng, pc, _ = w.shape
    tr = _tile(R, (256,))

    def f(p, w, s):
        ys = [_bdot(p[:, g * pc:(g + 1) * pc], w[g]) for g in range(ng)]
        return (jnp.concatenate(ys, axis=-1) * s,)

    op = _block_op(name, f, (R // tr,), [_row(tr, C), _par((ng, pc, pc)), _par((1, C))],
                   [((R, C), F32, (tr, C), lambda i: (i, 0))])
    return op(p, w, scale.reshape(1, C))[0]


def conv_post(h, ln_g, ln_b, w_pw, name):
    R, C = h.shape
    tr = _tile(R, (256,))

    def f(h, g, b, w):
        hc = h - jnp.mean(h, axis=-1, keepdims=True)
        y = hc * lax.rsqrt(jnp.mean(hc * hc, axis=-1, keepdims=True) + NORM_EPS) * g + b
        return (_bdot(jax.nn.silu(y), w),)

    op = _block_op(name, f, (R // tr,), [_row(tr, C), _par((1, C)), _par((1, C)), _par((C, C))],
                   [((R, C), F32, (tr, C), lambda i: (i, 0))])
    return op(h, ln_g.reshape(1, C), ln_b.reshape(1, C), w_pw)[0]


def _group_norm_op(name, R, C, n, out_dtype):
    tr = _tile(R, (256,))

    def f(*a):
        g = a[n]
        parts = [y * lax.rsqrt(jnp.mean(y * y, axis=-1, keepdims=True) + NORM_EPS) for y in a[:n]]
        return (jnp.concatenate(parts, axis=-1) * g,)

    return _block_op(name, f, (R // tr,), [_row(tr, C)] * n + [_par((1, n * C))],
                     [((R, n * C), out_dtype, (tr, n * C), lambda i: (i, 0))])


def _cross_attention_op(name, L, W, M, out_dtype):
    E = X_HEAD_DIM
    tq = _tile(L, (512,))

    def f(q, k, v):
        s = lax.dot_general(q.astype(BF16), k.astype(BF16), (((1,), (1,)), ((), ())),
                            preferred_element_type=F32) * (E ** -0.5)
        p = jax.nn.softmax(s, axis=-1)
        return (_bdot(p, v),)

    qspec = ((tq, E), lambda h, i: (i, h), None, True)
    kspec = ((M, E), lambda h, i: (0, h), (1,), True)
    return _block_op(name, f, (W // E, L // tq), [qspec, kspec, kspec],
                     [((L, W), out_dtype, (tq, E), lambda h, i: (i, h))])


def cross_attention(q, k, v, name):
    return _cross_attention_op(name, q.shape[0], q.shape[1], k.shape[0], F32)(q, k, v)[0]


def s5_discretise(lam_re, lam_im, log_dt, b_re_t, b_im_t, name):
    G, _, N = lam_re.shape
    C = b_re_t.shape[1]

    def f(lr, li, ldt, br, bi):
        dt = jnp.exp(ldt)
        mag = jnp.exp(lr * dt)
        ab_r, ab_i = mag * jnp.cos(li * dt), mag * jnp.sin(li * dt)
        den = lr * lr + li * li
        nr, ni = ab_r - 1.0, ab_i
        f_r = (nr * lr + ni * li) / den
        f_i = (ni * lr - nr * li) / den
        return ab_r, ab_i, f_r * br - f_i * bi, f_r * bi + f_i * br

    vec = ((G, 1, N), lambda i: (0, 0, 0), None, True)
    mat = ((G, C, N), lambda i: (0, 0, 0), None, True)
    ov = ((G, 1, N), F32, (G, 1, N), lambda i: (0, 0, 0))
    om = ((G, C, N), F32, (G, C, N), lambda i: (0, 0, 0))
    op = _block_op(name, f, (1,), [vec, vec, vec, mat, mat], [ov, ov, om, om])
    return op(lam_re, lam_im, log_dt, b_re_t, b_im_t)


def rel_bias_tables(rel_bias, onehot, name):
    B, H = rel_bias.shape
    P, _, Q = onehot.shape

    def f(rbt, oh):
        return (jnp.dot(rbt, oh, precision=lax.Precision.HIGHEST, preferred_element_type=F32),)

    op = _block_op(name, f, (P,), [((H, B), lambda p: (0, 0), (0,), True), ((None, B, Q), lambda p: (p, 0, 0), None, False)],
                   [((P, H, Q), F32, (None, H, Q), lambda p: (p, 0, 0))])
    return op(rel_bias.T, onehot)[0]


def _shift_down(x, s, row):
    return jnp.where(row >= s, pltpu.roll(x, s, 0), 0.0)


def _shift_up(x, s, row):
    n = x.shape[0]
    return jnp.where(row < n - s, pltpu.roll(x, n - s, 0), 0.0)


def _window_sum(x, w, row, shift):
    span = 1
    while span < w:
        x = x + shift(x, span, row)
        span *= 2
    return x


def _pool_call(u, d_out, name):
    L, C = u.shape
    pc = C // len(POOL_WINDOWS)
    assert pc % LANES == 0

    def body(x_ref, o_ref):
        row = lax.broadcasted_iota(jnp.int32, (L, pc), 0)
        for g, w in enumerate(POOL_WINDOWS):
            sl = slice(g * pc, (g + 1) * pc)
            x = x_ref[:, sl]
            cnt = jnp.minimum(row + 1, w).astype(F32)
            if d_out is None:
                o_ref[:, sl] = _window_sum(x, w, row, _shift_down) / cnt - x
            else:
                o_ref[:, sl] = _window_sum(x / cnt, w, row, _shift_up) - x

    src = u if d_out is None else d_out
    return pl.pallas_call(body, out_shape=jax.ShapeDtypeStruct((L, C), F32), name=name,
                          compiler_params=pltpu.CompilerParams(vmem_limit_bytes=VMEM_LIMIT))(src)


def _pool_mix(name):
    @jax.custom_vjp
    def op(u):
        return _pool_call(u, None, name + "_fwd")

    def fwd(u):
        return _pool_call(u, None, name + "_fwd"), u

    def bwd(u, dp):
        return (_pool_call(u, dp, name + "_bwd"),)

    op.defvjp(fwd, bwd)
    return op


def _conv_fwd(u, w, b, name):
    L, C2 = u.shape
    C = C2 // 2
    K = w.shape[0]
    nb = C // LANES

    def body(val_ref, gate_ref, w_ref, b_ref, o_ref):
        row = lax.broadcasted_iota(jnp.int32, (L, LANES), 0)
        h = val_ref[...] * jax.nn.sigmoid(gate_ref[...])
        acc = jnp.broadcast_to(b_ref[...], (L, LANES))
        for k in range(K):
            acc = acc + w_ref[k:k + 1, :] * _shift_down(h, K - 1 - k, row)
        o_ref[...] = acc

    blk = lambda off: pl.BlockSpec((L, LANES), lambda j: (0, j + off))
    return pl.pallas_call(
        body, grid=(nb,), in_specs=[blk(0), blk(nb), pl.BlockSpec((K, LANES), lambda j: (0, j)),
                                    pl.BlockSpec((1, LANES), lambda j: (0, j))],
        out_specs=blk(0), out_shape=jax.ShapeDtypeStruct((L, C), F32), name=name, compiler_params=_cp(1))(u, u, w, b)


def _conv_bwd(u, w, dh, name):
    L, C2 = u.shape
    C = C2 // 2
    K = w.shape[0]
    nb = C // LANES

    def body(val_ref, gate_ref, w_ref, dh_ref, dval_ref, dgate_ref, dw_ref, db_ref):
        row = lax.broadcasted_iota(jnp.int32, (L, LANES), 0)
        val = val_ref[...]
        sig = jax.nn.sigmoid(gate_ref[...])
        h = val * sig
        d = dh_ref[...]
        dh0 = jnp.zeros((L, LANES), F32)
        for k in range(K):
            s = K - 1 - k
            dh0 = dh0 + w_ref[k:k + 1, :] * _shift_up(d, s, row)
            dw_ref[k:k + 1, :] = jnp.sum(d * _shift_down(h, s, row), axis=0, keepdims=True)
        db_ref[...] = jnp.sum(d, axis=0, keepdims=True)
        dval_ref[...] = dh0 * sig
        dgate_ref[...] = dh0 * val * sig * (1.0 - sig)

    blk = lambda off: pl.BlockSpec((L, LANES), lambda j: (0, j + off))
    return pl.pallas_call(
        body, grid=(nb,), in_specs=[blk(0), blk(nb), pl.BlockSpec((K, LANES), lambda j: (0, j)), blk(0)],
        out_specs=[blk(0), blk(0), pl.BlockSpec((K, LANES), lambda j: (0, j)), pl.BlockSpec((1, LANES), lambda j: (0, j))],
        out_shape=[jax.ShapeDtypeStruct((L, C), F32), jax.ShapeDtypeStruct((L, C), F32),
                   jax.ShapeDtypeStruct((K, C), F32), jax.ShapeDtypeStruct((1, C), F32)],
        name=name, compiler_params=_cp(1))(u, u, w, dh)


def _glu_conv(name):
    @jax.custom_vjp
    def op(u, w, b):
        return _conv_fwd(u, w, b, name + "_fwd")

    def fwd(u, w, b):
        return _conv_fwd(u, w, b, name + "_fwd"), (u, w)

    def bwd(r, dh):
        u, w = r
        dval, dgate, dw, db = _conv_bwd(u, w, dh, name + "_bwd")
        return jnp.concatenate([dval, dgate], axis=-1), dw, db

    op.defvjp(fwd, bwd)
    return op


S5_BLOCK_CH = LANES
S5_BLOCK_ST = S5_BLOCK_CH // S5_CH_PER_GROUP * S5_STATE


def _s5_scan(br_ref, bi_ref, ar, ai, reverse):
    L, C = br_ref.shape
    T = SUBLANES
    row = lax.broadcasted_iota(jnp.int32, (T, C), 0)
    pw = [(ar, ai)]
    for _ in range(T - 1):
        pr, pi = pw[-1]
        pw.append((pr * ar - pi * ai, pr * ai + pi * ar))
    cr = jnp.zeros((T, C), F32)
    ci = jnp.zeros((T, C), F32)
    for r in range(T):
        e = (T - r) if reverse else (r + 1)
        cr = jnp.where(row == r, pw[e - 1][0], cr)
        ci = jnp.where(row == r, pw[e - 1][1], ci)
    steps = []
    s = 1
    while s < T:
        mask = (row < T - s) if reverse else (row >= s)
        steps.append((T - s if reverse else s, mask, pw[s - 1][0], pw[s - 1][1]))
        s *= 2
    nt = L // T
    last = 0 if reverse else T - 1

    def body(i, carry):
        kr, ki = carry
        t = (nt - 1 - i) if reverse else i
        off = pl.multiple_of(t * T, T)
        xr = br_ref[pl.ds(off, T), :]
        xi = bi_ref[pl.ds(off, T), :]
        for sh, mask, mr, mi in steps:
            sr = jnp.where(mask, pltpu.roll(xr, sh, 0), 0.0)
            si = jnp.where(mask, pltpu.roll(xi, sh, 0), 0.0)
            xr, xi = xr + mr * sr - mi * si, xi + mr * si + mi * sr
        xr, xi = xr + cr * kr - ci * ki, xi + cr * ki + ci * kr
        br_ref[pl.ds(off, T), :] = xr
        bi_ref[pl.ds(off, T), :] = xi
        return (jnp.broadcast_to(xr[last:last + 1, :], (T, C)), jnp.broadcast_to(xi[last:last + 1, :], (T, C)))

    z = jnp.zeros((T, C), F32)
    lax.fori_loop(0, nt, body, (z, z))


def _s5_specs(L):
    nb_axis = lambda j: (j, 0, 0)
    u = pl.BlockSpec((L, S5_BLOCK_CH), lambda j: (0, j))
    wb = pl.BlockSpec((None, S5_BLOCK_CH, S5_BLOCK_ST), nb_axis)
    a = pl.BlockSpec((1, S5_BLOCK_ST), lambda j: (0, j))
    wc = pl.BlockSpec((None, S5_BLOCK_ST, S5_BLOCK_CH), nb_axis)
    return u, wb, a, wc


def _s5_fwd(u, wbr, wbi, ar, ai, wcr, wci, name):
    L, C = u.shape
    nb = C // S5_BLOCK_CH
    us, wbs, as_, wcs = _s5_specs(L)

    def body(u_ref, wbr_ref, wbi_ref, ar_ref, ai_ref, wcr_ref, wci_ref, y_ref, xr, xi):
        ub = u_ref[...]
        xr[...] = _bdot(ub, wbr_ref[...])
        xi[...] = _bdot(ub, wbi_ref[...])
        _s5_scan(xr, xi, ar_ref[...], ai_ref[...], False)
        y_ref[...] = _bdot(xr[...], wcr_ref[...]) - _bdot(xi[...], wci_ref[...])

    return pl.pallas_call(
        body, grid=(nb,), in_specs=[us, wbs, wbs, as_, as_, wcs, wcs], out_specs=us,
        out_shape=jax.ShapeDtypeStruct((L, C), F32),
        scratch_shapes=[pltpu.VMEM((L, S5_BLOCK_ST), F32)] * 2, name=name, compiler_params=_cp(1))(
            u, wbr, wbi, ar, ai, wcr, wci)


def _dot_t(a, b):
    return lax.dot_general(a.astype(BF16), b.astype(BF16), (((0,), (0,)), ((), ())), preferred_element_type=F32)


def _dot_nt(a, b):
    return lax.dot_general(a.astype(BF16), b.astype(BF16), (((1,), (1,)), ((), ())), preferred_element_type=F32)


def _s5_bwd(u, wbr, wbi, ar, ai, wcr, wci, dy, name):
    L, C = u.shape
    nb = C // S5_BLOCK_CH
    us, wbs, as_, wcs = _s5_specs(L)
    T = SUBLANES

    def body(u_ref, wbr_ref, wbi_ref, ar_ref, ai_ref, wcr_ref, wci_ref, dy_ref,
             du_ref, dwbr_ref, dwbi_ref, dar_ref, dai_ref, dwcr_ref, dwci_ref, xr, xi, gr, gi):
        ub = u_ref[...]
        a_r, a_i = ar_ref[...], ai_ref[...]
        xr[...] = _bdot(ub, wbr_ref[...])
        xi[...] = _bdot(ub, wbi_ref[...])
        _s5_scan(xr, xi, a_r, a_i, False)
        d = dy_ref[...]
        dwcr_ref[...] = _dot_t(xr[...], d)
        dwci_ref[...] = -_dot_t(xi[...], d)
        gr[...] = _dot_nt(d, wcr_ref[...])
        gi[...] = -_dot_nt(d, wci_ref[...])
        _s5_scan(gr, gi, a_r, -a_i, True)

        row = lax.broadcasted_iota(jnp.int32, (T, S5_BLOCK_ST), 0)

        def da_body(i, carry):
            pr, pi, sr, si = carry
            off = pl.multiple_of(i * T, T)
            xr_t, xi_t = xr[pl.ds(off, T), :], xi[pl.ds(off, T), :]
            lr_t, li_t = gr[pl.ds(off, T), :], gi[pl.ds(off, T), :]
            qr = jnp.where(row == 0, pr, pltpu.roll(xr_t, 1, 0))
            qi = jnp.where(row == 0, pi, pltpu.roll(xi_t, 1, 0))
            sr = sr + qr * lr_t + qi * li_t
            si = si + qr * li_t - qi * lr_t
            return (jnp.broadcast_to(xr_t[T - 1:T, :], (T, S5_BLOCK_ST)),
                    jnp.broadcast_to(xi_t[T - 1:T, :], (T, S5_BLOCK_ST)), sr, si)

        z = jnp.zeros((T, S5_BLOCK_ST), F32)
        _, _, sr, si = lax.fori_loop(0, L // T, da_body, (z, z, z, z))
        dar_ref[...] = jnp.sum(sr, axis=0, keepdims=True)
        dai_ref[...] = jnp.sum(si, axis=0, keepdims=True)
        lr, li = gr[...], gi[...]
        dwbr_ref[...] = _dot_t(ub, lr)
        dwbi_ref[...] = _dot_t(ub, li)
        du_ref[...] = _dot_nt(lr, wbr_ref[...]) + _dot_nt(li, wbi_ref[...])

    sds = jax.ShapeDtypeStruct
    return pl.pallas_call(
        body, grid=(nb,), in_specs=[us, wbs, wbs, as_, as_, wcs, wcs, us],
        out_specs=[us, wbs, wbs, as_, as_, wcs, wcs],
        out_shape=[sds(u.shape, F32), sds(wbr.shape, F32), sds(wbi.shape, F32), sds(ar.shape, F32),
                   sds(ai.shape, F32), sds(wcr.shape, F32), sds(wci.shape, F32)],
        scratch_shapes=[pltpu.VMEM((L, S5_BLOCK_ST), F32)] * 4, name=name,
        compiler_params=_cp(1, VMEM_LIMIT_BIG))(u, wbr, wbi, ar, ai, wcr, wci, dy)


def _s5_core(name):
    @jax.custom_vjp
    def op(u, wbr, wbi, ar, ai, wcr, wci):
        return _s5_fwd(u, wbr, wbi, ar, ai, wcr, wci, name + "_fwd")

    def fwd(*a):
        return _s5_fwd(*a, name + "_fwd"), a

    def bwd(a, dy):
        return tuple(_s5_bwd(*a, dy, name + "_bwd"))

    op.defvjp(fwd, bwd)
    return op


def _att_tile_f(first, q, kp, kc, vp, vc, bias):
    nq = q.shape[0]
    hb = bias.shape[0]
    E = q.shape[1] // hb
    r = lax.broadcasted_iota(jnp.int32, (nq, 2 * nq), 0)
    c = lax.broadcasted_iota(jnp.int32, (nq, 2 * nq), 1)
    prev_ok = jnp.logical_and(jnp.logical_and(c < nq, c >= r), jnp.logical_not(first))
    valid = jnp.logical_or(prev_ok, jnp.logical_and(c >= nq, c - nq <= r))
    outs, lses = [], []
    for h in range(hb):
        sl = slice(h * E, (h + 1) * E)
        k = jnp.concatenate([kp[:, sl], kc[:, sl]], axis=0)
        v = jnp.concatenate([vp[:, sl], vc[:, sl]], axis=0)
        s = jnp.where(valid, _dot_nt(q[:, sl], k) * (E ** -0.5) + bias[h], NEG_INF)
        m = jnp.max(s, axis=-1, keepdims=True)
        p = jnp.exp(s - m)
        den = jnp.sum(p, axis=-1, keepdims=True)
        outs.append(_bdot(p, v) / den)
        lses.append(jnp.broadcast_to(m + jnp.log(den), (nq, E)))
    return jnp.concatenate(outs, axis=-1), jnp.concatenate(lses, axis=-1)


def _att_mix_f(*a):
    n = len(a) // 2
    o, l = a[:n], a[n:]
    m = functools.reduce(jnp.maximum, l)
    e = [jnp.exp(li - m) for li in l]
    return sum(ei * oi for ei, oi in zip(e, o)) / sum(e)


def _att_rows(start, dil):
    if dil == 1:
        return pl.ds(pl.multiple_of(start, ATT_BLOCK), ATT_BLOCK)
    return pl.ds(start, ATT_BLOCK, stride=dil)


def _att_blocks(L, dil):
    nb = L // dil // ATT_BLOCK
    return dil * nb, nb


def _att_specs(L, W):
    nblk = W // LANES
    col = lambda off: pl.BlockSpec((L, LANES), lambda j: (0, j + off))
    per_pattern = pl.BlockSpec((len(DILATED_PATTERNS), L, LANES), lambda j: (0, 0, j))
    hb = ATT_HEADS // nblk
    bias = pl.BlockSpec((len(DILATED_PATTERNS), hb, ATT_BLOCK, 2 * ATT_BLOCK), lambda j: (0, j, 0, 0))
    return nblk, col, per_pattern, bias


def _att_fwd(qkv, bias, name):
    L, W3 = qkv.shape
    W = W3 // 3
    nblk, col, per_pattern, bias_spec = _att_specs(L, W)
    P = len(DILATED_PATTERNS)

    def body(q_ref, k_ref, v_ref, b_ref, y_ref, o_ref, l_ref):
        for p, (_, dil) in enumerate(DILATED_PATTERNS):
            n_it, nb = _att_blocks(L, dil)

            def step(i, carry, p=p, dil=dil, nb=nb):
                n = i % nb
                cur = i // nb + n * (ATT_BLOCK * dil)
                prev = i // nb + jnp.maximum(n - 1, 0) * (ATT_BLOCK * dil)
                rc, rp = _att_rows(cur, dil), _att_rows(prev, dil)
                o, l = _att_tile_f(n == 0, q_ref[rc, :], k_ref[rp, :], k_ref[rc, :], v_ref[rp, :], v_ref[rc, :],
                                   b_ref[p])
                o_ref[p, rc, :] = o
                l_ref[p, rc, :] = l
                return carry

            lax.fori_loop(0, n_it, step, 0, unroll=ATT_UNROLL)

        def mix(i, carry):
            rows = pl.ds(pl.multiple_of(i * ATT_MIX_ROWS, ATT_MIX_ROWS), ATT_MIX_ROWS)
            y_ref[rows, :] = _att_mix_f(*[o_ref[p, rows, :] for p in range(P)], *[l_ref[p, rows, :] for p in range(P)])
            return carry

        lax.fori_loop(0, L // ATT_MIX_ROWS, mix, 0)

    sds = jax.ShapeDtypeStruct
    return pl.pallas_call(
        body, grid=(nblk,), in_specs=[col(0), col(nblk), col(2 * nblk), bias_spec],
        out_specs=[col(0), per_pattern, per_pattern],
        out_shape=[sds((L, W), F32), sds((P, L, W), F32), sds((P, L, W), F32)], name=name,
        compiler_params=_cp(1))(qkv, qkv, qkv, bias)


def _att_bwd(qkv, bias, o_all, l_all, dy, name):
    L, W3 = qkv.shape
    W = W3 // 3
    nblk, col, per_pattern, bias_spec = _att_specs(L, W)
    P = len(DILATED_PATTERNS)

    def body(q_ref, k_ref, v_ref, b_ref, o_ref, l_ref, dy_ref, dq_ref, dk_ref, dv_ref, db_ref, do_s, dl_s):
        def mix(i, carry):
            rows = pl.ds(pl.multiple_of(i * ATT_MIX_ROWS, ATT_MIX_ROWS), ATT_MIX_ROWS)
            _, mix_vjp = jax.vjp(_att_mix_f, *[o_ref[p, rows, :] for p in range(P)],
                                 *[l_ref[p, rows, :] for p in range(P)])
            g = mix_vjp(dy_ref[rows, :])
            for p in range(P):
                do_s[p, rows, :] = g[p]
                dl_s[p, rows, :] = g[P + p]
            return carry

        lax.fori_loop(0, L // ATT_MIX_ROWS, mix, 0)
        for ref in (dq_ref, dk_ref, dv_ref, db_ref):
            ref[...] = jnp.zeros_like(ref)

        def add(ref, rows, val):
            ref[rows, :] = ref[rows, :] + val

        for p, (_, dil) in enumerate(DILATED_PATTERNS):
            n_it, nb = _att_blocks(L, dil)

            def step(i, carry, p=p, dil=dil, nb=nb):
                n = i % nb
                first = n == 0
                cur = i // nb + n * (ATT_BLOCK * dil)
                prev = i // nb + jnp.maximum(n - 1, 0) * (ATT_BLOCK * dil)
                rc, rp = _att_rows(cur, dil), _att_rows(prev, dil)
                _, vjp = jax.vjp(functools.partial(_att_tile_f, first), q_ref[rc, :], k_ref[rp, :], k_ref[rc, :],
                                 v_ref[rp, :], v_ref[rc, :], b_ref[p])
                dq, dkp, dkc, dvp, dvc, db = vjp((do_s[p, rc, :], dl_s[p, rc, :]))
                add(dq_ref, rc, dq)
                add(dk_ref, rc, dkc)
                add(dv_ref, rc, dvc)
                db_ref[p] = db_ref[p] + db

                @pl.when(jnp.logical_not(first))
                def _():
                    add(dk_ref, rp, dkp)
                    add(dv_ref, rp, dvp)

                return carry

            lax.fori_loop(0, n_it, step, 0, unroll=ATT_UNROLL)

    sds = jax.ShapeDtypeStruct((L, W), F32)
    return pl.pallas_call(
        body, grid=(nblk,),
        in_specs=[col(0), col(nblk), col(2 * nblk), bias_spec, per_pattern, per_pattern, col(0)],
        out_specs=[col(0), col(0), col(0), bias_spec],
        out_shape=[sds, sds, sds, jax.ShapeDtypeStruct(bias.shape, F32)],
        scratch_shapes=[pltpu.VMEM((P, L, LANES), F32)] * 2, name=name, compiler_params=_cp(1, VMEM_LIMIT_BIG))(
            qkv, qkv, qkv, bias, o_all, l_all, dy)


def _dilated_attention(name):
    @jax.custom_vjp
    def op(qkv, bias):
        return _att_fwd(qkv, bias, name + "_fwd")[0]

    def fwd(qkv, bias):
        y, o_all, l_all = _att_fwd(qkv, bias, name + "_fwd")
        return y, (qkv, bias, o_all, l_all)

    def bwd(r, dy):
        dq, dk, dv, db = _att_bwd(*r, dy, name + "_bwd")
        return jnp.concatenate([dq, dk, dv], axis=-1), db

    op.defvjp(fwd, bwd)
    return op


def _t5_bucket(dist):
    n = np.maximum(dist, 0)
    max_exact = REL_BUCKETS // 2
    large = max_exact + (np.log(np.maximum(n, 1) / max_exact) / np.log(REL_MAX_DIST / max_exact)
                         * (REL_BUCKETS - max_exact)).astype(np.int64)
    large = np.minimum(large, REL_BUCKETS - 1)
    return np.where(n < max_exact, n, large).astype(np.int32)


def _bucket_onehot():
    a = np.arange(ATT_BLOCK)[:, None]
    b = np.arange(2 * ATT_BLOCK)[None, :]
    sub = a + ATT_BLOCK - b
    bucket = jnp.asarray(np.stack([_t5_bucket(sub * dil).reshape(-1) for _, dil in DILATED_PATTERNS]))
    ids = jnp.arange(REL_BUCKETS, dtype=jnp.int32)
    return (bucket[:, None, :] == ids[None, :, None]).astype(F32)


def loss_head(h, target, g, name):
    R, D = h.shape
    tr = _tile(R, (256,))

    def body(h_ref, t_ref, g_ref, l_ref, dh_ref, dg_ref):
        def lf(hv, gv):
            y = _rms_f(hv, gv)[0]
            return 0.5 * jnp.sum(jnp.mean(jnp.square(y - t_ref[...]), axis=-1))

        l, (dh, dg) = jax.value_and_grad(lf, argnums=(0, 1))(h_ref[...], g_ref[...])

        @pl.when(pl.program_id(0) == 0)
        def _():
            l_ref[...] = jnp.zeros_like(l_ref)
            dg_ref[...] = jnp.zeros_like(dg_ref)

        dh_ref[...] = dh
        dg_ref[...] += dg
        l_ref[...] += l

    rows = pl.BlockSpec((tr, D), lambda i: (i, 0))
    vec = pl.BlockSpec((1, D), lambda i: (0, 0))
    l, dh, dg = pl.pallas_call(
        body, grid=(R // tr,), in_specs=[rows, rows, vec],
        out_specs=[pl.BlockSpec((SUBLANES, LANES), lambda i: (0, 0)), rows, vec],
        out_shape=[jax.ShapeDtypeStruct((SUBLANES, LANES), F32), jax.ShapeDtypeStruct((R, D), F32),
                   jax.ShapeDtypeStruct((1, D), F32)], name=name, compiler_params=_cp(1))(h, target, g.reshape(1, D))
    return l[0, 0], dh, dg.reshape(D)


def _adamw_update(w, g, m, v):
    c1 = 1.0 - ADAM_B1 ** ADAM_STEP
    c2 = 1.0 - ADAM_B2 ** ADAM_STEP
    nm = ADAM_B1 * m + (1.0 - ADAM_B1) * g
    nv = ADAM_B2 * v + (1.0 - ADAM_B2) * jnp.square(g)
    return -ADAM_LR * ((nm / c1) / (jnp.sqrt(nv / c2) + ADAM_EPS) + ADAM_WD * w), nm, nv


def adamw_layers(w, parts, m, v, name):
    nl, a, b = w.shape
    n_parts = parts[0].shape[0]
    tr = _row_tile(a, b)

    def body(*refs):
        w_ref, p_refs, (m_ref, v_ref, g_ref, d_ref, nm_ref, nv_ref) = refs[0], refs[1:1 + nl], refs[1 + nl:]
        for l in range(nl):
            @pl.when(pl.program_id(0) == l)
            def _(p_ref=p_refs[l]):
                g = p_ref[0].astype(F32)
                for i in range(1, n_parts):
                    g = g + p_ref[i].astype(F32)
                d_ref[...], nm_ref[...], nv_ref[...] = _adamw_update(w_ref[...], g, m_ref[...], v_ref[...])
                g_ref[...] = g

    rows = pl.BlockSpec((None, tr, b), lambda l, i: (l, i, 0))
    part = lambda k: pl.BlockSpec((n_parts, tr, b), lambda l, i: (0, jnp.where(l == k, i, 0), 0))
    sds = jax.ShapeDtypeStruct((nl, a, b), F32)
    return pl.pallas_call(body, grid=(nl, a // tr), in_specs=[rows] + [part(k) for k in range(nl)] + [rows, rows],
                          out_specs=[rows] * 4, out_shape=[sds] * 4, name=name, compiler_params=_cp(2))(
                              w, *parts, m, v)


def adamw(w, parts, m, v, name):
    R, C = w.shape
    n_parts = parts.shape[0]
    tr = _row_tile(R, C)

    def body(w_ref, p_ref, m_ref, v_ref, g_ref, d_ref, nm_ref, nv_ref):
        g = p_ref[0].astype(F32)
        for i in range(1, n_parts):
            g = g + p_ref[i].astype(F32)
        d_ref[...], nm_ref[...], nv_ref[...] = _adamw_update(w_ref[...], g, m_ref[...], v_ref[...])
        g_ref[...] = g

    rows = pl.BlockSpec((tr, C), lambda i: (i, 0))
    sds = jax.ShapeDtypeStruct((R, C), F32)
    return pl.pallas_call(body, grid=(R // tr,),
                          in_specs=[rows, pl.BlockSpec((n_parts, tr, C), lambda i: (0, i, 0)), rows, rows],
                          out_specs=[rows] * 4, out_shape=[sds] * 4, name=name, compiler_params=_cp(1))(w, parts, m, v)


HBM_SPEC = pl.BlockSpec(memory_space=pltpu.HBM)
MESH_ID = pl.DeviceIdType.MESH


def _place():
    return lax.axis_index("x"), lax.axis_index("y"), lax.axis_index("c")


def _index(x, y, c):
    return 4 * x + 2 * y + c


def all_gather(x, name):
    def body(x_ref, out_ref, send_sems, recv_sems, local_sem):
        x_, y_, c_ = _place()
        me, sibling = (x_, y_, c_), (x_, y_, 1 - c_)
        chips = [(1 - x_, y_), (x_, 1 - y_), (1 - x_, 1 - y_)]

        def slot(px, py, pc):
            return out_ref.at[_index(px, py, pc)]

        def copy(k, block, to, src=None):
            return pltpu.make_async_remote_copy(
                src_ref=slot(*block) if src is None else src, dst_ref=slot(*block),
                send_sem=send_sems.at[k], recv_sem=recv_sems.at[k], device_id=to, device_id_type=MESH_ID)

        mine = pltpu.make_async_copy(x_ref, slot(*me), local_sem)
        mine.start()
        first = [copy(0, me, sibling, src=x_ref)]
        first += [copy(1 + j, me, (*chip, c_), src=x_ref) for j, chip in enumerate(chips)]
        for cp in first:
            cp.start()
        passed = [copy(4 + j, (*chip, c_), sibling) for j, chip in enumerate(chips)]
        for j, chip in enumerate(chips):
            copy(1 + j, (*chip, c_), me).wait_recv()
            passed[j].start()
        copy(0, sibling, me).wait_recv()
        for j, chip in enumerate(chips):
            copy(4 + j, (*chip, 1 - c_), me).wait_recv()
        for cp in first + passed:
            cp.wait_send()
        mine.wait()

    return pl.pallas_call(
        body, out_shape=jax.ShapeDtypeStruct((N_DEV,) + x.shape, x.dtype), in_specs=[HBM_SPEC], out_specs=HBM_SPEC,
        scratch_shapes=[pltpu.SemaphoreType.DMA((N_DEV - 1,)), pltpu.SemaphoreType.DMA((N_DEV - 1,)),
                        pltpu.SemaphoreType.DMA], name=name)(x)


def _chip(x, y):
    return 2 * x + y


def sibling_exchange(xs, name):
    n = len(xs)

    def body(*refs):
        x_refs, out_refs, (send_sems, recv_sems) = refs[:n], refs[n:2 * n], refs[2 * n:]
        x_, y_, c_ = _place()
        copies = [pltpu.make_async_remote_copy(src_ref=x_ref.at[:, 1 - c_], dst_ref=out_ref, send_sem=send_sems.at[p],
                                               recv_sem=recv_sems.at[p], device_id=(x_, y_, 1 - c_),
                                               device_id_type=MESH_ID)
                  for p, (x_ref, out_ref) in enumerate(zip(x_refs, out_refs))]
        for cp in copies:
            cp.start()
        for cp in copies:
            cp.wait()

    return pl.pallas_call(
        body, out_shape=[jax.ShapeDtypeStruct((x.shape[0],) + x.shape[2:], x.dtype) for x in xs],
        in_specs=[HBM_SPEC] * n, out_specs=[HBM_SPEC] * n,
        scratch_shapes=[pltpu.SemaphoreType.DMA((n,)), pltpu.SemaphoreType.DMA((n,))], name=name)(*xs)


def _row_tile(R, C):
    cap = max(SUBLANES, STREAM_BLOCK_BYTES // (4 * C))
    return _tile(R, [t for t in (512, 256, 128, 64, 32, 16, 8) if t <= cap])


def pair_sum(x, recv, name):
    nc, _, R, C = x.shape
    tr = _row_tile(R, C)
    core = lax.axis_index("c").astype(jnp.int32).reshape(1)

    def body(c_ref, a_ref, b_ref, o_ref):
        o_ref[...] = (a_ref[...].astype(F32) + b_ref[...].astype(F32)).astype(o_ref.dtype)

    blk = pl.BlockSpec((None, tr, C), lambda k, i, c_ref: (k, i, 0))
    grid_spec = pltpu.PrefetchScalarGridSpec(
        num_scalar_prefetch=1, grid=(nc, R // tr),
        in_specs=[pl.BlockSpec((None, None, tr, C), lambda k, i, c_ref: (k, c_ref[0], i, 0)), blk], out_specs=blk)
    return pl.pallas_call(body, grid_spec=grid_spec, out_shape=jax.ShapeDtypeStruct((nc, R, C), x.dtype), name=name,
                          compiler_params=_cp(2))(core, x, recv)


def chip_exchange(ss, name):
    n = len(ss)

    def body(*refs):
        copies = _chip_exchange_copies(refs[:n], refs[n:2 * n], 0, 1, *refs[2 * n:])
        for cp in copies:
            cp.start()
        for cp in copies:
            cp.wait()

    n_sem = n * (N_CHIP - 1)
    return pl.pallas_call(
        body, out_shape=[jax.ShapeDtypeStruct(s.shape, s.dtype) for s in ss], in_specs=[HBM_SPEC] * n,
        out_specs=[HBM_SPEC] * n,
        scratch_shapes=[pltpu.SemaphoreType.DMA((n_sem,)), pltpu.SemaphoreType.DMA((n_sem,)),
                        pltpu.SemaphoreType.DMA((n,))], name=name)(*ss)


def _block_diag(w, nb):
    G, a, b = w.shape
    gp = G // nb
    eye = jnp.eye(gp, dtype=w.dtype)
    return jnp.einsum('jgab,gh->jgahb', w.reshape(nb, gp, a, b), eye).reshape(nb, gp * a, gp * b)


def _split_columns(x, cuts):
    edges = (0,) + tuple(cuts) + (x.shape[1],)

    def split(x):
        return tuple(x[:, a:b] for a, b in zip(edges[:-1], edges[1:]))

    op = jax.custom_vjp(split)
    op.defvjp(lambda x: (split(x), None), lambda _, cts: (jnp.concatenate(cts, axis=-1),))
    return op(x)


def _mixers_and_memory(l, h, memn, bias_tabs, P):
    nm = lambda s: f"l{l}_{s}"
    L, D = h.shape
    GW = D // N_MIXERS
    G = GW // S5_CH_PER_GROUP

    row = lambda g: g.reshape(1, D)

    proj = _act_linear(nm("w_in"), _rms_op(nm("norm_mix"), L, D, BF16), 2, False)(h, row(P['norm_mix_g']), P['w_in'])
    u_a, u_b, u_c, qkv = _split_columns(proj, (GW, 2 * GW, 4 * GW))

    v3 = lambda a: a.reshape(G, 1, S5_STATE)
    log_dt = jnp.broadcast_to(P['s5_log_dt'][:, None, None], (G, 1, S5_STATE))
    a_r, a_i, bb_r, bb_i = s5_discretise(v3(P['s5_lam_re']), v3(P['s5_lam_im']), log_dt,
                                         P['s5_b_re'].transpose(0, 2, 1), P['s5_b_im'].transpose(0, 2, 1), nm("s5_disc"))
    nblk = GW // S5_BLOCK_CH
    y_s5 = _s5_core(nm("s5_core"))(
        u_a, _block_diag(bb_r, nblk), _block_diag(bb_i, nblk), a_r.reshape(1, G * S5_STATE), a_i.reshape(1, G * S5_STATE),
        _block_diag(P['s5_c_re'].transpose(0, 2, 1), nblk), _block_diag(P['s5_c_im'].transpose(0, 2, 1), nblk))
    y_a = s5_epilogue(y_s5, u_a, P['s5_d'], P['s5_w_glu'], nm("s5_glu"))

    y_b = pool_proj(_pool_mix(nm("pool_mix"))(u_b), P['pool_w'], P['pool_scale'], nm("pool_proj"))

    hc = _glu_conv(nm("conv_dw"))(u_c, P['conv_w_dw'], P['conv_b_dw'].reshape(1, GW))
    y_c = conv_post(hc, P['conv_ln_g'], P['conv_ln_b'], P['conv_w_pw'], nm("conv_post"))

    y_d = _dilated_attention(nm("att"))(qkv, bias_tabs)

    grp = _group_norm_op(nm("grp_norm"), L, GW, N_MIXERS, BF16)
    h = _act_linear(nm("w_out"), grp, N_MIXERS + 1, True)(y_a, y_b, y_c, y_d, row(P['grp_norm_g']), P['w_out'], h)

    xq = _act_linear(nm("w_xq"), _rms_op(nm("norm_x"), L, D, BF16), 2, False)(h, row(P['norm_x_g']), P['w_xq'])
    xk = _linear(nm("w_xk"))(memn, P['w_xk'])
    xv = _linear(nm("w_xv"))(memn, P['w_xv'])
    xat = _cross_attention_op(nm("xattn"), L, xq.shape[1], memn.shape[0], BF16)
    return _act_linear(nm("w_xo"), xat, 3, True)(xq, xk, xv, P['w_xo'], h)


def _bias_tables(rel_bias):
    tabs = rel_bias_tables(rel_bias, _bucket_onehot(), "rel_bias")
    return tabs.reshape(len(DILATED_PATTERNS), ATT_HEADS, ATT_BLOCK, 2 * ATT_BLOCK)


def _gather_weight(name, w):
    ax = SHARDED[name]
    dt = BF16 if name in GATHER_BF16 else F32
    nl, a, b = w.shape
    g = all_gather(w.astype(dt).reshape(nl * a, b), "ag_" + name).reshape(N_DEV, nl, a, b)
    if ax == 1:
        return g.transpose(1, 0, 2, 3).reshape(nl, N_DEV * a, b)
    return g.transpose(1, 2, 0, 3).reshape(nl, a, N_DEV * b)


def _scatter_grad(name, g):
    ax = SHARDED[name]
    nl = g.shape[0]
    if ax == 1:
        a, b = g.shape[1] // N_DEV, g.shape[2]
        s = g.reshape(nl, N_DEV, a, b).transpose(1, 0, 2, 3)
    else:
        a, b = g.shape[1], g.shape[2] // N_DEV
        s = g.reshape(nl, a, N_DEV, b).transpose(2, 0, 1, 3)
    s = s.reshape(N_CHIP, N_DEV // N_CHIP, nl * a, b)
    pair = pair_sum(s, sibling_exchange([s], "d2d_" + name)[0], "pairsum_" + name)
    return chip_exchange([pair], "ici_" + name)[0]


def _by_destination(name, g):
    if g.ndim == 2 and SHARDED[name] == 1:
        g = g.reshape(N_DEV, g.shape[0] // N_DEV, g.shape[1])
    elif g.ndim == 2:
        g = g.reshape(g.shape[0], N_DEV, g.shape[1] // N_DEV).transpose(1, 0, 2)
    return g.reshape(N_CHIP, N_DEV // N_CHIP, *g.shape[1:])


def _flatten_small(d):
    flat = jnp.concatenate([d[n].reshape(-1).astype(F32) for n in SMALL])
    pad = (-flat.shape[0]) % (LANES * SMALL_ROW_TILE)
    return jnp.pad(flat, (0, pad)).reshape(-1, LANES)


def _split_small(flat, like):
    flat = flat.reshape(-1)
    out, off = {}, 0
    for n in SMALL:
        sz = math.prod(like[n].shape)
        out[n] = flat[off:off + sz].reshape(like[n].shape)
        off += sz
    return out


def kernel(x, mem, rel_bias, mem_norm_g, norm_mix_g, w_in, s5_lam_re, s5_lam_im, s5_log_dt, s5_b_re, s5_b_im, s5_c_re, s5_c_im, s5_d, s5_w_glu, pool_w, pool_scale, conv_w_dw, conv_b_dw, conv_ln_g, conv_ln_b, conv_w_pw, grp_norm_g, w_out, norm_x_g, w_xq, w_xk, w_xv, w_xo, norm_mlp_g, w_up, w_down, norm_final_g, loss_target, m_rel_bias, m_mem_norm_g, m_norm_mix_g, m_w_in, m_s5_lam_re, m_s5_lam_im, m_s5_log_dt, m_s5_b_re, m_s5_b_im, m_s5_c_re, m_s5_c_im, m_s5_d, m_s5_w_glu, m_pool_w, m_pool_scale, m_conv_w_dw, m_conv_b_dw, m_conv_ln_g, m_conv_ln_b, m_conv_w_pw, m_grp_norm_g, m_w_out, m_norm_x_g, m_w_xq, m_w_xk, m_w_xv, m_w_xo, m_norm_mlp_g, m_w_up, m_w_down, m_norm_final_g, v_rel_bias, v_mem_norm_g, v_norm_mix_g, v_w_in, v_s5_lam_re, v_s5_lam_im, v_s5_log_dt, v_s5_b_re, v_s5_b_im, v_s5_c_re, v_s5_c_im, v_s5_d, v_s5_w_glu, v_pool_w, v_pool_scale, v_conv_w_dw, v_conv_b_dw, v_conv_ln_g, v_conv_ln_b, v_conv_w_pw, v_grp_norm_g, v_w_out, v_norm_x_g, v_w_xq, v_w_xk, v_w_xv, v_w_xo, v_norm_mlp_g, v_w_up, v_w_down, v_norm_final_g):
    w = dict(zip(WEIGHTS, (rel_bias, mem_norm_g, norm_mix_g, w_in, s5_lam_re, s5_lam_im, s5_log_dt, s5_b_re, s5_b_im, s5_c_re, s5_c_im, s5_d, s5_w_glu, pool_w, pool_scale, conv_w_dw, conv_b_dw, conv_ln_g, conv_ln_b, conv_w_pw, grp_norm_g, w_out, norm_x_g, w_xq, w_xk, w_xv, w_xo, norm_mlp_g, w_up, w_down, norm_final_g)))
    m = dict(zip(WEIGHTS, (m_rel_bias, m_mem_norm_g, m_norm_mix_g, m_w_in, m_s5_lam_re, m_s5_lam_im, m_s5_log_dt, m_s5_b_re, m_s5_b_im, m_s5_c_re, m_s5_c_im, m_s5_d, m_s5_w_glu, m_pool_w, m_pool_scale, m_conv_w_dw, m_conv_b_dw, m_conv_ln_g, m_conv_ln_b, m_conv_w_pw, m_grp_norm_g, m_w_out, m_norm_x_g, m_w_xq, m_w_xk, m_w_xv, m_w_xo, m_norm_mlp_g, m_w_up, m_w_down, m_norm_final_g)))
    v = dict(zip(WEIGHTS, (v_rel_bias, v_mem_norm_g, v_norm_mix_g, v_w_in, v_s5_lam_re, v_s5_lam_im, v_s5_log_dt, v_s5_b_re, v_s5_b_im, v_s5_c_re, v_s5_c_im, v_s5_d, v_s5_w_glu, v_pool_w, v_pool_scale, v_conv_w_dw, v_conv_b_dw, v_conv_ln_g, v_conv_ln_b, v_conv_w_pw, v_grp_norm_g, v_w_out, v_norm_x_g, v_w_xq, v_w_xk, v_w_xv, v_w_xo, v_norm_mlp_g, v_w_up, v_w_down, v_norm_final_g)))

    full = {n: (_gather_weight(n, w[n]) if n in SHARDED else w[n]) for n in WEIGHTS if n != 'norm_final_g'}
    L, D = x.shape[1:]

    memn, mem_vjp = jax.vjp(lambda a, g: rmsnorm(a, g, "mem_norm"), mem[0], w['mem_norm_g'])
    tabs, tabs_vjp = jax.vjp(_bias_tables, w['rel_bias'])
    h = x[0]
    front_vjps, mlps = [], []
    for l in range(DEPTH):
        h, fv = jax.vjp(functools.partial(_mixers_and_memory, l), h, memn, tabs, {n: full[n][l] for n in FRONT_WEIGHTS})
        norm = _rms_op(f"l{l}_norm_mlp", L, D, BF16)
        h, saved = _mlp_fwd(f"l{l}_mlp", norm, h, full['norm_mlp_g'][l].reshape(1, D), full['w_up'][l], full['w_down'][l])
        front_vjps.append(fv)
        mlps.append((norm, saved))
    loss_local, dh, d_final_g = loss_head(h, loss_target[0], w['norm_final_g'], "loss_head")
    loss = lax.psum(loss_local, MESH_AXES)

    layer_grads, arrived = [None] * DEPTH, [None] * DEPTH
    pending, dmemn, dtabs = None, 0.0, 0.0
    for l in reversed(range(DEPTH)):
        norm, saved = mlps[l]
        dh, dg_mlp, dw_up, dw_down, lands = _mlp_bwd(f"l{l}_mlp", norm, saved, dh, pending)
        if pending is not None:
            arrived[l + 1] = lands
        dh, dmemn_l, dtabs_l, dP = front_vjps[l](dh)
        dmemn, dtabs = dmemn + dmemn_l, dtabs + dtabs_l
        layer_grads[l] = dict(dP, norm_mlp_g=dg_mlp.reshape(D), w_up=dw_up, w_down=dw_down)
        by_dest = [_by_destination(n, layer_grads[l][n]) for n in GATHER_BF16]
        theirs = sibling_exchange(by_dest, f"d2d_l{l}")
        pending = [pair_sum(s, t, f"pairsum_l{l}_{n}") for n, s, t in zip(GATHER_BF16, by_dest, theirs)]
    arrived[0] = chip_exchange(pending, "ici_l0")
    dx = dh
    dfull = {n: jnp.concatenate([layer_grads[l][n][None] for l in range(DEPTH)])
             for n in LAYER_WEIGHTS if n not in GATHER_BF16}
    dfull['mem_norm_g'] = mem_vjp(dmemn)[1]
    dfull['rel_bias'] = tabs_vjp(dtabs)[0]
    dfull['norm_final_g'] = d_final_g

    grads, deltas, new_m, new_v = {}, {}, {}, {}
    for k, n in enumerate(GATHER_BF16):
        res = adamw_layers(w[n], [arrived[l][k] for l in range(DEPTH)], m[n], v[n], "adamw_" + n)
        grads[n], deltas[n], new_m[n], new_v[n] = res
    for n in SHARDED:
        if n in GATHER_BF16:
            continue
        parts = _scatter_grad(n, dfull[n])
        shp = w[n].shape
        two_d = lambda a: a.reshape(shp[0] * shp[1], shp[2])
        res = adamw(two_d(w[n]), parts, two_d(m[n]), two_d(v[n]), "adamw_" + n)
        grads[n], deltas[n], new_m[n], new_v[n] = (r.reshape(shp) for r in res)

    parts = all_gather(_flatten_small(dfull), "ag_small_grads")
    res = adamw(_flatten_small(w), parts, _flatten_small(m), _flatten_small(v), "adamw_small")
    for dst, r in zip((grads, deltas, new_m, new_v), res):
        dst.update(_split_small(r, w))

    return (loss, dx[None], *[grads[n] for n in WEIGHTS], *[deltas[n] for n in WEIGHTS],
            *[new_m[n] for n in WEIGHTS], *[new_v[n] for n in WEIGHTS])
```

```python
import functools
import math

import numpy as np
import jax
import jax.numpy as jnp
from jax import lax
from jax.experimental import pallas as pl
from jax.experimental.pallas import tpu as pltpu

F32 = jnp.float32
BF16 = jnp.bfloat16

DEPTH = 4
N_MIXERS = 4
S5_CH_PER_GROUP = 16
S5_STATE = 64
POOL_WINDOWS = (2, 4, 8, 16)
CONV_WIDTH = 31
ATT_HEADS = 8
DILATED_PATTERNS = ((128, 1), (512, 4), (2048, 16))
ATT_BLOCK = 128
ATT_MIX_ROWS = 256
ATT_UNROLL = 2
REL_BUCKETS = 32
REL_MAX_DIST = 2048
X_HEADS = 4
X_HEAD_DIM = 128
NORM_EPS = 1e-6
NEG_INF = -1e30
ADAM_LR = 0.001
ADAM_B1 = 0.9
ADAM_B2 = 0.999
ADAM_EPS = 1e-08
ADAM_WD = 0.01
ADAM_STEP = 10

LANES = 128
SUBLANES = 8
VMEM_BYTES = 64 * 1024 * 1024
VMEM_LIMIT = (VMEM_BYTES * 3) // 4
VMEM_LIMIT_BIG = (VMEM_BYTES * 7) // 8
STREAM_BLOCK_BYTES = 1024 * 1024
SMALL_ROW_TILE = 512
N_DEV = 8
N_CHIP = 4
MESH_AXES = ("x", "y", "c")

WEIGHTS = ['rel_bias', 'mem_norm_g', 'norm_mix_g', 'w_in', 's5_lam_re', 's5_lam_im', 's5_log_dt', 's5_b_re',
           's5_b_im', 's5_c_re', 's5_c_im', 's5_d', 's5_w_glu', 'pool_w', 'pool_scale', 'conv_w_dw', 'conv_b_dw',
           'conv_ln_g', 'conv_ln_b', 'conv_w_pw', 'grp_norm_g', 'w_out', 'norm_x_g', 'w_xq', 'w_xk', 'w_xv', 'w_xo',
           'norm_mlp_g', 'w_up', 'w_down', 'norm_final_g']
SHARDED = {'w_in': 2, 's5_w_glu': 1, 'conv_w_dw': 2, 'conv_w_pw': 1, 'w_out': 1, 'w_xq': 1, 'w_xk': 1, 'w_xv': 1,
           'w_xo': 2, 'w_up': 2, 'w_down': 1}
GATHER_BF16 = ('w_in', 'w_out', 'w_xq', 'w_xk', 'w_xv', 'w_xo', 'w_up', 'w_down')
SMALL = [n for n in WEIGHTS if n not in SHARDED]
LAYER_WEIGHTS = [n for n in WEIGHTS if n not in ('rel_bias', 'mem_norm_g', 'norm_final_g')]
FRONT_WEIGHTS = [n for n in LAYER_WEIGHTS if n not in ('norm_mlp_g', 'w_up', 'w_down')]

def _cp(n_axes, vmem=VMEM_LIMIT):
    return pltpu.CompilerParams(dimension_semantics=("arbitrary",) * n_axes, vmem_limit_bytes=vmem)


def _tile(n, prefs):
    for t in prefs:
        if n % t == 0:
            return t
    return n


MM_TILE = 1024
MM_TILE_K = 2048
MM_FULL_K = 4096


def _chip_exchange_copies(srcs, dsts, part, n_parts, send_sems, recv_sems, local_sems):
    x_, y_, c_ = _place()
    me = _chip(x_, y_)
    copies = []
    for p, (src, dst) in enumerate(zip(srcs, dsts)):
        n = src.shape[1] // n_parts
        rows = pl.ds(part * n, n)
        copies.append(pltpu.make_async_copy(src.at[me, rows], dst.at[me, rows], local_sems.at[p]))
        for k in range(1, N_CHIP):
            px = 1 - x_ if k & 2 else x_
            py = 1 - y_ if k & 1 else y_
            s = p * (N_CHIP - 1) + k - 1
            copies.append(pltpu.make_async_remote_copy(
                src_ref=src.at[_chip(px, py), rows], dst_ref=dst.at[me, rows], send_sem=send_sems.at[s],
                recv_sem=recv_sems.at[s], device_id=(px, py, c_), device_id_type=MESH_ID))
    return copies


def _mm(a, b, *, ta=False, tb=False, res=None, out_dtype=F32, epilogue=None, pre=None, rider=None, dest_cols=None,
        name):
    if ta:
        K, M = a.shape
    else:
        M, K = a.shape
    if tb:
        N, K2 = b.shape
    else:
        K2, N = b.shape
    assert K == K2, (a.shape, b.shape, ta, tb)
    wide_f32 = K >= MM_TILE_K and F32 in (a.dtype, b.dtype)
    tm = _tile(M, (MM_TILE // 2 if wide_f32 and not ta else MM_TILE, 512, 256, 128))
    tn = _tile(N, (MM_TILE, 512, 256, 128)) if dest_cols is None else dest_cols
    tk = K if K <= MM_FULL_K else _tile(K, (MM_TILE_K, 1024, 512, 256, 128))
    nk = K // tk
    a_spec = pl.BlockSpec((tk, tm), lambda i, j, k: (k, i)) if ta else pl.BlockSpec((tm, tk), lambda i, j, k: (i, k))
    b_spec = pl.BlockSpec((tn, tk), lambda i, j, k: (j, k)) if tb else pl.BlockSpec((tk, tn), lambda i, j, k: (k, j))
    o_spec = pl.BlockSpec((tm, tn), lambda i, j, k: (i, j))
    dn = (((0 if ta else 1,), (1 if tb else 0,)), ((), ()))
    extra = [x for x in (res, pre) if x is not None]
    assert not (res is not None and pre is not None)
    assert dest_cols is None or (epilogue is None and not extra and tn <= MM_TILE)
    n_out = 2 if epilogue == 'relu_sq' else 1
    n_pairs = len(rider[0]) if rider is not None else 0
    grid = (M // tm, N // tn, nk)

    def body(*refs):
        a_ref, b_ref = refs[:2]
        x_ref = refs[2] if extra else None
        n_in = 2 + len(extra) + 2 * n_pairs
        o_refs = refs[n_in:n_in + n_out]
        scratch = refs[n_in + n_out + n_pairs:]
        acc = scratch[0] if nk > 1 else None
        if rider is not None:
            ride = lambda: _chip_exchange_copies(refs[n_in - 2 * n_pairs:n_in - n_pairs],
                                                 refs[n_in + n_out:n_in + n_out + n_pairs], rider[2], rider[3],
                                                 *scratch[-3:])
            ids = [pl.program_id(ax) for ax in range(3)]
            at_start = functools.reduce(jnp.logical_and, [i == 0 for i in ids])
            at_end = functools.reduce(jnp.logical_and, [i == g - 1 for i, g in zip(ids, grid)])

            @pl.when(at_start)
            def _():
                for cp in ride():
                    cp.start()

        def finish(r):
            if res is not None:
                r = r + x_ref[...]
            if epilogue == 'relu_sq':
                o_refs[0][...] = r
                o_refs[1][...] = jnp.square(jnp.maximum(r, 0.0)).astype(out_dtype)
            elif epilogue == 'relu_sq_grad':
                o_refs[0][...] = (r * (2.0 * jnp.maximum(x_ref[...], 0.0))).astype(out_dtype)
            else:
                o_refs[0][...] = r.astype(out_dtype)

        part = lax.dot_general(a_ref[...].astype(BF16), b_ref[...].astype(BF16), dn, preferred_element_type=F32)
        if nk == 1:
            finish(part)
        else:
            k = pl.program_id(2)

            @pl.when(k == 0)
            def _():
                acc[...] = part

            @pl.when(k > 0)
            def _():
                acc[...] += part

            @pl.when(k == nk - 1)
            def _():
                finish(acc[...])

        if rider is not None:
            @pl.when(at_end)
            def _():
                for cp in ride():
                    cp.wait()

    out_shape = [jax.ShapeDtypeStruct((M, N), F32 if epilogue == 'relu_sq' else out_dtype)]
    if n_out == 2:
        out_shape.append(jax.ShapeDtypeStruct((M, N), out_dtype))
    in_specs = [a_spec, b_spec] + [o_spec] * len(extra)
    out_specs = [o_spec] * n_out
    if dest_cols is not None:
        out_shape = [jax.ShapeDtypeStruct((N // tn, M, tn), out_dtype)]
        out_specs = [pl.BlockSpec((None, tm, tn), lambda i, j, k: (j, i, 0))]
    scratch = [pltpu.VMEM((tm, tn), F32)] if nk > 1 else []
    args, aliases = (a, b, *extra), {}
    if rider is not None:
        srcs, dsts = rider[0], rider[1]
        in_specs += [HBM_SPEC] * (2 * n_pairs)
        out_specs += [HBM_SPEC] * n_pairs
        out_shape += [jax.ShapeDtypeStruct(d.shape, d.dtype) for d in dsts]
        n_sem = n_pairs * (N_CHIP - 1)
        scratch += [pltpu.SemaphoreType.DMA((n_sem,)), pltpu.SemaphoreType.DMA((n_sem,)),
                    pltpu.SemaphoreType.DMA((n_pairs,))]
        aliases = {len(args) + n_pairs + p: n_out + p for p in range(n_pairs)}
        args += (*srcs, *dsts)
    outs = pl.pallas_call(
        body, grid=grid, in_specs=in_specs, out_specs=out_specs, out_shape=out_shape, scratch_shapes=scratch,
        input_output_aliases=aliases, name=name, compiler_params=_cp(3, VMEM_LIMIT_BIG))(*args)
    return outs[0] if len(outs) == 1 else tuple(outs)


def _linear(name):
    @jax.custom_vjp
    def lin(a, w):
        return _mm(a, w, name=name + "_fwd")

    def fwd(a, w):
        return _mm(a, w, name=name + "_fwd"), (a, w)

    def bwd(r, dy):
        a, w = r
        da = _mm(dy, w, tb=True, name=name + "_dx")
        dw = _mm(a, dy, ta=True, out_dtype=w.dtype, name=name + "_dw")
        return da, dw

    lin.defvjp(fwd, bwd)
    return lin


def _act_linear(name, act, n_in, with_res):
    def run(*a):
        ins, w = a[:n_in], a[n_in]
        x = act.fwd_call(*ins)[0]
        return _mm(x, w, res=a[n_in + 1] if with_res else None, name=name + "_fwd"), (ins, x, w)

    @jax.custom_vjp
    def op(*a):
        return run(*a)[0]

    def bwd(r, dy):
        ins, x, w = r
        dx = _mm(dy, w, tb=True, name=name + "_dx")
        dw = _mm(x, dy, ta=True, out_dtype=w.dtype, name=name + "_dw")
        return (*act.bwd_all(ins, (dx,)), dw) + ((dy,) if with_res else ())

    op.defvjp(run, bwd)
    return op


def _mlp_fwd(name, norm, h, g, w_up, w_down):
    hn = norm.fwd_call(h, g)[0]
    a, r = _mm(hn, w_up, epilogue='relu_sq', out_dtype=BF16, name=name + "_up_fwd")
    return _mm(r, w_down, res=h, name=name + "_down_fwd"), (h, g, hn, a, r, w_up, w_down)


MLP_BWD_MATMULS = 4


def _mlp_bwd(name, norm, saved, dy, pending):
    h, g, hn, a, r, w_up, w_down = saved
    lands = None if pending is None else [lax.empty(p.shape, p.dtype) for p in pending]

    def mm(i, *args, **kw):
        nonlocal lands
        if pending is None:
            return _mm(*args, **kw)
        out, *lands = _mm(*args, rider=(pending, lands, i, MLP_BWD_MATMULS), **kw)
        return out

    da = mm(0, dy, w_down, tb=True, epilogue='relu_sq_grad', pre=a, out_dtype=BF16, name=name + "_down_dx")
    dw_down = mm(1, r, dy, ta=True, out_dtype=w_down.dtype, name=name + "_down_dw")
    dhn = mm(2, da, w_up, tb=True, name=name + "_up_dx")
    dw_up = mm(3, hn, da, ta=True, out_dtype=w_up.dtype, dest_cols=w_up.shape[1] // N_DEV, name=name + "_up_dw")
    dh, dg = norm.bwd_all((h, g), (dhn,))
    return dh + dy, dg, dw_up, dw_down, lands


def _block_op(name, f, grid, ins, outs, vmem=VMEM_LIMIT):
    n_in, n_out = len(ins), len(outs)
    in_specs = [pl.BlockSpec(bs, im) for bs, im, _, _ in ins]
    out_specs = [pl.BlockSpec(bs, im) for _, _, bs, im in outs]
    out_shape = [jax.ShapeDtypeStruct(s, d) for s, d, _, _ in outs]
    didx = [i for i in range(n_in) if ins[i][3]]

    def fwd_call(*args):
        def body(*refs):
            res = f(*[r[...] for r in refs[:n_in]])
            for r, o in zip(refs[n_in:], res):
                r[...] = o.astype(r.dtype)

        return pl.pallas_call(body, grid=grid, in_specs=in_specs, out_specs=out_specs, out_shape=out_shape,
                              name=name + "_fwd", compiler_params=_cp(len(grid), vmem))(*args)

    def bwd_call(args, cts):
        def body(*refs):
            vals = [r[...] for r in refs[:n_in]]
            ct_refs = refs[n_in:n_in + n_out]
            g_refs = refs[n_in + n_out:]

            def fd(*dv):
                full = list(vals)
                for i, v in zip(didx, dv):
                    full[i] = v
                return f(*full)

            _, vjp = jax.vjp(fd, *[vals[i] for i in didx])
            grads = vjp(tuple(r[...] for r in ct_refs))
            for gref, i, g in zip(g_refs, didx, grads):
                acc = ins[i][2]
                if acc:
                    first = functools.reduce(jnp.logical_and, [pl.program_id(ax) == 0 for ax in acc])

                    @pl.when(first)
                    def _(gref=gref):
                        gref[...] = jnp.zeros_like(gref)

                    gref[...] += g.astype(gref.dtype)
                else:
                    gref[...] = g.astype(gref.dtype)

        g_specs = [pl.BlockSpec(ins[i][0], ins[i][1]) for i in didx]
        g_shape = [jax.ShapeDtypeStruct(args[i].shape, args[i].dtype) for i in didx]
        return pl.pallas_call(body, grid=grid, in_specs=in_specs + out_specs, out_specs=g_specs, out_shape=g_shape,
                              name=name + "_bwd", compiler_params=_cp(len(grid), vmem))(*args, *cts)

    @jax.custom_vjp
    def op(*args):
        return tuple(fwd_call(*args))

    def op_fwd(*args):
        return tuple(fwd_call(*args)), args

    def op_bwd(args, cts):
        it = iter(bwd_call(args, cts))
        return tuple(next(it) if ins[i][3] else jnp.zeros_like(args[i]) for i in range(n_in))

    op.defvjp(op_fwd, op_bwd)
    op.fwd_call = fwd_call
    op.bwd_all = op_bwd
    return op


def _row(tr, c):
    return ((tr, c), lambda i: (i, 0), None, True)


def _par(shape):
    nd = len(shape)
    return (shape, lambda i: (0,) * nd, (0,), True)


def _bdot(a, w):
    return jnp.dot(a.astype(BF16), w.astype(BF16), preferred_element_type=F32)


def _rms_f(x, g):
    return (x * lax.rsqrt(jnp.mean(x * x, axis=-1, keepdims=True) + NORM_EPS) * g,)


def _rms_op(name, R, D, out_dtype):
    tr = _tile(R, (256,))
    return _block_op(name, _rms_f, (R // tr,), [_row(tr, D), _par((1, D))],
                     [((R, D), out_dtype, (tr, D), lambda i: (i, 0))])


def rmsnorm(x, g, name):
    R, D = x.shape
    return _rms_op(name, R, D, F32)(x, g.reshape(1, D))[0]


def s5_epilogue(yc, u, d, w_glu, name):
    R, C = yc.shape
    tr = _tile(R, (256,))

    def f(yc, u, d, w):
        g = jax.nn.gelu(yc + d * u)
        return (g * jax.nn.sigmoid(_bdot(g, w)),)

    op = _block_op(name, f, (R // tr,), [_row(tr, C), _row(tr, C), _par((1, C)), _par((C, C))],
                   [((R, C), F32, (tr, C), lambda i: (i, 0))])
    return op(yc, u, d.reshape(1, C), w_glu)[0]


def pool_proj(p, w, scale, name):
    R, C = p.shape
    ng, pc, _ = w.shape
    tr = _tile(R, (256,))

    def f(p, w, s):
        ys = [_bdot(p[:, g * pc:(g + 1) * pc], w[g]) for g in range(ng)]
        return (jnp.concatenate(ys, axis=-1) * s,)

    op = _block_op(name, f, (R // tr,), [_row(tr, C), _par((ng, pc, pc)), _par((1, C))],
                   [((R, C), F32, (tr, C), lambda i: (i, 0))])
    return op(p, w, scale.reshape(1, C))[0]


def conv_post(h, ln_g, ln_b, w_pw, name):
    R, C = h.shape
    tr = _tile(R, (256,))

    def f(h, g, b, w):
        hc = h - jnp.mean(h, axis=-1, keepdims=True)
        y = hc * lax.rsqrt(jnp.mean(hc * hc, axis=-1, keepdims=True) + NORM_EPS) * g + b
        return (_bdot(jax.nn.silu(y), w),)

    op = _block_op(name, f, (R // tr,), [_row(tr, C), _par((1, C)), _par((1, C)), _par((C, C))],
                   [((R, C), F32, (tr, C), lambda i: (i, 0))])
    return op(h, ln_g.reshape(1, C), ln_b.reshape(1, C), w_pw)[0]


def _group_norm_op(name, R, C, n, out_dtype):
    tr = _tile(R, (256,))

    def f(*a):
        g = a[n]
        parts = [y * lax.rsqrt(jnp.mean(y * y, axis=-1, keepdims=True) + NORM_EPS) for y in a[:n]]
        return (jnp.concatenate(parts, axis=-1) * g,)

    return _block_op(name, f, (R // tr,), [_row(tr, C)] * n + [_par((1, n * C))],
                     [((R, n * C), out_dtype, (tr, n * C), lambda i: (i, 0))])


def _cross_attention_op(name, L, W, M, out_dtype):
    E = X_HEAD_DIM
    tq = _tile(L, (512,))

    def f(q, k, v):
        s = lax.dot_general(q.astype(BF16), k.astype(BF16), (((1,), (1,)), ((), ())),
                            preferred_element_type=F32) * (E ** -0.5)
        p = jax.nn.softmax(s, axis=-1)
        return (_bdot(p, v),)

    qspec = ((tq, E), lambda h, i: (i, h), None, True)
    kspec = ((M, E), lambda h, i: (0, h), (1,), True)
    return _block_op(name, f, (W // E, L // tq), [qspec, kspec, kspec],
                     [((L, W), out_dtype, (tq, E), lambda h, i: (i, h))])


def cross_attention(q, k, v, name):
    return _cross_attention_op(name, q.shape[0], q.shape[1], k.shape[0], F32)(q, k, v)[0]


def s5_discretise(lam_re, lam_im, log_dt, b_re_t, b_im_t, name):
    G, _, N = lam_re.shape
    C = b_re_t.shape[1]

    def f(lr, li, ldt, br, bi):
        dt = jnp.exp(ldt)
        mag = jnp.exp(lr * dt)
        ab_r, ab_i = mag * jnp.cos(li * dt), mag * jnp.sin(li * dt)
        den = lr * lr + li * li
        nr, ni = ab_r - 1.0, ab_i
        f_r = (nr * lr + ni * li) / den
        f_i = (ni * lr - nr * li) / den
        return ab_r, ab_i, f_r * br - f_i * bi, f_r * bi + f_i * br

    vec = ((G, 1, N), lambda i: (0, 0, 0), None, True)
    mat = ((G, C, N), lambda i: (0, 0, 0), None, True)
    ov = ((G, 1, N), F32, (G, 1, N), lambda i: (0, 0, 0))
    om = ((G, C, N), F32, (G, C, N), lambda i: (0, 0, 0))
    op = _block_op(name, f, (1,), [vec, vec, vec, mat, mat], [ov, ov, om, om])
    return op(lam_re, lam_im, log_dt, b_re_t, b_im_t)


def rel_bias_tables(rel_bias, onehot, name):
    B, H = rel_bias.shape
    P, _, Q = onehot.shape

    def f(rbt, oh):
        return (jnp.dot(rbt, oh, precision=lax.Precision.HIGHEST, preferred_element_type=F32),)

    op = _block_op(name, f, (P,), [((H, B), lambda p: (0, 0), (0,), True), ((None, B, Q), lambda p: (p, 0, 0), None, False)],
                   [((P, H, Q), F32, (None, H, Q), lambda p: (p, 0, 0))])
    return op(rel_bias.T, onehot)[0]


def _shift_down(x, s, row):
    return jnp.where(row >= s, pltpu.roll(x, s, 0), 0.0)


def _shift_up(x, s, row):
    n = x.shape[0]
    return jnp.where(row < n - s, pltpu.roll(x, n - s, 0), 0.0)


def _window_sum(x, w, row, shift):
    span = 1
    while span < w:
        x = x + shift(x, span, row)
        span *= 2
    return x


def _pool_call(u, d_out, name):
    L, C = u.shape
    pc = C // len(POOL_WINDOWS)
    assert pc % LANES == 0

    def body(x_ref, o_ref):
        row = lax.broadcasted_iota(jnp.int32, (L, pc), 0)
        for g, w in enumerate(POOL_WINDOWS):
            sl = slice(g * pc, (g + 1) * pc)
            x = x_ref[:, sl]
            cnt = jnp.minimum(row + 1, w).astype(F32)
            if d_out is None:
                o_ref[:, sl] = _window_sum(x, w, row, _shift_down) / cnt - x
            else:
                o_ref[:, sl] = _window_sum(x / cnt, w, row, _shift_up) - x

    src = u if d_out is None else d_out
    return pl.pallas_call(body, out_shape=jax.ShapeDtypeStruct((L, C), F32), name=name,
                          compiler_params=pltpu.CompilerParams(vmem_limit_bytes=VMEM_LIMIT))(src)


def _pool_mix(name):
    @jax.custom_vjp
    def op(u):
        return _pool_call(u, None, name + "_fwd")

    def fwd(u):
        return _pool_call(u, None, name + "_fwd"), u

    def bwd(u, dp):
        return (_pool_call(u, dp, name + "_bwd"),)

    op.defvjp(fwd, bwd)
    return op


def _conv_fwd(u, w, b, name):
    L, C2 = u.shape
    C = C2 // 2
    K = w.shape[0]
    nb = C // LANES

    def body(val_ref, gate_ref, w_ref, b_ref, o_ref):
        row = lax.broadcasted_iota(jnp.int32, (L, LANES), 0)
        h = val_ref[...] * jax.nn.sigmoid(gate_ref[...])
        acc = jnp.broadcast_to(b_ref[...], (L, LANES))
        for k in range(K):
            acc = acc + w_ref[k:k + 1, :] * _shift_down(h, K - 1 - k, row)
        o_ref[...] = acc

    blk = lambda off: pl.BlockSpec((L, LANES), lambda j: (0, j + off))
    return pl.pallas_call(
        body, grid=(nb,), in_specs=[blk(0), blk(nb), pl.BlockSpec((K, LANES), lambda j: (0, j)),
                                    pl.BlockSpec((1, LANES), lambda j: (0, j))],
        out_specs=blk(0), out_shape=jax.ShapeDtypeStruct((L, C), F32), name=name, compiler_params=_cp(1))(u, u, w, b)


def _conv_bwd(u, w, dh, name):
    L, C2 = u.shape
    C = C2 // 2
    K = w.shape[0]
    nb = C // LANES

    def body(val_ref, gate_ref, w_ref, dh_ref, dval_ref, dgate_ref, dw_ref, db_ref):
        row = lax.broadcasted_iota(jnp.int32, (L, LANES), 0)
        val = val_ref[...]
        sig = jax.nn.sigmoid(gate_ref[...])
        h = val * sig
        d = dh_ref[...]
        dh0 = jnp.zeros((L, LANES), F32)
        for k in range(K):
            s = K - 1 - k
            dh0 = dh0 + w_ref[k:k + 1, :] * _shift_up(d, s, row)
            dw_ref[k:k + 1, :] = jnp.sum(d * _shift_down(h, s, row), axis=0, keepdims=True)
        db_ref[...] = jnp.sum(d, axis=0, keepdims=True)
        dval_ref[...] = dh0 * sig
        dgate_ref[...] = dh0 * val * sig * (1.0 - sig)

    blk = lambda off: pl.BlockSpec((L, LANES), lambda j: (0, j + off))
    return pl.pallas_call(
        body, grid=(nb,), in_specs=[blk(0), blk(nb), pl.BlockSpec((K, LANES), lambda j: (0, j)), blk(0)],
        out_specs=[blk(0), blk(0), pl.BlockSpec((K, LANES), lambda j: (0, j)), pl.BlockSpec((1, LANES), lambda j: (0, j))],
        out_shape=[jax.ShapeDtypeStruct((L, C), F32), jax.ShapeDtypeStruct((L, C), F32),
                   jax.ShapeDtypeStruct((K, C), F32), jax.ShapeDtypeStruct((1, C), F32)],
        name=name, compiler_params=_cp(1))(u, u, w, dh)


def _glu_conv(name):
    @jax.custom_vjp
    def op(u, w, b):
        return _conv_fwd(u, w, b, name + "_fwd")

    def fwd(u, w, b):
        return _conv_fwd(u, w, b, name + "_fwd"), (u, w)

    def bwd(r, dh):
        u, w = r
        dval, dgate, dw, db = _conv_bwd(u, w, dh, name + "_bwd")
        return jnp.concatenate([dval, dgate], axis=-1), dw, db

    op.defvjp(fwd, bwd)
    return op


S5_BLOCK_CH = LANES
S5_BLOCK_ST = S5_BLOCK_CH // S5_CH_PER_GROUP * S5_STATE


def _s5_scan(br_ref, bi_ref, ar, ai, reverse):
    L, C = br_ref.shape
    T = SUBLANES
    row = lax.broadcasted_iota(jnp.int32, (T, C), 0)
    pw = [(ar, ai)]
    for _ in range(T - 1):
        pr, pi = pw[-1]
        pw.append((pr * ar - pi * ai, pr * ai + pi * ar))
    cr = jnp.zeros((T, C), F32)
    ci = jnp.zeros((T, C), F32)
    for r in range(T):
        e = (T - r) if reverse else (r + 1)
        cr = jnp.where(row == r, pw[e - 1][0], cr)
        ci = jnp.where(row == r, pw[e - 1][1], ci)
    steps = []
    s = 1
    while s < T:
        mask = (row < T - s) if reverse else (row >= s)
        steps.append((T - s if reverse else s, mask, pw[s - 1][0], pw[s - 1][1]))
        s *= 2
    nt = L // T
    last = 0 if reverse else T - 1

    def body(i, carry):
        kr, ki = carry
        t = (nt - 1 - i) if reverse else i
        off = pl.multiple_of(t * T, T)
        xr = br_ref[pl.ds(off, T), :]
        xi = bi_ref[pl.ds(off, T), :]
        for sh, mask, mr, mi in steps:
            sr = jnp.where(mask, pltpu.roll(xr, sh, 0), 0.0)
            si = jnp.where(mask, pltpu.roll(xi, sh, 0), 0.0)
            xr, xi = xr + mr * sr - mi * si, xi + mr * si + mi * sr
        xr, xi = xr + cr * kr - ci * ki, xi + cr * ki + ci * kr
        br_ref[pl.ds(off, T), :] = xr
        bi_ref[pl.ds(off, T), :] = xi
        return (jnp.broadcast_to(xr[last:last + 1, :], (T, C)), jnp.broadcast_to(xi[last:last + 1, :], (T, C)))

    z = jnp.zeros((T, C), F32)
    lax.fori_loop(0, nt, body, (z, z))


def _s5_specs(L):
    nb_axis = lambda j: (j, 0, 0)
    u = pl.BlockSpec((L, S5_BLOCK_CH), lambda j: (0, j))
    wb = pl.BlockSpec((None, S5_BLOCK_CH, S5_BLOCK_ST), nb_axis)
    a = pl.BlockSpec((1, S5_BLOCK_ST), lambda j: (0, j))
    wc = pl.BlockSpec((None, S5_BLOCK_ST, S5_BLOCK_CH), nb_axis)
    return u, wb, a, wc


def _s5_fwd(u, wbr, wbi, ar, ai, wcr, wci, name):
    L, C = u.shape
    nb = C // S5_BLOCK_CH
    us, wbs, as_, wcs = _s5_specs(L)

    def body(u_ref, wbr_ref, wbi_ref, ar_ref, ai_ref, wcr_ref, wci_ref, y_ref, xr, xi):
        ub = u_ref[...]
        xr[...] = _bdot(ub, wbr_ref[...])
        xi[...] = _bdot(ub, wbi_ref[...])
        _s5_scan(xr, xi, ar_ref[...], ai_ref[...], False)
        y_ref[...] = _bdot(xr[...], wcr_ref[...]) - _bdot(xi[...], wci_ref[...])

    return pl.pallas_call(
        body, grid=(nb,), in_specs=[us, wbs, wbs, as_, as_, wcs, wcs], out_specs=us,
        out_shape=jax.ShapeDtypeStruct((L, C), F32),
        scratch_shapes=[pltpu.VMEM((L, S5_BLOCK_ST), F32)] * 2, name=name, compiler_params=_cp(1))(
            u, wbr, wbi, ar, ai, wcr, wci)


def _dot_t(a, b):
    return lax.dot_general(a.astype(BF16), b.astype(BF16), (((0,), (0,)), ((), ())), preferred_element_type=F32)


def _dot_nt(a, b):
    return lax.dot_general(a.astype(BF16), b.astype(BF16), (((1,), (1,)), ((), ())), preferred_element_type=F32)


def _s5_bwd(u, wbr, wbi, ar, ai, wcr, wci, dy, name):
    L, C = u.shape
    nb = C // S5_BLOCK_CH
    us, wbs, as_, wcs = _s5_specs(L)
    T = SUBLANES

    def body(u_ref, wbr_ref, wbi_ref, ar_ref, ai_ref, wcr_ref, wci_ref, dy_ref,
             du_ref, dwbr_ref, dwbi_ref, dar_ref, dai_ref, dwcr_ref, dwci_ref, xr, xi, gr, gi):
        ub = u_ref[...]
        a_r, a_i = ar_ref[...], ai_ref[...]
        xr[...] = _bdot(ub, wbr_ref[...])
        xi[...] = _bdot(ub, wbi_ref[...])
        _s5_scan(xr, xi, a_r, a_i, False)
        d = dy_ref[...]
        dwcr_ref[...] = _dot_t(xr[...], d)
        dwci_ref[...] = -_dot_t(xi[...], d)
        gr[...] = _dot_nt(d, wcr_ref[...])
        gi[...] = -_dot_nt(d, wci_ref[...])
        _s5_scan(gr, gi, a_r, -a_i, True)

        row = lax.broadcasted_iota(jnp.int32, (T, S5_BLOCK_ST), 0)

        def da_body(i, carry):
            pr, pi, sr, si = carry
            off = pl.multiple_of(i * T, T)
            xr_t, xi_t = xr[pl.ds(off, T), :], xi[pl.ds(off, T), :]
            lr_t, li_t = gr[pl.ds(off, T), :], gi[pl.ds(off, T), :]
            qr = jnp.where(row == 0, pr, pltpu.roll(xr_t, 1, 0))
            qi = jnp.where(row == 0, pi, pltpu.roll(xi_t, 1, 0))
            sr = sr + qr * lr_t + qi * li_t
            si = si + qr * li_t - qi * lr_t
            return (jnp.broadcast_to(xr_t[T - 1:T, :], (T, S5_BLOCK_ST)),
                    jnp.broadcast_to(xi_t[T - 1:T, :], (T, S5_BLOCK_ST)), sr, si)

        z = jnp.zeros((T, S5_BLOCK_ST), F32)
        _, _, sr, si = lax.fori_loop(0, L // T, da_body, (z, z, z, z))
        dar_ref[...] = jnp.sum(sr, axis=0, keepdims=True)
        dai_ref[...] = jnp.sum(si, axis=0, keepdims=True)
        lr, li = gr[...], gi[...]
        dwbr_ref[...] = _dot_t(ub, lr)
        dwbi_ref[...] = _dot_t(ub, li)
        du_ref[...] = _dot_nt(lr, wbr_ref[...]) + _dot_nt(li, wbi_ref[...])

    sds = jax.ShapeDtypeStruct
    return pl.pallas_call(
        body, grid=(nb,), in_specs=[us, wbs, wbs, as_, as_, wcs, wcs, us],
        out_specs=[us, wbs, wbs, as_, as_, wcs, wcs],
        out_shape=[sds(u.shape, F32), sds(wbr.shape, F32), sds(wbi.shape, F32), sds(ar.shape, F32),
                   sds(ai.shape, F32), sds(wcr.shape, F32), sds(wci.shape, F32)],
        scratch_shapes=[pltpu.VMEM((L, S5_BLOCK_ST), F32)] * 4, name=name,
        compiler_params=_cp(1, VMEM_LIMIT_BIG))(u, wbr, wbi, ar, ai, wcr, wci, dy)


def _s5_core(name):
    @jax.custom_vjp
    def op(u, wbr, wbi, ar, ai, wcr, wci):
        return _s5_fwd(u, wbr, wbi, ar, ai, wcr, wci, name + "_fwd")

    def fwd(*a):
        return _s5_fwd(*a, name + "_fwd"), a

    def bwd(a, dy):
        return tuple(_s5_bwd(*a, dy, name + "_bwd"))

    op.defvjp(fwd, bwd)
    return op


def _att_tile_f(first, q, kp, kc, vp, vc, bias):
    nq = q.shape[0]
    hb = bias.shape[0]
    E = q.shape[1] // hb
    r = lax.broadcasted_iota(jnp.int32, (nq, 2 * nq), 0)
    c = lax.broadcasted_iota(jnp.int32, (nq, 2 * nq), 1)
    prev_ok = jnp.logical_and(jnp.logical_and(c < nq, c >= r), jnp.logical_not(first))
    valid = jnp.logical_or(prev_ok, jnp.logical_and(c >= nq, c - nq <= r))
    outs, lses = [], []
    for h in range(hb):
        sl = slice(h * E, (h + 1) * E)
        k = jnp.concatenate([kp[:, sl], kc[:, sl]], axis=0)
        v = jnp.concatenate([vp[:, sl], vc[:, sl]], axis=0)
        s = jnp.where(valid, _dot_nt(q[:, sl], k) * (E ** -0.5) + bias[h], NEG_INF)
        m = jnp.max(s, axis=-1, keepdims=True)
        p = jnp.exp(s - m)
        den = jnp.sum(p, axis=-1, keepdims=True)
        outs.append(_bdot(p, v) / den)
        lses.append(jnp.broadcast_to(m + jnp.log(den), (nq, E)))
    return jnp.concatenate(outs, axis=-1), jnp.concatenate(lses, axis=-1)


def _att_tile_grad(first, q, kp, kc, vp, vc, bias, o, lse, do, dlse):
    nq = q.shape[0]
    hb = bias.shape[0]
    E = q.shape[1] // hb
    scale = E ** -0.5
    r = lax.broadcasted_iota(jnp.int32, (nq, 2 * nq), 0)
    c = lax.broadcasted_iota(jnp.int32, (nq, 2 * nq), 1)
    prev_ok = jnp.logical_and(jnp.logical_and(c < nq, c >= r), jnp.logical_not(first))
    valid = jnp.logical_or(prev_ok, jnp.logical_and(c >= nq, c - nq <= r))
    dq, dk, dv, db = [], [], [], []
    for h in range(hb):
        sl = slice(h * E, (h + 1) * E)
        qh = q[:, sl]
        k = jnp.concatenate([kp[:, sl], kc[:, sl]], axis=0)
        v = jnp.concatenate([vp[:, sl], vc[:, sl]], axis=0)
        s = jnp.where(valid, _dot_nt(qh, k) * scale + bias[h], NEG_INF)
        p = jnp.exp(s - lse[:, h * E:h * E + 1])
        doh = do[:, sl]
        row = jnp.sum(dlse[:, sl], axis=-1, keepdims=True) - jnp.sum(doh * o[:, sl], axis=-1, keepdims=True)
        ds = p * (_dot_nt(doh, v) + row)
        db.append(ds)
        dv.append(_dot_t(p, doh))
        dq.append(_bdot(ds, k) * scale)
        dk.append(_dot_t(ds, qh) * scale)
    cat = lambda parts, rows: jnp.concatenate([x[rows] for x in parts], axis=-1)
    lo, hi = slice(0, nq), slice(nq, 2 * nq)
    return jnp.concatenate(dq, axis=-1), cat(dk, lo), cat(dk, hi), cat(dv, lo), cat(dv, hi), db


def _att_mix_f(*a):
    n = len(a) // 2
    o, l = a[:n], a[n:]
    m = functools.reduce(jnp.maximum, l)
    e = [jnp.exp(li - m) for li in l]
    return sum(ei * oi for ei, oi in zip(e, o)) / sum(e)


def _att_rows(start, dil):
    if dil == 1:
        return pl.ds(pl.multiple_of(start, ATT_BLOCK), ATT_BLOCK)
    return pl.ds(start, ATT_BLOCK, stride=dil)


def _att_blocks(L, dil):
    nb = L // dil // ATT_BLOCK
    return dil * nb, nb


def _att_specs(L, W):
    nblk = W // LANES
    col = lambda off: pl.BlockSpec((L, LANES), lambda j: (0, j + off))
    per_pattern = pl.BlockSpec((len(DILATED_PATTERNS), L, LANES), lambda j: (0, 0, j))
    hb = ATT_HEADS // nblk
    bias = pl.BlockSpec((len(DILATED_PATTERNS), hb, ATT_BLOCK, 2 * ATT_BLOCK), lambda j: (0, j, 0, 0))
    return nblk, col, per_pattern, bias


def _att_fwd(qkv, bias, name):
    L, W3 = qkv.shape
    W = W3 // 3
    nblk, col, per_pattern, bias_spec = _att_specs(L, W)
    P = len(DILATED_PATTERNS)

    def body(q_ref, k_ref, v_ref, b_ref, y_ref, o_ref, l_ref):
        for p, (_, dil) in enumerate(DILATED_PATTERNS):
            n_it, nb = _att_blocks(L, dil)

            def step(i, carry, p=p, dil=dil, nb=nb):
                n = i % nb
                cur = i // nb + n * (ATT_BLOCK * dil)
                prev = i // nb + jnp.maximum(n - 1, 0) * (ATT_BLOCK * dil)
                rc, rp = _att_rows(cur, dil), _att_rows(prev, dil)
                o, l = _att_tile_f(n == 0, q_ref[rc, :], k_ref[rp, :], k_ref[rc, :], v_ref[rp, :], v_ref[rc, :],
                                   b_ref[p])
                o_ref[p, rc, :] = o
                l_ref[p, rc, :] = l
                return carry

            lax.fori_loop(0, n_it, step, 0, unroll=ATT_UNROLL)

        def mix(i, carry):
            rows = pl.ds(pl.multiple_of(i * ATT_MIX_ROWS, ATT_MIX_ROWS), ATT_MIX_ROWS)
            y_ref[rows, :] = _att_mix_f(*[o_ref[p, rows, :] for p in range(P)], *[l_ref[p, rows, :] for p in range(P)])
            return carry

        lax.fori_loop(0, L // ATT_MIX_ROWS, mix, 0)

    sds = jax.ShapeDtypeStruct
    return pl.pallas_call(
        body, grid=(nblk,), in_specs=[col(0), col(nblk), col(2 * nblk), bias_spec],
        out_specs=[col(0), per_pattern, per_pattern],
        out_shape=[sds((L, W), F32), sds((P, L, W), F32), sds((P, L, W), F32)], name=name,
        compiler_params=_cp(1))(qkv, qkv, qkv, bias)


def _att_bwd(qkv, bias, o_all, l_all, dy, name):
    L, W3 = qkv.shape
    W = W3 // 3
    nblk, col, per_pattern, bias_spec = _att_specs(L, W)
    P = len(DILATED_PATTERNS)

    def body(q_ref, k_ref, v_ref, b_ref, o_ref, l_ref, dy_ref, dq_ref, dk_ref, dv_ref, db_ref, do_s, dl_s):
        def mix(i, carry):
            rows = pl.ds(pl.multiple_of(i * ATT_MIX_ROWS, ATT_MIX_ROWS), ATT_MIX_ROWS)
            _, mix_vjp = jax.vjp(_att_mix_f, *[o_ref[p, rows, :] for p in range(P)],
                                 *[l_ref[p, rows, :] for p in range(P)])
            g = mix_vjp(dy_ref[rows, :])
            for p in range(P):
                do_s[p, rows, :] = g[p]
                dl_s[p, rows, :] = g[P + p]
            return carry

        lax.fori_loop(0, L // ATT_MIX_ROWS, mix, 0)
        for ref in (dq_ref, dk_ref, dv_ref, db_ref):
            ref[...] = jnp.zeros_like(ref)

        def add(ref, rows, val):
            ref[rows, :] = ref[rows, :] + val

        for p, (_, dil) in enumerate(DILATED_PATTERNS):
            n_it, nb = _att_blocks(L, dil)

            def step(i, carry, p=p, dil=dil, nb=nb):
                n = i % nb
                first = n == 0
                cur = i // nb + n * (ATT_BLOCK * dil)
                prev = i // nb + jnp.maximum(n - 1, 0) * (ATT_BLOCK * dil)
                rc, rp = _att_rows(cur, dil), _att_rows(prev, dil)
                dq, dkp, dkc, dvp, dvc, db = _att_tile_grad(
                    first, q_ref[rc, :], k_ref[rp, :], k_ref[rc, :], v_ref[rp, :], v_ref[rc, :], b_ref[p],
                    o_ref[p, rc, :], l_ref[p, rc, :], do_s[p, rc, :], dl_s[p, rc, :])
                add(dq_ref, rc, dq)
                add(dk_ref, rc, dkc)
                add(dv_ref, rc, dvc)
                for h, dbh in enumerate(db):
                    db_ref[p, h] = db_ref[p, h] + dbh

                @pl.when(jnp.logical_not(first))
                def _():
                    add(dk_ref, rp, dkp)
                    add(dv_ref, rp, dvp)

                return carry

            lax.fori_loop(0, n_it, step, 0, unroll=ATT_UNROLL)

    sds = jax.ShapeDtypeStruct((L, W), F32)
    return pl.pallas_call(
        body, grid=(nblk,),
        in_specs=[col(0), col(nblk), col(2 * nblk), bias_spec, per_pattern, per_pattern, col(0)],
        out_specs=[col(0), col(0), col(0), bias_spec],
        out_shape=[sds, sds, sds, jax.ShapeDtypeStruct(bias.shape, F32)],
        scratch_shapes=[pltpu.VMEM((P, L, LANES), F32)] * 2, name=name, compiler_params=_cp(1, VMEM_LIMIT_BIG))(
            qkv, qkv, qkv, bias, o_all, l_all, dy)


def _dilated_attention(name):
    @jax.custom_vjp
    def op(qkv, bias):
        return _att_fwd(qkv, bias, name + "_fwd")[0]

    def fwd(qkv, bias):
        y, o_all, l_all = _att_fwd(qkv, bias, name + "_fwd")
        return y, (qkv, bias, o_all, l_all)

    def bwd(r, dy):
        dq, dk, dv, db = _att_bwd(*r, dy, name + "_bwd")
        return jnp.concatenate([dq, dk, dv], axis=-1), db

    op.defvjp(fwd, bwd)
    return op


def _t5_bucket(dist):
    n = np.maximum(dist, 0)
    max_exact = REL_BUCKETS // 2
    large = max_exact + (np.log(np.maximum(n, 1) / max_exact) / np.log(REL_MAX_DIST / max_exact)
                         * (REL_BUCKETS - max_exact)).astype(np.int64)
    large = np.minimum(large, REL_BUCKETS - 1)
    return np.where(n < max_exact, n, large).astype(np.int32)


def _bucket_onehot():
    a = np.arange(ATT_BLOCK)[:, None]
    b = np.arange(2 * ATT_BLOCK)[None, :]
    sub = a + ATT_BLOCK - b
    bucket = jnp.asarray(np.stack([_t5_bucket(sub * dil).reshape(-1) for _, dil in DILATED_PATTERNS]))
    ids = jnp.arange(REL_BUCKETS, dtype=jnp.int32)
    return (bucket[:, None, :] == ids[None, :, None]).astype(F32)


def loss_head(h, target, g, name):
    R, D = h.shape
    tr = _tile(R, (256,))

    def body(h_ref, t_ref, g_ref, l_ref, dh_ref, dg_ref):
        def lf(hv, gv):
            y = _rms_f(hv, gv)[0]
            return 0.5 * jnp.sum(jnp.mean(jnp.square(y - t_ref[...]), axis=-1))

        l, (dh, dg) = jax.value_and_grad(lf, argnums=(0, 1))(h_ref[...], g_ref[...])

        @pl.when(pl.program_id(0) == 0)
        def _():
            l_ref[...] = jnp.zeros_like(l_ref)
            dg_ref[...] = jnp.zeros_like(dg_ref)

        dh_ref[...] = dh
        dg_ref[...] += dg
        l_ref[...] += l

    rows = pl.BlockSpec((tr, D), lambda i: (i, 0))
    vec = pl.BlockSpec((1, D), lambda i: (0, 0))
    l, dh, dg = pl.pallas_call(
        body, grid=(R // tr,), in_specs=[rows, rows, vec],
        out_specs=[pl.BlockSpec((SUBLANES, LANES), lambda i: (0, 0)), rows, vec],
        out_shape=[jax.ShapeDtypeStruct((SUBLANES, LANES), F32), jax.ShapeDtypeStruct((R, D), F32),
                   jax.ShapeDtypeStruct((1, D), F32)], name=name, compiler_params=_cp(1))(h, target, g.reshape(1, D))
    return l[0, 0], dh, dg.reshape(D)


def _adamw_update(w, g, m, v):
    c1 = 1.0 - ADAM_B1 ** ADAM_STEP
    c2 = 1.0 - ADAM_B2 ** ADAM_STEP
    nm = ADAM_B1 * m + (1.0 - ADAM_B1) * g
    nv = ADAM_B2 * v + (1.0 - ADAM_B2) * jnp.square(g)
    return -ADAM_LR * ((nm / c1) / (jnp.sqrt(nv / c2) + ADAM_EPS) + ADAM_WD * w), nm, nv


def adamw_layers(w, parts, m, v, name):
    nl, a, b = w.shape
    n_parts = parts[0].shape[0]
    tr = _row_tile(a, b)

    def body(*refs):
        w_ref, p_refs, (m_ref, v_ref, g_ref, d_ref, nm_ref, nv_ref) = refs[0], refs[1:1 + nl], refs[1 + nl:]
        for l in range(nl):
            @pl.when(pl.program_id(0) == l)
            def _(p_ref=p_refs[l]):
                g = p_ref[0].astype(F32)
                for i in range(1, n_parts):
                    g = g + p_ref[i].astype(F32)
                d_ref[...], nm_ref[...], nv_ref[...] = _adamw_update(w_ref[...], g, m_ref[...], v_ref[...])
                g_ref[...] = g

    rows = pl.BlockSpec((None, tr, b), lambda l, i: (l, i, 0))
    part = lambda k: pl.BlockSpec((n_parts, tr, b), lambda l, i: (0, jnp.where(l == k, i, 0), 0))
    sds = jax.ShapeDtypeStruct((nl, a, b), F32)
    return pl.pallas_call(body, grid=(nl, a // tr), in_specs=[rows] + [part(k) for k in range(nl)] + [rows, rows],
                          out_specs=[rows] * 4, out_shape=[sds] * 4, name=name, compiler_params=_cp(2))(
                              w, *parts, m, v)


def adamw(w, parts, m, v, name):
    R, C = w.shape
    n_parts = parts.shape[0]
    tr = _row_tile(R, C)

    def body(w_ref, p_ref, m_ref, v_ref, g_ref, d_ref, nm_ref, nv_ref):
        g = p_ref[0].astype(F32)
        for i in range(1, n_parts):
            g = g + p_ref[i].astype(F32)
        d_ref[...], nm_ref[...], nv_ref[...] = _adamw_update(w_ref[...], g, m_ref[...], v_ref[...])
        g_ref[...] = g

    rows = pl.BlockSpec((tr, C), lambda i: (i, 0))
    sds = jax.ShapeDtypeStruct((R, C), F32)
    return pl.pallas_call(body, grid=(R // tr,),
                          in_specs=[rows, pl.BlockSpec((n_parts, tr, C), lambda i: (0, i, 0)), rows, rows],
                          out_specs=[rows] * 4, out_shape=[sds] * 4, name=name, compiler_params=_cp(1))(w, parts, m, v)


HBM_SPEC = pl.BlockSpec(memory_space=pltpu.HBM)
MESH_ID = pl.DeviceIdType.MESH


def _place():
    return lax.axis_index("x"), lax.axis_index("y"), lax.axis_index("c")


def _index(x, y, c):
    return 4 * x + 2 * y + c


AG_COPIES = 9
AG_ROW_UNIT = 32


def all_gather(x, name):
    R, C = x.shape
    assert R % AG_ROW_UNIT == 0, x.shape
    half = R // 2

    def body(x_ref, out_ref, send_sems, recv_sems, local_sem):
        x_, y_, c_ = _place()
        me, sib = (x_, y_, c_), (x_, y_, 1 - c_)
        nx, ny, nd = (1 - x_, y_, c_), (x_, 1 - y_, c_), (1 - x_, 1 - y_, c_)
        upper, lower = pl.ds(0, half), pl.ds(half, half)

        def slot(dev, rows=None):
            ref = out_ref.at[_index(*dev)]
            return ref if rows is None else ref.at[rows]

        def copy(k, block, to, rows=None, src=None):
            return pltpu.make_async_remote_copy(
                src_ref=slot(block, rows) if src is None else src, dst_ref=slot(block, rows),
                send_sem=send_sems.at[k], recv_sem=recv_sems.at[k], device_id=to, device_id_type=MESH_ID)

        def other(dev):
            return (dev[0], dev[1], 1 - c_)

        mine = pltpu.make_async_copy(x_ref, slot(me), local_sem)
        mine.start()
        sent = [copy(0, me, sib, src=x_ref), copy(1, me, nx, src=x_ref), copy(2, me, ny, src=x_ref)]
        for cp in sent:
            cp.start()

        def then(arrival, *forwards):
            arrival.wait_recv()
            for cp in forwards:
                cp.start()
            sent.extend(forwards)

        then(copy(1, nx, me), copy(4, nx, ny, upper), copy(5, nx, sib))
        then(copy(2, ny, me), copy(3, ny, nx, lower), copy(6, ny, sib))
        then(copy(3, nd, me, lower), copy(8, nd, sib, lower))
        then(copy(4, nd, me, upper), copy(7, nd, sib, upper))
        copy(0, sib, me).wait_recv()
        copy(5, other(nx), me).wait_recv()
        copy(6, other(ny), me).wait_recv()
        copy(7, other(nd), me, upper).wait_recv()
        copy(8, other(nd), me, lower).wait_recv()
        for cp in sent:
            cp.wait_send()
        mine.wait()

    return pl.pallas_call(
        body, out_shape=jax.ShapeDtypeStruct((N_DEV,) + x.shape, x.dtype), in_specs=[HBM_SPEC], out_specs=HBM_SPEC,
        scratch_shapes=[pltpu.SemaphoreType.DMA((AG_COPIES,)), pltpu.SemaphoreType.DMA((AG_COPIES,)),
                        pltpu.SemaphoreType.DMA], name=name)(x)


def _chip(x, y):
    return 2 * x + y


def sibling_exchange(xs, name):
    n = len(xs)

    def body(*refs):
        x_refs, out_refs, (send_sems, recv_sems) = refs[:n], refs[n:2 * n], refs[2 * n:]
        x_, y_, c_ = _place()
        copies = [pltpu.make_async_remote_copy(src_ref=x_ref.at[:, 1 - c_], dst_ref=out_ref, send_sem=send_sems.at[p],
                                               recv_sem=recv_sems.at[p], device_id=(x_, y_, 1 - c_),
                                               device_id_type=MESH_ID)
                  for p, (x_ref, out_ref) in enumerate(zip(x_refs, out_refs))]
        for cp in copies:
            cp.start()
        for cp in copies:
            cp.wait()

    return pl.pallas_call(
        body, out_shape=[jax.ShapeDtypeStruct((x.shape[0],) + x.shape[2:], x.dtype) for x in xs],
        in_specs=[HBM_SPEC] * n, out_specs=[HBM_SPEC] * n,
        scratch_shapes=[pltpu.SemaphoreType.DMA((n,)), pltpu.SemaphoreType.DMA((n,))], name=name)(*xs)


def _row_tile(R, C):
    cap = max(SUBLANES, STREAM_BLOCK_BYTES // (4 * C))
    return _tile(R, [t for t in (512, 256, 128, 64, 32, 16, 8) if t <= cap])


def pair_sum(x, recv, name):
    nc, _, R, C = x.shape
    tr = _row_tile(R, C)
    core = lax.axis_index("c").astype(jnp.int32).reshape(1)

    def body(c_ref, a_ref, b_ref, o_ref):
        o_ref[...] = (a_ref[...].astype(F32) + b_ref[...].astype(F32)).astype(o_ref.dtype)

    blk = pl.BlockSpec((None, tr, C), lambda k, i, c_ref: (k, i, 0))
    grid_spec = pltpu.PrefetchScalarGridSpec(
        num_scalar_prefetch=1, grid=(nc, R // tr),
        in_specs=[pl.BlockSpec((None, None, tr, C), lambda k, i, c_ref: (k, c_ref[0], i, 0)), blk], out_specs=blk)
    return pl.pallas_call(body, grid_spec=grid_spec, out_shape=jax.ShapeDtypeStruct((nc, R, C), x.dtype), name=name,
                          compiler_params=_cp(2))(core, x, recv)


def chip_exchange(ss, name):
    n = len(ss)

    def body(*refs):
        copies = _chip_exchange_copies(refs[:n], refs[n:2 * n], 0, 1, *refs[2 * n:])
        for cp in copies:
            cp.start()
        for cp in copies:
            cp.wait()

    n_sem = n * (N_CHIP - 1)
    return pl.pallas_call(
        body, out_shape=[jax.ShapeDtypeStruct(s.shape, s.dtype) for s in ss], in_specs=[HBM_SPEC] * n,
        out_specs=[HBM_SPEC] * n,
        scratch_shapes=[pltpu.SemaphoreType.DMA((n_sem,)), pltpu.SemaphoreType.DMA((n_sem,)),
                        pltpu.SemaphoreType.DMA((n,))], name=name)(*ss)


def _block_diag(w, nb):
    G, a, b = w.shape
    gp = G // nb
    eye = jnp.eye(gp, dtype=w.dtype)
    return jnp.einsum('jgab,gh->jgahb', w.reshape(nb, gp, a, b), eye).reshape(nb, gp * a, gp * b)


def _split_columns(x, cuts):
    edges = (0,) + tuple(cuts) + (x.shape[1],)

    def split(x):
        return tuple(x[:, a:b] for a, b in zip(edges[:-1], edges[1:]))

    op = jax.custom_vjp(split)
    op.defvjp(lambda x: (split(x), None), lambda _, cts: (jnp.concatenate(cts, axis=-1),))
    return op(x)


def _mixers_and_memory(l, h, memn, bias_tabs, P):
    nm = lambda s: f"l{l}_{s}"
    L, D = h.shape
    GW = D // N_MIXERS
    G = GW // S5_CH_PER_GROUP

    row = lambda g: g.reshape(1, D)

    proj = _act_linear(nm("w_in"), _rms_op(nm("norm_mix"), L, D, BF16), 2, False)(h, row(P['norm_mix_g']), P['w_in'])
    u_a, u_b, u_c, qkv = _split_columns(proj, (GW, 2 * GW, 4 * GW))

    v3 = lambda a: a.reshape(G, 1, S5_STATE)
    log_dt = jnp.broadcast_to(P['s5_log_dt'][:, None, None], (G, 1, S5_STATE))
    a_r, a_i, bb_r, bb_i = s5_discretise(v3(P['s5_lam_re']), v3(P['s5_lam_im']), log_dt,
                                         P['s5_b_re'].transpose(0, 2, 1), P['s5_b_im'].transpose(0, 2, 1), nm("s5_disc"))
    nblk = GW // S5_BLOCK_CH
    y_s5 = _s5_core(nm("s5_core"))(
        u_a, _block_diag(bb_r, nblk), _block_diag(bb_i, nblk), a_r.reshape(1, G * S5_STATE), a_i.reshape(1, G * S5_STATE),
        _block_diag(P['s5_c_re'].transpose(0, 2, 1), nblk), _block_diag(P['s5_c_im'].transpose(0, 2, 1), nblk))
    y_a = s5_epilogue(y_s5, u_a, P['s5_d'], P['s5_w_glu'], nm("s5_glu"))

    y_b = pool_proj(_pool_mix(nm("pool_mix"))(u_b), P['pool_w'], P['pool_scale'], nm("pool_proj"))

    hc = _glu_conv(nm("conv_dw"))(u_c, P['conv_w_dw'], P['conv_b_dw'].reshape(1, GW))
    y_c = conv_post(hc, P['conv_ln_g'], P['conv_ln_b'], P['conv_w_pw'], nm("conv_post"))

    y_d = _dilated_attention(nm("att"))(qkv, bias_tabs)

    grp = _group_norm_op(nm("grp_norm"), L, GW, N_MIXERS, BF16)
    h = _act_linear(nm("w_out"), grp, N_MIXERS + 1, True)(y_a, y_b, y_c, y_d, row(P['grp_norm_g']), P['w_out'], h)

    xq = _act_linear(nm("w_xq"), _rms_op(nm("norm_x"), L, D, BF16), 2, False)(h, row(P['norm_x_g']), P['w_xq'])
    xk = _linear(nm("w_xk"))(memn, P['w_xk'])
    xv = _linear(nm("w_xv"))(memn, P['w_xv'])
    xat = _cross_attention_op(nm("xattn"), L, xq.shape[1], memn.shape[0], BF16)
    return _act_linear(nm("w_xo"), xat, 3, True)(xq, xk, xv, P['w_xo'], h)


def _bias_tables(rel_bias):
    tabs = rel_bias_tables(rel_bias, _bucket_onehot(), "rel_bias")
    return tabs.reshape(len(DILATED_PATTERNS), ATT_HEADS, ATT_BLOCK, 2 * ATT_BLOCK)


def _gather_weight(name, w):
    ax = SHARDED[name]
    dt = BF16 if name in GATHER_BF16 else F32
    nl, a, b = w.shape
    rows = nl * a
    flat = jnp.pad(w.astype(dt).reshape(rows, b), ((0, (-rows) % AG_ROW_UNIT), (0, 0)))
    g = all_gather(flat, "ag_" + name)[:, :rows].reshape(N_DEV, nl, a, b)
    if ax == 1:
        return g.transpose(1, 0, 2, 3).reshape(nl, N_DEV * a, b)
    return g.transpose(1, 2, 0, 3).reshape(nl, a, N_DEV * b)


def _scatter_grad(name, g):
    ax = SHARDED[name]
    nl = g.shape[0]
    if ax == 1:
        a, b = g.shape[1] // N_DEV, g.shape[2]
        s = g.reshape(nl, N_DEV, a, b).transpose(1, 0, 2, 3)
    else:
        a, b = g.shape[1], g.shape[2] // N_DEV
        s = g.reshape(nl, a, N_DEV, b).transpose(2, 0, 1, 3)
    s = s.reshape(N_CHIP, N_DEV // N_CHIP, nl * a, b)
    pair = pair_sum(s, sibling_exchange([s], "d2d_" + name)[0], "pairsum_" + name)
    return chip_exchange([pair], "ici_" + name)[0]


def _by_destination(name, g):
    if g.ndim == 2 and SHARDED[name] == 1:
        g = g.reshape(N_DEV, g.shape[0] // N_DEV, g.shape[1])
    elif g.ndim == 2:
        g = g.reshape(g.shape[0], N_DEV, g.shape[1] // N_DEV).transpose(1, 0, 2)
    return g.reshape(N_CHIP, N_DEV // N_CHIP, *g.shape[1:])


def _flatten_small(d):
    flat = jnp.concatenate([d[n].reshape(-1).astype(F32) for n in SMALL])
    pad = (-flat.shape[0]) % (LANES * SMALL_ROW_TILE)
    return jnp.pad(flat, (0, pad)).reshape(-1, LANES)


def _split_small(flat, like):
    flat = flat.reshape(-1)
    out, off = {}, 0
    for n in SMALL:
        sz = math.prod(like[n].shape)
        out[n] = flat[off:off + sz].reshape(like[n].shape)
        off += sz
    return out


def kernel(x, mem, rel_bias, mem_norm_g, norm_mix_g, w_in, s5_lam_re, s5_lam_im, s5_log_dt, s5_b_re, s5_b_im, s5_c_re, s5_c_im, s5_d, s5_w_glu, pool_w, pool_scale, conv_w_dw, conv_b_dw, conv_ln_g, conv_ln_b, conv_w_pw, grp_norm_g, w_out, norm_x_g, w_xq, w_xk, w_xv, w_xo, norm_mlp_g, w_up, w_down, norm_final_g, loss_target, m_rel_bias, m_mem_norm_g, m_norm_mix_g, m_w_in, m_s5_lam_re, m_s5_lam_im, m_s5_log_dt, m_s5_b_re, m_s5_b_im, m_s5_c_re, m_s5_c_im, m_s5_d, m_s5_w_glu, m_pool_w, m_pool_scale, m_conv_w_dw, m_conv_b_dw, m_conv_ln_g, m_conv_ln_b, m_conv_w_pw, m_grp_norm_g, m_w_out, m_norm_x_g, m_w_xq, m_w_xk, m_w_xv, m_w_xo, m_norm_mlp_g, m_w_up, m_w_down, m_norm_final_g, v_rel_bias, v_mem_norm_g, v_norm_mix_g, v_w_in, v_s5_lam_re, v_s5_lam_im, v_s5_log_dt, v_s5_b_re, v_s5_b_im, v_s5_c_re, v_s5_c_im, v_s5_d, v_s5_w_glu, v_pool_w, v_pool_scale, v_conv_w_dw, v_conv_b_dw, v_conv_ln_g, v_conv_ln_b, v_conv_w_pw, v_grp_norm_g, v_w_out, v_norm_x_g, v_w_xq, v_w_xk, v_w_xv, v_w_xo, v_norm_mlp_g, v_w_up, v_w_down, v_norm_final_g):
    w = dict(zip(WEIGHTS, (rel_bias, mem_norm_g, norm_mix_g, w_in, s5_lam_re, s5_lam_im, s5_log_dt, s5_b_re, s5_b_im, s5_c_re, s5_c_im, s5_d, s5_w_glu, pool_w, pool_scale, conv_w_dw, conv_b_dw, conv_ln_g, conv_ln_b, conv_w_pw, grp_norm_g, w_out, norm_x_g, w_xq, w_xk, w_xv, w_xo, norm_mlp_g, w_up, w_down, norm_final_g)))
    m = dict(zip(WEIGHTS, (m_rel_bias, m_mem_norm_g, m_norm_mix_g, m_w_in, m_s5_lam_re, m_s5_lam_im, m_s5_log_dt, m_s5_b_re, m_s5_b_im, m_s5_c_re, m_s5_c_im, m_s5_d, m_s5_w_glu, m_pool_w, m_pool_scale, m_conv_w_dw, m_conv_b_dw, m_conv_ln_g, m_conv_ln_b, m_conv_w_pw, m_grp_norm_g, m_w_out, m_norm_x_g, m_w_xq, m_w_xk, m_w_xv, m_w_xo, m_norm_mlp_g, m_w_up, m_w_down, m_norm_final_g)))
    v = dict(zip(WEIGHTS, (v_rel_bias, v_mem_norm_g, v_norm_mix_g, v_w_in, v_s5_lam_re, v_s5_lam_im, v_s5_log_dt, v_s5_b_re, v_s5_b_im, v_s5_c_re, v_s5_c_im, v_s5_d, v_s5_w_glu, v_pool_w, v_pool_scale, v_conv_w_dw, v_conv_b_dw, v_conv_ln_g, v_conv_ln_b, v_conv_w_pw, v_grp_norm_g, v_w_out, v_norm_x_g, v_w_xq, v_w_xk, v_w_xv, v_w_xo, v_norm_mlp_g, v_w_up, v_w_down, v_norm_final_g)))

    full = {n: (_gather_weight(n, w[n]) if n in SHARDED else w[n]) for n in WEIGHTS if n != 'norm_final_g'}
    L, D = x.shape[1:]

    memn, mem_vjp = jax.vjp(lambda a, g: rmsnorm(a, g, "mem_norm"), mem[0], w['mem_norm_g'])
    tabs, tabs_vjp = jax.vjp(_bias_tables, w['rel_bias'])
    h = x[0]
    front_vjps, mlps = [], []
    for l in range(DEPTH):
        h, fv = jax.vjp(functools.partial(_mixers_and_memory, l), h, memn, tabs, {n: full[n][l] for n in FRONT_WEIGHTS})
        norm = _rms_op(f"l{l}_norm_mlp", L, D, BF16)
        h, saved = _mlp_fwd(f"l{l}_mlp", norm, h, full['norm_mlp_g'][l].reshape(1, D), full['w_up'][l], full['w_down'][l])
        front_vjps.append(fv)
        mlps.append((norm, saved))
    loss_local, dh, d_final_g = loss_head(h, loss_target[0], w['norm_final_g'], "loss_head")
    loss = lax.psum(loss_local, MESH_AXES)

    layer_grads, arrived = [None] * DEPTH, [None] * DEPTH
    pending, dmemn, dtabs = None, 0.0, 0.0
    for l in reversed(range(DEPTH)):
        norm, saved = mlps[l]
        dh, dg_mlp, dw_up, dw_down, lands = _mlp_bwd(f"l{l}_mlp", norm, saved, dh, pending)
        if pending is not None:
            arrived[l + 1] = lands
        dh, dmemn_l, dtabs_l, dP = front_vjps[l](dh)
        dmemn, dtabs = dmemn + dmemn_l, dtabs + dtabs_l
        layer_grads[l] = dict(dP, norm_mlp_g=dg_mlp.reshape(D), w_up=dw_up, w_down=dw_down)
        by_dest = [_by_destination(n, layer_grads[l][n]) for n in GATHER_BF16]
        theirs = sibling_exchange(by_dest, f"d2d_l{l}")
        pending = [pair_sum(s, t, f"pairsum_l{l}_{n}") for n, s, t in zip(GATHER_BF16, by_dest, theirs)]
    arrived[0] = chip_exchange(pending, "ici_l0")
    dx = dh
    dfull = {n: jnp.concatenate([layer_grads[l][n][None] for l in range(DEPTH)])
             for n in LAYER_WEIGHTS if n not in GATHER_BF16}
    dfull['mem_norm_g'] = mem_vjp(dmemn)[1]
    dfull['rel_bias'] = tabs_vjp(dtabs)[0]
    dfull['norm_final_g'] = d_final_g

    grads, deltas, new_m, new_v = {}, {}, {}, {}
    for k, n in enumerate(GATHER_BF16):
        res = adamw_layers(w[n], [arrived[l][k] for l in range(DEPTH)], m[n], v[n], "adamw_" + n)
        grads[n], deltas[n], new_m[n], new_v[n] = res
    for n in SHARDED:
        if n in GATHER_BF16:
            continue
        parts = _scatter_grad(n, dfull[n])
        shp = w[n].shape
        two_d = lambda a: a.reshape(shp[0] * shp[1], shp[2])
        res = adamw(two_d(w[n]), parts, two_d(m[n]), two_d(v[n]), "adamw_" + n)
        grads[n], deltas[n], new_m[n], new_v[n] = (r.reshape(shp) for r in res)

    parts = all_gather(_flatten_small(dfull), "ag_small_grads")
    res = adamw(_flatten_small(w), parts, _flatten_small(m), _flatten_small(v), "adamw_small")
    for dst, r in zip((grads, deltas, new_m, new_v), res):
        dst.update(_split_small(r, w))

    return (loss, dx[None], *[grads[n] for n in WEIGHTS], *[deltas[n] for n in WEIGHTS],
            *[new_m[n] for n in WEIGHTS], *[new_v[n] for n in WEIGHTS])
```

```python
import functools
import math

import numpy as np
import jax
import jax.numpy as jnp
from jax import lax
from jax.experimental import pallas as pl
from jax.experimental.pallas import tpu as pltpu

F32 = jnp.float32
BF16 = jnp.bfloat16

DEPTH = 4
N_MIXERS = 4
S5_CH_PER_GROUP = 16
S5_STATE = 64
POOL_WINDOWS = (2, 4, 8, 16)
CONV_WIDTH = 31
ATT_HEADS = 8
DILATED_PATTERNS = ((128, 1), (512, 4), (2048, 16))
ATT_BLOCK = 128
ATT_MIX_ROWS = 256
ATT_UNROLL = 2
REL_BUCKETS = 32
REL_MAX_DIST = 2048
X_HEADS = 4
X_HEAD_DIM = 128
NORM_EPS = 1e-6
NEG_INF = -1e30
ADAM_LR = 0.001
ADAM_B1 = 0.9
ADAM_B2 = 0.999
ADAM_EPS = 1e-08
ADAM_WD = 0.01
ADAM_STEP = 10

LANES = 128
SUBLANES = 8
VMEM_BYTES = 64 * 1024 * 1024
VMEM_LIMIT = (VMEM_BYTES * 3) // 4
VMEM_LIMIT_BIG = (VMEM_BYTES * 7) // 8
STREAM_BLOCK_BYTES = 1024 * 1024
SMALL_ROW_TILE = 512
N_DEV = 8
N_CHIP = 4
MESH_AXES = ("x", "y", "c")

WEIGHTS = ['rel_bias', 'mem_norm_g', 'norm_mix_g', 'w_in', 's5_lam_re', 's5_lam_im', 's5_log_dt', 's5_b_re',
           's5_b_im', 's5_c_re', 's5_c_im', 's5_d', 's5_w_glu', 'pool_w', 'pool_scale', 'conv_w_dw', 'conv_b_dw',
           'conv_ln_g', 'conv_ln_b', 'conv_w_pw', 'grp_norm_g', 'w_out', 'norm_x_g', 'w_xq', 'w_xk', 'w_xv', 'w_xo',
           'norm_mlp_g', 'w_up', 'w_down', 'norm_final_g']
SHARDED = {'w_in': 2, 's5_w_glu': 1, 'conv_w_dw': 2, 'conv_w_pw': 1, 'w_out': 1, 'w_xq': 1, 'w_xk': 1, 'w_xv': 1,
           'w_xo': 2, 'w_up': 2, 'w_down': 1}
GATHER_BF16 = ('w_in', 'w_out', 'w_xq', 'w_xk', 'w_xv', 'w_xo', 'w_up', 'w_down')
SMALL = [n for n in WEIGHTS if n not in SHARDED]
LAYER_WEIGHTS = [n for n in WEIGHTS if n not in ('rel_bias', 'mem_norm_g', 'norm_final_g')]
IN_WEIGHTS = ('norm_mix_g', 'w_in')
MLP_WEIGHTS = ('norm_mlp_g', 'w_up', 'w_down')
MIX_WEIGHTS = [n for n in LAYER_WEIGHTS if n not in IN_WEIGHTS + MLP_WEIGHTS]
MLP_SHARDED = ('w_up', 'w_down')

def _cp(n_axes, vmem=VMEM_LIMIT):
    return pltpu.CompilerParams(dimension_semantics=("arbitrary",) * n_axes, vmem_limit_bytes=vmem)


def _tile(n, prefs):
    for t in prefs:
        if n % t == 0:
            return t
    return n


MM_TILE = 1024
MM_TILE_K = 2048
MM_FULL_K = 4096


def _chip_exchange_copies(srcs, dsts, window, send_sems, recv_sems, local_sems, base=0):
    x_, y_, c_ = _place()
    me = _chip(x_, y_)
    first, count, total = window
    copies = []
    for p, (src, dst) in enumerate(zip(srcs, dsts)):
        unit = src.shape[1] // total
        rows = pl.ds(first * unit, count * unit)
        copies.append(pltpu.make_async_copy(src.at[me, rows], dst.at[me, rows], local_sems.at[base + p]))
        for k in range(1, N_CHIP):
            px = 1 - x_ if k & 2 else x_
            py = 1 - y_ if k & 1 else y_
            s = (base + p) * (N_CHIP - 1) + k - 1
            copies.append(pltpu.make_async_remote_copy(
                src_ref=src.at[_chip(px, py), rows], dst_ref=dst.at[me, rows], send_sem=send_sems.at[s],
                recv_sem=recv_sems.at[s], device_id=(px, py, c_), device_id_type=MESH_ID))
    return copies


class _Riders:
    def __init__(self, rides):
        self.rides = rides or []
        self.srcs = [s for r in self.rides for s in r[0]]
        self.dsts = [d for r in self.rides for d in r[1]]
        self.n = len(self.srcs)

    def operands(self):
        return (*self.srcs, *self.dsts)

    def in_specs(self):
        return [HBM_SPEC] * (2 * self.n)

    def out_specs(self):
        return [HBM_SPEC] * self.n

    def out_shape(self):
        return [jax.ShapeDtypeStruct(d.shape, d.dtype) for d in self.dsts]

    def scratch(self):
        if not self.n:
            return []
        n_sem = self.n * (N_CHIP - 1)
        return [pltpu.SemaphoreType.DMA((n_sem,)), pltpu.SemaphoreType.DMA((n_sem,)), pltpu.SemaphoreType.DMA((self.n,))]

    def aliases(self, first_in, first_out):
        return {first_in + self.n + p: first_out + p for p in range(self.n)}

    def hooks(self, grid, src_refs, dst_refs, sems):
        if not self.n:
            return (lambda: None), (lambda: None)

        def copies():
            out, base = [], 0
            for srcs, _, window in self.rides:
                k = len(srcs)
                out += _chip_exchange_copies(src_refs[base:base + k], dst_refs[base:base + k], window, *sems, base=base)
                base += k
            return out

        ids = [pl.program_id(ax) for ax in range(len(grid))]
        at_start = functools.reduce(jnp.logical_and, [i == 0 for i in ids])
        at_end = functools.reduce(jnp.logical_and, [i == g - 1 for i, g in zip(ids, grid)])

        def start():
            @pl.when(at_start)
            def _():
                for cp in copies():
                    cp.start()

        def wait():
            @pl.when(at_end)
            def _():
                for cp in copies():
                    cp.wait()

        return start, wait


def _mm(a, b, *, ta=False, tb=False, res=None, out_dtype=F32, epilogue=None, pre=None, riders=None, dest_cols=None,
        name):
    if ta:
        K, M = a.shape
    else:
        M, K = a.shape
    if tb:
        N, K2 = b.shape
    else:
        K2, N = b.shape
    assert K == K2, (a.shape, b.shape, ta, tb)
    wide_f32 = K >= MM_TILE_K and F32 in (a.dtype, b.dtype)
    tm = _tile(M, (MM_TILE // 2 if wide_f32 and not ta else MM_TILE, 512, 256, 128))
    tn = _tile(N, (MM_TILE, 512, 256, 128)) if dest_cols is None else dest_cols
    tk = K if K <= MM_FULL_K else _tile(K, (MM_TILE_K, 1024, 512, 256, 128))
    nk = K // tk
    a_spec = pl.BlockSpec((tk, tm), lambda i, j, k: (k, i)) if ta else pl.BlockSpec((tm, tk), lambda i, j, k: (i, k))
    b_spec = pl.BlockSpec((tn, tk), lambda i, j, k: (j, k)) if tb else pl.BlockSpec((tk, tn), lambda i, j, k: (k, j))
    o_spec = pl.BlockSpec((tm, tn), lambda i, j, k: (i, j))
    dn = (((0 if ta else 1,), (1 if tb else 0,)), ((), ()))
    extra = [x for x in (res, pre) if x is not None]
    assert not (res is not None and pre is not None)
    assert dest_cols is None or (epilogue is None and not extra and tn <= MM_TILE)
    n_out = 2 if epilogue == 'relu_sq' else 1
    ride = _Riders(riders)
    n_pairs = ride.n
    grid = (M // tm, N // tn, nk)

    def body(*refs):
        a_ref, b_ref = refs[:2]
        x_ref = refs[2] if extra else None
        n_in = 2 + len(extra) + 2 * n_pairs
        o_refs = refs[n_in:n_in + n_out]
        scratch = refs[n_in + n_out + n_pairs:]
        acc = scratch[0] if nk > 1 else None
        start, wait = ride.hooks(grid, refs[n_in - 2 * n_pairs:n_in - n_pairs],
                                 refs[n_in + n_out:n_in + n_out + n_pairs], scratch[-3:])
        start()

        def finish(r):
            if res is not None:
                r = r + x_ref[...]
            if epilogue == 'relu_sq':
                o_refs[0][...] = r
                o_refs[1][...] = jnp.square(jnp.maximum(r, 0.0)).astype(out_dtype)
            elif epilogue == 'relu_sq_grad':
                o_refs[0][...] = (r * (2.0 * jnp.maximum(x_ref[...], 0.0))).astype(out_dtype)
            else:
                o_refs[0][...] = r.astype(out_dtype)

        part = lax.dot_general(a_ref[...].astype(BF16), b_ref[...].astype(BF16), dn, preferred_element_type=F32)
        if nk == 1:
            finish(part)
        else:
            k = pl.program_id(2)

            @pl.when(k == 0)
            def _():
                acc[...] = part

            @pl.when(k > 0)
            def _():
                acc[...] += part

            @pl.when(k == nk - 1)
            def _():
                finish(acc[...])

        wait()

    out_shape = [jax.ShapeDtypeStruct((M, N), F32 if epilogue == 'relu_sq' else out_dtype)]
    if n_out == 2:
        out_shape.append(jax.ShapeDtypeStruct((M, N), out_dtype))
    in_specs = [a_spec, b_spec] + [o_spec] * len(extra)
    out_specs = [o_spec] * n_out
    if dest_cols is not None:
        out_shape = [jax.ShapeDtypeStruct((N // tn, M, tn), out_dtype)]
        out_specs = [pl.BlockSpec((None, tm, tn), lambda i, j, k: (j, i, 0))]
    scratch = ([pltpu.VMEM((tm, tn), F32)] if nk > 1 else []) + ride.scratch()
    args = (a, b, *extra)
    outs = pl.pallas_call(
        body, grid=grid, in_specs=in_specs + ride.in_specs(), out_specs=out_specs + ride.out_specs(),
        out_shape=out_shape + ride.out_shape(), scratch_shapes=scratch,
        input_output_aliases=ride.aliases(len(args), n_out), name=name, compiler_params=_cp(3, VMEM_LIMIT_BIG))(
            *args, *ride.operands())
    return outs[0] if len(outs) == 1 else tuple(outs)


def _linear(name):
    @jax.custom_vjp
    def lin(a, w):
        return _mm(a, w, name=name + "_fwd")

    def fwd(a, w):
        return _mm(a, w, name=name + "_fwd"), (a, w)

    def bwd(r, dy):
        a, w = r
        da = _mm(dy, w, tb=True, name=name + "_dx")
        dw = _mm(a, dy, ta=True, out_dtype=w.dtype, name=name + "_dw")
        return da, dw

    lin.defvjp(fwd, bwd)
    return lin


def _act_linear(name, act, n_in, with_res):
    def run(*a):
        ins, w = a[:n_in], a[n_in]
        x = act.fwd_call(*ins)[0]
        return _mm(x, w, res=a[n_in + 1] if with_res else None, name=name + "_fwd"), (ins, x, w)

    @jax.custom_vjp
    def op(*a):
        return run(*a)[0]

    def bwd(r, dy):
        ins, x, w = r
        dx = _mm(dy, w, tb=True, name=name + "_dx")
        dw = _mm(x, dy, ta=True, out_dtype=w.dtype, name=name + "_dw")
        return (*act.bwd_all(ins, (dx,)), dw) + ((dy,) if with_res else ())

    op.defvjp(run, bwd)
    return op


def _mlp_fwd(name, norm, h, g, w_up, w_down):
    hn = norm.fwd_call(h, g)[0]
    a, r = _mm(hn, w_up, epilogue='relu_sq', out_dtype=BF16, name=name + "_up_fwd")
    return _mm(r, w_down, res=h, name=name + "_down_fwd"), (h, g, hn, a, r, w_up, w_down)


RIDE_UNITS = 16
MLP_RIDE_UNITS = (3, 3, 2, 2)
ATT_RIDE_WINDOW = (sum(MLP_RIDE_UNITS), RIDE_UNITS - sum(MLP_RIDE_UNITS), RIDE_UNITS)


def _mlp_bwd(name, norm, saved, dy, pending, lands):
    h, g, hn, a, r, w_up, w_down = saved

    def mm(i, *args, **kw):
        nonlocal lands
        if pending is None:
            return _mm(*args, **kw)
        window = (sum(MLP_RIDE_UNITS[:i]), MLP_RIDE_UNITS[i], RIDE_UNITS)
        out, *lands = _mm(*args, riders=[(pending, lands, window)], **kw)
        return out

    da = mm(0, dy, w_down, tb=True, epilogue='relu_sq_grad', pre=a, out_dtype=BF16, name=name + "_down_dx")
    dw_down = mm(1, r, dy, ta=True, out_dtype=w_down.dtype, name=name + "_down_dw")
    dhn = mm(2, da, w_up, tb=True, name=name + "_up_dx")
    dw_up = mm(3, hn, da, ta=True, out_dtype=w_up.dtype, dest_cols=w_up.shape[1] // N_DEV, name=name + "_up_dw")
    dh, dg = norm.bwd_all((h, g), (dhn,))
    return dh + dy, dg, dw_up, dw_down, lands


def _block_op(name, f, grid, ins, outs, vmem=VMEM_LIMIT):
    n_in, n_out = len(ins), len(outs)
    in_specs = [pl.BlockSpec(bs, im) for bs, im, _, _ in ins]
    out_specs = [pl.BlockSpec(bs, im) for _, _, bs, im in outs]
    out_shape = [jax.ShapeDtypeStruct(s, d) for s, d, _, _ in outs]
    didx = [i for i in range(n_in) if ins[i][3]]

    def fwd_call(*args):
        def body(*refs):
            res = f(*[r[...] for r in refs[:n_in]])
            for r, o in zip(refs[n_in:], res):
                r[...] = o.astype(r.dtype)

        return pl.pallas_call(body, grid=grid, in_specs=in_specs, out_specs=out_specs, out_shape=out_shape,
                              name=name + "_fwd", compiler_params=_cp(len(grid), vmem))(*args)

    def bwd_call(args, cts):
        def body(*refs):
            vals = [r[...] for r in refs[:n_in]]
            ct_refs = refs[n_in:n_in + n_out]
            g_refs = refs[n_in + n_out:]

            def fd(*dv):
                full = list(vals)
                for i, v in zip(didx, dv):
                    full[i] = v
                return f(*full)

            _, vjp = jax.vjp(fd, *[vals[i] for i in didx])
            grads = vjp(tuple(r[...] for r in ct_refs))
            for gref, i, g in zip(g_refs, didx, grads):
                acc = ins[i][2]
                if acc:
                    first = functools.reduce(jnp.logical_and, [pl.program_id(ax) == 0 for ax in acc])

                    @pl.when(first)
                    def _(gref=gref):
                        gref[...] = jnp.zeros_like(gref)

                    gref[...] += g.astype(gref.dtype)
                else:
                    gref[...] = g.astype(gref.dtype)

        g_specs = [pl.BlockSpec(ins[i][0], ins[i][1]) for i in didx]
        g_shape = [jax.ShapeDtypeStruct(args[i].shape, args[i].dtype) for i in didx]
        return pl.pallas_call(body, grid=grid, in_specs=in_specs + out_specs, out_specs=g_specs, out_shape=g_shape,
                              name=name + "_bwd", compiler_params=_cp(len(grid), vmem))(*args, *cts)

    @jax.custom_vjp
    def op(*args):
        return tuple(fwd_call(*args))

    def op_fwd(*args):
        return tuple(fwd_call(*args)), args

    def op_bwd(args, cts):
        it = iter(bwd_call(args, cts))
        return tuple(next(it) if ins[i][3] else jnp.zeros_like(args[i]) for i in range(n_in))

    op.defvjp(op_fwd, op_bwd)
    op.fwd_call = fwd_call
    op.bwd_all = op_bwd
    return op


def _row(tr, c):
    return ((tr, c), lambda i: (i, 0), None, True)


def _par(shape):
    nd = len(shape)
    return (shape, lambda i: (0,) * nd, (0,), True)


def _bdot(a, w):
    return jnp.dot(a.astype(BF16), w.astype(BF16), preferred_element_type=F32)


def _rms_f(x, g):
    return (x * lax.rsqrt(jnp.mean(x * x, axis=-1, keepdims=True) + NORM_EPS) * g,)


def _rms_op(name, R, D, out_dtype):
    tr = _tile(R, (256,))
    return _block_op(name, _rms_f, (R // tr,), [_row(tr, D), _par((1, D))],
                     [((R, D), out_dtype, (tr, D), lambda i: (i, 0))])


def rmsnorm(x, g, name):
    R, D = x.shape
    return _rms_op(name, R, D, F32)(x, g.reshape(1, D))[0]


def s5_epilogue(yc, u, d, w_glu, name):
    R, C = yc.shape
    tr = _tile(R, (256,))

    def f(yc, u, d, w):
        g = jax.nn.gelu(yc + d * u)
        return (g * jax.nn.sigmoid(_bdot(g, w)),)

    op = _block_op(name, f, (R // tr,), [_row(tr, C), _row(tr, C), _par((1, C)), _par((C, C))],
                   [((R, C), F32, (tr, C), lambda i: (i, 0))])
    return op(yc, u, d.reshape(1, C), w_glu)[0]


def pool_proj(p, w, scale, name):
    R, C = p.shape
    ng, pc, _ = w.shape
    tr = _tile(R, (256,))

    def f(p, w, s):
        ys = [_bdot(p[:, g * pc:(g + 1) * pc], w[g]) for g in range(ng)]
        return (jnp.concatenate(ys, axis=-1) * s,)

    op = _block_op(name, f, (R // tr,), [_row(tr, C), _par((ng, pc, pc)), _par((1, C))],
                   [((R, C), F32, (tr, C), lambda i: (i, 0))])
    return op(p, w, scale.reshape(1, C))[0]


def conv_post(h, ln_g, ln_b, w_pw, name):
    R, C = h.shape
    tr = _tile(R, (256,))

    def f(h, g, b, w):
        hc = h - jnp.mean(h, axis=-1, keepdims=True)
        y = hc * lax.rsqrt(jnp.mean(hc * hc, axis=-1, keepdims=True) + NORM_EPS) * g + b
        return (_bdot(jax.nn.silu(y), w),)

    op = _block_op(name, f, (R // tr,), [_row(tr, C), _par((1, C)), _par((1, C)), _par((C, C))],
                   [((R, C), F32, (tr, C), lambda i: (i, 0))])
    return op(h, ln_g.reshape(1, C), ln_b.reshape(1, C), w_pw)[0]


def _group_norm_op(name, R, C, n, out_dtype):
    tr = _tile(R, (256,))

    def f(*a):
        g = a[n]
        parts = [y * lax.rsqrt(jnp.mean(y * y, axis=-1, keepdims=True) + NORM_EPS) for y in a[:n]]
        return (jnp.concatenate(parts, axis=-1) * g,)

    return _block_op(name, f, (R // tr,), [_row(tr, C)] * n + [_par((1, n * C))],
                     [((R, n * C), out_dtype, (tr, n * C), lambda i: (i, 0))])


def _cross_attention_op(name, L, W, M, out_dtype):
    E = X_HEAD_DIM
    tq = _tile(L, (512,))

    def f(q, k, v):
        s = lax.dot_general(q.astype(BF16), k.astype(BF16), (((1,), (1,)), ((), ())),
                            preferred_element_type=F32) * (E ** -0.5)
        p = jax.nn.softmax(s, axis=-1)
        return (_bdot(p, v),)

    qspec = ((tq, E), lambda h, i: (i, h), None, True)
    kspec = ((M, E), lambda h, i: (0, h), (1,), True)
    return _block_op(name, f, (W // E, L // tq), [qspec, kspec, kspec],
                     [((L, W), out_dtype, (tq, E), lambda h, i: (i, h))])


def cross_attention(q, k, v, name):
    return _cross_attention_op(name, q.shape[0], q.shape[1], k.shape[0], F32)(q, k, v)[0]


def s5_discretise(lam_re, lam_im, log_dt, b_re_t, b_im_t, name):
    G, _, N = lam_re.shape
    C = b_re_t.shape[1]

    def f(lr, li, ldt, br, bi):
        dt = jnp.exp(ldt)
        mag = jnp.exp(lr * dt)
        ab_r, ab_i = mag * jnp.cos(li * dt), mag * jnp.sin(li * dt)
        den = lr * lr + li * li
        nr, ni = ab_r - 1.0, ab_i
        f_r = (nr * lr + ni * li) / den
        f_i = (ni * lr - nr * li) / den
        return ab_r, ab_i, f_r * br - f_i * bi, f_r * bi + f_i * br

    vec = ((G, 1, N), lambda i: (0, 0, 0), None, True)
    mat = ((G, C, N), lambda i: (0, 0, 0), None, True)
    ov = ((G, 1, N), F32, (G, 1, N), lambda i: (0, 0, 0))
    om = ((G, C, N), F32, (G, C, N), lambda i: (0, 0, 0))
    op = _block_op(name, f, (1,), [vec, vec, vec, mat, mat], [ov, ov, om, om])
    return op(lam_re, lam_im, log_dt, b_re_t, b_im_t)


def rel_bias_tables(rel_bias, onehot, name):
    B, H = rel_bias.shape
    P, _, Q = onehot.shape

    def f(rbt, oh):
        return (jnp.dot(rbt, oh, precision=lax.Precision.HIGHEST, preferred_element_type=F32),)

    op = _block_op(name, f, (P,), [((H, B), lambda p: (0, 0), (0,), True), ((None, B, Q), lambda p: (p, 0, 0), None, False)],
                   [((P, H, Q), F32, (None, H, Q), lambda p: (p, 0, 0))])
    return op(rel_bias.T, onehot)[0]


def _shift_down(x, s, row):
    return jnp.where(row >= s, pltpu.roll(x, s, 0), 0.0)


def _shift_up(x, s, row):
    n = x.shape[0]
    return jnp.where(row < n - s, pltpu.roll(x, n - s, 0), 0.0)


def _window_sum(x, w, row, shift):
    span = 1
    while span < w:
        x = x + shift(x, span, row)
        span *= 2
    return x


def _pool_call(u, d_out, name):
    L, C = u.shape
    pc = C // len(POOL_WINDOWS)
    assert pc % LANES == 0

    def body(x_ref, o_ref):
        row = lax.broadcasted_iota(jnp.int32, (L, pc), 0)
        for g, w in enumerate(POOL_WINDOWS):
            sl = slice(g * pc, (g + 1) * pc)
            x = x_ref[:, sl]
            cnt = jnp.minimum(row + 1, w).astype(F32)
            if d_out is None:
                o_ref[:, sl] = _window_sum(x, w, row, _shift_down) / cnt - x
            else:
                o_ref[:, sl] = _window_sum(x / cnt, w, row, _shift_up) - x

    src = u if d_out is None else d_out
    return pl.pallas_call(body, out_shape=jax.ShapeDtypeStruct((L, C), F32), name=name,
                          compiler_params=pltpu.CompilerParams(vmem_limit_bytes=VMEM_LIMIT))(src)


def _pool_mix(name):
    @jax.custom_vjp
    def op(u):
        return _pool_call(u, None, name + "_fwd")

    def fwd(u):
        return _pool_call(u, None, name + "_fwd"), u

    def bwd(u, dp):
        return (_pool_call(u, dp, name + "_bwd"),)

    op.defvjp(fwd, bwd)
    return op


def _conv_fwd(u, w, b, name):
    L, C2 = u.shape
    C = C2 // 2
    K = w.shape[0]
    nb = C // LANES

    def body(val_ref, gate_ref, w_ref, b_ref, o_ref):
        row = lax.broadcasted_iota(jnp.int32, (L, LANES), 0)
        h = val_ref[...] * jax.nn.sigmoid(gate_ref[...])
        acc = jnp.broadcast_to(b_ref[...], (L, LANES))
        for k in range(K):
            acc = acc + w_ref[k:k + 1, :] * _shift_down(h, K - 1 - k, row)
        o_ref[...] = acc

    blk = lambda off: pl.BlockSpec((L, LANES), lambda j: (0, j + off))
    return pl.pallas_call(
        body, grid=(nb,), in_specs=[blk(0), blk(nb), pl.BlockSpec((K, LANES), lambda j: (0, j)),
                                    pl.BlockSpec((1, LANES), lambda j: (0, j))],
        out_specs=blk(0), out_shape=jax.ShapeDtypeStruct((L, C), F32), name=name, compiler_params=_cp(1))(u, u, w, b)


def _conv_bwd(u, w, dh, name):
    L, C2 = u.shape
    C = C2 // 2
    K = w.shape[0]
    nb = C // LANES

    def body(val_ref, gate_ref, w_ref, dh_ref, dval_ref, dgate_ref, dw_ref, db_ref):
        row = lax.broadcasted_iota(jnp.int32, (L, LANES), 0)
        val = val_ref[...]
        sig = jax.nn.sigmoid(gate_ref[...])
        h = val * sig
        d = dh_ref[...]
        dh0 = jnp.zeros((L, LANES), F32)
        for k in range(K):
            s = K - 1 - k
            dh0 = dh0 + w_ref[k:k + 1, :] * _shift_up(d, s, row)
            dw_ref[k:k + 1, :] = jnp.sum(d * _shift_down(h, s, row), axis=0, keepdims=True)
        db_ref[...] = jnp.sum(d, axis=0, keepdims=True)
        dval_ref[...] = dh0 * sig
        dgate_ref[...] = dh0 * val * sig * (1.0 - sig)

    blk = lambda off: pl.BlockSpec((L, LANES), lambda j: (0, j + off))
    return pl.pallas_call(
        body, grid=(nb,), in_specs=[blk(0), blk(nb), pl.BlockSpec((K, LANES), lambda j: (0, j)), blk(0)],
        out_specs=[blk(0), blk(0), pl.BlockSpec((K, LANES), lambda j: (0, j)), pl.BlockSpec((1, LANES), lambda j: (0, j))],
        out_shape=[jax.ShapeDtypeStruct((L, C), F32), jax.ShapeDtypeStruct((L, C), F32),
                   jax.ShapeDtypeStruct((K, C), F32), jax.ShapeDtypeStruct((1, C), F32)],
        name=name, compiler_params=_cp(1))(u, u, w, dh)


def _glu_conv(name):
    @jax.custom_vjp
    def op(u, w, b):
        return _conv_fwd(u, w, b, name + "_fwd")

    def fwd(u, w, b):
        return _conv_fwd(u, w, b, name + "_fwd"), (u, w)

    def bwd(r, dh):
        u, w = r
        dval, dgate, dw, db = _conv_bwd(u, w, dh, name + "_bwd")
        return jnp.concatenate([dval, dgate], axis=-1), dw, db

    op.defvjp(fwd, bwd)
    return op


S5_BLOCK_CH = LANES
S5_BLOCK_ST = S5_BLOCK_CH // S5_CH_PER_GROUP * S5_STATE


def _s5_scan(br_ref, bi_ref, ar, ai, reverse):
    L, C = br_ref.shape
    T = SUBLANES
    row = lax.broadcasted_iota(jnp.int32, (T, C), 0)
    pw = [(ar, ai)]
    for _ in range(T - 1):
        pr, pi = pw[-1]
        pw.append((pr * ar - pi * ai, pr * ai + pi * ar))
    cr = jnp.zeros((T, C), F32)
    ci = jnp.zeros((T, C), F32)
    for r in range(T):
        e = (T - r) if reverse else (r + 1)
        cr = jnp.where(row == r, pw[e - 1][0], cr)
        ci = jnp.where(row == r, pw[e - 1][1], ci)
    steps = []
    s = 1
    while s < T:
        mask = (row < T - s) if reverse else (row >= s)
        steps.append((T - s if reverse else s, mask, pw[s - 1][0], pw[s - 1][1]))
        s *= 2
    nt = L // T
    last = 0 if reverse else T - 1

    def body(i, carry):
        kr, ki = carry
        t = (nt - 1 - i) if reverse else i
        off = pl.multiple_of(t * T, T)
        xr = br_ref[pl.ds(off, T), :]
        xi = bi_ref[pl.ds(off, T), :]
        for sh, mask, mr, mi in steps:
            sr = jnp.where(mask, pltpu.roll(xr, sh, 0), 0.0)
            si = jnp.where(mask, pltpu.roll(xi, sh, 0), 0.0)
            xr, xi = xr + mr * sr - mi * si, xi + mr * si + mi * sr
        xr, xi = xr + cr * kr - ci * ki, xi + cr * ki + ci * kr
        br_ref[pl.ds(off, T), :] = xr
        bi_ref[pl.ds(off, T), :] = xi
        return (jnp.broadcast_to(xr[last:last + 1, :], (T, C)), jnp.broadcast_to(xi[last:last + 1, :], (T, C)))

    z = jnp.zeros((T, C), F32)
    lax.fori_loop(0, nt, body, (z, z))


def _s5_specs(L):
    nb_axis = lambda j: (j, 0, 0)
    u = pl.BlockSpec((L, S5_BLOCK_CH), lambda j: (0, j))
    wb = pl.BlockSpec((None, S5_BLOCK_CH, S5_BLOCK_ST), nb_axis)
    a = pl.BlockSpec((1, S5_BLOCK_ST), lambda j: (0, j))
    wc = pl.BlockSpec((None, S5_BLOCK_ST, S5_BLOCK_CH), nb_axis)
    return u, wb, a, wc


def _s5_fwd(u, wbr, wbi, ar, ai, wcr, wci, name):
    L, C = u.shape
    nb = C // S5_BLOCK_CH
    us, wbs, as_, wcs = _s5_specs(L)

    def body(u_ref, wbr_ref, wbi_ref, ar_ref, ai_ref, wcr_ref, wci_ref, y_ref, xr, xi):
        ub = u_ref[...]
        xr[...] = _bdot(ub, wbr_ref[...])
        xi[...] = _bdot(ub, wbi_ref[...])
        _s5_scan(xr, xi, ar_ref[...], ai_ref[...], False)
        y_ref[...] = _bdot(xr[...], wcr_ref[...]) - _bdot(xi[...], wci_ref[...])

    return pl.pallas_call(
        body, grid=(nb,), in_specs=[us, wbs, wbs, as_, as_, wcs, wcs], out_specs=us,
        out_shape=jax.ShapeDtypeStruct((L, C), F32),
        scratch_shapes=[pltpu.VMEM((L, S5_BLOCK_ST), F32)] * 2, name=name, compiler_params=_cp(1))(
            u, wbr, wbi, ar, ai, wcr, wci)


def _dot_t(a, b):
    return lax.dot_general(a.astype(BF16), b.astype(BF16), (((0,), (0,)), ((), ())), preferred_element_type=F32)


def _dot_nt(a, b):
    return lax.dot_general(a.astype(BF16), b.astype(BF16), (((1,), (1,)), ((), ())), preferred_element_type=F32)


def _s5_bwd(u, wbr, wbi, ar, ai, wcr, wci, dy, name):
    L, C = u.shape
    nb = C // S5_BLOCK_CH
    us, wbs, as_, wcs = _s5_specs(L)
    T = SUBLANES

    def body(u_ref, wbr_ref, wbi_ref, ar_ref, ai_ref, wcr_ref, wci_ref, dy_ref,
             du_ref, dwbr_ref, dwbi_ref, dar_ref, dai_ref, dwcr_ref, dwci_ref, xr, xi, gr, gi):
        ub = u_ref[...]
        a_r, a_i = ar_ref[...], ai_ref[...]
        xr[...] = _bdot(ub, wbr_ref[...])
        xi[...] = _bdot(ub, wbi_ref[...])
        _s5_scan(xr, xi, a_r, a_i, False)
        d = dy_ref[...]
        dwcr_ref[...] = _dot_t(xr[...], d)
        dwci_ref[...] = -_dot_t(xi[...], d)
        gr[...] = _dot_nt(d, wcr_ref[...])
        gi[...] = -_dot_nt(d, wci_ref[...])
        _s5_scan(gr, gi, a_r, -a_i, True)

        row = lax.broadcasted_iota(jnp.int32, (T, S5_BLOCK_ST), 0)

        def da_body(i, carry):
            pr, pi, sr, si = carry
            off = pl.multiple_of(i * T, T)
            xr_t, xi_t = xr[pl.ds(off, T), :], xi[pl.ds(off, T), :]
            lr_t, li_t = gr[pl.ds(off, T), :], gi[pl.ds(off, T), :]
            qr = jnp.where(row == 0, pr, pltpu.roll(xr_t, 1, 0))
            qi = jnp.where(row == 0, pi, pltpu.roll(xi_t, 1, 0))
            sr = sr + qr * lr_t + qi * li_t
            si = si + qr * li_t - qi * lr_t
            return (jnp.broadcast_to(xr_t[T - 1:T, :], (T, S5_BLOCK_ST)),
                    jnp.broadcast_to(xi_t[T - 1:T, :], (T, S5_BLOCK_ST)), sr, si)

        z = jnp.zeros((T, S5_BLOCK_ST), F32)
        _, _, sr, si = lax.fori_loop(0, L // T, da_body, (z, z, z, z))
        dar_ref[...] = jnp.sum(sr, axis=0, keepdims=True)
        dai_ref[...] = jnp.sum(si, axis=0, keepdims=True)
        lr, li = gr[...], gi[...]
        dwbr_ref[...] = _dot_t(ub, lr)
        dwbi_ref[...] = _dot_t(ub, li)
        du_ref[...] = _dot_nt(lr, wbr_ref[...]) + _dot_nt(li, wbi_ref[...])

    sds = jax.ShapeDtypeStruct
    return pl.pallas_call(
        body, grid=(nb,), in_specs=[us, wbs, wbs, as_, as_, wcs, wcs, us],
        out_specs=[us, wbs, wbs, as_, as_, wcs, wcs],
        out_shape=[sds(u.shape, F32), sds(wbr.shape, F32), sds(wbi.shape, F32), sds(ar.shape, F32),
                   sds(ai.shape, F32), sds(wcr.shape, F32), sds(wci.shape, F32)],
        scratch_shapes=[pltpu.VMEM((L, S5_BLOCK_ST), F32)] * 4, name=name,
        compiler_params=_cp(1, VMEM_LIMIT_BIG))(u, wbr, wbi, ar, ai, wcr, wci, dy)


def _s5_core(name):
    @jax.custom_vjp
    def op(u, wbr, wbi, ar, ai, wcr, wci):
        return _s5_fwd(u, wbr, wbi, ar, ai, wcr, wci, name + "_fwd")

    def fwd(*a):
        return _s5_fwd(*a, name + "_fwd"), a

    def bwd(a, dy):
        return tuple(_s5_bwd(*a, dy, name + "_bwd"))

    op.defvjp(fwd, bwd)
    return op


def _att_tile_f(first, q, kp, kc, vp, vc, bias):
    nq = q.shape[0]
    hb = bias.shape[0]
    E = q.shape[1] // hb
    r = lax.broadcasted_iota(jnp.int32, (nq, 2 * nq), 0)
    c = lax.broadcasted_iota(jnp.int32, (nq, 2 * nq), 1)
    prev_ok = jnp.logical_and(jnp.logical_and(c < nq, c >= r), jnp.logical_not(first))
    valid = jnp.logical_or(prev_ok, jnp.logical_and(c >= nq, c - nq <= r))
    outs, lses = [], []
    for h in range(hb):
        sl = slice(h * E, (h + 1) * E)
        k = jnp.concatenate([kp[:, sl], kc[:, sl]], axis=0)
        v = jnp.concatenate([vp[:, sl], vc[:, sl]], axis=0)
        s = jnp.where(valid, _dot_nt(q[:, sl], k) * (E ** -0.5) + bias[h], NEG_INF)
        m = jnp.max(s, axis=-1, keepdims=True)
        p = jnp.exp(s - m)
        den = jnp.sum(p, axis=-1, keepdims=True)
        outs.append(_bdot(p, v) / den)
        lses.append(jnp.broadcast_to(m + jnp.log(den), (nq, E)))
    return jnp.concatenate(outs, axis=-1), jnp.concatenate(lses, axis=-1)


def _att_tile_grad(first, q, kp, kc, vp, vc, bias, o, lse, do, dlse):
    nq = q.shape[0]
    hb = bias.shape[0]
    E = q.shape[1] // hb
    scale = E ** -0.5
    r = lax.broadcasted_iota(jnp.int32, (nq, 2 * nq), 0)
    c = lax.broadcasted_iota(jnp.int32, (nq, 2 * nq), 1)
    prev_ok = jnp.logical_and(jnp.logical_and(c < nq, c >= r), jnp.logical_not(first))
    valid = jnp.logical_or(prev_ok, jnp.logical_and(c >= nq, c - nq <= r))
    dq, dk, dv, db = [], [], [], []
    for h in range(hb):
        sl = slice(h * E, (h + 1) * E)
        qh = q[:, sl]
        k = jnp.concatenate([kp[:, sl], kc[:, sl]], axis=0)
        v = jnp.concatenate([vp[:, sl], vc[:, sl]], axis=0)
        s = jnp.where(valid, _dot_nt(qh, k) * scale + bias[h], NEG_INF)
        p = jnp.exp(s - lse[:, h * E:h * E + 1])
        doh = do[:, sl]
        row = jnp.sum(dlse[:, sl], axis=-1, keepdims=True) - jnp.sum(doh * o[:, sl], axis=-1, keepdims=True)
        ds = p * (_dot_nt(doh, v) + row)
        db.append(ds)
        dv.append(_dot_t(p, doh))
        dq.append(_bdot(ds, k) * scale)
        dk.append(_dot_t(ds, qh) * scale)
    cat = lambda parts, rows: jnp.concatenate([x[rows] for x in parts], axis=-1)
    lo, hi = slice(0, nq), slice(nq, 2 * nq)
    return jnp.concatenate(dq, axis=-1), cat(dk, lo), cat(dk, hi), cat(dv, lo), cat(dv, hi), db


def _att_mix_f(*a):
    n = len(a) // 2
    o, l = a[:n], a[n:]
    m = functools.reduce(jnp.maximum, l)
    e = [jnp.exp(li - m) for li in l]
    return sum(ei * oi for ei, oi in zip(e, o)) / sum(e)


def _att_rows(start, dil):
    if dil == 1:
        return pl.ds(pl.multiple_of(start, ATT_BLOCK), ATT_BLOCK)
    return pl.ds(start, ATT_BLOCK, stride=dil)


def _att_blocks(L, dil):
    nb = L // dil // ATT_BLOCK
    return dil * nb, nb


def _att_specs(L, W):
    nblk = W // LANES
    col = lambda off: pl.BlockSpec((L, LANES), lambda j: (0, j + off))
    per_pattern = pl.BlockSpec((len(DILATED_PATTERNS), L, LANES), lambda j: (0, 0, j))
    hb = ATT_HEADS // nblk
    bias = pl.BlockSpec((len(DILATED_PATTERNS), hb, ATT_BLOCK, 2 * ATT_BLOCK), lambda j: (0, j, 0, 0))
    return nblk, col, per_pattern, bias


def _att_fwd(qkv, bias, name):
    L, W3 = qkv.shape
    W = W3 // 3
    nblk, col, per_pattern, bias_spec = _att_specs(L, W)
    P = len(DILATED_PATTERNS)

    def body(q_ref, k_ref, v_ref, b_ref, y_ref, o_ref, l_ref):
        for p, (_, dil) in enumerate(DILATED_PATTERNS):
            n_it, nb = _att_blocks(L, dil)

            def step(i, carry, p=p, dil=dil, nb=nb):
                n = i % nb
                cur = i // nb + n * (ATT_BLOCK * dil)
                prev = i // nb + jnp.maximum(n - 1, 0) * (ATT_BLOCK * dil)
                rc, rp = _att_rows(cur, dil), _att_rows(prev, dil)
                o, l = _att_tile_f(n == 0, q_ref[rc, :], k_ref[rp, :], k_ref[rc, :], v_ref[rp, :], v_ref[rc, :],
                                   b_ref[p])
                o_ref[p, rc, :] = o
                l_ref[p, rc, :] = l
                return carry

            lax.fori_loop(0, n_it, step, 0, unroll=ATT_UNROLL)

        def mix(i, carry):
            rows = pl.ds(pl.multiple_of(i * ATT_MIX_ROWS, ATT_MIX_ROWS), ATT_MIX_ROWS)
            y_ref[rows, :] = _att_mix_f(*[o_ref[p, rows, :] for p in range(P)], *[l_ref[p, rows, :] for p in range(P)])
            return carry

        lax.fori_loop(0, L // ATT_MIX_ROWS, mix, 0)

    sds = jax.ShapeDtypeStruct
    return pl.pallas_call(
        body, grid=(nblk,), in_specs=[col(0), col(nblk), col(2 * nblk), bias_spec],
        out_specs=[col(0), per_pattern, per_pattern],
        out_shape=[sds((L, W), F32), sds((P, L, W), F32), sds((P, L, W), F32)], name=name,
        compiler_params=_cp(1))(qkv, qkv, qkv, bias)


def _att_bwd(qkv, bias, o_all, l_all, dy, name, riders=None):
    L, W3 = qkv.shape
    W = W3 // 3
    nblk, col, per_pattern, bias_spec = _att_specs(L, W)
    P = len(DILATED_PATTERNS)
    ride = _Riders(riders)
    n_in, n_out = 7, 4

    def body(*refs):
        q_ref, k_ref, v_ref, b_ref, o_ref, l_ref, dy_ref = refs[:n_in]
        outs = refs[n_in + 2 * ride.n:]
        dq_ref, dk_ref, dv_ref, db_ref = outs[:n_out]
        do_s, dl_s = outs[n_out + ride.n:n_out + ride.n + 2]
        start, wait = ride.hooks((nblk,), refs[n_in:n_in + ride.n], outs[n_out:n_out + ride.n],
                                 outs[n_out + ride.n + 2:])
        start()

        def mix(i, carry):
            rows = pl.ds(pl.multiple_of(i * ATT_MIX_ROWS, ATT_MIX_ROWS), ATT_MIX_ROWS)
            _, mix_vjp = jax.vjp(_att_mix_f, *[o_ref[p, rows, :] for p in range(P)],
                                 *[l_ref[p, rows, :] for p in range(P)])
            g = mix_vjp(dy_ref[rows, :])
            for p in range(P):
                do_s[p, rows, :] = g[p]
                dl_s[p, rows, :] = g[P + p]
            return carry

        lax.fori_loop(0, L // ATT_MIX_ROWS, mix, 0)
        for ref in (dq_ref, dk_ref, dv_ref, db_ref):
            ref[...] = jnp.zeros_like(ref)

        def add(ref, rows, val):
            ref[rows, :] = ref[rows, :] + val

        for p, (_, dil) in enumerate(DILATED_PATTERNS):
            n_it, nb = _att_blocks(L, dil)

            def step(i, carry, p=p, dil=dil, nb=nb):
                n = i % nb
                first = n == 0
                cur = i // nb + n * (ATT_BLOCK * dil)
                prev = i // nb + jnp.maximum(n - 1, 0) * (ATT_BLOCK * dil)
                rc, rp = _att_rows(cur, dil), _att_rows(prev, dil)
                dq, dkp, dkc, dvp, dvc, db = _att_tile_grad(
                    first, q_ref[rc, :], k_ref[rp, :], k_ref[rc, :], v_ref[rp, :], v_ref[rc, :], b_ref[p],
                    o_ref[p, rc, :], l_ref[p, rc, :], do_s[p, rc, :], dl_s[p, rc, :])
                add(dq_ref, rc, dq)
                add(dk_ref, rc, dkc)
                add(dv_ref, rc, dvc)
                for h, dbh in enumerate(db):
                    db_ref[p, h] = db_ref[p, h] + dbh

                @pl.when(jnp.logical_not(first))
                def _():
                    add(dk_ref, rp, dkp)
                    add(dv_ref, rp, dvp)

                return carry

            lax.fori_loop(0, n_it, step, 0, unroll=ATT_UNROLL)
        wait()

    sds = jax.ShapeDtypeStruct((L, W), F32)
    return pl.pallas_call(
        body, grid=(nblk,),
        in_specs=[col(0), col(nblk), col(2 * nblk), bias_spec, per_pattern, per_pattern, col(0)] + ride.in_specs(),
        out_specs=[col(0), col(0), col(0), bias_spec] + ride.out_specs(),
        out_shape=[sds, sds, sds, jax.ShapeDtypeStruct(bias.shape, F32)] + ride.out_shape(),
        scratch_shapes=[pltpu.VMEM((P, L, LANES), F32)] * 2 + ride.scratch(),
        input_output_aliases=ride.aliases(n_in, n_out), name=name, compiler_params=_cp(1, VMEM_LIMIT_BIG))(
            qkv, qkv, qkv, bias, o_all, l_all, dy, *ride.operands())


def _t5_bucket(dist):
    n = np.maximum(dist, 0)
    max_exact = REL_BUCKETS // 2
    large = max_exact + (np.log(np.maximum(n, 1) / max_exact) / np.log(REL_MAX_DIST / max_exact)
                         * (REL_BUCKETS - max_exact)).astype(np.int64)
    large = np.minimum(large, REL_BUCKETS - 1)
    return np.where(n < max_exact, n, large).astype(np.int32)


def _bucket_onehot():
    a = np.arange(ATT_BLOCK)[:, None]
    b = np.arange(2 * ATT_BLOCK)[None, :]
    sub = a + ATT_BLOCK - b
    bucket = jnp.asarray(np.stack([_t5_bucket(sub * dil).reshape(-1) for _, dil in DILATED_PATTERNS]))
    ids = jnp.arange(REL_BUCKETS, dtype=jnp.int32)
    return (bucket[:, None, :] == ids[None, :, None]).astype(F32)


def loss_head(h, target, g, name):
    R, D = h.shape
    tr = _tile(R, (256,))

    def body(h_ref, t_ref, g_ref, l_ref, dh_ref, dg_ref):
        def lf(hv, gv):
            y = _rms_f(hv, gv)[0]
            return 0.5 * jnp.sum(jnp.mean(jnp.square(y - t_ref[...]), axis=-1))

        l, (dh, dg) = jax.value_and_grad(lf, argnums=(0, 1))(h_ref[...], g_ref[...])

        @pl.when(pl.program_id(0) == 0)
        def _():
            l_ref[...] = jnp.zeros_like(l_ref)
            dg_ref[...] = jnp.zeros_like(dg_ref)

        dh_ref[...] = dh
        dg_ref[...] += dg
        l_ref[...] += l

    rows = pl.BlockSpec((tr, D), lambda i: (i, 0))
    vec = pl.BlockSpec((1, D), lambda i: (0, 0))
    l, dh, dg = pl.pallas_call(
        body, grid=(R // tr,), in_specs=[rows, rows, vec],
        out_specs=[pl.BlockSpec((SUBLANES, LANES), lambda i: (0, 0)), rows, vec],
        out_shape=[jax.ShapeDtypeStruct((SUBLANES, LANES), F32), jax.ShapeDtypeStruct((R, D), F32),
                   jax.ShapeDtypeStruct((1, D), F32)], name=name, compiler_params=_cp(1))(h, target, g.reshape(1, D))
    return l[0, 0], dh, dg.reshape(D)


def _adamw_update(w, g, m, v):
    c1 = 1.0 - ADAM_B1 ** ADAM_STEP
    c2 = 1.0 - ADAM_B2 ** ADAM_STEP
    nm = ADAM_B1 * m + (1.0 - ADAM_B1) * g
    nv = ADAM_B2 * v + (1.0 - ADAM_B2) * jnp.square(g)
    return -ADAM_LR * ((nm / c1) / (jnp.sqrt(nv / c2) + ADAM_EPS) + ADAM_WD * w), nm, nv


def adamw_layers(w, parts, m, v, name):
    nl, a, b = w.shape
    n_parts = parts[0].shape[0]
    tr = _row_tile(a, b)

    def body(*refs):
        w_ref, p_refs, (m_ref, v_ref, g_ref, d_ref, nm_ref, nv_ref) = refs[0], refs[1:1 + nl], refs[1 + nl:]
        for l in range(nl):
            @pl.when(pl.program_id(0) == l)
            def _(p_ref=p_refs[l]):
                g = p_ref[0].astype(F32)
                for i in range(1, n_parts):
                    g = g + p_ref[i].astype(F32)
                d_ref[...], nm_ref[...], nv_ref[...] = _adamw_update(w_ref[...], g, m_ref[...], v_ref[...])
                g_ref[...] = g

    rows = pl.BlockSpec((None, tr, b), lambda l, i: (l, i, 0))
    part = lambda k: pl.BlockSpec((n_parts, tr, b), lambda l, i: (0, jnp.where(l == k, i, 0), 0))
    sds = jax.ShapeDtypeStruct((nl, a, b), F32)
    return pl.pallas_call(body, grid=(nl, a // tr), in_specs=[rows] + [part(k) for k in range(nl)] + [rows, rows],
                          out_specs=[rows] * 4, out_shape=[sds] * 4, name=name, compiler_params=_cp(2))(
                              w, *parts, m, v)


def adamw(w, parts, m, v, name):
    R, C = w.shape
    n_parts = parts.shape[0]
    tr = _row_tile(R, C)

    def body(w_ref, p_ref, m_ref, v_ref, g_ref, d_ref, nm_ref, nv_ref):
        g = p_ref[0].astype(F32)
        for i in range(1, n_parts):
            g = g + p_ref[i].astype(F32)
        d_ref[...], nm_ref[...], nv_ref[...] = _adamw_update(w_ref[...], g, m_ref[...], v_ref[...])
        g_ref[...] = g

    rows = pl.BlockSpec((tr, C), lambda i: (i, 0))
    sds = jax.ShapeDtypeStruct((R, C), F32)
    return pl.pallas_call(body, grid=(R // tr,),
                          in_specs=[rows, pl.BlockSpec((n_parts, tr, C), lambda i: (0, i, 0)), rows, rows],
                          out_specs=[rows] * 4, out_shape=[sds] * 4, name=name, compiler_params=_cp(1))(w, parts, m, v)


HBM_SPEC = pl.BlockSpec(memory_space=pltpu.HBM)
MESH_ID = pl.DeviceIdType.MESH


def _place():
    return lax.axis_index("x"), lax.axis_index("y"), lax.axis_index("c")


def _index(x, y, c):
    return 4 * x + 2 * y + c


AG_COPIES = 9
AG_ROW_UNIT = 32


def all_gather(x, name):
    R, C = x.shape
    assert R % AG_ROW_UNIT == 0, x.shape
    half = R // 2

    def body(x_ref, out_ref, send_sems, recv_sems, local_sem):
        x_, y_, c_ = _place()
        me, sib = (x_, y_, c_), (x_, y_, 1 - c_)
        nx, ny, nd = (1 - x_, y_, c_), (x_, 1 - y_, c_), (1 - x_, 1 - y_, c_)
        upper, lower = pl.ds(0, half), pl.ds(half, half)

        def slot(dev, rows=None):
            ref = out_ref.at[_index(*dev)]
            return ref if rows is None else ref.at[rows]

        def copy(k, block, to, rows=None, src=None):
            return pltpu.make_async_remote_copy(
                src_ref=slot(block, rows) if src is None else src, dst_ref=slot(block, rows),
                send_sem=send_sems.at[k], recv_sem=recv_sems.at[k], device_id=to, device_id_type=MESH_ID)

        def other(dev):
            return (dev[0], dev[1], 1 - c_)

        mine = pltpu.make_async_copy(x_ref, slot(me), local_sem)
        mine.start()
        sent = [copy(0, me, sib, src=x_ref), copy(1, me, nx, src=x_ref), copy(2, me, ny, src=x_ref)]
        for cp in sent:
            cp.start()

        def then(arrival, *forwards):
            arrival.wait_recv()
            for cp in forwards:
                cp.start()
            sent.extend(forwards)

        then(copy(1, nx, me), copy(4, nx, ny, upper), copy(5, nx, sib))
        then(copy(2, ny, me), copy(3, ny, nx, lower), copy(6, ny, sib))
        then(copy(3, nd, me, lower), copy(8, nd, sib, lower))
        then(copy(4, nd, me, upper), copy(7, nd, sib, upper))
        copy(0, sib, me).wait_recv()
        copy(5, other(nx), me).wait_recv()
        copy(6, other(ny), me).wait_recv()
        copy(7, other(nd), me, upper).wait_recv()
        copy(8, other(nd), me, lower).wait_recv()
        for cp in sent:
            cp.wait_send()
        mine.wait()

    return pl.pallas_call(
        body, out_shape=jax.ShapeDtypeStruct((N_DEV,) + x.shape, x.dtype), in_specs=[HBM_SPEC], out_specs=HBM_SPEC,
        scratch_shapes=[pltpu.SemaphoreType.DMA((AG_COPIES,)), pltpu.SemaphoreType.DMA((AG_COPIES,)),
                        pltpu.SemaphoreType.DMA], name=name)(x)


def _chip(x, y):
    return 2 * x + y


def sibling_exchange(xs, name):
    n = len(xs)

    def body(*refs):
        x_refs, out_refs, (send_sems, recv_sems) = refs[:n], refs[n:2 * n], refs[2 * n:]
        x_, y_, c_ = _place()
        copies = [pltpu.make_async_remote_copy(src_ref=x_ref.at[:, 1 - c_], dst_ref=out_ref, send_sem=send_sems.at[p],
                                               recv_sem=recv_sems.at[p], device_id=(x_, y_, 1 - c_),
                                               device_id_type=MESH_ID)
                  for p, (x_ref, out_ref) in enumerate(zip(x_refs, out_refs))]
        for cp in copies:
            cp.start()
        for cp in copies:
            cp.wait()

    return pl.pallas_call(
        body, out_shape=[jax.ShapeDtypeStruct((x.shape[0],) + x.shape[2:], x.dtype) for x in xs],
        in_specs=[HBM_SPEC] * n, out_specs=[HBM_SPEC] * n,
        scratch_shapes=[pltpu.SemaphoreType.DMA((n,)), pltpu.SemaphoreType.DMA((n,))], name=name)(*xs)


def _row_tile(R, C):
    cap = max(SUBLANES, STREAM_BLOCK_BYTES // (4 * C))
    return _tile(R, [t for t in (512, 256, 128, 64, 32, 16, 8) if t <= cap])


def pair_sum(x, recv, name):
    nc, _, R, C = x.shape
    tr = _row_tile(R, C)
    core = lax.axis_index("c").astype(jnp.int32).reshape(1)

    def body(c_ref, a_ref, b_ref, o_ref):
        o_ref[...] = (a_ref[...].astype(F32) + b_ref[...].astype(F32)).astype(o_ref.dtype)

    blk = pl.BlockSpec((None, tr, C), lambda k, i, c_ref: (k, i, 0))
    grid_spec = pltpu.PrefetchScalarGridSpec(
        num_scalar_prefetch=1, grid=(nc, R // tr),
        in_specs=[pl.BlockSpec((None, None, tr, C), lambda k, i, c_ref: (k, c_ref[0], i, 0)), blk], out_specs=blk)
    return pl.pallas_call(body, grid_spec=grid_spec, out_shape=jax.ShapeDtypeStruct((nc, R, C), x.dtype), name=name,
                          compiler_params=_cp(2))(core, x, recv)


def chip_exchange(ss, name):
    n = len(ss)

    def body(*refs):
        copies = _chip_exchange_copies(refs[:n], refs[n:2 * n], (0, 1, 1), *refs[2 * n:])
        for cp in copies:
            cp.start()
        for cp in copies:
            cp.wait()

    n_sem = n * (N_CHIP - 1)
    return pl.pallas_call(
        body, out_shape=[jax.ShapeDtypeStruct(s.shape, s.dtype) for s in ss], in_specs=[HBM_SPEC] * n,
        out_specs=[HBM_SPEC] * n,
        scratch_shapes=[pltpu.SemaphoreType.DMA((n_sem,)), pltpu.SemaphoreType.DMA((n_sem,)),
                        pltpu.SemaphoreType.DMA((n,))], name=name)(*ss)


def _block_diag(w, nb):
    G, a, b = w.shape
    gp = G // nb
    eye = jnp.eye(gp, dtype=w.dtype)
    return jnp.einsum('jgab,gh->jgahb', w.reshape(nb, gp, a, b), eye).reshape(nb, gp * a, gp * b)


def _split_columns(x, cuts):
    edges = (0,) + tuple(cuts) + (x.shape[1],)

    def split(x):
        return tuple(x[:, a:b] for a, b in zip(edges[:-1], edges[1:]))

    op = jax.custom_vjp(split)
    op.defvjp(lambda x: (split(x), None), lambda _, cts: (jnp.concatenate(cts, axis=-1),))
    return op(x)


def _project_in(l, h, P):
    L, D = h.shape
    GW = D // N_MIXERS
    proj = _act_linear(f"l{l}_w_in", _rms_op(f"l{l}_norm_mix", L, D, BF16), 2, False)(
        h, P['norm_mix_g'].reshape(1, D), P['w_in'])
    return _split_columns(proj, (GW, 2 * GW, 4 * GW))


def _mix_and_memory(l, u_a, u_b, u_c, y_d, h, memn, P):
    nm = lambda s: f"l{l}_{s}"
    L, D = h.shape
    GW = D // N_MIXERS
    G = GW // S5_CH_PER_GROUP

    row = lambda g: g.reshape(1, D)

    v3 = lambda a: a.reshape(G, 1, S5_STATE)
    log_dt = jnp.broadcast_to(P['s5_log_dt'][:, None, None], (G, 1, S5_STATE))
    a_r, a_i, bb_r, bb_i = s5_discretise(v3(P['s5_lam_re']), v3(P['s5_lam_im']), log_dt,
                                         P['s5_b_re'].transpose(0, 2, 1), P['s5_b_im'].transpose(0, 2, 1), nm("s5_disc"))
    nblk = GW // S5_BLOCK_CH
    y_s5 = _s5_core(nm("s5_core"))(
        u_a, _block_diag(bb_r, nblk), _block_diag(bb_i, nblk), a_r.reshape(1, G * S5_STATE), a_i.reshape(1, G * S5_STATE),
        _block_diag(P['s5_c_re'].transpose(0, 2, 1), nblk), _block_diag(P['s5_c_im'].transpose(0, 2, 1), nblk))
    y_a = s5_epilogue(y_s5, u_a, P['s5_d'], P['s5_w_glu'], nm("s5_glu"))

    y_b = pool_proj(_pool_mix(nm("pool_mix"))(u_b), P['pool_w'], P['pool_scale'], nm("pool_proj"))

    hc = _glu_conv(nm("conv_dw"))(u_c, P['conv_w_dw'], P['conv_b_dw'].reshape(1, GW))
    y_c = conv_post(hc, P['conv_ln_g'], P['conv_ln_b'], P['conv_w_pw'], nm("conv_post"))

    grp = _group_norm_op(nm("grp_norm"), L, GW, N_MIXERS, BF16)
    h = _act_linear(nm("w_out"), grp, N_MIXERS + 1, True)(y_a, y_b, y_c, y_d, row(P['grp_norm_g']), P['w_out'], h)

    xq = _act_linear(nm("w_xq"), _rms_op(nm("norm_x"), L, D, BF16), 2, False)(h, row(P['norm_x_g']), P['w_xq'])
    xk = _linear(nm("w_xk"))(memn, P['w_xk'])
    xv = _linear(nm("w_xv"))(memn, P['w_xv'])
    xat = _cross_attention_op(nm("xattn"), L, xq.shape[1], memn.shape[0], BF16)
    return _act_linear(nm("w_xo"), xat, 3, True)(xq, xk, xv, P['w_xo'], h)


def _bias_tables(rel_bias):
    tabs = rel_bias_tables(rel_bias, _bucket_onehot(), "rel_bias")
    return tabs.reshape(len(DILATED_PATTERNS), ATT_HEADS, ATT_BLOCK, 2 * ATT_BLOCK)


def _gather_weight(name, w):
    ax = SHARDED[name]
    dt = BF16 if name in GATHER_BF16 else F32
    nl, a, b = w.shape
    rows = nl * a
    flat = jnp.pad(w.astype(dt).reshape(rows, b), ((0, (-rows) % AG_ROW_UNIT), (0, 0)))
    g = all_gather(flat, "ag_" + name)[:, :rows].reshape(N_DEV, nl, a, b)
    if ax == 1:
        return g.transpose(1, 0, 2, 3).reshape(nl, N_DEV * a, b)
    return g.transpose(1, 2, 0, 3).reshape(nl, a, N_DEV * b)


def _scatter_grad(name, g):
    ax = SHARDED[name]
    nl = g.shape[0]
    if ax == 1:
        a, b = g.shape[1] // N_DEV, g.shape[2]
        s = g.reshape(nl, N_DEV, a, b).transpose(1, 0, 2, 3)
    else:
        a, b = g.shape[1], g.shape[2] // N_DEV
        s = g.reshape(nl, a, N_DEV, b).transpose(2, 0, 1, 3)
    s = s.reshape(N_CHIP, N_DEV // N_CHIP, nl * a, b)
    pair = pair_sum(s, sibling_exchange([s], "d2d_" + name)[0], "pairsum_" + name)
    return chip_exchange([pair], "ici_" + name)[0]


def _by_destination(name, g):
    if g.ndim == 2 and SHARDED[name] == 1:
        g = g.reshape(N_DEV, g.shape[0] // N_DEV, g.shape[1])
    elif g.ndim == 2:
        g = g.reshape(g.shape[0], N_DEV, g.shape[1] // N_DEV).transpose(1, 0, 2)
    return g.reshape(N_CHIP, N_DEV // N_CHIP, *g.shape[1:])


def _flatten_small(d):
    flat = jnp.concatenate([d[n].reshape(-1).astype(F32) for n in SMALL])
    pad = (-flat.shape[0]) % (LANES * SMALL_ROW_TILE)
    return jnp.pad(flat, (0, pad)).reshape(-1, LANES)


def _split_small(flat, like):
    flat = flat.reshape(-1)
    out, off = {}, 0
    for n in SMALL:
        sz = math.prod(like[n].shape)
        out[n] = flat[off:off + sz].reshape(like[n].shape)
        off += sz
    return out


def kernel(x, mem, rel_bias, mem_norm_g, norm_mix_g, w_in, s5_lam_re, s5_lam_im, s5_log_dt, s5_b_re, s5_b_im, s5_c_re, s5_c_im, s5_d, s5_w_glu, pool_w, pool_scale, conv_w_dw, conv_b_dw, conv_ln_g, conv_ln_b, conv_w_pw, grp_norm_g, w_out, norm_x_g, w_xq, w_xk, w_xv, w_xo, norm_mlp_g, w_up, w_down, norm_final_g, loss_target, m_rel_bias, m_mem_norm_g, m_norm_mix_g, m_w_in, m_s5_lam_re, m_s5_lam_im, m_s5_log_dt, m_s5_b_re, m_s5_b_im, m_s5_c_re, m_s5_c_im, m_s5_d, m_s5_w_glu, m_pool_w, m_pool_scale, m_conv_w_dw, m_conv_b_dw, m_conv_ln_g, m_conv_ln_b, m_conv_w_pw, m_grp_norm_g, m_w_out, m_norm_x_g, m_w_xq, m_w_xk, m_w_xv, m_w_xo, m_norm_mlp_g, m_w_up, m_w_down, m_norm_final_g, v_rel_bias, v_mem_norm_g, v_norm_mix_g, v_w_in, v_s5_lam_re, v_s5_lam_im, v_s5_log_dt, v_s5_b_re, v_s5_b_im, v_s5_c_re, v_s5_c_im, v_s5_d, v_s5_w_glu, v_pool_w, v_pool_scale, v_conv_w_dw, v_conv_b_dw, v_conv_ln_g, v_conv_ln_b, v_conv_w_pw, v_grp_norm_g, v_w_out, v_norm_x_g, v_w_xq, v_w_xk, v_w_xv, v_w_xo, v_norm_mlp_g, v_w_up, v_w_down, v_norm_final_g):
    w = dict(zip(WEIGHTS, (rel_bias, mem_norm_g, norm_mix_g, w_in, s5_lam_re, s5_lam_im, s5_log_dt, s5_b_re, s5_b_im, s5_c_re, s5_c_im, s5_d, s5_w_glu, pool_w, pool_scale, conv_w_dw, conv_b_dw, conv_ln_g, conv_ln_b, conv_w_pw, grp_norm_g, w_out, norm_x_g, w_xq, w_xk, w_xv, w_xo, norm_mlp_g, w_up, w_down, norm_final_g)))
    m = dict(zip(WEIGHTS, (m_rel_bias, m_mem_norm_g, m_norm_mix_g, m_w_in, m_s5_lam_re, m_s5_lam_im, m_s5_log_dt, m_s5_b_re, m_s5_b_im, m_s5_c_re, m_s5_c_im, m_s5_d, m_s5_w_glu, m_pool_w, m_pool_scale, m_conv_w_dw, m_conv_b_dw, m_conv_ln_g, m_conv_ln_b, m_conv_w_pw, m_grp_norm_g, m_w_out, m_norm_x_g, m_w_xq, m_w_xk, m_w_xv, m_w_xo, m_norm_mlp_g, m_w_up, m_w_down, m_norm_final_g)))
    v = dict(zip(WEIGHTS, (v_rel_bias, v_mem_norm_g, v_norm_mix_g, v_w_in, v_s5_lam_re, v_s5_lam_im, v_s5_log_dt, v_s5_b_re, v_s5_b_im, v_s5_c_re, v_s5_c_im, v_s5_d, v_s5_w_glu, v_pool_w, v_pool_scale, v_conv_w_dw, v_conv_b_dw, v_conv_ln_g, v_conv_ln_b, v_conv_w_pw, v_grp_norm_g, v_w_out, v_norm_x_g, v_w_xq, v_w_xk, v_w_xv, v_w_xo, v_norm_mlp_g, v_w_up, v_w_down, v_norm_final_g)))

    full = {n: (_gather_weight(n, w[n]) if n in SHARDED else w[n]) for n in WEIGHTS if n != 'norm_final_g'}
    L, D = x.shape[1:]

    memn, mem_vjp = jax.vjp(lambda a, g: rmsnorm(a, g, "mem_norm"), mem[0], w['mem_norm_g'])
    tabs, tabs_vjp = jax.vjp(_bias_tables, w['rel_bias'])
    h = x[0]
    stages = []
    for l in range(DEPTH):
        layer = lambda names: {n: full[n][l] for n in names}
        (u_a, u_b, u_c, qkv), in_vjp = jax.vjp(functools.partial(_project_in, l), h, layer(IN_WEIGHTS))
        y_d, o_all, l_all = _att_fwd(qkv, tabs, f"l{l}_att_fwd")
        h, mix_vjp = jax.vjp(functools.partial(_mix_and_memory, l), u_a, u_b, u_c, y_d, h, memn, layer(MIX_WEIGHTS))
        norm = _rms_op(f"l{l}_norm_mlp", L, D, BF16)
        h, saved = _mlp_fwd(f"l{l}_mlp", norm, h, full['norm_mlp_g'][l].reshape(1, D), full['w_up'][l], full['w_down'][l])
        stages.append((in_vjp, (qkv, tabs, o_all, l_all), mix_vjp, norm, saved))
    loss_local, dh, d_final_g = loss_head(h, loss_target[0], w['norm_final_g'], "loss_head")
    loss = lax.psum(loss_local, MESH_AXES)

    def pair_sums(l, names, grads_l):
        by_dest = [_by_destination(n, grads_l[n]) for n in names]
        theirs = sibling_exchange(by_dest, f"d2d_l{l}_{names[0]}")
        return [pair_sum(s, t, f"pairsum_l{l}_{n}") for n, s, t in zip(names, by_dest, theirs)]

    empties = lambda like: [lax.empty(p.shape, p.dtype) for p in like]
    layer_grads, arrived = [None] * DEPTH, [None] * DEPTH
    pending, dmemn, dtabs = None, 0.0, 0.0
    for l in reversed(range(DEPTH)):
        in_vjp, att_saved, mix_vjp, norm, saved = stages[l]
        lands = None if pending is None else empties(pending)
        dh, dg_mlp, dw_up, dw_down, lands = _mlp_bwd(f"l{l}_mlp", norm, saved, dh, pending, lands)
        mlp_grads = dict(norm_mlp_g=dg_mlp.reshape(D), w_up=dw_up, w_down=dw_down)
        du_a, du_b, du_c, dy_d, dh_res, dmemn_l, d_mix = mix_vjp(dh)
        rides = [] if pending is None else [(pending, lands, ATT_RIDE_WINDOW)]
        if l == 0:
            early = pair_sums(l, MLP_SHARDED, mlp_grads)
            rides.append((early, empties(early), (0, 1, 1)))
        dq, dk, dv, dtabs_l, *landed = _att_bwd(*att_saved, dy_d, f"l{l}_att_bwd", riders=rides)
        if pending is not None:
            arrived[l + 1] = landed[:len(pending)]
        dh_in, d_in = in_vjp((du_a, du_b, du_c, jnp.concatenate([dq, dk, dv], axis=-1)))
        dh = dh_in + dh_res
        dmemn, dtabs = dmemn + dmemn_l, dtabs + dtabs_l
        layer_grads[l] = {**d_in, **d_mix, **mlp_grads}
        if l > 0:
            pending = pair_sums(l, GATHER_BF16, layer_grads[l])
        else:
            late = [n for n in GATHER_BF16 if n not in MLP_SHARDED]
            got = dict(zip(late, chip_exchange(pair_sums(l, late, layer_grads[l]), "ici_l0")))
            got.update(zip(MLP_SHARDED, landed[-len(MLP_SHARDED):]))
            arrived[0] = [got[n] for n in GATHER_BF16]
    dx = dh
    dfull = {n: jnp.concatenate([layer_grads[l][n][None] for l in range(DEPTH)])
             for n in LAYER_WEIGHTS if n not in GATHER_BF16}
    dfull['mem_norm_g'] = mem_vjp(dmemn)[1]
    dfull['rel_bias'] = tabs_vjp(dtabs)[0]
    dfull['norm_final_g'] = d_final_g

    grads, deltas, new_m, new_v = {}, {}, {}, {}
    for k, n in enumerate(GATHER_BF16):
        res = adamw_layers(w[n], [arrived[l][k] for l in range(DEPTH)], m[n], v[n], "adamw_" + n)
        grads[n], deltas[n], new_m[n], new_v[n] = res
    for n in SHARDED:
        if n in GATHER_BF16:
            continue
        parts = _scatter_grad(n, dfull[n])
        shp = w[n].shape
        two_d = lambda a: a.reshape(shp[0] * shp[1], shp[2])
        res = adamw(two_d(w[n]), parts, two_d(m[n]), two_d(v[n]), "adamw_" + n)
        grads[n], deltas[n], new_m[n], new_v[n] = (r.reshape(shp) for r in res)

    parts = all_gather(_flatten_small(dfull), "ag_small_grads")
    res = adamw(_flatten_small(w), parts, _flatten_small(m), _flatten_small(v), "adamw_small")
    for dst, r in zip((grads, deltas, new_m, new_v), res):
        dst.update(_split_small(r, w))

    return (loss, dx[None], *[grads[n] for n in WEIGHTS], *[deltas[n] for n in WEIGHTS],
            *[new_m[n] for n in WEIGHTS], *[new_v[n] for n in WEIGHTS])
```

```python
import functools
import math

import numpy as np
import jax
import jax.numpy as jnp
from jax import lax
from jax.experimental import pallas as pl
from jax.experimental.pallas import tpu as pltpu

F32 = jnp.float32
BF16 = jnp.bfloat16

DEPTH = 4
N_MIXERS = 4
S5_CH_PER_GROUP = 16
S5_STATE = 64
POOL_WINDOWS = (2, 4, 8, 16)
CONV_WIDTH = 31
ATT_HEADS = 8
DILATED_PATTERNS = ((128, 1), (512, 4), (2048, 16))
ATT_BLOCK = 128
ATT_MIX_ROWS = 256
ATT_UNROLL = 4
REL_BUCKETS = 32
REL_MAX_DIST = 2048
X_HEADS = 4
X_HEAD_DIM = 128
NORM_EPS = 1e-6
NEG_INF = -1e30
ADAM_LR = 0.001
ADAM_B1 = 0.9
ADAM_B2 = 0.999
ADAM_EPS = 1e-08
ADAM_WD = 0.01
ADAM_STEP = 10

LANES = 128
SUBLANES = 8
VMEM_BYTES = 64 * 1024 * 1024
VMEM_LIMIT = (VMEM_BYTES * 3) // 4
VMEM_LIMIT_BIG = (VMEM_BYTES * 7) // 8
STREAM_BLOCK_BYTES = 1024 * 1024
SMALL_ROW_TILE = 512
N_DEV = 8
N_CHIP = 4
MESH_AXES = ("x", "y", "c")

WEIGHTS = ['rel_bias', 'mem_norm_g', 'norm_mix_g', 'w_in', 's5_lam_re', 's5_lam_im', 's5_log_dt', 's5_b_re',
           's5_b_im', 's5_c_re', 's5_c_im', 's5_d', 's5_w_glu', 'pool_w', 'pool_scale', 'conv_w_dw', 'conv_b_dw',
           'conv_ln_g', 'conv_ln_b', 'conv_w_pw', 'grp_norm_g', 'w_out', 'norm_x_g', 'w_xq', 'w_xk', 'w_xv', 'w_xo',
           'norm_mlp_g', 'w_up', 'w_down', 'norm_final_g']
SHARDED = {'w_in': 2, 's5_w_glu': 1, 'conv_w_dw': 2, 'conv_w_pw': 1, 'w_out': 1, 'w_xq': 1, 'w_xk': 1, 'w_xv': 1,
           'w_xo': 2, 'w_up': 2, 'w_down': 1}
GATHER_BF16 = ('w_in', 'w_out', 'w_xq', 'w_xk', 'w_xv', 'w_xo', 'w_up', 'w_down')
SMALL = [n for n in WEIGHTS if n not in SHARDED]
LAYER_WEIGHTS = [n for n in WEIGHTS if n not in ('rel_bias', 'mem_norm_g', 'norm_final_g')]
IN_WEIGHTS = ('norm_mix_g', 'w_in')
MLP_WEIGHTS = ('norm_mlp_g', 'w_up', 'w_down')
MIX_WEIGHTS = [n for n in LAYER_WEIGHTS if n not in IN_WEIGHTS + MLP_WEIGHTS]
MLP_SHARDED = ('w_up', 'w_down')

def _cp(n_axes, vmem=VMEM_LIMIT):
    return pltpu.CompilerParams(dimension_semantics=("arbitrary",) * n_axes, vmem_limit_bytes=vmem)


def _tile(n, prefs):
    for t in prefs:
        if n % t == 0:
            return t
    return n


MM_TILE = 1024
MM_TILE_K = 2048
MM_FULL_K = 4096


def _chip_exchange_copies(srcs, dsts, window, send_sems, recv_sems, local_sems, base=0):
    x_, y_, c_ = _place()
    me = _chip(x_, y_)
    first, count, total = window
    copies = []
    for p, (src, dst) in enumerate(zip(srcs, dsts)):
        unit = src.shape[1] // total
        rows = pl.ds(first * unit, count * unit)
        copies.append(pltpu.make_async_copy(src.at[me, rows], dst.at[me, rows], local_sems.at[base + p]))
        for k in range(1, N_CHIP):
            px = 1 - x_ if k & 2 else x_
            py = 1 - y_ if k & 1 else y_
            s = (base + p) * (N_CHIP - 1) + k - 1
            copies.append(pltpu.make_async_remote_copy(
                src_ref=src.at[_chip(px, py), rows], dst_ref=dst.at[me, rows], send_sem=send_sems.at[s],
                recv_sem=recv_sems.at[s], device_id=(px, py, c_), device_id_type=MESH_ID))
    return copies


class _Riders:
    def __init__(self, rides):
        self.rides = rides or []
        self.srcs = [s for r in self.rides for s in r[0]]
        self.dsts = [d for r in self.rides for d in r[1]]
        self.n = len(self.srcs)

    def operands(self):
        return (*self.srcs, *self.dsts)

    def in_specs(self):
        return [HBM_SPEC] * (2 * self.n)

    def out_specs(self):
        return [HBM_SPEC] * self.n

    def out_shape(self):
        return [jax.ShapeDtypeStruct(d.shape, d.dtype) for d in self.dsts]

    def scratch(self):
        if not self.n:
            return []
        n_sem = self.n * (N_CHIP - 1)
        return [pltpu.SemaphoreType.DMA((n_sem,)), pltpu.SemaphoreType.DMA((n_sem,)), pltpu.SemaphoreType.DMA((self.n,))]

    def aliases(self, first_in, first_out):
        return {first_in + self.n + p: first_out + p for p in range(self.n)}

    def hooks(self, grid, src_refs, dst_refs, sems):
        if not self.n:
            return (lambda: None), (lambda: None)

        def copies():
            out, base = [], 0
            for srcs, _, window in self.rides:
                k = len(srcs)
                out += _chip_exchange_copies(src_refs[base:base + k], dst_refs[base:base + k], window, *sems, base=base)
                base += k
            return out

        ids = [pl.program_id(ax) for ax in range(len(grid))]
        at_start = functools.reduce(jnp.logical_and, [i == 0 for i in ids])
        at_end = functools.reduce(jnp.logical_and, [i == g - 1 for i, g in zip(ids, grid)])

        def start():
            @pl.when(at_start)
            def _():
                for cp in copies():
                    cp.start()

        def wait():
            @pl.when(at_end)
            def _():
                for cp in copies():
                    cp.wait()

        return start, wait


def _mm(a, b, *, ta=False, tb=False, res=None, out_dtype=F32, epilogue=None, pre=None, riders=None, dest_cols=None,
        b_slots=None, name):
    if ta:
        K, M = a.shape
    else:
        M, K = a.shape
    if b_slots is not None:
        layer, n_layers, axis = b_slots
        a_blk, b_blk = b.shape[1] // n_layers, b.shape[2]
        w_shape = (N_DEV * a_blk, b_blk) if axis == 1 else (a_blk, N_DEV * b_blk)
    else:
        w_shape = b.shape
    N, K2 = w_shape if tb else w_shape[::-1]
    assert K == K2, (a.shape, b.shape, ta, tb)
    wide_f32 = K >= MM_TILE_K and F32 in (a.dtype, b.dtype)
    tm = _tile(M, (MM_TILE // 2 if wide_f32 and not ta else MM_TILE, 512, 256, 128))
    tn = _tile(N, (MM_TILE, 512, 256, 128)) if dest_cols is None else dest_cols
    tk = K if K <= MM_FULL_K else _tile(K, (MM_TILE_K, 1024, 512, 256, 128))
    n_b = 1
    if b_slots is not None:
        assert not ta
        sharded_is_k = (axis == 1) != tb
        slot = a_blk if axis == 1 else b_blk
        if sharded_is_k:
            n_b = _tile(N_DEV, [n for n in (8, 4, 2) if n * slot <= MM_TILE_K])
            tk = n_b * slot
        else:
            tn = slot
    nk = K // tk
    a_spec = pl.BlockSpec((tk, tm), lambda i, j, k: (k, i)) if ta else pl.BlockSpec((tm, tk), lambda i, j, k: (i, k))
    if b_slots is None:
        b_specs = [pl.BlockSpec((tn, tk), lambda i, j, k: (j, k)) if tb
                   else pl.BlockSpec((tk, tn), lambda i, j, k: (k, j))]
    elif axis == 1 and not tb:
        b_specs = [pl.BlockSpec((None, slot, tn), lambda i, j, k, q=q: (n_b * k + q, layer, j)) for q in range(n_b)]
    elif axis == 1:
        b_specs = [pl.BlockSpec((None, tn, tk), lambda i, j, k: (j, layer, k))]
    elif not tb:
        b_specs = [pl.BlockSpec((None, tk, tn), lambda i, j, k: (j, layer * (a_blk // tk) + k, 0))]
    else:
        b_specs = [pl.BlockSpec((None, tn, slot), lambda i, j, k, q=q: (n_b * k + q, layer * (a_blk // tn) + j, 0))
                   for q in range(n_b)]
    o_spec = pl.BlockSpec((tm, tn), lambda i, j, k: (i, j))
    dn = (((0 if ta else 1,), (1 if tb else 0,)), ((), ()))
    extra = [x for x in (res, pre) if x is not None]
    assert not (res is not None and pre is not None)
    assert dest_cols is None or (epilogue is None and not extra and tn <= MM_TILE)
    n_out = 2 if epilogue == 'relu_sq' else 1
    ride = _Riders(riders)
    n_pairs = ride.n
    grid = (M // tm, N // tn, nk)

    def body(*refs):
        a_ref, b_refs = refs[0], refs[1:1 + n_b]
        x_ref = refs[1 + n_b] if extra else None
        n_in = 1 + n_b + len(extra) + 2 * n_pairs
        o_refs = refs[n_in:n_in + n_out]
        scratch = refs[n_in + n_out + n_pairs:]
        acc = scratch[0] if nk > 1 else None
        start, wait = ride.hooks(grid, refs[n_in - 2 * n_pairs:n_in - n_pairs],
                                 refs[n_in + n_out:n_in + n_out + n_pairs], scratch[-3:])
        start()

        def finish(r):
            if res is not None:
                r = r + x_ref[...]
            if epilogue == 'relu_sq':
                o_refs[0][...] = r
                o_refs[1][...] = jnp.square(jnp.maximum(r, 0.0)).astype(out_dtype)
            elif epilogue == 'relu_sq_grad':
                o_refs[0][...] = (r * (2.0 * jnp.maximum(x_ref[...], 0.0))).astype(out_dtype)
            else:
                o_refs[0][...] = r.astype(out_dtype)

        dot = lambda x, y: lax.dot_general(x.astype(BF16), y[...].astype(BF16), dn, preferred_element_type=F32)
        if n_b == 1:
            part = dot(a_ref[...], b_refs[0])
        else:
            part = sum(dot(a_ref[:, q * slot:(q + 1) * slot], b_refs[q]) for q in range(n_b))
        if nk == 1:
            finish(part)
        else:
            k = pl.program_id(2)

            @pl.when(k == 0)
            def _():
                acc[...] = part

            @pl.when(k > 0)
            def _():
                acc[...] += part

            @pl.when(k == nk - 1)
            def _():
                finish(acc[...])

        wait()

    out_shape = [jax.ShapeDtypeStruct((M, N), F32 if epilogue == 'relu_sq' else out_dtype)]
    if n_out == 2:
        out_shape.append(jax.ShapeDtypeStruct((M, N), out_dtype))
    in_specs = [a_spec, *b_specs] + [o_spec] * len(extra)
    out_specs = [o_spec] * n_out
    if dest_cols is not None:
        out_shape = [jax.ShapeDtypeStruct((N // tn, M, tn), out_dtype)]
        out_specs = [pl.BlockSpec((None, tm, tn), lambda i, j, k: (j, i, 0))]
    scratch = ([pltpu.VMEM((tm, tn), F32)] if nk > 1 else []) + ride.scratch()
    args = (a, *[b] * n_b, *extra)
    outs = pl.pallas_call(
        body, grid=grid, in_specs=in_specs + ride.in_specs(), out_specs=out_specs + ride.out_specs(),
        out_shape=out_shape + ride.out_shape(), scratch_shapes=scratch,
        input_output_aliases=ride.aliases(len(args), n_out), name=name, compiler_params=_cp(3, VMEM_LIMIT_BIG))(
            *args, *ride.operands())
    return outs[0] if len(outs) == 1 else tuple(outs)


def _linear(name):
    @jax.custom_vjp
    def lin(a, w):
        return _mm(a, w, name=name + "_fwd")

    def fwd(a, w):
        return _mm(a, w, name=name + "_fwd"), (a, w)

    def bwd(r, dy):
        a, w = r
        da = _mm(dy, w, tb=True, name=name + "_dx")
        dw = _mm(a, dy, ta=True, out_dtype=w.dtype, name=name + "_dw")
        return da, dw

    lin.defvjp(fwd, bwd)
    return lin


def _act_linear(name, act, n_in, with_res):
    def run(*a):
        ins, w = a[:n_in], a[n_in]
        x = act.fwd_call(*ins)[0]
        return _mm(x, w, res=a[n_in + 1] if with_res else None, name=name + "_fwd"), (ins, x, w)

    @jax.custom_vjp
    def op(*a):
        return run(*a)[0]

    def bwd(r, dy):
        ins, x, w = r
        dx = _mm(dy, w, tb=True, name=name + "_dx")
        dw = _mm(x, dy, ta=True, out_dtype=w.dtype, name=name + "_dw")
        return (*act.bwd_all(ins, (dx,)), dw) + ((dy,) if with_res else ())

    op.defvjp(run, bwd)
    return op


def _mlp_fwd(name, norm, h, g, w_up, w_down, layer):
    hn = norm.fwd_call(h, g)[0]
    up, down = (layer, DEPTH, SHARDED['w_up']), (layer, DEPTH, SHARDED['w_down'])
    a, r = _mm(hn, w_up, epilogue='relu_sq', out_dtype=BF16, b_slots=up, name=name + "_up_fwd")
    return _mm(r, w_down, res=h, b_slots=down, name=name + "_down_fwd"), (h, g, hn, a, r, w_up, w_down, up, down)


RIDE_UNITS = 16
MLP_RIDE_UNITS = (3, 3, 2, 2)
ATT_RIDE_WINDOW = (sum(MLP_RIDE_UNITS), RIDE_UNITS - sum(MLP_RIDE_UNITS), RIDE_UNITS)


def _mlp_bwd(name, norm, saved, dy, pending, lands):
    h, g, hn, a, r, w_up, w_down, up, down = saved

    def mm(i, *args, **kw):
        nonlocal lands
        if pending is None:
            return _mm(*args, **kw)
        window = (sum(MLP_RIDE_UNITS[:i]), MLP_RIDE_UNITS[i], RIDE_UNITS)
        out, *lands = _mm(*args, riders=[(pending, lands, window)], **kw)
        return out

    da = mm(0, dy, w_down, tb=True, epilogue='relu_sq_grad', pre=a, out_dtype=BF16, b_slots=down,
            name=name + "_down_dx")
    dw_down = mm(1, r, dy, ta=True, out_dtype=w_down.dtype, name=name + "_down_dw")
    dhn = mm(2, da, w_up, tb=True, b_slots=up, name=name + "_up_dx")
    dw_up = mm(3, hn, da, ta=True, out_dtype=w_up.dtype, dest_cols=w_up.shape[2], name=name + "_up_dw")
    dh, dg = norm.bwd_all((h, g), (dhn,))
    return dh + dy, dg, dw_up, dw_down, lands


def _block_op(name, f, grid, ins, outs, vmem=VMEM_LIMIT):
    n_in, n_out = len(ins), len(outs)
    in_specs = [pl.BlockSpec(bs, im) for bs, im, _, _ in ins]
    out_specs = [pl.BlockSpec(bs, im) for _, _, bs, im in outs]
    out_shape = [jax.ShapeDtypeStruct(s, d) for s, d, _, _ in outs]
    didx = [i for i in range(n_in) if ins[i][3]]

    def fwd_call(*args):
        def body(*refs):
            res = f(*[r[...] for r in refs[:n_in]])
            for r, o in zip(refs[n_in:], res):
                r[...] = o.astype(r.dtype)

        return pl.pallas_call(body, grid=grid, in_specs=in_specs, out_specs=out_specs, out_shape=out_shape,
                              name=name + "_fwd", compiler_params=_cp(len(grid), vmem))(*args)

    def bwd_call(args, cts):
        def body(*refs):
            vals = [r[...] for r in refs[:n_in]]
            ct_refs = refs[n_in:n_in + n_out]
            g_refs = refs[n_in + n_out:]

            def fd(*dv):
                full = list(vals)
                for i, v in zip(didx, dv):
                    full[i] = v
                return f(*full)

            _, vjp = jax.vjp(fd, *[vals[i] for i in didx])
            grads = vjp(tuple(r[...] for r in ct_refs))
            for gref, i, g in zip(g_refs, didx, grads):
                acc = ins[i][2]
                if acc:
                    first = functools.reduce(jnp.logical_and, [pl.program_id(ax) == 0 for ax in acc])

                    @pl.when(first)
                    def _(gref=gref):
                        gref[...] = jnp.zeros_like(gref)

                    gref[...] += g.astype(gref.dtype)
                else:
                    gref[...] = g.astype(gref.dtype)

        g_specs = [pl.BlockSpec(ins[i][0], ins[i][1]) for i in didx]
        g_shape = [jax.ShapeDtypeStruct(args[i].shape, args[i].dtype) for i in didx]
        return pl.pallas_call(body, grid=grid, in_specs=in_specs + out_specs, out_specs=g_specs, out_shape=g_shape,
                              name=name + "_bwd", compiler_params=_cp(len(grid), vmem))(*args, *cts)

    @jax.custom_vjp
    def op(*args):
        return tuple(fwd_call(*args))

    def op_fwd(*args):
        return tuple(fwd_call(*args)), args

    def op_bwd(args, cts):
        it = iter(bwd_call(args, cts))
        return tuple(next(it) if ins[i][3] else jnp.zeros_like(args[i]) for i in range(n_in))

    op.defvjp(op_fwd, op_bwd)
    op.fwd_call = fwd_call
    op.bwd_all = op_bwd
    return op


def _row(tr, c):
    return ((tr, c), lambda i: (i, 0), None, True)


def _par(shape):
    nd = len(shape)
    return (shape, lambda i: (0,) * nd, (0,), True)


def _bdot(a, w):
    return jnp.dot(a.astype(BF16), w.astype(BF16), preferred_element_type=F32)


def _rms_f(x, g):
    return (x * lax.rsqrt(jnp.mean(x * x, axis=-1, keepdims=True) + NORM_EPS) * g,)


def _rms_op(name, R, D, out_dtype):
    tr = _tile(R, (256,))
    return _block_op(name, _rms_f, (R // tr,), [_row(tr, D), _par((1, D))],
                     [((R, D), out_dtype, (tr, D), lambda i: (i, 0))])


def rmsnorm(x, g, name):
    R, D = x.shape
    return _rms_op(name, R, D, F32)(x, g.reshape(1, D))[0]


def s5_epilogue(yc, u, d, w_glu, name):
    R, C = yc.shape
    tr = _tile(R, (256,))

    def f(yc, u, d, w):
        g = jax.nn.gelu(yc + d * u)
        return (g * jax.nn.sigmoid(_bdot(g, w)),)

    op = _block_op(name, f, (R // tr,), [_row(tr, C), _row(tr, C), _par((1, C)), _par((C, C))],
                   [((R, C), F32, (tr, C), lambda i: (i, 0))])
    return op(yc, u, d.reshape(1, C), w_glu)[0]


def pool_proj(p, w, scale, name):
    R, C = p.shape
    ng, pc, _ = w.shape
    tr = _tile(R, (256,))

    def f(p, w, s):
        ys = [_bdot(p[:, g * pc:(g + 1) * pc], w[g]) for g in range(ng)]
        return (jnp.concatenate(ys, axis=-1) * s,)

    op = _block_op(name, f, (R // tr,), [_row(tr, C), _par((ng, pc, pc)), _par((1, C))],
                   [((R, C), F32, (tr, C), lambda i: (i, 0))])
    return op(p, w, scale.reshape(1, C))[0]


def conv_post(h, ln_g, ln_b, w_pw, name):
    R, C = h.shape
    tr = _tile(R, (256,))

    def f(h, g, b, w):
        hc = h - jnp.mean(h, axis=-1, keepdims=True)
        y = hc * lax.rsqrt(jnp.mean(hc * hc, axis=-1, keepdims=True) + NORM_EPS) * g + b
        return (_bdot(jax.nn.silu(y), w),)

    op = _block_op(name, f, (R // tr,), [_row(tr, C), _par((1, C)), _par((1, C)), _par((C, C))],
                   [((R, C), F32, (tr, C), lambda i: (i, 0))])
    return op(h, ln_g.reshape(1, C), ln_b.reshape(1, C), w_pw)[0]


def _group_norm_op(name, R, C, n, out_dtype):
    tr = _tile(R, (256,))

    def f(*a):
        g = a[n]
        parts = [y * lax.rsqrt(jnp.mean(y * y, axis=-1, keepdims=True) + NORM_EPS) for y in a[:n]]
        return (jnp.concatenate(parts, axis=-1) * g,)

    return _block_op(name, f, (R // tr,), [_row(tr, C)] * n + [_par((1, n * C))],
                     [((R, n * C), out_dtype, (tr, n * C), lambda i: (i, 0))])


def _cross_attention_op(name, L, W, M, out_dtype):
    E = X_HEAD_DIM
    tq = _tile(L, (512,))

    def f(q, k, v):
        s = lax.dot_general(q.astype(BF16), k.astype(BF16), (((1,), (1,)), ((), ())),
                            preferred_element_type=F32) * (E ** -0.5)
        p = jax.nn.softmax(s, axis=-1)
        return (_bdot(p, v),)

    qspec = ((tq, E), lambda h, i: (i, h), None, True)
    kspec = ((M, E), lambda h, i: (0, h), (1,), True)
    return _block_op(name, f, (W // E, L // tq), [qspec, kspec, kspec],
                     [((L, W), out_dtype, (tq, E), lambda h, i: (i, h))])


def cross_attention(q, k, v, name):
    return _cross_attention_op(name, q.shape[0], q.shape[1], k.shape[0], F32)(q, k, v)[0]


def s5_discretise(lam_re, lam_im, log_dt, b_re_t, b_im_t, name):
    G, _, N = lam_re.shape
    C = b_re_t.shape[1]

    def f(lr, li, ldt, br, bi):
        dt = jnp.exp(ldt)
        mag = jnp.exp(lr * dt)
        ab_r, ab_i = mag * jnp.cos(li * dt), mag * jnp.sin(li * dt)
        den = lr * lr + li * li
        nr, ni = ab_r - 1.0, ab_i
        f_r = (nr * lr + ni * li) / den
        f_i = (ni * lr - nr * li) / den
        return ab_r, ab_i, f_r * br - f_i * bi, f_r * bi + f_i * br

    vec = ((G, 1, N), lambda i: (0, 0, 0), None, True)
    mat = ((G, C, N), lambda i: (0, 0, 0), None, True)
    ov = ((G, 1, N), F32, (G, 1, N), lambda i: (0, 0, 0))
    om = ((G, C, N), F32, (G, C, N), lambda i: (0, 0, 0))
    op = _block_op(name, f, (1,), [vec, vec, vec, mat, mat], [ov, ov, om, om])
    return op(lam_re, lam_im, log_dt, b_re_t, b_im_t)


def rel_bias_tables(rel_bias, onehot, name):
    B, H = rel_bias.shape
    P, _, Q = onehot.shape

    def f(rbt, oh):
        return (jnp.dot(rbt, oh, precision=lax.Precision.HIGHEST, preferred_element_type=F32),)

    op = _block_op(name, f, (P,), [((H, B), lambda p: (0, 0), (0,), True), ((None, B, Q), lambda p: (p, 0, 0), None, False)],
                   [((P, H, Q), F32, (None, H, Q), lambda p: (p, 0, 0))])
    return op(rel_bias.T, onehot)[0]


def _shift_down(x, s, row):
    return jnp.where(row >= s, pltpu.roll(x, s, 0), 0.0)


def _shift_up(x, s, row):
    n = x.shape[0]
    return jnp.where(row < n - s, pltpu.roll(x, n - s, 0), 0.0)


def _window_sum(x, w, row, shift):
    span = 1
    while span < w:
        x = x + shift(x, span, row)
        span *= 2
    return x


def _pool_call(u, d_out, name):
    L, C = u.shape
    pc = C // len(POOL_WINDOWS)
    assert pc % LANES == 0

    def body(x_ref, o_ref):
        row = lax.broadcasted_iota(jnp.int32, (L, pc), 0)
        for g, w in enumerate(POOL_WINDOWS):
            sl = slice(g * pc, (g + 1) * pc)
            x = x_ref[:, sl]
            cnt = jnp.minimum(row + 1, w).astype(F32)
            if d_out is None:
                o_ref[:, sl] = _window_sum(x, w, row, _shift_down) / cnt - x
            else:
                o_ref[:, sl] = _window_sum(x / cnt, w, row, _shift_up) - x

    src = u if d_out is None else d_out
    return pl.pallas_call(body, out_shape=jax.ShapeDtypeStruct((L, C), F32), name=name,
                          compiler_params=pltpu.CompilerParams(vmem_limit_bytes=VMEM_LIMIT))(src)


def _pool_mix(name):
    @jax.custom_vjp
    def op(u):
        return _pool_call(u, None, name + "_fwd")

    def fwd(u):
        return _pool_call(u, None, name + "_fwd"), u

    def bwd(u, dp):
        return (_pool_call(u, dp, name + "_bwd"),)

    op.defvjp(fwd, bwd)
    return op


def _conv_fwd(u, w, b, name):
    L, C2 = u.shape
    C = C2 // 2
    K = w.shape[0]
    nb = C // LANES

    def body(val_ref, gate_ref, w_ref, b_ref, o_ref):
        row = lax.broadcasted_iota(jnp.int32, (L, LANES), 0)
        h = val_ref[...] * jax.nn.sigmoid(gate_ref[...])
        acc = jnp.broadcast_to(b_ref[...], (L, LANES))
        for k in range(K):
            acc = acc + w_ref[k:k + 1, :] * _shift_down(h, K - 1 - k, row)
        o_ref[...] = acc

    blk = lambda off: pl.BlockSpec((L, LANES), lambda j: (0, j + off))
    return pl.pallas_call(
        body, grid=(nb,), in_specs=[blk(0), blk(nb), pl.BlockSpec((K, LANES), lambda j: (0, j)),
                                    pl.BlockSpec((1, LANES), lambda j: (0, j))],
        out_specs=blk(0), out_shape=jax.ShapeDtypeStruct((L, C), F32), name=name, compiler_params=_cp(1))(u, u, w, b)


def _conv_bwd(u, w, dh, name):
    L, C2 = u.shape
    C = C2 // 2
    K = w.shape[0]
    nb = C // LANES

    def body(val_ref, gate_ref, w_ref, dh_ref, dval_ref, dgate_ref, dw_ref, db_ref):
        row = lax.broadcasted_iota(jnp.int32, (L, LANES), 0)
        val = val_ref[...]
        sig = jax.nn.sigmoid(gate_ref[...])
        h = val * sig
        d = dh_ref[...]
        dh0 = jnp.zeros((L, LANES), F32)
        for k in range(K):
            s = K - 1 - k
            dh0 = dh0 + w_ref[k:k + 1, :] * _shift_up(d, s, row)
            dw_ref[k:k + 1, :] = jnp.sum(d * _shift_down(h, s, row), axis=0, keepdims=True)
        db_ref[...] = jnp.sum(d, axis=0, keepdims=True)
        dval_ref[...] = dh0 * sig
        dgate_ref[...] = dh0 * val * sig * (1.0 - sig)

    blk = lambda off: pl.BlockSpec((L, LANES), lambda j: (0, j + off))
    return pl.pallas_call(
        body, grid=(nb,), in_specs=[blk(0), blk(nb), pl.BlockSpec((K, LANES), lambda j: (0, j)), blk(0)],
        out_specs=[blk(0), blk(0), pl.BlockSpec((K, LANES), lambda j: (0, j)), pl.BlockSpec((1, LANES), lambda j: (0, j))],
        out_shape=[jax.ShapeDtypeStruct((L, C), F32), jax.ShapeDtypeStruct((L, C), F32),
                   jax.ShapeDtypeStruct((K, C), F32), jax.ShapeDtypeStruct((1, C), F32)],
        name=name, compiler_params=_cp(1))(u, u, w, dh)


def _glu_conv(name):
    @jax.custom_vjp
    def op(u, w, b):
        return _conv_fwd(u, w, b, name + "_fwd")

    def fwd(u, w, b):
        return _conv_fwd(u, w, b, name + "_fwd"), (u, w)

    def bwd(r, dh):
        u, w = r
        dval, dgate, dw, db = _conv_bwd(u, w, dh, name + "_bwd")
        return jnp.concatenate([dval, dgate], axis=-1), dw, db

    op.defvjp(fwd, bwd)
    return op


S5_BLOCK_CH = LANES
S5_BLOCK_ST = S5_BLOCK_CH // S5_CH_PER_GROUP * S5_STATE


def _s5_scan(br_ref, bi_ref, ar, ai, reverse):
    L, C = br_ref.shape
    T = SUBLANES
    row = lax.broadcasted_iota(jnp.int32, (T, C), 0)
    pw = [(ar, ai)]
    for _ in range(T - 1):
        pr, pi = pw[-1]
        pw.append((pr * ar - pi * ai, pr * ai + pi * ar))
    cr = jnp.zeros((T, C), F32)
    ci = jnp.zeros((T, C), F32)
    for r in range(T):
        e = (T - r) if reverse else (r + 1)
        cr = jnp.where(row == r, pw[e - 1][0], cr)
        ci = jnp.where(row == r, pw[e - 1][1], ci)
    steps = []
    s = 1
    while s < T:
        mask = (row < T - s) if reverse else (row >= s)
        steps.append((T - s if reverse else s, mask, pw[s - 1][0], pw[s - 1][1]))
        s *= 2
    nt = L // T
    last = 0 if reverse else T - 1

    def body(i, carry):
        kr, ki = carry
        t = (nt - 1 - i) if reverse else i
        off = pl.multiple_of(t * T, T)
        xr = br_ref[pl.ds(off, T), :]
        xi = bi_ref[pl.ds(off, T), :]
        for sh, mask, mr, mi in steps:
            sr = jnp.where(mask, pltpu.roll(xr, sh, 0), 0.0)
            si = jnp.where(mask, pltpu.roll(xi, sh, 0), 0.0)
            xr, xi = xr + mr * sr - mi * si, xi + mr * si + mi * sr
        xr, xi = xr + cr * kr - ci * ki, xi + cr * ki + ci * kr
        br_ref[pl.ds(off, T), :] = xr
        bi_ref[pl.ds(off, T), :] = xi
        return (jnp.broadcast_to(xr[last:last + 1, :], (T, C)), jnp.broadcast_to(xi[last:last + 1, :], (T, C)))

    z = jnp.zeros((T, C), F32)
    lax.fori_loop(0, nt, body, (z, z))


def _s5_specs(L):
    nb_axis = lambda j: (j, 0, 0)
    u = pl.BlockSpec((L, S5_BLOCK_CH), lambda j: (0, j))
    wb = pl.BlockSpec((None, S5_BLOCK_CH, S5_BLOCK_ST), nb_axis)
    a = pl.BlockSpec((1, S5_BLOCK_ST), lambda j: (0, j))
    wc = pl.BlockSpec((None, S5_BLOCK_ST, S5_BLOCK_CH), nb_axis)
    return u, wb, a, wc


def _s5_fwd(u, wbr, wbi, ar, ai, wcr, wci, name):
    L, C = u.shape
    nb = C // S5_BLOCK_CH
    us, wbs, as_, wcs = _s5_specs(L)

    def body(u_ref, wbr_ref, wbi_ref, ar_ref, ai_ref, wcr_ref, wci_ref, y_ref, xr, xi):
        ub = u_ref[...]
        xr[...] = _bdot(ub, wbr_ref[...])
        xi[...] = _bdot(ub, wbi_ref[...])
        _s5_scan(xr, xi, ar_ref[...], ai_ref[...], False)
        y_ref[...] = _bdot(xr[...], wcr_ref[...]) - _bdot(xi[...], wci_ref[...])

    return pl.pallas_call(
        body, grid=(nb,), in_specs=[us, wbs, wbs, as_, as_, wcs, wcs], out_specs=us,
        out_shape=jax.ShapeDtypeStruct((L, C), F32),
        scratch_shapes=[pltpu.VMEM((L, S5_BLOCK_ST), F32)] * 2, name=name, compiler_params=_cp(1))(
            u, wbr, wbi, ar, ai, wcr, wci)


def _dot_t(a, b):
    return lax.dot_general(a.astype(BF16), b.astype(BF16), (((0,), (0,)), ((), ())), preferred_element_type=F32)


def _dot_nt(a, b):
    return lax.dot_general(a.astype(BF16), b.astype(BF16), (((1,), (1,)), ((), ())), preferred_element_type=F32)


def _s5_bwd(u, wbr, wbi, ar, ai, wcr, wci, dy, name):
    L, C = u.shape
    nb = C // S5_BLOCK_CH
    us, wbs, as_, wcs = _s5_specs(L)
    T = SUBLANES

    def body(u_ref, wbr_ref, wbi_ref, ar_ref, ai_ref, wcr_ref, wci_ref, dy_ref,
             du_ref, dwbr_ref, dwbi_ref, dar_ref, dai_ref, dwcr_ref, dwci_ref, xr, xi, gr, gi):
        ub = u_ref[...]
        a_r, a_i = ar_ref[...], ai_ref[...]
        xr[...] = _bdot(ub, wbr_ref[...])
        xi[...] = _bdot(ub, wbi_ref[...])
        _s5_scan(xr, xi, a_r, a_i, False)
        d = dy_ref[...]
        dwcr_ref[...] = _dot_t(xr[...], d)
        dwci_ref[...] = -_dot_t(xi[...], d)
        gr[...] = _dot_nt(d, wcr_ref[...])
        gi[...] = -_dot_nt(d, wci_ref[...])
        _s5_scan(gr, gi, a_r, -a_i, True)

        row = lax.broadcasted_iota(jnp.int32, (T, S5_BLOCK_ST), 0)

        def da_body(i, carry):
            pr, pi, sr, si = carry
            off = pl.multiple_of(i * T, T)
            xr_t, xi_t = xr[pl.ds(off, T), :], xi[pl.ds(off, T), :]
            lr_t, li_t = gr[pl.ds(off, T), :], gi[pl.ds(off, T), :]
            qr = jnp.where(row == 0, pr, pltpu.roll(xr_t, 1, 0))
            qi = jnp.where(row == 0, pi, pltpu.roll(xi_t, 1, 0))
            sr = sr + qr * lr_t + qi * li_t
            si = si + qr * li_t - qi * lr_t
            return (jnp.broadcast_to(xr_t[T - 1:T, :], (T, S5_BLOCK_ST)),
                    jnp.broadcast_to(xi_t[T - 1:T, :], (T, S5_BLOCK_ST)), sr, si)

        z = jnp.zeros((T, S5_BLOCK_ST), F32)
        _, _, sr, si = lax.fori_loop(0, L // T, da_body, (z, z, z, z))
        dar_ref[...] = jnp.sum(sr, axis=0, keepdims=True)
        dai_ref[...] = jnp.sum(si, axis=0, keepdims=True)
        lr, li = gr[...], gi[...]
        dwbr_ref[...] = _dot_t(ub, lr)
        dwbi_ref[...] = _dot_t(ub, li)
        du_ref[...] = _dot_nt(lr, wbr_ref[...]) + _dot_nt(li, wbi_ref[...])

    sds = jax.ShapeDtypeStruct
    return pl.pallas_call(
        body, grid=(nb,), in_specs=[us, wbs, wbs, as_, as_, wcs, wcs, us],
        out_specs=[us, wbs, wbs, as_, as_, wcs, wcs],
        out_shape=[sds(u.shape, F32), sds(wbr.shape, F32), sds(wbi.shape, F32), sds(ar.shape, F32),
                   sds(ai.shape, F32), sds(wcr.shape, F32), sds(wci.shape, F32)],
        scratch_shapes=[pltpu.VMEM((L, S5_BLOCK_ST), F32)] * 4, name=name,
        compiler_params=_cp(1, VMEM_LIMIT_BIG))(u, wbr, wbi, ar, ai, wcr, wci, dy)


def _s5_core(name):
    @jax.custom_vjp
    def op(u, wbr, wbi, ar, ai, wcr, wci):
        return _s5_fwd(u, wbr, wbi, ar, ai, wcr, wci, name + "_fwd")

    def fwd(*a):
        return _s5_fwd(*a, name + "_fwd"), a

    def bwd(a, dy):
        return tuple(_s5_bwd(*a, dy, name + "_bwd"))

    op.defvjp(fwd, bwd)
    return op


def _att_tile_f(first, q, kp, kc, vp, vc, bias):
    nq = q.shape[0]
    hb = bias.shape[0]
    E = q.shape[1] // hb
    r = lax.broadcasted_iota(jnp.int32, (nq, 2 * nq), 0)
    c = lax.broadcasted_iota(jnp.int32, (nq, 2 * nq), 1)
    prev_ok = jnp.logical_and(jnp.logical_and(c < nq, c >= r), jnp.logical_not(first))
    valid = jnp.logical_or(prev_ok, jnp.logical_and(c >= nq, c - nq <= r))
    outs, lses = [], []
    for h in range(hb):
        sl = slice(h * E, (h + 1) * E)
        k = jnp.concatenate([kp[:, sl], kc[:, sl]], axis=0)
        v = jnp.concatenate([vp[:, sl], vc[:, sl]], axis=0)
        s = jnp.where(valid, _dot_nt(q[:, sl], k) * (E ** -0.5) + bias[h], NEG_INF)
        m = jnp.max(s, axis=-1, keepdims=True)
        p = jnp.exp(s - m)
        den = jnp.sum(p, axis=-1, keepdims=True)
        outs.append(_bdot(p, v) / den)
        lses.append(jnp.broadcast_to(m + jnp.log(den), (nq, E)))
    return jnp.concatenate(outs, axis=-1), jnp.concatenate(lses, axis=-1)


def _att_tile_grad(first, q, kp, kc, vp, vc, bias, o, lse, do, dlse):
    nq = q.shape[0]
    hb = bias.shape[0]
    E = q.shape[1] // hb
    scale = E ** -0.5
    r = lax.broadcasted_iota(jnp.int32, (nq, 2 * nq), 0)
    c = lax.broadcasted_iota(jnp.int32, (nq, 2 * nq), 1)
    prev_ok = jnp.logical_and(jnp.logical_and(c < nq, c >= r), jnp.logical_not(first))
    valid = jnp.logical_or(prev_ok, jnp.logical_and(c >= nq, c - nq <= r))
    dq, dk, dv, db = [], [], [], []
    for h in range(hb):
        sl = slice(h * E, (h + 1) * E)
        qh = q[:, sl]
        k = jnp.concatenate([kp[:, sl], kc[:, sl]], axis=0)
        v = jnp.concatenate([vp[:, sl], vc[:, sl]], axis=0)
        s = jnp.where(valid, _dot_nt(qh, k) * scale + bias[h], NEG_INF)
        p = jnp.exp(s - lse[:, h * E:h * E + 1])
        doh = do[:, sl]
        row = jnp.sum(dlse[:, sl], axis=-1, keepdims=True) - jnp.sum(doh * o[:, sl], axis=-1, keepdims=True)
        ds = p * (_dot_nt(doh, v) + row)
        db.append(ds)
        dv.append(_dot_t(p, doh))
        dq.append(_bdot(ds, k) * scale)
        dk.append(_dot_t(ds, qh) * scale)
    cat = lambda parts, rows: jnp.concatenate([x[rows] for x in parts], axis=-1)
    lo, hi = slice(0, nq), slice(nq, 2 * nq)
    return jnp.concatenate(dq, axis=-1), cat(dk, lo), cat(dk, hi), cat(dv, lo), cat(dv, hi), db


def _att_mix_f(*a):
    n = len(a) // 2
    o, l = a[:n], a[n:]
    m = functools.reduce(jnp.maximum, l)
    e = [jnp.exp(li - m) for li in l]
    return sum(ei * oi for ei, oi in zip(e, o)) / sum(e)


def _att_rows(start, dil):
    if dil == 1:
        return pl.ds(pl.multiple_of(start, ATT_BLOCK), ATT_BLOCK)
    return pl.ds(start, ATT_BLOCK, stride=dil)


def _att_blocks(L, dil):
    nb = L // dil // ATT_BLOCK
    return dil * nb, nb


def _att_specs(L, W):
    nblk = W // LANES
    col = lambda off: pl.BlockSpec((L, LANES), lambda j: (0, j + off))
    per_pattern = pl.BlockSpec((len(DILATED_PATTERNS), L, LANES), lambda j: (0, 0, j))
    hb = ATT_HEADS // nblk
    bias = pl.BlockSpec((len(DILATED_PATTERNS), hb, ATT_BLOCK, 2 * ATT_BLOCK), lambda j: (0, j, 0, 0))
    return nblk, col, per_pattern, bias


def _att_fwd(qkv, bias, name):
    L, W3 = qkv.shape
    W = W3 // 3
    nblk, col, per_pattern, bias_spec = _att_specs(L, W)
    P = len(DILATED_PATTERNS)

    def body(q_ref, k_ref, v_ref, b_ref, y_ref, o_ref, l_ref):
        for p, (_, dil) in enumerate(DILATED_PATTERNS):
            n_it, nb = _att_blocks(L, dil)

            def step(i, carry, p=p, dil=dil, nb=nb):
                n = i % nb
                cur = i // nb + n * (ATT_BLOCK * dil)
                prev = i // nb + jnp.maximum(n - 1, 0) * (ATT_BLOCK * dil)
                rc, rp = _att_rows(cur, dil), _att_rows(prev, dil)
                o, l = _att_tile_f(n == 0, q_ref[rc, :], k_ref[rp, :], k_ref[rc, :], v_ref[rp, :], v_ref[rc, :],
                                   b_ref[p])
                o_ref[p, rc, :] = o
                l_ref[p, rc, :] = l
                return carry

            lax.fori_loop(0, n_it, step, 0, unroll=ATT_UNROLL)

        def mix(i, carry):
            rows = pl.ds(pl.multiple_of(i * ATT_MIX_ROWS, ATT_MIX_ROWS), ATT_MIX_ROWS)
            y_ref[rows, :] = _att_mix_f(*[o_ref[p, rows, :] for p in range(P)], *[l_ref[p, rows, :] for p in range(P)])
            return carry

        lax.fori_loop(0, L // ATT_MIX_ROWS, mix, 0)

    sds = jax.ShapeDtypeStruct
    return pl.pallas_call(
        body, grid=(nblk,), in_specs=[col(0), col(nblk), col(2 * nblk), bias_spec],
        out_specs=[col(0), per_pattern, per_pattern],
        out_shape=[sds((L, W), F32), sds((P, L, W), F32), sds((P, L, W), F32)], name=name,
        compiler_params=_cp(1))(qkv, qkv, qkv, bias)


def _att_bwd(qkv, bias, o_all, l_all, dy, name, riders=None):
    L, W3 = qkv.shape
    W = W3 // 3
    nblk, col, per_pattern, bias_spec = _att_specs(L, W)
    P = len(DILATED_PATTERNS)
    ride = _Riders(riders)
    n_in, n_out = 7, 4

    def body(*refs):
        q_ref, k_ref, v_ref, b_ref, o_ref, l_ref, dy_ref = refs[:n_in]
        outs = refs[n_in + 2 * ride.n:]
        dq_ref, dk_ref, dv_ref, db_ref = outs[:n_out]
        do_s, dl_s = outs[n_out + ride.n:n_out + ride.n + 2]
        start, wait = ride.hooks((nblk,), refs[n_in:n_in + ride.n], outs[n_out:n_out + ride.n],
                                 outs[n_out + ride.n + 2:])
        start()

        def mix(i, carry):
            rows = pl.ds(pl.multiple_of(i * ATT_MIX_ROWS, ATT_MIX_ROWS), ATT_MIX_ROWS)
            _, mix_vjp = jax.vjp(_att_mix_f, *[o_ref[p, rows, :] for p in range(P)],
                                 *[l_ref[p, rows, :] for p in range(P)])
            g = mix_vjp(dy_ref[rows, :])
            for p in range(P):
                do_s[p, rows, :] = g[p]
                dl_s[p, rows, :] = g[P + p]
            return carry

        lax.fori_loop(0, L // ATT_MIX_ROWS, mix, 0)
        for ref in (dq_ref, dk_ref, dv_ref, db_ref):
            ref[...] = jnp.zeros_like(ref)

        def add(ref, rows, val):
            ref[rows, :] = ref[rows, :] + val

        for p, (_, dil) in enumerate(DILATED_PATTERNS):
            n_it, nb = _att_blocks(L, dil)

            def step(i, carry, p=p, dil=dil, nb=nb):
                n = i % nb
                first = n == 0
                cur = i // nb + n * (ATT_BLOCK * dil)
                prev = i // nb + jnp.maximum(n - 1, 0) * (ATT_BLOCK * dil)
                rc, rp = _att_rows(cur, dil), _att_rows(prev, dil)
                dq, dkp, dkc, dvp, dvc, db = _att_tile_grad(
                    first, q_ref[rc, :], k_ref[rp, :], k_ref[rc, :], v_ref[rp, :], v_ref[rc, :], b_ref[p],
                    o_ref[p, rc, :], l_ref[p, rc, :], do_s[p, rc, :], dl_s[p, rc, :])
                add(dq_ref, rc, dq)
                add(dk_ref, rc, dkc)
                add(dv_ref, rc, dvc)
                for h, dbh in enumerate(db):
                    db_ref[p, h] = db_ref[p, h] + dbh

                @pl.when(jnp.logical_not(first))
                def _():
                    add(dk_ref, rp, dkp)
                    add(dv_ref, rp, dvp)

                return carry

            lax.fori_loop(0, n_it, step, 0, unroll=ATT_UNROLL)
        wait()

    sds = jax.ShapeDtypeStruct((L, W), F32)
    return pl.pallas_call(
        body, grid=(nblk,),
        in_specs=[col(0), col(nblk), col(2 * nblk), bias_spec, per_pattern, per_pattern, col(0)] + ride.in_specs(),
        out_specs=[col(0), col(0), col(0), bias_spec] + ride.out_specs(),
        out_shape=[sds, sds, sds, jax.ShapeDtypeStruct(bias.shape, F32)] + ride.out_shape(),
        scratch_shapes=[pltpu.VMEM((P, L, LANES), F32)] * 2 + ride.scratch(),
        input_output_aliases=ride.aliases(n_in, n_out), name=name, compiler_params=_cp(1, VMEM_LIMIT_BIG))(
            qkv, qkv, qkv, bias, o_all, l_all, dy, *ride.operands())


def _t5_bucket(dist):
    n = np.maximum(dist, 0)
    max_exact = REL_BUCKETS // 2
    large = max_exact + (np.log(np.maximum(n, 1) / max_exact) / np.log(REL_MAX_DIST / max_exact)
                         * (REL_BUCKETS - max_exact)).astype(np.int64)
    large = np.minimum(large, REL_BUCKETS - 1)
    return np.where(n < max_exact, n, large).astype(np.int32)


def _bucket_onehot():
    a = np.arange(ATT_BLOCK)[:, None]
    b = np.arange(2 * ATT_BLOCK)[None, :]
    sub = a + ATT_BLOCK - b
    bucket = jnp.asarray(np.stack([_t5_bucket(sub * dil).reshape(-1) for _, dil in DILATED_PATTERNS]))
    ids = jnp.arange(REL_BUCKETS, dtype=jnp.int32)
    return (bucket[:, None, :] == ids[None, :, None]).astype(F32)


def loss_head(h, target, g, name):
    R, D = h.shape
    tr = _tile(R, (256,))

    def body(h_ref, t_ref, g_ref, l_ref, dh_ref, dg_ref):
        def lf(hv, gv):
            y = _rms_f(hv, gv)[0]
            return 0.5 * jnp.sum(jnp.mean(jnp.square(y - t_ref[...]), axis=-1))

        l, (dh, dg) = jax.value_and_grad(lf, argnums=(0, 1))(h_ref[...], g_ref[...])

        @pl.when(pl.program_id(0) == 0)
        def _():
            l_ref[...] = jnp.zeros_like(l_ref)
            dg_ref[...] = jnp.zeros_like(dg_ref)

        dh_ref[...] = dh
        dg_ref[...] += dg
        l_ref[...] += l

    rows = pl.BlockSpec((tr, D), lambda i: (i, 0))
    vec = pl.BlockSpec((1, D), lambda i: (0, 0))
    l, dh, dg = pl.pallas_call(
        body, grid=(R // tr,), in_specs=[rows, rows, vec],
        out_specs=[pl.BlockSpec((SUBLANES, LANES), lambda i: (0, 0)), rows, vec],
        out_shape=[jax.ShapeDtypeStruct((SUBLANES, LANES), F32), jax.ShapeDtypeStruct((R, D), F32),
                   jax.ShapeDtypeStruct((1, D), F32)], name=name, compiler_params=_cp(1))(h, target, g.reshape(1, D))
    return l[0, 0], dh, dg.reshape(D)


def _adamw_update(w, g, m, v):
    c1 = 1.0 - ADAM_B1 ** ADAM_STEP
    c2 = 1.0 - ADAM_B2 ** ADAM_STEP
    nm = ADAM_B1 * m + (1.0 - ADAM_B1) * g
    nv = ADAM_B2 * v + (1.0 - ADAM_B2) * jnp.square(g)
    return -ADAM_LR * ((nm / c1) / (jnp.sqrt(nv / c2) + ADAM_EPS) + ADAM_WD * w), nm, nv


def adamw_layers(w, parts, m, v, name):
    nl, a, b = w.shape
    n_parts = parts[0].shape[0]
    tr = _row_tile(a, b)

    def body(*refs):
        w_ref, p_refs, (m_ref, v_ref, g_ref, d_ref, nm_ref, nv_ref) = refs[0], refs[1:1 + nl], refs[1 + nl:]
        for l in range(nl):
            @pl.when(pl.program_id(0) == l)
            def _(p_ref=p_refs[l]):
                g = p_ref[0].astype(F32)
                for i in range(1, n_parts):
                    g = g + p_ref[i].astype(F32)
                d_ref[...], nm_ref[...], nv_ref[...] = _adamw_update(w_ref[...], g, m_ref[...], v_ref[...])
                g_ref[...] = g

    rows = pl.BlockSpec((None, tr, b), lambda l, i: (l, i, 0))
    part = lambda k: pl.BlockSpec((n_parts, tr, b), lambda l, i: (0, jnp.where(l == k, i, 0), 0))
    sds = jax.ShapeDtypeStruct((nl, a, b), F32)
    return pl.pallas_call(body, grid=(nl, a // tr), in_specs=[rows] + [part(k) for k in range(nl)] + [rows, rows],
                          out_specs=[rows] * 4, out_shape=[sds] * 4, name=name, compiler_params=_cp(2))(
                              w, *parts, m, v)


def adamw(w, parts, m, v, name):
    R, C = w.shape
    n_parts = parts.shape[0]
    tr = _row_tile(R, C)

    def body(w_ref, p_ref, m_ref, v_ref, g_ref, d_ref, nm_ref, nv_ref):
        g = p_ref[0].astype(F32)
        for i in range(1, n_parts):
            g = g + p_ref[i].astype(F32)
        d_ref[...], nm_ref[...], nv_ref[...] = _adamw_update(w_ref[...], g, m_ref[...], v_ref[...])
        g_ref[...] = g

    rows = pl.BlockSpec((tr, C), lambda i: (i, 0))
    sds = jax.ShapeDtypeStruct((R, C), F32)
    return pl.pallas_call(body, grid=(R // tr,),
                          in_specs=[rows, pl.BlockSpec((n_parts, tr, C), lambda i: (0, i, 0)), rows, rows],
                          out_specs=[rows] * 4, out_shape=[sds] * 4, name=name, compiler_params=_cp(1))(w, parts, m, v)


HBM_SPEC = pl.BlockSpec(memory_space=pltpu.HBM)
MESH_ID = pl.DeviceIdType.MESH


def _place():
    return lax.axis_index("x"), lax.axis_index("y"), lax.axis_index("c")


def _index(x, y, c):
    return 4 * x + 2 * y + c


AG_COPIES = 9
AG_ROW_UNIT = 32


def all_gather(x, name):
    R, C = x.shape
    assert R % AG_ROW_UNIT == 0, x.shape
    half = R // 2

    def body(x_ref, out_ref, send_sems, recv_sems, local_sem):
        x_, y_, c_ = _place()
        me, sib = (x_, y_, c_), (x_, y_, 1 - c_)
        nx, ny, nd = (1 - x_, y_, c_), (x_, 1 - y_, c_), (1 - x_, 1 - y_, c_)
        upper, lower = pl.ds(0, half), pl.ds(half, half)

        def slot(dev, rows=None):
            ref = out_ref.at[_index(*dev)]
            return ref if rows is None else ref.at[rows]

        def copy(k, block, to, rows=None, src=None):
            return pltpu.make_async_remote_copy(
                src_ref=slot(block, rows) if src is None else src, dst_ref=slot(block, rows),
                send_sem=send_sems.at[k], recv_sem=recv_sems.at[k], device_id=to, device_id_type=MESH_ID)

        def other(dev):
            return (dev[0], dev[1], 1 - c_)

        mine = pltpu.make_async_copy(x_ref, slot(me), local_sem)
        mine.start()
        sent = [copy(0, me, sib, src=x_ref), copy(1, me, nx, src=x_ref), copy(2, me, ny, src=x_ref)]
        for cp in sent:
            cp.start()

        def then(arrival, *forwards):
            arrival.wait_recv()
            for cp in forwards:
                cp.start()
            sent.extend(forwards)

        then(copy(1, nx, me), copy(4, nx, ny, upper), copy(5, nx, sib))
        then(copy(2, ny, me), copy(3, ny, nx, lower), copy(6, ny, sib))
        then(copy(3, nd, me, lower), copy(8, nd, sib, lower))
        then(copy(4, nd, me, upper), copy(7, nd, sib, upper))
        copy(0, sib, me).wait_recv()
        copy(5, other(nx), me).wait_recv()
        copy(6, other(ny), me).wait_recv()
        copy(7, other(nd), me, upper).wait_recv()
        copy(8, other(nd), me, lower).wait_recv()
        for cp in sent:
            cp.wait_send()
        mine.wait()

    return pl.pallas_call(
        body, out_shape=jax.ShapeDtypeStruct((N_DEV,) + x.shape, x.dtype), in_specs=[HBM_SPEC], out_specs=HBM_SPEC,
        scratch_shapes=[pltpu.SemaphoreType.DMA((AG_COPIES,)), pltpu.SemaphoreType.DMA((AG_COPIES,)),
                        pltpu.SemaphoreType.DMA], name=name)(x)


def _chip(x, y):
    return 2 * x + y


def sibling_exchange(xs, name):
    n = len(xs)

    def body(*refs):
        x_refs, out_refs, (send_sems, recv_sems) = refs[:n], refs[n:2 * n], refs[2 * n:]
        x_, y_, c_ = _place()
        copies = [pltpu.make_async_remote_copy(src_ref=x_ref.at[:, 1 - c_], dst_ref=out_ref, send_sem=send_sems.at[p],
                                               recv_sem=recv_sems.at[p], device_id=(x_, y_, 1 - c_),
                                               device_id_type=MESH_ID)
                  for p, (x_ref, out_ref) in enumerate(zip(x_refs, out_refs))]
        for cp in copies:
            cp.start()
        for cp in copies:
            cp.wait()

    return pl.pallas_call(
        body, out_shape=[jax.ShapeDtypeStruct((x.shape[0],) + x.shape[2:], x.dtype) for x in xs],
        in_specs=[HBM_SPEC] * n, out_specs=[HBM_SPEC] * n,
        scratch_shapes=[pltpu.SemaphoreType.DMA((n,)), pltpu.SemaphoreType.DMA((n,))], name=name)(*xs)


def _row_tile(R, C):
    cap = max(SUBLANES, STREAM_BLOCK_BYTES // (4 * C))
    return _tile(R, [t for t in (512, 256, 128, 64, 32, 16, 8) if t <= cap])


def pair_sum(x, recv, name):
    nc, _, R, C = x.shape
    tr = _row_tile(R, C)
    core = lax.axis_index("c").astype(jnp.int32).reshape(1)

    def body(c_ref, a_ref, b_ref, o_ref):
        o_ref[...] = (a_ref[...].astype(F32) + b_ref[...].astype(F32)).astype(o_ref.dtype)

    blk = pl.BlockSpec((None, tr, C), lambda k, i, c_ref: (k, i, 0))
    grid_spec = pltpu.PrefetchScalarGridSpec(
        num_scalar_prefetch=1, grid=(nc, R // tr),
        in_specs=[pl.BlockSpec((None, None, tr, C), lambda k, i, c_ref: (k, c_ref[0], i, 0)), blk], out_specs=blk)
    return pl.pallas_call(body, grid_spec=grid_spec, out_shape=jax.ShapeDtypeStruct((nc, R, C), x.dtype), name=name,
                          compiler_params=_cp(2))(core, x, recv)


def chip_exchange(ss, name):
    n = len(ss)

    def body(*refs):
        copies = _chip_exchange_copies(refs[:n], refs[n:2 * n], (0, 1, 1), *refs[2 * n:])
        for cp in copies:
            cp.start()
        for cp in copies:
            cp.wait()

    n_sem = n * (N_CHIP - 1)
    return pl.pallas_call(
        body, out_shape=[jax.ShapeDtypeStruct(s.shape, s.dtype) for s in ss], in_specs=[HBM_SPEC] * n,
        out_specs=[HBM_SPEC] * n,
        scratch_shapes=[pltpu.SemaphoreType.DMA((n_sem,)), pltpu.SemaphoreType.DMA((n_sem,)),
                        pltpu.SemaphoreType.DMA((n,))], name=name)(*ss)


def _block_diag(w, nb):
    G, a, b = w.shape
    gp = G // nb
    eye = jnp.eye(gp, dtype=w.dtype)
    return jnp.einsum('jgab,gh->jgahb', w.reshape(nb, gp, a, b), eye).reshape(nb, gp * a, gp * b)


def _split_columns(x, cuts):
    edges = (0,) + tuple(cuts) + (x.shape[1],)

    def split(x):
        return tuple(x[:, a:b] for a, b in zip(edges[:-1], edges[1:]))

    op = jax.custom_vjp(split)
    op.defvjp(lambda x: (split(x), None), lambda _, cts: (jnp.concatenate(cts, axis=-1),))
    return op(x)


def _project_in(l, h, P):
    L, D = h.shape
    GW = D // N_MIXERS
    proj = _act_linear(f"l{l}_w_in", _rms_op(f"l{l}_norm_mix", L, D, BF16), 2, False)(
        h, P['norm_mix_g'].reshape(1, D), P['w_in'])
    return _split_columns(proj, (GW, 2 * GW, 4 * GW))


def _mix_and_memory(l, u_a, u_b, u_c, y_d, h, memn, P):
    nm = lambda s: f"l{l}_{s}"
    L, D = h.shape
    GW = D // N_MIXERS
    G = GW // S5_CH_PER_GROUP

    row = lambda g: g.reshape(1, D)

    v3 = lambda a: a.reshape(G, 1, S5_STATE)
    log_dt = jnp.broadcast_to(P['s5_log_dt'][:, None, None], (G, 1, S5_STATE))
    a_r, a_i, bb_r, bb_i = s5_discretise(v3(P['s5_lam_re']), v3(P['s5_lam_im']), log_dt,
                                         P['s5_b_re'].transpose(0, 2, 1), P['s5_b_im'].transpose(0, 2, 1), nm("s5_disc"))
    nblk = GW // S5_BLOCK_CH
    y_s5 = _s5_core(nm("s5_core"))(
        u_a, _block_diag(bb_r, nblk), _block_diag(bb_i, nblk), a_r.reshape(1, G * S5_STATE), a_i.reshape(1, G * S5_STATE),
        _block_diag(P['s5_c_re'].transpose(0, 2, 1), nblk), _block_diag(P['s5_c_im'].transpose(0, 2, 1), nblk))
    y_a = s5_epilogue(y_s5, u_a, P['s5_d'], P['s5_w_glu'], nm("s5_glu"))

    y_b = pool_proj(_pool_mix(nm("pool_mix"))(u_b), P['pool_w'], P['pool_scale'], nm("pool_proj"))

    hc = _glu_conv(nm("conv_dw"))(u_c, P['conv_w_dw'], P['conv_b_dw'].reshape(1, GW))
    y_c = conv_post(hc, P['conv_ln_g'], P['conv_ln_b'], P['conv_w_pw'], nm("conv_post"))

    grp = _group_norm_op(nm("grp_norm"), L, GW, N_MIXERS, BF16)
    h = _act_linear(nm("w_out"), grp, N_MIXERS + 1, True)(y_a, y_b, y_c, y_d, row(P['grp_norm_g']), P['w_out'], h)

    xq = _act_linear(nm("w_xq"), _rms_op(nm("norm_x"), L, D, BF16), 2, False)(h, row(P['norm_x_g']), P['w_xq'])
    xk = _linear(nm("w_xk"))(memn, P['w_xk'])
    xv = _linear(nm("w_xv"))(memn, P['w_xv'])
    xat = _cross_attention_op(nm("xattn"), L, xq.shape[1], memn.shape[0], BF16)
    return _act_linear(nm("w_xo"), xat, 3, True)(xq, xk, xv, P['w_xo'], h)


def _bias_tables(rel_bias):
    tabs = rel_bias_tables(rel_bias, _bucket_onehot(), "rel_bias")
    return tabs.reshape(len(DILATED_PATTERNS), ATT_HEADS, ATT_BLOCK, 2 * ATT_BLOCK)


def _gather_weight(name, w):
    ax = SHARDED[name]
    dt = BF16 if name in GATHER_BF16 else F32
    nl, a, b = w.shape
    rows = nl * a
    flat = jnp.pad(w.astype(dt).reshape(rows, b), ((0, (-rows) % AG_ROW_UNIT), (0, 0)))
    g = all_gather(flat, "ag_" + name)
    if name in MLP_SHARDED:
        assert rows % AG_ROW_UNIT == 0
        return g
    g = g[:, :rows].reshape(N_DEV, nl, a, b)
    if ax == 1:
        return g.transpose(1, 0, 2, 3).reshape(nl, N_DEV * a, b)
    return g.transpose(1, 2, 0, 3).reshape(nl, a, N_DEV * b)


def _scatter_grad(name, g):
    ax = SHARDED[name]
    nl = g.shape[0]
    if ax == 1:
        a, b = g.shape[1] // N_DEV, g.shape[2]
        s = g.reshape(nl, N_DEV, a, b).transpose(1, 0, 2, 3)
    else:
        a, b = g.shape[1], g.shape[2] // N_DEV
        s = g.reshape(nl, a, N_DEV, b).transpose(2, 0, 1, 3)
    s = s.reshape(N_CHIP, N_DEV // N_CHIP, nl * a, b)
    pair = pair_sum(s, sibling_exchange([s], "d2d_" + name)[0], "pairsum_" + name)
    return chip_exchange([pair], "ici_" + name)[0]


def _by_destination(name, g):
    if g.ndim == 2 and SHARDED[name] == 1:
        g = g.reshape(N_DEV, g.shape[0] // N_DEV, g.shape[1])
    elif g.ndim == 2:
        g = g.reshape(g.shape[0], N_DEV, g.shape[1] // N_DEV).transpose(1, 0, 2)
    return g.reshape(N_CHIP, N_DEV // N_CHIP, *g.shape[1:])


def _flatten_small(d):
    flat = jnp.concatenate([d[n].reshape(-1).astype(F32) for n in SMALL])
    pad = (-flat.shape[0]) % (LANES * SMALL_ROW_TILE)
    return jnp.pad(flat, (0, pad)).reshape(-1, LANES)


def _split_small(flat, like):
    flat = flat.reshape(-1)
    out, off = {}, 0
    for n in SMALL:
        sz = math.prod(like[n].shape)
        out[n] = flat[off:off + sz].reshape(like[n].shape)
        off += sz
    return out


def kernel(x, mem, rel_bias, mem_norm_g, norm_mix_g, w_in, s5_lam_re, s5_lam_im, s5_log_dt, s5_b_re, s5_b_im, s5_c_re, s5_c_im, s5_d, s5_w_glu, pool_w, pool_scale, conv_w_dw, conv_b_dw, conv_ln_g, conv_ln_b, conv_w_pw, grp_norm_g, w_out, norm_x_g, w_xq, w_xk, w_xv, w_xo, norm_mlp_g, w_up, w_down, norm_final_g, loss_target, m_rel_bias, m_mem_norm_g, m_norm_mix_g, m_w_in, m_s5_lam_re, m_s5_lam_im, m_s5_log_dt, m_s5_b_re, m_s5_b_im, m_s5_c_re, m_s5_c_im, m_s5_d, m_s5_w_glu, m_pool_w, m_pool_scale, m_conv_w_dw, m_conv_b_dw, m_conv_ln_g, m_conv_ln_b, m_conv_w_pw, m_grp_norm_g, m_w_out, m_norm_x_g, m_w_xq, m_w_xk, m_w_xv, m_w_xo, m_norm_mlp_g, m_w_up, m_w_down, m_norm_final_g, v_rel_bias, v_mem_norm_g, v_norm_mix_g, v_w_in, v_s5_lam_re, v_s5_lam_im, v_s5_log_dt, v_s5_b_re, v_s5_b_im, v_s5_c_re, v_s5_c_im, v_s5_d, v_s5_w_glu, v_pool_w, v_pool_scale, v_conv_w_dw, v_conv_b_dw, v_conv_ln_g, v_conv_ln_b, v_conv_w_pw, v_grp_norm_g, v_w_out, v_norm_x_g, v_w_xq, v_w_xk, v_w_xv, v_w_xo, v_norm_mlp_g, v_w_up, v_w_down, v_norm_final_g):
    w = dict(zip(WEIGHTS, (rel_bias, mem_norm_g, norm_mix_g, w_in, s5_lam_re, s5_lam_im, s5_log_dt, s5_b_re, s5_b_im, s5_c_re, s5_c_im, s5_d, s5_w_glu, pool_w, pool_scale, conv_w_dw, conv_b_dw, conv_ln_g, conv_ln_b, conv_w_pw, grp_norm_g, w_out, norm_x_g, w_xq, w_xk, w_xv, w_xo, norm_mlp_g, w_up, w_down, norm_final_g)))
    m = dict(zip(WEIGHTS, (m_rel_bias, m_mem_norm_g, m_norm_mix_g, m_w_in, m_s5_lam_re, m_s5_lam_im, m_s5_log_dt, m_s5_b_re, m_s5_b_im, m_s5_c_re, m_s5_c_im, m_s5_d, m_s5_w_glu, m_pool_w, m_pool_scale, m_conv_w_dw, m_conv_b_dw, m_conv_ln_g, m_conv_ln_b, m_conv_w_pw, m_grp_norm_g, m_w_out, m_norm_x_g, m_w_xq, m_w_xk, m_w_xv, m_w_xo, m_norm_mlp_g, m_w_up, m_w_down, m_norm_final_g)))
    v = dict(zip(WEIGHTS, (v_rel_bias, v_mem_norm_g, v_norm_mix_g, v_w_in, v_s5_lam_re, v_s5_lam_im, v_s5_log_dt, v_s5_b_re, v_s5_b_im, v_s5_c_re, v_s5_c_im, v_s5_d, v_s5_w_glu, v_pool_w, v_pool_scale, v_conv_w_dw, v_conv_b_dw, v_conv_ln_g, v_conv_ln_b, v_conv_w_pw, v_grp_norm_g, v_w_out, v_norm_x_g, v_w_xq, v_w_xk, v_w_xv, v_w_xo, v_norm_mlp_g, v_w_up, v_w_down, v_norm_final_g)))

    full = {n: (_gather_weight(n, w[n]) if n in SHARDED else w[n]) for n in WEIGHTS if n != 'norm_final_g'}
    L, D = x.shape[1:]

    memn, mem_vjp = jax.vjp(lambda a, g: rmsnorm(a, g, "mem_norm"), mem[0], w['mem_norm_g'])
    tabs, tabs_vjp = jax.vjp(_bias_tables, w['rel_bias'])
    h = x[0]
    stages = []
    for l in range(DEPTH):
        layer = lambda names: {n: full[n][l] for n in names}
        (u_a, u_b, u_c, qkv), in_vjp = jax.vjp(functools.partial(_project_in, l), h, layer(IN_WEIGHTS))
        y_d, o_all, l_all = _att_fwd(qkv, tabs, f"l{l}_att_fwd")
        h, mix_vjp = jax.vjp(functools.partial(_mix_and_memory, l), u_a, u_b, u_c, y_d, h, memn, layer(MIX_WEIGHTS))
        norm = _rms_op(f"l{l}_norm_mlp", L, D, BF16)
        h, saved = _mlp_fwd(f"l{l}_mlp", norm, h, full['norm_mlp_g'][l].reshape(1, D), full['w_up'], full['w_down'], l)
        stages.append((in_vjp, (qkv, tabs, o_all, l_all), mix_vjp, norm, saved))
    loss_local, dh, d_final_g = loss_head(h, loss_target[0], w['norm_final_g'], "loss_head")
    loss = lax.psum(loss_local, MESH_AXES)

    def pair_sums(l, names, grads_l):
        by_dest = [_by_destination(n, grads_l[n]) for n in names]
        theirs = sibling_exchange(by_dest, f"d2d_l{l}_{names[0]}")
        return [pair_sum(s, t, f"pairsum_l{l}_{n}") for n, s, t in zip(names, by_dest, theirs)]

    empties = lambda like: [lax.empty(p.shape, p.dtype) for p in like]
    layer_grads, arrived = [None] * DEPTH, [None] * DEPTH
    pending, dmemn, dtabs = None, 0.0, 0.0
    for l in reversed(range(DEPTH)):
        in_vjp, att_saved, mix_vjp, norm, saved = stages[l]
        lands = None if pending is None else empties(pending)
        dh, dg_mlp, dw_up, dw_down, lands = _mlp_bwd(f"l{l}_mlp", norm, saved, dh, pending, lands)
        mlp_grads = dict(norm_mlp_g=dg_mlp.reshape(D), w_up=dw_up, w_down=dw_down)
        du_a, du_b, du_c, dy_d, dh_res, dmemn_l, d_mix = mix_vjp(dh)
        rides = [] if pending is None else [(pending, lands, ATT_RIDE_WINDOW)]
        if l == 0:
            early = pair_sums(l, MLP_SHARDED, mlp_grads)
            rides.append((early, empties(early), (0, 1, 1)))
        dq, dk, dv, dtabs_l, *landed = _att_bwd(*att_saved, dy_d, f"l{l}_att_bwd", riders=rides)
        if pending is not None:
            arrived[l + 1] = landed[:len(pending)]
        dh_in, d_in = in_vjp((du_a, du_b, du_c, jnp.concatenate([dq, dk, dv], axis=-1)))
        dh = dh_in + dh_res
        dmemn, dtabs = dmemn + dmemn_l, dtabs + dtabs_l
        layer_grads[l] = {**d_in, **d_mix, **mlp_grads}
        if l > 0:
            pending = pair_sums(l, GATHER_BF16, layer_grads[l])
        else:
            late = [n for n in GATHER_BF16 if n not in MLP_SHARDED]
            got = dict(zip(late, chip_exchange(pair_sums(l, late, layer_grads[l]), "ici_l0")))
            got.update(zip(MLP_SHARDED, landed[-len(MLP_SHARDED):]))
            arrived[0] = [got[n] for n in GATHER_BF16]
    dx = dh
    dfull = {n: jnp.concatenate([layer_grads[l][n][None] for l in range(DEPTH)])
             for n in LAYER_WEIGHTS if n not in GATHER_BF16}
    dfull['mem_norm_g'] = mem_vjp(dmemn)[1]
    dfull['rel_bias'] = tabs_vjp(dtabs)[0]
    dfull['norm_final_g'] = d_final_g

    grads, deltas, new_m, new_v = {}, {}, {}, {}
    for k, n in enumerate(GATHER_BF16):
        res = adamw_layers(w[n], [arrived[l][k] for l in range(DEPTH)], m[n], v[n], "adamw_" + n)
        grads[n], deltas[n], new_m[n], new_v[n] = res
    for n in SHARDED:
        if n in GATHER_BF16:
            continue
        parts = _scatter_grad(n, dfull[n])
        shp = w[n].shape
        two_d = lambda a: a.reshape(shp[0] * shp[1], shp[2])
        res = adamw(two_d(w[n]), parts, two_d(m[n]), two_d(v[n]), "adamw_" + n)
        grads[n], deltas[n], new_m[n], new_v[n] = (r.reshape(shp) for r in res)

    parts = all_gather(_flatten_small(dfull), "ag_small_grads")
    res = adamw(_flatten_small(w), parts, _flatten_small(m), _flatten_small(v), "adamw_small")
    for dst, r in zip((grads, deltas, new_m, new_v), res):
        dst.update(_split_small(r, w))

    return (loss, dx[None], *[grads[n] for n in WEIGHTS], *[deltas[n] for n in WEIGHTS],
            *[new_m[n] for n in WEIGHTS], *[new_v[n] for n in WEIGHTS])
```

```python
import functools
import math

import numpy as np
import jax
import jax.numpy as jnp
from jax import lax
from jax.experimental import pallas as pl
from jax.experimental.pallas import tpu as pltpu

F32 = jnp.float32
BF16 = jnp.bfloat16

DEPTH = 4
N_MIXERS = 4
S5_CH_PER_GROUP = 16
S5_STATE = 64
POOL_WINDOWS = (2, 4, 8, 16)
CONV_WIDTH = 31
ATT_HEADS = 8
DILATED_PATTERNS = ((128, 1), (512, 4), (2048, 16))
ATT_BLOCK = 128
ATT_MIX_ROWS = 256
ATT_UNROLL = 4
REL_BUCKETS = 32
REL_MAX_DIST = 2048
X_HEADS = 4
X_HEAD_DIM = 128
NORM_EPS = 1e-6
NEG_INF = -1e30
ADAM_LR = 0.001
ADAM_B1 = 0.9
ADAM_B2 = 0.999
ADAM_EPS = 1e-08
ADAM_WD = 0.01
ADAM_STEP = 10

LANES = 128
SUBLANES = 8
VMEM_BYTES = 64 * 1024 * 1024
VMEM_LIMIT = (VMEM_BYTES * 3) // 4
VMEM_LIMIT_BIG = (VMEM_BYTES * 7) // 8
STREAM_BLOCK_BYTES = 1024 * 1024
SMALL_ROW_TILE = 512
N_DEV = 8
N_CHIP = 4
MESH_AXES = ("x", "y", "c")

WEIGHTS = ['rel_bias', 'mem_norm_g', 'norm_mix_g', 'w_in', 's5_lam_re', 's5_lam_im', 's5_log_dt', 's5_b_re',
           's5_b_im', 's5_c_re', 's5_c_im', 's5_d', 's5_w_glu', 'pool_w', 'pool_scale', 'conv_w_dw', 'conv_b_dw',
           'conv_ln_g', 'conv_ln_b', 'conv_w_pw', 'grp_norm_g', 'w_out', 'norm_x_g', 'w_xq', 'w_xk', 'w_xv', 'w_xo',
           'norm_mlp_g', 'w_up', 'w_down', 'norm_final_g']
SHARDED = {'w_in': 2, 's5_w_glu': 1, 'conv_w_dw': 2, 'conv_w_pw': 1, 'w_out': 1, 'w_xq': 1, 'w_xk': 1, 'w_xv': 1,
           'w_xo': 2, 'w_up': 2, 'w_down': 1}
GATHER_BF16 = ('w_in', 'w_out', 'w_xq', 'w_xk', 'w_xv', 'w_xo', 'w_up', 'w_down')
SMALL = [n for n in WEIGHTS if n not in SHARDED]
LAYER_WEIGHTS = [n for n in WEIGHTS if n not in ('rel_bias', 'mem_norm_g', 'norm_final_g')]
IN_WEIGHTS = ('norm_mix_g', 'w_in')
MLP_WEIGHTS = ('norm_mlp_g', 'w_up', 'w_down')
MIX_WEIGHTS = [n for n in LAYER_WEIGHTS if n not in IN_WEIGHTS + MLP_WEIGHTS]
MLP_SHARDED = ('w_up', 'w_down')

def _cp(n_axes, vmem=VMEM_LIMIT):
    return pltpu.CompilerParams(dimension_semantics=("arbitrary",) * n_axes, vmem_limit_bytes=vmem)


def _tile(n, prefs):
    for t in prefs:
        if n % t == 0:
            return t
    return n


MM_TILE = 1024
MM_TILE_K = 2048
MM_FULL_K = 4096


def _chip_exchange_copies(srcs, dsts, window, send_sems, recv_sems, local_sems, base=0):
    x_, y_, c_ = _place()
    me = _chip(x_, y_)
    first, count, total = window
    copies = []
    for p, (src, dst) in enumerate(zip(srcs, dsts)):
        unit = src.shape[1] // total
        rows = pl.ds(first * unit, count * unit)
        copies.append(pltpu.make_async_copy(src.at[me, rows], dst.at[me, rows], local_sems.at[base + p]))
        for k in range(1, N_CHIP):
            px = 1 - x_ if k & 2 else x_
            py = 1 - y_ if k & 1 else y_
            s = (base + p) * (N_CHIP - 1) + k - 1
            copies.append(pltpu.make_async_remote_copy(
                src_ref=src.at[_chip(px, py), rows], dst_ref=dst.at[me, rows], send_sem=send_sems.at[s],
                recv_sem=recv_sems.at[s], device_id=(px, py, c_), device_id_type=MESH_ID))
    return copies


class _Riders:
    def __init__(self, rides):
        self.rides = rides or []
        self.srcs = [s for r in self.rides for s in r[0]]
        self.dsts = [d for r in self.rides for d in r[1]]
        self.n = len(self.srcs)

    def operands(self):
        return (*self.srcs, *self.dsts)

    def in_specs(self):
        return [HBM_SPEC] * (2 * self.n)

    def out_specs(self):
        return [HBM_SPEC] * self.n

    def out_shape(self):
        return [jax.ShapeDtypeStruct(d.shape, d.dtype) for d in self.dsts]

    def scratch(self):
        if not self.n:
            return []
        n_sem = self.n * (N_CHIP - 1)
        return [pltpu.SemaphoreType.DMA((n_sem,)), pltpu.SemaphoreType.DMA((n_sem,)), pltpu.SemaphoreType.DMA((self.n,))]

    def aliases(self, first_in, first_out):
        return {first_in + self.n + p: first_out + p for p in range(self.n)}

    def hooks(self, grid, src_refs, dst_refs, sems):
        if not self.n:
            return (lambda: None), (lambda: None)

        def copies():
            out, base = [], 0
            for srcs, _, window in self.rides:
                k = len(srcs)
                out += _chip_exchange_copies(src_refs[base:base + k], dst_refs[base:base + k], window, *sems, base=base)
                base += k
            return out

        ids = [pl.program_id(ax) for ax in range(len(grid))]
        at_start = functools.reduce(jnp.logical_and, [i == 0 for i in ids])
        at_end = functools.reduce(jnp.logical_and, [i == g - 1 for i, g in zip(ids, grid)])

        def start():
            @pl.when(at_start)
            def _():
                for cp in copies():
                    cp.start()

        def wait():
            @pl.when(at_end)
            def _():
                for cp in copies():
                    cp.wait()

        return start, wait


def _mm(a, b, *, ta=False, tb=False, res=None, out_dtype=F32, epilogue=None, pre=None, riders=None, dest_cols=None,
        b_slots=None, name):
    if ta:
        K, M = a.shape
    else:
        M, K = a.shape
    if b_slots is not None:
        layer, n_layers, axis = b_slots
        a_blk, b_blk = b.shape[1] // n_layers, b.shape[2]
        w_shape = (N_DEV * a_blk, b_blk) if axis == 1 else (a_blk, N_DEV * b_blk)
    else:
        w_shape = b.shape
    N, K2 = w_shape if tb else w_shape[::-1]
    assert K == K2, (a.shape, b.shape, ta, tb)
    wide_f32 = K >= MM_TILE_K and F32 in (a.dtype, b.dtype)
    tm = _tile(M, (MM_TILE // 2 if wide_f32 and not ta else MM_TILE, 512, 256, 128))
    tn = _tile(N, (MM_TILE, 512, 256, 128)) if dest_cols is None else dest_cols
    tk = K if K <= MM_FULL_K else _tile(K, (MM_TILE_K, 1024, 512, 256, 128))
    n_b = 1
    if b_slots is not None:
        assert not ta
        sharded_is_k = (axis == 1) != tb
        slot = a_blk if axis == 1 else b_blk
        if sharded_is_k:
            n_b = _tile(N_DEV, [n for n in (8, 4, 2) if n * slot <= MM_TILE_K])
            tk = n_b * slot
        else:
            tn = slot
    nk = K // tk
    a_spec = pl.BlockSpec((tk, tm), lambda i, j, k: (k, i)) if ta else pl.BlockSpec((tm, tk), lambda i, j, k: (i, k))
    if b_slots is None:
        b_specs = [pl.BlockSpec((tn, tk), lambda i, j, k: (j, k)) if tb
                   else pl.BlockSpec((tk, tn), lambda i, j, k: (k, j))]
    elif axis == 1 and not tb:
        b_specs = [pl.BlockSpec((None, slot, tn), lambda i, j, k, q=q: (n_b * k + q, layer, j)) for q in range(n_b)]
    elif axis == 1:
        b_specs = [pl.BlockSpec((None, tn, tk), lambda i, j, k: (j, layer, k))]
    elif not tb:
        b_specs = [pl.BlockSpec((None, tk, tn), lambda i, j, k: (j, layer * (a_blk // tk) + k, 0))]
    else:
        b_specs = [pl.BlockSpec((None, tn, slot), lambda i, j, k, q=q: (n_b * k + q, layer * (a_blk // tn) + j, 0))
                   for q in range(n_b)]
    o_spec = pl.BlockSpec((tm, tn), lambda i, j, k: (i, j))
    dn = (((0 if ta else 1,), (1 if tb else 0,)), ((), ()))
    extra = [x for x in (res, pre) if x is not None]
    assert not (res is not None and pre is not None)
    assert dest_cols is None or (epilogue is None and not extra and tn <= MM_TILE)
    n_out = 2 if epilogue == 'relu_sq' else 1
    ride = _Riders(riders)
    n_pairs = ride.n
    grid = (M // tm, N // tn, nk)

    def body(*refs):
        a_ref, b_refs = refs[0], refs[1:1 + n_b]
        x_ref = refs[1 + n_b] if extra else None
        n_in = 1 + n_b + len(extra) + 2 * n_pairs
        o_refs = refs[n_in:n_in + n_out]
        scratch = refs[n_in + n_out + n_pairs:]
        acc = scratch[0] if nk > 1 else None
        start, wait = ride.hooks(grid, refs[n_in - 2 * n_pairs:n_in - n_pairs],
                                 refs[n_in + n_out:n_in + n_out + n_pairs], scratch[-3:])
        start()

        def finish(r):
            if res is not None:
                r = r + x_ref[...]
            if epilogue == 'relu_sq':
                o_refs[0][...] = r
                o_refs[1][...] = jnp.square(jnp.maximum(r, 0.0)).astype(out_dtype)
            elif epilogue == 'relu_sq_grad':
                o_refs[0][...] = (r * (2.0 * jnp.maximum(x_ref[...], 0.0))).astype(out_dtype)
            else:
                o_refs[0][...] = r.astype(out_dtype)

        dot = lambda x, y: lax.dot_general(x.astype(BF16), y[...].astype(BF16), dn, preferred_element_type=F32)
        if n_b == 1:
            part = dot(a_ref[...], b_refs[0])
        else:
            part = sum(dot(a_ref[:, q * slot:(q + 1) * slot], b_refs[q]) for q in range(n_b))
        if nk == 1:
            finish(part)
        else:
            k = pl.program_id(2)

            @pl.when(k == 0)
            def _():
                acc[...] = part

            @pl.when(k > 0)
            def _():
                acc[...] += part

            @pl.when(k == nk - 1)
            def _():
                finish(acc[...])

        wait()

    out_shape = [jax.ShapeDtypeStruct((M, N), F32 if epilogue == 'relu_sq' else out_dtype)]
    if n_out == 2:
        out_shape.append(jax.ShapeDtypeStruct((M, N), out_dtype))
    in_specs = [a_spec, *b_specs] + [o_spec] * len(extra)
    out_specs = [o_spec] * n_out
    if dest_cols is not None:
        out_shape = [jax.ShapeDtypeStruct((N // tn, M, tn), out_dtype)]
        out_specs = [pl.BlockSpec((None, tm, tn), lambda i, j, k: (j, i, 0))]
    scratch = ([pltpu.VMEM((tm, tn), F32)] if nk > 1 else []) + ride.scratch()
    args = (a, *[b] * n_b, *extra)
    outs = pl.pallas_call(
        body, grid=grid, in_specs=in_specs + ride.in_specs(), out_specs=out_specs + ride.out_specs(),
        out_shape=out_shape + ride.out_shape(), scratch_shapes=scratch,
        input_output_aliases=ride.aliases(len(args), n_out), name=name, compiler_params=_cp(3, VMEM_LIMIT_BIG))(
            *args, *ride.operands())
    return outs[0] if len(outs) == 1 else tuple(outs)


def _linear(name):
    @jax.custom_vjp
    def lin(a, w):
        return _mm(a, w, name=name + "_fwd")

    def fwd(a, w):
        return _mm(a, w, name=name + "_fwd"), (a, w)

    def bwd(r, dy):
        a, w = r
        da = _mm(dy, w, tb=True, name=name + "_dx")
        dw = _mm(a, dy, ta=True, out_dtype=w.dtype, name=name + "_dw")
        return da, dw

    lin.defvjp(fwd, bwd)
    return lin


def _act_linear(name, act, n_in, with_res):
    def run(*a):
        ins, w = a[:n_in], a[n_in]
        x = act.fwd_call(*ins)[0]
        return _mm(x, w, res=a[n_in + 1] if with_res else None, name=name + "_fwd"), (ins, x, w)

    @jax.custom_vjp
    def op(*a):
        return run(*a)[0]

    def bwd(r, dy):
        ins, x, w = r
        dx = _mm(dy, w, tb=True, name=name + "_dx")
        dw = _mm(x, dy, ta=True, out_dtype=w.dtype, name=name + "_dw")
        return (*act.bwd_all(ins, (dx,)), dw) + ((dy,) if with_res else ())

    op.defvjp(run, bwd)
    return op


def _mlp_fwd(name, norm, h, g, w_up, w_down, layer):
    hn = norm.fwd_call(h, g)[0]
    up, down = (layer, DEPTH, SHARDED['w_up']), (layer, DEPTH, SHARDED['w_down'])
    a, r = _mm(hn, w_up, epilogue='relu_sq', out_dtype=BF16, b_slots=up, name=name + "_up_fwd")
    return _mm(r, w_down, res=h, b_slots=down, name=name + "_down_fwd"), (h, g, hn, a, r, w_up, w_down, up, down)


RIDE_UNITS = 16
MLP_RIDE_UNITS = (3, 3, 3, 2)
ATT_RIDE_WINDOW = (sum(MLP_RIDE_UNITS), RIDE_UNITS - sum(MLP_RIDE_UNITS), RIDE_UNITS)


def _mlp_bwd(name, norm, saved, dy, pending, lands):
    h, g, hn, a, r, w_up, w_down, up, down = saved

    def mm(i, *args, **kw):
        nonlocal lands
        if pending is None:
            return _mm(*args, **kw)
        window = (sum(MLP_RIDE_UNITS[:i]), MLP_RIDE_UNITS[i], RIDE_UNITS)
        out, *lands = _mm(*args, riders=[(pending, lands, window)], **kw)
        return out

    da = mm(0, dy, w_down, tb=True, epilogue='relu_sq_grad', pre=a, out_dtype=BF16, b_slots=down,
            name=name + "_down_dx")
    dw_down = mm(1, r, dy, ta=True, out_dtype=w_down.dtype, name=name + "_down_dw")
    dhn = mm(2, da, w_up, tb=True, b_slots=up, name=name + "_up_dx")
    dw_up = mm(3, hn, da, ta=True, out_dtype=w_up.dtype, dest_cols=w_up.shape[2], name=name + "_up_dw")
    dh, dg = norm.bwd_all((h, g), (dhn,), add_to_first=dy)
    return dh, dg, dw_up, dw_down, lands


def _block_op(name, f, grid, ins, outs, vmem=VMEM_LIMIT):
    n_in, n_out = len(ins), len(outs)
    in_specs = [pl.BlockSpec(bs, im) for bs, im, _, _ in ins]
    out_specs = [pl.BlockSpec(bs, im) for _, _, bs, im in outs]
    out_shape = [jax.ShapeDtypeStruct(s, d) for s, d, _, _ in outs]
    didx = [i for i in range(n_in) if ins[i][3]]

    def fwd_call(*args):
        def body(*refs):
            res = f(*[r[...] for r in refs[:n_in]])
            for r, o in zip(refs[n_in:], res):
                r[...] = o.astype(r.dtype)

        return pl.pallas_call(body, grid=grid, in_specs=in_specs, out_specs=out_specs, out_shape=out_shape,
                              name=name + "_fwd", compiler_params=_cp(len(grid), vmem))(*args)

    def bwd_call(args, cts, add_to_first=None):
        n_add = 0 if add_to_first is None else 1

        def body(*refs):
            vals = [r[...] for r in refs[:n_in]]
            ct_refs = refs[n_in:n_in + n_out]
            g_refs = refs[n_in + n_out + n_add:]

            def fd(*dv):
                full = list(vals)
                for i, v in zip(didx, dv):
                    full[i] = v
                return f(*full)

            _, vjp = jax.vjp(fd, *[vals[i] for i in didx])
            grads = list(vjp(tuple(r[...] for r in ct_refs)))
            if n_add:
                grads[0] = grads[0] + refs[n_in + n_out][...]
            for gref, i, g in zip(g_refs, didx, grads):
                acc = ins[i][2]
                if acc:
                    first = functools.reduce(jnp.logical_and, [pl.program_id(ax) == 0 for ax in acc])

                    @pl.when(first)
                    def _(gref=gref):
                        gref[...] = jnp.zeros_like(gref)

                    gref[...] += g.astype(gref.dtype)
                else:
                    gref[...] = g.astype(gref.dtype)

        g_specs = [pl.BlockSpec(ins[i][0], ins[i][1]) for i in didx]
        g_shape = [jax.ShapeDtypeStruct(args[i].shape, args[i].dtype) for i in didx]
        assert not n_add or (didx[0] == 0 and not ins[0][2])
        added = [] if add_to_first is None else [add_to_first]
        return pl.pallas_call(body, grid=grid, in_specs=in_specs + out_specs + in_specs[:n_add], out_specs=g_specs,
                              out_shape=g_shape, name=name + "_bwd", compiler_params=_cp(len(grid), vmem))(
                                  *args, *cts, *added)

    @jax.custom_vjp
    def op(*args):
        return tuple(fwd_call(*args))

    def op_fwd(*args):
        return tuple(fwd_call(*args)), args

    def op_bwd(args, cts, add_to_first=None):
        it = iter(bwd_call(args, cts, add_to_first))
        return tuple(next(it) if ins[i][3] else jnp.zeros_like(args[i]) for i in range(n_in))

    op.defvjp(op_fwd, op_bwd)
    op.fwd_call = fwd_call
    op.bwd_all = op_bwd
    return op


def _row(tr, c):
    return ((tr, c), lambda i: (i, 0), None, True)


def _par(shape):
    nd = len(shape)
    return (shape, lambda i: (0,) * nd, (0,), True)


def _bdot(a, w):
    return jnp.dot(a.astype(BF16), w.astype(BF16), preferred_element_type=F32)


def _rms_f(x, g):
    return (x * lax.rsqrt(jnp.mean(x * x, axis=-1, keepdims=True) + NORM_EPS) * g,)


def _rms_op(name, R, D, out_dtype):
    tr = _tile(R, (256,))
    return _block_op(name, _rms_f, (R // tr,), [_row(tr, D), _par((1, D))],
                     [((R, D), out_dtype, (tr, D), lambda i: (i, 0))])


def rmsnorm(x, g, name):
    R, D = x.shape
    return _rms_op(name, R, D, F32)(x, g.reshape(1, D))[0]


def s5_epilogue(yc, u, d, w_glu, name):
    R, C = yc.shape
    tr = _tile(R, (256,))

    def f(yc, u, d, w):
        g = jax.nn.gelu(yc + d * u)
        return (g * jax.nn.sigmoid(_bdot(g, w)),)

    op = _block_op(name, f, (R // tr,), [_row(tr, C), _row(tr, C), _par((1, C)), _par((C, C))],
                   [((R, C), F32, (tr, C), lambda i: (i, 0))])
    return op(yc, u, d.reshape(1, C), w_glu)[0]


def pool_proj(p, w, scale, name):
    R, C = p.shape
    ng, pc, _ = w.shape
    tr = _tile(R, (256,))

    def f(p, w, s):
        ys = [_bdot(p[:, g * pc:(g + 1) * pc], w[g]) for g in range(ng)]
        return (jnp.concatenate(ys, axis=-1) * s,)

    op = _block_op(name, f, (R // tr,), [_row(tr, C), _par((ng, pc, pc)), _par((1, C))],
                   [((R, C), F32, (tr, C), lambda i: (i, 0))])
    return op(p, w, scale.reshape(1, C))[0]


def conv_post(h, ln_g, ln_b, w_pw, name):
    R, C = h.shape
    tr = _tile(R, (256,))

    def f(h, g, b, w):
        hc = h - jnp.mean(h, axis=-1, keepdims=True)
        y = hc * lax.rsqrt(jnp.mean(hc * hc, axis=-1, keepdims=True) + NORM_EPS) * g + b
        return (_bdot(jax.nn.silu(y), w),)

    op = _block_op(name, f, (R // tr,), [_row(tr, C), _par((1, C)), _par((1, C)), _par((C, C))],
                   [((R, C), F32, (tr, C), lambda i: (i, 0))])
    return op(h, ln_g.reshape(1, C), ln_b.reshape(1, C), w_pw)[0]


def _group_norm_op(name, R, C, n, out_dtype):
    tr = _tile(R, (256,))

    def f(*a):
        g = a[n]
        parts = [y * lax.rsqrt(jnp.mean(y * y, axis=-1, keepdims=True) + NORM_EPS) for y in a[:n]]
        return (jnp.concatenate(parts, axis=-1) * g,)

    return _block_op(name, f, (R // tr,), [_row(tr, C)] * n + [_par((1, n * C))],
                     [((R, n * C), out_dtype, (tr, n * C), lambda i: (i, 0))])


def _cross_attention_op(name, L, W, M, out_dtype):
    E = X_HEAD_DIM
    tq = _tile(L, (512,))

    def f(q, k, v):
        s = lax.dot_general(q.astype(BF16), k.astype(BF16), (((1,), (1,)), ((), ())),
                            preferred_element_type=F32) * (E ** -0.5)
        p = jax.nn.softmax(s, axis=-1)
        return (_bdot(p, v),)

    qspec = ((tq, E), lambda h, i: (i, h), None, True)
    kspec = ((M, E), lambda h, i: (0, h), (1,), True)
    return _block_op(name, f, (W // E, L // tq), [qspec, kspec, kspec],
                     [((L, W), out_dtype, (tq, E), lambda h, i: (i, h))])


def cross_attention(q, k, v, name):
    return _cross_attention_op(name, q.shape[0], q.shape[1], k.shape[0], F32)(q, k, v)[0]


def s5_discretise(lam_re, lam_im, log_dt, b_re_t, b_im_t, name):
    G, _, N = lam_re.shape
    C = b_re_t.shape[1]

    def f(lr, li, ldt, br, bi):
        dt = jnp.exp(ldt)
        mag = jnp.exp(lr * dt)
        ab_r, ab_i = mag * jnp.cos(li * dt), mag * jnp.sin(li * dt)
        den = lr * lr + li * li
        nr, ni = ab_r - 1.0, ab_i
        f_r = (nr * lr + ni * li) / den
        f_i = (ni * lr - nr * li) / den
        return ab_r, ab_i, f_r * br - f_i * bi, f_r * bi + f_i * br

    vec = ((G, 1, N), lambda i: (0, 0, 0), None, True)
    mat = ((G, C, N), lambda i: (0, 0, 0), None, True)
    ov = ((G, 1, N), F32, (G, 1, N), lambda i: (0, 0, 0))
    om = ((G, C, N), F32, (G, C, N), lambda i: (0, 0, 0))
    op = _block_op(name, f, (1,), [vec, vec, vec, mat, mat], [ov, ov, om, om])
    return op(lam_re, lam_im, log_dt, b_re_t, b_im_t)


def rel_bias_tables(rel_bias, onehot, name):
    B, H = rel_bias.shape
    P, _, Q = onehot.shape

    def f(rbt, oh):
        return (jnp.dot(rbt, oh, precision=lax.Precision.HIGHEST, preferred_element_type=F32),)

    op = _block_op(name, f, (P,), [((H, B), lambda p: (0, 0), (0,), True), ((None, B, Q), lambda p: (p, 0, 0), None, False)],
                   [((P, H, Q), F32, (None, H, Q), lambda p: (p, 0, 0))])
    return op(rel_bias.T, onehot)[0]


def _shift_down(x, s, row):
    return jnp.where(row >= s, pltpu.roll(x, s, 0), 0.0)


def _shift_up(x, s, row):
    n = x.shape[0]
    return jnp.where(row < n - s, pltpu.roll(x, n - s, 0), 0.0)


def _window_sum(x, w, row, shift):
    span = 1
    while span < w:
        x = x + shift(x, span, row)
        span *= 2
    return x


def _pool_call(u, d_out, name):
    L, C = u.shape
    pc = C // len(POOL_WINDOWS)
    assert pc % LANES == 0

    def body(x_ref, o_ref):
        row = lax.broadcasted_iota(jnp.int32, (L, pc), 0)
        for g, w in enumerate(POOL_WINDOWS):
            sl = slice(g * pc, (g + 1) * pc)
            x = x_ref[:, sl]
            cnt = jnp.minimum(row + 1, w).astype(F32)
            if d_out is None:
                o_ref[:, sl] = _window_sum(x, w, row, _shift_down) / cnt - x
            else:
                o_ref[:, sl] = _window_sum(x / cnt, w, row, _shift_up) - x

    src = u if d_out is None else d_out
    return pl.pallas_call(body, out_shape=jax.ShapeDtypeStruct((L, C), F32), name=name,
                          compiler_params=pltpu.CompilerParams(vmem_limit_bytes=VMEM_LIMIT))(src)


def _pool_mix(name):
    @jax.custom_vjp
    def op(u):
        return _pool_call(u, None, name + "_fwd")

    def fwd(u):
        return _pool_call(u, None, name + "_fwd"), u

    def bwd(u, dp):
        return (_pool_call(u, dp, name + "_bwd"),)

    op.defvjp(fwd, bwd)
    return op


def _conv_fwd(u, w, b, name):
    L, C2 = u.shape
    C = C2 // 2
    K = w.shape[0]
    nb = C // LANES

    def body(val_ref, gate_ref, w_ref, b_ref, o_ref):
        row = lax.broadcasted_iota(jnp.int32, (L, LANES), 0)
        h = val_ref[...] * jax.nn.sigmoid(gate_ref[...])
        acc = jnp.broadcast_to(b_ref[...], (L, LANES))
        for k in range(K):
            acc = acc + w_ref[k:k + 1, :] * _shift_down(h, K - 1 - k, row)
        o_ref[...] = acc

    blk = lambda off: pl.BlockSpec((L, LANES), lambda j: (0, j + off))
    return pl.pallas_call(
        body, grid=(nb,), in_specs=[blk(0), blk(nb), pl.BlockSpec((K, LANES), lambda j: (0, j)),
                                    pl.BlockSpec((1, LANES), lambda j: (0, j))],
        out_specs=blk(0), out_shape=jax.ShapeDtypeStruct((L, C), F32), name=name, compiler_params=_cp(1))(u, u, w, b)


def _conv_bwd(u, w, dh, name):
    L, C2 = u.shape
    C = C2 // 2
    K = w.shape[0]
    nb = C // LANES

    def body(val_ref, gate_ref, w_ref, dh_ref, dval_ref, dgate_ref, dw_ref, db_ref):
        row = lax.broadcasted_iota(jnp.int32, (L, LANES), 0)
        val = val_ref[...]
        sig = jax.nn.sigmoid(gate_ref[...])
        h = val * sig
        d = dh_ref[...]
        dh0 = jnp.zeros((L, LANES), F32)
        for k in range(K):
            s = K - 1 - k
            dh0 = dh0 + w_ref[k:k + 1, :] * _shift_up(d, s, row)
            dw_ref[k:k + 1, :] = jnp.sum(d * _shift_down(h, s, row), axis=0, keepdims=True)
        db_ref[...] = jnp.sum(d, axis=0, keepdims=True)
        dval_ref[...] = dh0 * sig
        dgate_ref[...] = dh0 * val * sig * (1.0 - sig)

    blk = lambda off: pl.BlockSpec((L, LANES), lambda j: (0, j + off))
    return pl.pallas_call(
        body, grid=(nb,), in_specs=[blk(0), blk(nb), pl.BlockSpec((K, LANES), lambda j: (0, j)), blk(0)],
        out_specs=[blk(0), blk(0), pl.BlockSpec((K, LANES), lambda j: (0, j)), pl.BlockSpec((1, LANES), lambda j: (0, j))],
        out_shape=[jax.ShapeDtypeStruct((L, C), F32), jax.ShapeDtypeStruct((L, C), F32),
                   jax.ShapeDtypeStruct((K, C), F32), jax.ShapeDtypeStruct((1, C), F32)],
        name=name, compiler_params=_cp(1))(u, u, w, dh)


def _glu_conv(name):
    @jax.custom_vjp
    def op(u, w, b):
        return _conv_fwd(u, w, b, name + "_fwd")

    def fwd(u, w, b):
        return _conv_fwd(u, w, b, name + "_fwd"), (u, w)

    def bwd(r, dh):
        u, w = r
        dval, dgate, dw, db = _conv_bwd(u, w, dh, name + "_bwd")
        return jnp.concatenate([dval, dgate], axis=-1), dw, db

    op.defvjp(fwd, bwd)
    return op


S5_BLOCK_CH = LANES
S5_BLOCK_ST = S5_BLOCK_CH // S5_CH_PER_GROUP * S5_STATE


def _s5_scan(br_ref, bi_ref, ar, ai, reverse):
    L, C = br_ref.shape
    T = SUBLANES
    row = lax.broadcasted_iota(jnp.int32, (T, C), 0)
    pw = [(ar, ai)]
    for _ in range(T - 1):
        pr, pi = pw[-1]
        pw.append((pr * ar - pi * ai, pr * ai + pi * ar))
    cr = jnp.zeros((T, C), F32)
    ci = jnp.zeros((T, C), F32)
    for r in range(T):
        e = (T - r) if reverse else (r + 1)
        cr = jnp.where(row == r, pw[e - 1][0], cr)
        ci = jnp.where(row == r, pw[e - 1][1], ci)
    steps = []
    s = 1
    while s < T:
        mask = (row < T - s) if reverse else (row >= s)
        steps.append((T - s if reverse else s, mask, pw[s - 1][0], pw[s - 1][1]))
        s *= 2
    nt = L // T
    last = 0 if reverse else T - 1

    def body(i, carry):
        kr, ki = carry
        t = (nt - 1 - i) if reverse else i
        off = pl.multiple_of(t * T, T)
        xr = br_ref[pl.ds(off, T), :]
        xi = bi_ref[pl.ds(off, T), :]
        for sh, mask, mr, mi in steps:
            sr = jnp.where(mask, pltpu.roll(xr, sh, 0), 0.0)
            si = jnp.where(mask, pltpu.roll(xi, sh, 0), 0.0)
            xr, xi = xr + mr * sr - mi * si, xi + mr * si + mi * sr
        xr, xi = xr + cr * kr - ci * ki, xi + cr * ki + ci * kr
        br_ref[pl.ds(off, T), :] = xr
        bi_ref[pl.ds(off, T), :] = xi
        return (jnp.broadcast_to(xr[last:last + 1, :], (T, C)), jnp.broadcast_to(xi[last:last + 1, :], (T, C)))

    z = jnp.zeros((T, C), F32)
    lax.fori_loop(0, nt, body, (z, z))


def _s5_specs(L):
    nb_axis = lambda j: (j, 0, 0)
    u = pl.BlockSpec((L, S5_BLOCK_CH), lambda j: (0, j))
    wb = pl.BlockSpec((None, S5_BLOCK_CH, S5_BLOCK_ST), nb_axis)
    a = pl.BlockSpec((1, S5_BLOCK_ST), lambda j: (0, j))
    wc = pl.BlockSpec((None, S5_BLOCK_ST, S5_BLOCK_CH), nb_axis)
    return u, wb, a, wc


def _s5_fwd(u, wbr, wbi, ar, ai, wcr, wci, name):
    L, C = u.shape
    nb = C // S5_BLOCK_CH
    us, wbs, as_, wcs = _s5_specs(L)

    def body(u_ref, wbr_ref, wbi_ref, ar_ref, ai_ref, wcr_ref, wci_ref, y_ref, xr, xi):
        ub = u_ref[...]
        xr[...] = _bdot(ub, wbr_ref[...])
        xi[...] = _bdot(ub, wbi_ref[...])
        _s5_scan(xr, xi, ar_ref[...], ai_ref[...], False)
        y_ref[...] = _bdot(xr[...], wcr_ref[...]) - _bdot(xi[...], wci_ref[...])

    return pl.pallas_call(
        body, grid=(nb,), in_specs=[us, wbs, wbs, as_, as_, wcs, wcs], out_specs=us,
        out_shape=jax.ShapeDtypeStruct((L, C), F32),
        scratch_shapes=[pltpu.VMEM((L, S5_BLOCK_ST), F32)] * 2, name=name, compiler_params=_cp(1))(
            u, wbr, wbi, ar, ai, wcr, wci)


def _dot_t(a, b):
    return lax.dot_general(a.astype(BF16), b.astype(BF16), (((0,), (0,)), ((), ())), preferred_element_type=F32)


def _dot_nt(a, b):
    return lax.dot_general(a.astype(BF16), b.astype(BF16), (((1,), (1,)), ((), ())), preferred_element_type=F32)


def _s5_bwd(u, wbr, wbi, ar, ai, wcr, wci, dy, name):
    L, C = u.shape
    nb = C // S5_BLOCK_CH
    us, wbs, as_, wcs = _s5_specs(L)
    T = SUBLANES

    def body(u_ref, wbr_ref, wbi_ref, ar_ref, ai_ref, wcr_ref, wci_ref, dy_ref,
             du_ref, dwbr_ref, dwbi_ref, dar_ref, dai_ref, dwcr_ref, dwci_ref, xr, xi, gr, gi):
        ub = u_ref[...]
        a_r, a_i = ar_ref[...], ai_ref[...]
        xr[...] = _bdot(ub, wbr_ref[...])
        xi[...] = _bdot(ub, wbi_ref[...])
        _s5_scan(xr, xi, a_r, a_i, False)
        d = dy_ref[...]
        dwcr_ref[...] = _dot_t(xr[...], d)
        dwci_ref[...] = -_dot_t(xi[...], d)
        gr[...] = _dot_nt(d, wcr_ref[...])
        gi[...] = -_dot_nt(d, wci_ref[...])
        _s5_scan(gr, gi, a_r, -a_i, True)

        row = lax.broadcasted_iota(jnp.int32, (T, S5_BLOCK_ST), 0)

        def da_body(i, carry):
            pr, pi, sr, si = carry
            off = pl.multiple_of(i * T, T)
            xr_t, xi_t = xr[pl.ds(off, T), :], xi[pl.ds(off, T), :]
            lr_t, li_t = gr[pl.ds(off, T), :], gi[pl.ds(off, T), :]
            qr = jnp.where(row == 0, pr, pltpu.roll(xr_t, 1, 0))
            qi = jnp.where(row == 0, pi, pltpu.roll(xi_t, 1, 0))
            sr = sr + qr * lr_t + qi * li_t
            si = si + qr * li_t - qi * lr_t
            return (jnp.broadcast_to(xr_t[T - 1:T, :], (T, S5_BLOCK_ST)),
                    jnp.broadcast_to(xi_t[T - 1:T, :], (T, S5_BLOCK_ST)), sr, si)

        z = jnp.zeros((T, S5_BLOCK_ST), F32)
        _, _, sr, si = lax.fori_loop(0, L // T, da_body, (z, z, z, z))
        dar_ref[...] = jnp.sum(sr, axis=0, keepdims=True)
        dai_ref[...] = jnp.sum(si, axis=0, keepdims=True)
        lr, li = gr[...], gi[...]
        dwbr_ref[...] = _dot_t(ub, lr)
        dwbi_ref[...] = _dot_t(ub, li)
        du_ref[...] = _dot_nt(lr, wbr_ref[...]) + _dot_nt(li, wbi_ref[...])

    sds = jax.ShapeDtypeStruct
    return pl.pallas_call(
        body, grid=(nb,), in_specs=[us, wbs, wbs, as_, as_, wcs, wcs, us],
        out_specs=[us, wbs, wbs, as_, as_, wcs, wcs],
        out_shape=[sds(u.shape, F32), sds(wbr.shape, F32), sds(wbi.shape, F32), sds(ar.shape, F32),
                   sds(ai.shape, F32), sds(wcr.shape, F32), sds(wci.shape, F32)],
        scratch_shapes=[pltpu.VMEM((L, S5_BLOCK_ST), F32)] * 4, name=name,
        compiler_params=_cp(1, VMEM_LIMIT_BIG))(u, wbr, wbi, ar, ai, wcr, wci, dy)


def _s5_core(name):
    @jax.custom_vjp
    def op(u, wbr, wbi, ar, ai, wcr, wci):
        return _s5_fwd(u, wbr, wbi, ar, ai, wcr, wci, name + "_fwd")

    def fwd(*a):
        return _s5_fwd(*a, name + "_fwd"), a

    def bwd(a, dy):
        return tuple(_s5_bwd(*a, dy, name + "_bwd"))

    op.defvjp(fwd, bwd)
    return op


def _att_tile_f(first, q, kp, kc, vp, vc, bias):
    nq = q.shape[0]
    hb = bias.shape[0]
    E = q.shape[1] // hb
    r = lax.broadcasted_iota(jnp.int32, (nq, 2 * nq), 0)
    c = lax.broadcasted_iota(jnp.int32, (nq, 2 * nq), 1)
    prev_ok = jnp.logical_and(jnp.logical_and(c < nq, c >= r), jnp.logical_not(first))
    valid = jnp.logical_or(prev_ok, jnp.logical_and(c >= nq, c - nq <= r))
    lane = lax.broadcasted_iota(jnp.int32, (1, hb * E), 1)
    k = jnp.concatenate([kp, kc], axis=0)
    v = jnp.concatenate([vp, vc], axis=0)
    o = jnp.zeros((nq, hb * E), F32)
    lse = jnp.zeros((nq, hb * E), F32)
    for h in range(hb):
        mine = jnp.logical_and(lane >= h * E, lane < (h + 1) * E)
        s = jnp.where(valid, _dot_nt(jnp.where(mine, q, 0.0), k) * (E ** -0.5) + bias[h], NEG_INF)
        m = jnp.max(s, axis=-1, keepdims=True)
        p = jnp.exp(s - m)
        den = jnp.sum(p, axis=-1, keepdims=True)
        o = jnp.where(mine, _bdot(p, v) / den, o)
        lse = jnp.where(mine, m + jnp.log(den), lse)
    return o, lse


def _att_tile_grad(first, q, kp, kc, vp, vc, bias, o, lse, do, dlse):
    nq = q.shape[0]
    hb = bias.shape[0]
    E = q.shape[1] // hb
    scale = E ** -0.5
    r = lax.broadcasted_iota(jnp.int32, (nq, 2 * nq), 0)
    c = lax.broadcasted_iota(jnp.int32, (nq, 2 * nq), 1)
    prev_ok = jnp.logical_and(jnp.logical_and(c < nq, c >= r), jnp.logical_not(first))
    valid = jnp.logical_or(prev_ok, jnp.logical_and(c >= nq, c - nq <= r))
    lane = lax.broadcasted_iota(jnp.int32, (1, hb * E), 1)
    k = jnp.concatenate([kp, kc], axis=0)
    v = jnp.concatenate([vp, vc], axis=0)
    dq = jnp.zeros((nq, hb * E), F32)
    dk = jnp.zeros((2 * nq, hb * E), F32)
    dv = jnp.zeros((2 * nq, hb * E), F32)
    db = []
    for h in range(hb):
        mine = jnp.logical_and(lane >= h * E, lane < (h + 1) * E)
        qh = jnp.where(mine, q, 0.0)
        doh = jnp.where(mine, do, 0.0)
        s = jnp.where(valid, _dot_nt(qh, k) * scale + bias[h], NEG_INF)
        p = jnp.exp(s - jnp.max(jnp.where(mine, lse, NEG_INF), axis=-1, keepdims=True))
        row = jnp.sum(jnp.where(mine, dlse, 0.0) - doh * o, axis=-1, keepdims=True)
        ds = p * (_dot_nt(doh, v) + row)
        db.append(ds)
        dv = dv + _dot_t(p, doh)
        dk = dk + _dot_t(ds, qh) * scale
        dq = jnp.where(mine, _bdot(ds, k) * scale, dq)
    return dq, dk[:nq], dk[nq:], dv[:nq], dv[nq:], db


def _att_mix_f(*a):
    n = len(a) // 2
    o, l = a[:n], a[n:]
    m = functools.reduce(jnp.maximum, l)
    e = [jnp.exp(li - m) for li in l]
    return sum(ei * oi for ei, oi in zip(e, o)) / sum(e)


def _att_rows(start, dil):
    if dil == 1:
        return pl.ds(pl.multiple_of(start, ATT_BLOCK), ATT_BLOCK)
    return pl.ds(start, ATT_BLOCK, stride=dil)


def _att_blocks(L, dil):
    nb = L // dil // ATT_BLOCK
    return dil * nb, nb


def _att_specs(L, W):
    nblk = W // LANES
    col = lambda off: pl.BlockSpec((L, LANES), lambda j: (0, j + off))
    per_pattern = pl.BlockSpec((len(DILATED_PATTERNS), L, LANES), lambda j: (0, 0, j))
    hb = ATT_HEADS // nblk
    bias = pl.BlockSpec((len(DILATED_PATTERNS), hb, ATT_BLOCK, 2 * ATT_BLOCK), lambda j: (0, j, 0, 0))
    return nblk, col, per_pattern, bias


def _att_fwd(qkv, bias, name):
    L, W3 = qkv.shape
    W = W3 // 3
    nblk, col, per_pattern, bias_spec = _att_specs(L, W)
    P = len(DILATED_PATTERNS)

    def body(q_ref, k_ref, v_ref, b_ref, y_ref, o_ref, l_ref):
        for p, (_, dil) in enumerate(DILATED_PATTERNS):
            n_it, nb = _att_blocks(L, dil)

            def step(i, carry, p=p, dil=dil, nb=nb):
                n = i % nb
                cur = i // nb + n * (ATT_BLOCK * dil)
                prev = i // nb + jnp.maximum(n - 1, 0) * (ATT_BLOCK * dil)
                rc, rp = _att_rows(cur, dil), _att_rows(prev, dil)
                o, l = _att_tile_f(n == 0, q_ref[rc, :], k_ref[rp, :], k_ref[rc, :], v_ref[rp, :], v_ref[rc, :],
                                   b_ref[p])
                o_ref[p, rc, :] = o
                l_ref[p, rc, :] = l
                return carry

            lax.fori_loop(0, n_it, step, 0, unroll=ATT_UNROLL)

        def mix(i, carry):
            rows = pl.ds(pl.multiple_of(i * ATT_MIX_ROWS, ATT_MIX_ROWS), ATT_MIX_ROWS)
            y_ref[rows, :] = _att_mix_f(*[o_ref[p, rows, :] for p in range(P)], *[l_ref[p, rows, :] for p in range(P)])
            return carry

        lax.fori_loop(0, L // ATT_MIX_ROWS, mix, 0)

    sds = jax.ShapeDtypeStruct
    return pl.pallas_call(
        body, grid=(nblk,), in_specs=[col(0), col(nblk), col(2 * nblk), bias_spec],
        out_specs=[col(0), per_pattern, per_pattern],
        out_shape=[sds((L, W), F32), sds((P, L, W), F32), sds((P, L, W), F32)], name=name,
        compiler_params=_cp(1))(qkv, qkv, qkv, bias)


def _att_bwd(qkv, bias, o_all, l_all, dy, name, riders=None):
    L, W3 = qkv.shape
    W = W3 // 3
    nblk, col, per_pattern, bias_spec = _att_specs(L, W)
    P = len(DILATED_PATTERNS)
    ride = _Riders(riders)
    n_in, n_out = 7, 4

    def body(*refs):
        q_ref, k_ref, v_ref, b_ref, o_ref, l_ref, dy_ref = refs[:n_in]
        outs = refs[n_in + 2 * ride.n:]
        dq_ref, dk_ref, dv_ref, db_ref = outs[:n_out]
        do_s, dl_s = outs[n_out + ride.n:n_out + ride.n + 2]
        start, wait = ride.hooks((nblk,), refs[n_in:n_in + ride.n], outs[n_out:n_out + ride.n],
                                 outs[n_out + ride.n + 2:])
        start()

        def mix(i, carry):
            rows = pl.ds(pl.multiple_of(i * ATT_MIX_ROWS, ATT_MIX_ROWS), ATT_MIX_ROWS)
            _, mix_vjp = jax.vjp(_att_mix_f, *[o_ref[p, rows, :] for p in range(P)],
                                 *[l_ref[p, rows, :] for p in range(P)])
            g = mix_vjp(dy_ref[rows, :])
            for p in range(P):
                do_s[p, rows, :] = g[p]
                dl_s[p, rows, :] = g[P + p]
            return carry

        lax.fori_loop(0, L // ATT_MIX_ROWS, mix, 0)
        for ref in (dq_ref, dk_ref, dv_ref, db_ref):
            ref[...] = jnp.zeros_like(ref)

        def add(ref, rows, val):
            ref[rows, :] = ref[rows, :] + val

        for p, (_, dil) in enumerate(DILATED_PATTERNS):
            n_it, nb = _att_blocks(L, dil)

            def step(i, carry, p=p, dil=dil, nb=nb):
                n = i % nb
                first = n == 0
                cur = i // nb + n * (ATT_BLOCK * dil)
                prev = i // nb + jnp.maximum(n - 1, 0) * (ATT_BLOCK * dil)
                rc, rp = _att_rows(cur, dil), _att_rows(prev, dil)
                dq, dkp, dkc, dvp, dvc, db = _att_tile_grad(
                    first, q_ref[rc, :], k_ref[rp, :], k_ref[rc, :], v_ref[rp, :], v_ref[rc, :], b_ref[p],
                    o_ref[p, rc, :], l_ref[p, rc, :], do_s[p, rc, :], dl_s[p, rc, :])
                add(dq_ref, rc, dq)
                add(dk_ref, rc, dkc)
                add(dv_ref, rc, dvc)
                for h, dbh in enumerate(db):
                    db_ref[p, h] = db_ref[p, h] + dbh

                @pl.when(jnp.logical_not(first))
                def _():
                    add(dk_ref, rp, dkp)
                    add(dv_ref, rp, dvp)

                return carry

            lax.fori_loop(0, n_it, step, 0, unroll=ATT_UNROLL)
        wait()

    sds = jax.ShapeDtypeStruct((L, W), F32)
    return pl.pallas_call(
        body, grid=(nblk,),
        in_specs=[col(0), col(nblk), col(2 * nblk), bias_spec, per_pattern, per_pattern, col(0)] + ride.in_specs(),
        out_specs=[col(0), col(0), col(0), bias_spec] + ride.out_specs(),
        out_shape=[sds, sds, sds, jax.ShapeDtypeStruct(bias.shape, F32)] + ride.out_shape(),
        scratch_shapes=[pltpu.VMEM((P, L, LANES), F32)] * 2 + ride.scratch(),
        input_output_aliases=ride.aliases(n_in, n_out), name=name, compiler_params=_cp(1, VMEM_LIMIT_BIG))(
            qkv, qkv, qkv, bias, o_all, l_all, dy, *ride.operands())


def _t5_bucket(dist):
    n = np.maximum(dist, 0)
    max_exact = REL_BUCKETS // 2
    large = max_exact + (np.log(np.maximum(n, 1) / max_exact) / np.log(REL_MAX_DIST / max_exact)
                         * (REL_BUCKETS - max_exact)).astype(np.int64)
    large = np.minimum(large, REL_BUCKETS - 1)
    return np.where(n < max_exact, n, large).astype(np.int32)


def _bucket_onehot():
    a = np.arange(ATT_BLOCK)[:, None]
    b = np.arange(2 * ATT_BLOCK)[None, :]
    sub = a + ATT_BLOCK - b
    bucket = jnp.asarray(np.stack([_t5_bucket(sub * dil).reshape(-1) for _, dil in DILATED_PATTERNS]))
    ids = jnp.arange(REL_BUCKETS, dtype=jnp.int32)
    return (bucket[:, None, :] == ids[None, :, None]).astype(F32)


def loss_head(h, target, g, name):
    R, D = h.shape
    tr = _tile(R, (256,))

    def body(h_ref, t_ref, g_ref, l_ref, dh_ref, dg_ref):
        def lf(hv, gv):
            y = _rms_f(hv, gv)[0]
            return 0.5 * jnp.sum(jnp.mean(jnp.square(y - t_ref[...]), axis=-1))

        l, (dh, dg) = jax.value_and_grad(lf, argnums=(0, 1))(h_ref[...], g_ref[...])

        @pl.when(pl.program_id(0) == 0)
        def _():
            l_ref[...] = jnp.zeros_like(l_ref)
            dg_ref[...] = jnp.zeros_like(dg_ref)

        dh_ref[...] = dh
        dg_ref[...] += dg
        l_ref[...] += l

    rows = pl.BlockSpec((tr, D), lambda i: (i, 0))
    vec = pl.BlockSpec((1, D), lambda i: (0, 0))
    l, dh, dg = pl.pallas_call(
        body, grid=(R // tr,), in_specs=[rows, rows, vec],
        out_specs=[pl.BlockSpec((SUBLANES, LANES), lambda i: (0, 0)), rows, vec],
        out_shape=[jax.ShapeDtypeStruct((SUBLANES, LANES), F32), jax.ShapeDtypeStruct((R, D), F32),
                   jax.ShapeDtypeStruct((1, D), F32)], name=name, compiler_params=_cp(1))(h, target, g.reshape(1, D))
    return l[0, 0], dh, dg.reshape(D)


def _adamw_update(w, g, m, v):
    c1 = 1.0 - ADAM_B1 ** ADAM_STEP
    c2 = 1.0 - ADAM_B2 ** ADAM_STEP
    nm = ADAM_B1 * m + (1.0 - ADAM_B1) * g
    nv = ADAM_B2 * v + (1.0 - ADAM_B2) * jnp.square(g)
    return -ADAM_LR * ((nm / c1) / (jnp.sqrt(nv / c2) + ADAM_EPS) + ADAM_WD * w), nm, nv


def adamw_layers(w, parts, m, v, name):
    nl, a, b = w.shape
    n_parts = parts[0].shape[0]
    tr = _row_tile(a, b)

    def body(*refs):
        w_ref, p_refs, (m_ref, v_ref, g_ref, d_ref, nm_ref, nv_ref) = refs[0], refs[1:1 + nl], refs[1 + nl:]
        for l in range(nl):
            @pl.when(pl.program_id(0) == l)
            def _(p_ref=p_refs[l]):
                g = p_ref[0].astype(F32)
                for i in range(1, n_parts):
                    g = g + p_ref[i].astype(F32)
                d_ref[...], nm_ref[...], nv_ref[...] = _adamw_update(w_ref[...], g, m_ref[...], v_ref[...])
                g_ref[...] = g

    rows = pl.BlockSpec((None, tr, b), lambda l, i: (l, i, 0))
    part = lambda k: pl.BlockSpec((n_parts, tr, b), lambda l, i: (0, jnp.where(l == k, i, 0), 0))
    sds = jax.ShapeDtypeStruct((nl, a, b), F32)
    return pl.pallas_call(body, grid=(nl, a // tr), in_specs=[rows] + [part(k) for k in range(nl)] + [rows, rows],
                          out_specs=[rows] * 4, out_shape=[sds] * 4, name=name, compiler_params=_cp(2))(
                              w, *parts, m, v)


def adamw(w, parts, m, v, name):
    R, C = w.shape
    n_parts = parts.shape[0]
    tr = _row_tile(R, C)

    def body(w_ref, p_ref, m_ref, v_ref, g_ref, d_ref, nm_ref, nv_ref):
        g = p_ref[0].astype(F32)
        for i in range(1, n_parts):
            g = g + p_ref[i].astype(F32)
        d_ref[...], nm_ref[...], nv_ref[...] = _adamw_update(w_ref[...], g, m_ref[...], v_ref[...])
        g_ref[...] = g

    rows = pl.BlockSpec((tr, C), lambda i: (i, 0))
    sds = jax.ShapeDtypeStruct((R, C), F32)
    return pl.pallas_call(body, grid=(R // tr,),
                          in_specs=[rows, pl.BlockSpec((n_parts, tr, C), lambda i: (0, i, 0)), rows, rows],
                          out_specs=[rows] * 4, out_shape=[sds] * 4, name=name, compiler_params=_cp(1))(w, parts, m, v)


HBM_SPEC = pl.BlockSpec(memory_space=pltpu.HBM)
MESH_ID = pl.DeviceIdType.MESH


def _place():
    return lax.axis_index("x"), lax.axis_index("y"), lax.axis_index("c")


def _index(x, y, c):
    return 4 * x + 2 * y + c


AG_COPIES = 9
AG_ROW_UNIT = 32


def all_gather(x, name):
    R, C = x.shape
    assert R % AG_ROW_UNIT == 0, x.shape
    half = R // 2

    def body(x_ref, out_ref, send_sems, recv_sems, local_sem):
        x_, y_, c_ = _place()
        me, sib = (x_, y_, c_), (x_, y_, 1 - c_)
        nx, ny, nd = (1 - x_, y_, c_), (x_, 1 - y_, c_), (1 - x_, 1 - y_, c_)
        upper, lower = pl.ds(0, half), pl.ds(half, half)

        def slot(dev, rows=None):
            ref = out_ref.at[_index(*dev)]
            return ref if rows is None else ref.at[rows]

        def copy(k, block, to, rows=None, src=None):
            return pltpu.make_async_remote_copy(
                src_ref=slot(block, rows) if src is None else src, dst_ref=slot(block, rows),
                send_sem=send_sems.at[k], recv_sem=recv_sems.at[k], device_id=to, device_id_type=MESH_ID)

        def other(dev):
            return (dev[0], dev[1], 1 - c_)

        mine = pltpu.make_async_copy(x_ref, slot(me), local_sem)
        mine.start()
        sent = [copy(0, me, sib, src=x_ref), copy(1, me, nx, src=x_ref), copy(2, me, ny, src=x_ref)]
        for cp in sent:
            cp.start()

        def then(arrival, *forwards):
            arrival.wait_recv()
            for cp in forwards:
                cp.start()
            sent.extend(forwards)

        then(copy(1, nx, me), copy(4, nx, ny, upper), copy(5, nx, sib))
        then(copy(2, ny, me), copy(3, ny, nx, lower), copy(6, ny, sib))
        then(copy(3, nd, me, lower), copy(8, nd, sib, lower))
        then(copy(4, nd, me, upper), copy(7, nd, sib, upper))
        copy(0, sib, me).wait_recv()
        copy(5, other(nx), me).wait_recv()
        copy(6, other(ny), me).wait_recv()
        copy(7, other(nd), me, upper).wait_recv()
        copy(8, other(nd), me, lower).wait_recv()
        for cp in sent:
            cp.wait_send()
        mine.wait()

    return pl.pallas_call(
        body, out_shape=jax.ShapeDtypeStruct((N_DEV,) + x.shape, x.dtype), in_specs=[HBM_SPEC], out_specs=HBM_SPEC,
        scratch_shapes=[pltpu.SemaphoreType.DMA((AG_COPIES,)), pltpu.SemaphoreType.DMA((AG_COPIES,)),
                        pltpu.SemaphoreType.DMA], name=name)(x)


def _chip(x, y):
    return 2 * x + y


def sibling_exchange(xs, name):
    n = len(xs)

    def body(*refs):
        x_refs, out_refs, (send_sems, recv_sems) = refs[:n], refs[n:2 * n], refs[2 * n:]
        x_, y_, c_ = _place()
        copies = [pltpu.make_async_remote_copy(src_ref=x_ref.at[:, 1 - c_], dst_ref=out_ref, send_sem=send_sems.at[p],
                                               recv_sem=recv_sems.at[p], device_id=(x_, y_, 1 - c_),
                                               device_id_type=MESH_ID)
                  for p, (x_ref, out_ref) in enumerate(zip(x_refs, out_refs))]
        for cp in copies:
            cp.start()
        for cp in copies:
            cp.wait()

    return pl.pallas_call(
        body, out_shape=[jax.ShapeDtypeStruct((x.shape[0],) + x.shape[2:], x.dtype) for x in xs],
        in_specs=[HBM_SPEC] * n, out_specs=[HBM_SPEC] * n,
        scratch_shapes=[pltpu.SemaphoreType.DMA((n,)), pltpu.SemaphoreType.DMA((n,))], name=name)(*xs)


def _row_tile(R, C):
    cap = max(SUBLANES, STREAM_BLOCK_BYTES // (4 * C))
    return _tile(R, [t for t in (512, 256, 128, 64, 32, 16, 8) if t <= cap])


def pair_sum(x, recv, name):
    nc, _, R, C = x.shape
    tr = _row_tile(R, C)
    core = lax.axis_index("c").astype(jnp.int32).reshape(1)

    def body(c_ref, a_ref, b_ref, o_ref):
        o_ref[...] = (a_ref[...].astype(F32) + b_ref[...].astype(F32)).astype(o_ref.dtype)

    blk = pl.BlockSpec((None, tr, C), lambda k, i, c_ref: (k, i, 0))
    grid_spec = pltpu.PrefetchScalarGridSpec(
        num_scalar_prefetch=1, grid=(nc, R // tr),
        in_specs=[pl.BlockSpec((None, None, tr, C), lambda k, i, c_ref: (k, c_ref[0], i, 0)), blk], out_specs=blk)
    return pl.pallas_call(body, grid_spec=grid_spec, out_shape=jax.ShapeDtypeStruct((nc, R, C), x.dtype), name=name,
                          compiler_params=_cp(2))(core, x, recv)


def chip_exchange(ss, name):
    n = len(ss)

    def body(*refs):
        copies = _chip_exchange_copies(refs[:n], refs[n:2 * n], (0, 1, 1), *refs[2 * n:])
        for cp in copies:
            cp.start()
        for cp in copies:
            cp.wait()

    n_sem = n * (N_CHIP - 1)
    return pl.pallas_call(
        body, out_shape=[jax.ShapeDtypeStruct(s.shape, s.dtype) for s in ss], in_specs=[HBM_SPEC] * n,
        out_specs=[HBM_SPEC] * n,
        scratch_shapes=[pltpu.SemaphoreType.DMA((n_sem,)), pltpu.SemaphoreType.DMA((n_sem,)),
                        pltpu.SemaphoreType.DMA((n,))], name=name)(*ss)


def _block_diag(w, nb):
    G, a, b = w.shape
    gp = G // nb
    eye = jnp.eye(gp, dtype=w.dtype)
    return jnp.einsum('jgab,gh->jgahb', w.reshape(nb, gp, a, b), eye).reshape(nb, gp * a, gp * b)


def _split_columns(x, cuts):
    edges = (0,) + tuple(cuts) + (x.shape[1],)

    def split(x):
        return tuple(x[:, a:b] for a, b in zip(edges[:-1], edges[1:]))

    op = jax.custom_vjp(split)
    op.defvjp(lambda x: (split(x), None), lambda _, cts: (jnp.concatenate(cts, axis=-1),))
    return op(x)


def _project_in(l, h, P):
    L, D = h.shape
    GW = D // N_MIXERS
    proj = _act_linear(f"l{l}_w_in", _rms_op(f"l{l}_norm_mix", L, D, BF16), 2, False)(
        h, P['norm_mix_g'].reshape(1, D), P['w_in'])
    return _split_columns(proj, (GW, 2 * GW, 4 * GW))


def _mix_and_memory(l, u_a, u_b, u_c, y_d, h, memn, P):
    nm = lambda s: f"l{l}_{s}"
    L, D = h.shape
    GW = D // N_MIXERS
    G = GW // S5_CH_PER_GROUP

    row = lambda g: g.reshape(1, D)

    v3 = lambda a: a.reshape(G, 1, S5_STATE)
    log_dt = jnp.broadcast_to(P['s5_log_dt'][:, None, None], (G, 1, S5_STATE))
    a_r, a_i, bb_r, bb_i = s5_discretise(v3(P['s5_lam_re']), v3(P['s5_lam_im']), log_dt,
                                         P['s5_b_re'].transpose(0, 2, 1), P['s5_b_im'].transpose(0, 2, 1), nm("s5_disc"))
    nblk = GW // S5_BLOCK_CH
    y_s5 = _s5_core(nm("s5_core"))(
        u_a, _block_diag(bb_r, nblk), _block_diag(bb_i, nblk), a_r.reshape(1, G * S5_STATE), a_i.reshape(1, G * S5_STATE),
        _block_diag(P['s5_c_re'].transpose(0, 2, 1), nblk), _block_diag(P['s5_c_im'].transpose(0, 2, 1), nblk))
    y_a = s5_epilogue(y_s5, u_a, P['s5_d'], P['s5_w_glu'], nm("s5_glu"))

    y_b = pool_proj(_pool_mix(nm("pool_mix"))(u_b), P['pool_w'], P['pool_scale'], nm("pool_proj"))

    hc = _glu_conv(nm("conv_dw"))(u_c, P['conv_w_dw'], P['conv_b_dw'].reshape(1, GW))
    y_c = conv_post(hc, P['conv_ln_g'], P['conv_ln_b'], P['conv_w_pw'], nm("conv_post"))

    grp = _group_norm_op(nm("grp_norm"), L, GW, N_MIXERS, BF16)
    h = _act_linear(nm("w_out"), grp, N_MIXERS + 1, True)(y_a, y_b, y_c, y_d, row(P['grp_norm_g']), P['w_out'], h)

    xq = _act_linear(nm("w_xq"), _rms_op(nm("norm_x"), L, D, BF16), 2, False)(h, row(P['norm_x_g']), P['w_xq'])
    xk = _linear(nm("w_xk"))(memn, P['w_xk'])
    xv = _linear(nm("w_xv"))(memn, P['w_xv'])
    xat = _cross_attention_op(nm("xattn"), L, xq.shape[1], memn.shape[0], BF16)
    return _act_linear(nm("w_xo"), xat, 3, True)(xq, xk, xv, P['w_xo'], h)


def _bias_tables(rel_bias):
    tabs = rel_bias_tables(rel_bias, _bucket_onehot(), "rel_bias")
    return tabs.reshape(len(DILATED_PATTERNS), ATT_HEADS, ATT_BLOCK, 2 * ATT_BLOCK)


def _gather_weight(name, w):
    ax = SHARDED[name]
    dt = BF16 if name in GATHER_BF16 else F32
    nl, a, b = w.shape
    rows = nl * a
    flat = jnp.pad(w.astype(dt).reshape(rows, b), ((0, (-rows) % AG_ROW_UNIT), (0, 0)))
    g = all_gather(flat, "ag_" + name)
    if name in MLP_SHARDED:
        assert rows % AG_ROW_UNIT == 0
        return g
    g = g[:, :rows].reshape(N_DEV, nl, a, b)
    if ax == 1:
        return g.transpose(1, 0, 2, 3).reshape(nl, N_DEV * a, b)
    return g.transpose(1, 2, 0, 3).reshape(nl, a, N_DEV * b)


def _scatter_grad(name, g):
    ax = SHARDED[name]
    nl = g.shape[0]
    if ax == 1:
        a, b = g.shape[1] // N_DEV, g.shape[2]
        s = g.reshape(nl, N_DEV, a, b).transpose(1, 0, 2, 3)
    else:
        a, b = g.shape[1], g.shape[2] // N_DEV
        s = g.reshape(nl, a, N_DEV, b).transpose(2, 0, 1, 3)
    s = s.reshape(N_CHIP, N_DEV // N_CHIP, nl * a, b)
    pair = pair_sum(s, sibling_exchange([s], "d2d_" + name)[0], "pairsum_" + name)
    return chip_exchange([pair], "ici_" + name)[0]


def _by_destination(name, g):
    if g.ndim == 2 and SHARDED[name] == 1:
        g = g.reshape(N_DEV, g.shape[0] // N_DEV, g.shape[1])
    elif g.ndim == 2:
        g = g.reshape(g.shape[0], N_DEV, g.shape[1] // N_DEV).transpose(1, 0, 2)
    return g.reshape(N_CHIP, N_DEV // N_CHIP, *g.shape[1:])


def _flatten_small(d):
    flat = jnp.concatenate([d[n].reshape(-1).astype(F32) for n in SMALL])
    pad = (-flat.shape[0]) % (LANES * SMALL_ROW_TILE)
    return jnp.pad(flat, (0, pad)).reshape(-1, LANES)


def _split_small(flat, like):
    flat = flat.reshape(-1)
    out, off = {}, 0
    for n in SMALL:
        sz = math.prod(like[n].shape)
        out[n] = flat[off:off + sz].reshape(like[n].shape)
        off += sz
    return out


def kernel(x, mem, rel_bias, mem_norm_g, norm_mix_g, w_in, s5_lam_re, s5_lam_im, s5_log_dt, s5_b_re, s5_b_im, s5_c_re, s5_c_im, s5_d, s5_w_glu, pool_w, pool_scale, conv_w_dw, conv_b_dw, conv_ln_g, conv_ln_b, conv_w_pw, grp_norm_g, w_out, norm_x_g, w_xq, w_xk, w_xv, w_xo, norm_mlp_g, w_up, w_down, norm_final_g, loss_target, m_rel_bias, m_mem_norm_g, m_norm_mix_g, m_w_in, m_s5_lam_re, m_s5_lam_im, m_s5_log_dt, m_s5_b_re, m_s5_b_im, m_s5_c_re, m_s5_c_im, m_s5_d, m_s5_w_glu, m_pool_w, m_pool_scale, m_conv_w_dw, m_conv_b_dw, m_conv_ln_g, m_conv_ln_b, m_conv_w_pw, m_grp_norm_g, m_w_out, m_norm_x_g, m_w_xq, m_w_xk, m_w_xv, m_w_xo, m_norm_mlp_g, m_w_up, m_w_down, m_norm_final_g, v_rel_bias, v_mem_norm_g, v_norm_mix_g, v_w_in, v_s5_lam_re, v_s5_lam_im, v_s5_log_dt, v_s5_b_re, v_s5_b_im, v_s5_c_re, v_s5_c_im, v_s5_d, v_s5_w_glu, v_pool_w, v_pool_scale, v_conv_w_dw, v_conv_b_dw, v_conv_ln_g, v_conv_ln_b, v_conv_w_pw, v_grp_norm_g, v_w_out, v_norm_x_g, v_w_xq, v_w_xk, v_w_xv, v_w_xo, v_norm_mlp_g, v_w_up, v_w_down, v_norm_final_g):
    w = dict(zip(WEIGHTS, (rel_bias, mem_norm_g, norm_mix_g, w_in, s5_lam_re, s5_lam_im, s5_log_dt, s5_b_re, s5_b_im, s5_c_re, s5_c_im, s5_d, s5_w_glu, pool_w, pool_scale, conv_w_dw, conv_b_dw, conv_ln_g, conv_ln_b, conv_w_pw, grp_norm_g, w_out, norm_x_g, w_xq, w_xk, w_xv, w_xo, norm_mlp_g, w_up, w_down, norm_final_g)))
    m = dict(zip(WEIGHTS, (m_rel_bias, m_mem_norm_g, m_norm_mix_g, m_w_in, m_s5_lam_re, m_s5_lam_im, m_s5_log_dt, m_s5_b_re, m_s5_b_im, m_s5_c_re, m_s5_c_im, m_s5_d, m_s5_w_glu, m_pool_w, m_pool_scale, m_conv_w_dw, m_conv_b_dw, m_conv_ln_g, m_conv_ln_b, m_conv_w_pw, m_grp_norm_g, m_w_out, m_norm_x_g, m_w_xq, m_w_xk, m_w_xv, m_w_xo, m_norm_mlp_g, m_w_up, m_w_down, m_norm_final_g)))
    v = dict(zip(WEIGHTS, (v_rel_bias, v_mem_norm_g, v_norm_mix_g, v_w_in, v_s5_lam_re, v_s5_lam_im, v_s5_log_dt, v_s5_b_re, v_s5_b_im, v_s5_c_re, v_s5_c_im, v_s5_d, v_s5_w_glu, v_pool_w, v_pool_scale, v_conv_w_dw, v_conv_b_dw, v_conv_ln_g, v_conv_ln_b, v_conv_w_pw, v_grp_norm_g, v_w_out, v_norm_x_g, v_w_xq, v_w_xk, v_w_xv, v_w_xo, v_norm_mlp_g, v_w_up, v_w_down, v_norm_final_g)))

    full = {n: (_gather_weight(n, w[n]) if n in SHARDED else w[n]) for n in WEIGHTS if n != 'norm_final_g'}
    L, D = x.shape[1:]

    memn, mem_vjp = jax.vjp(lambda a, g: rmsnorm(a, g, "mem_norm"), mem[0], w['mem_norm_g'])
    tabs, tabs_vjp = jax.vjp(_bias_tables, w['rel_bias'])
    h = x[0]
    stages = []
    for l in range(DEPTH):
        layer = lambda names: {n: full[n][l] for n in names}
        (u_a, u_b, u_c, qkv), in_vjp = jax.vjp(functools.partial(_project_in, l), h, layer(IN_WEIGHTS))
        y_d, o_all, l_all = _att_fwd(qkv, tabs, f"l{l}_att_fwd")
        h, mix_vjp = jax.vjp(functools.partial(_mix_and_memory, l), u_a, u_b, u_c, y_d, h, memn, layer(MIX_WEIGHTS))
        norm = _rms_op(f"l{l}_norm_mlp", L, D, BF16)
        h, saved = _mlp_fwd(f"l{l}_mlp", norm, h, full['norm_mlp_g'][l].reshape(1, D), full['w_up'], full['w_down'], l)
        stages.append((in_vjp, (qkv, tabs, o_all, l_all), mix_vjp, norm, saved))
    loss_local, dh, d_final_g = loss_head(h, loss_target[0], w['norm_final_g'], "loss_head")
    loss = lax.psum(loss_local, MESH_AXES)

    def pair_sums(l, names, grads_l):
        by_dest = [_by_destination(n, grads_l[n]) for n in names]
        theirs = sibling_exchange(by_dest, f"d2d_l{l}_{names[0]}")
        return [pair_sum(s, t, f"pairsum_l{l}_{n}") for n, s, t in zip(names, by_dest, theirs)]

    empties = lambda like: [lax.empty(p.shape, p.dtype) for p in like]
    layer_grads, arrived = [None] * DEPTH, [None] * DEPTH
    pending, dmemn, dtabs = None, 0.0, 0.0
    for l in reversed(range(DEPTH)):
        in_vjp, att_saved, mix_vjp, norm, saved = stages[l]
        lands = None if pending is None else empties(pending)
        dh, dg_mlp, dw_up, dw_down, lands = _mlp_bwd(f"l{l}_mlp", norm, saved, dh, pending, lands)
        mlp_grads = dict(norm_mlp_g=dg_mlp.reshape(D), w_up=dw_up, w_down=dw_down)
        du_a, du_b, du_c, dy_d, dh_res, dmemn_l, d_mix = mix_vjp(dh)
        rides = [] if pending is None else [(pending, lands, ATT_RIDE_WINDOW)]
        if l == 0:
            early = pair_sums(l, MLP_SHARDED, mlp_grads)
            rides.append((early, empties(early), (0, 1, 1)))
        dq, dk, dv, dtabs_l, *landed = _att_bwd(*att_saved, dy_d, f"l{l}_att_bwd", riders=rides)
        if pending is not None:
            arrived[l + 1] = landed[:len(pending)]
        dh_in, d_in = in_vjp((du_a, du_b, du_c, jnp.concatenate([dq, dk, dv], axis=-1)))
        dh = dh_in + dh_res
        dmemn, dtabs = dmemn + dmemn_l, dtabs + dtabs_l
        layer_grads[l] = {**d_in, **d_mix, **mlp_grads}
        if l > 0:
            pending = pair_sums(l, GATHER_BF16, layer_grads[l])
        else:
            late = [n for n in GATHER_BF16 if n not in MLP_SHARDED]
            got = dict(zip(late, chip_exchange(pair_sums(l, late, layer_grads[l]), "ici_l0")))
            got.update(zip(MLP_SHARDED, landed[-len(MLP_SHARDED):]))
            arrived[0] = [got[n] for n in GATHER_BF16]
    dx = dh
    dfull = {n: jnp.concatenate([layer_grads[l][n][None] for l in range(DEPTH)])
             for n in LAYER_WEIGHTS if n not in GATHER_BF16}
    dfull['mem_norm_g'] = mem_vjp(dmemn)[1]
    dfull['rel_bias'] = tabs_vjp(dtabs)[0]
    dfull['norm_final_g'] = d_final_g

    grads, deltas, new_m, new_v = {}, {}, {}, {}
    for k, n in enumerate(GATHER_BF16):
        res = adamw_layers(w[n], [arrived[l][k] for l in range(DEPTH)], m[n], v[n], "adamw_" + n)
        grads[n], deltas[n], new_m[n], new_v[n] = res
    for n in SHARDED:
        if n in GATHER_BF16:
            continue
        parts = _scatter_grad(n, dfull[n])
        shp = w[n].shape
        two_d = lambda a: a.reshape(shp[0] * shp[1], shp[2])
        res = adamw(two_d(w[n]), parts, two_d(m[n]), two_d(v[n]), "adamw_" + n)
        grads[n], deltas[n], new_m[n], new_v[n] = (r.reshape(shp) for r in res)

    parts = all_gather(_flatten_small(dfull), "ag_small_grads")
    res = adamw(_flatten_small(w), parts, _flatten_small(m), _flatten_small(v), "adamw_small")
    for dst, r in zip((grads, deltas, new_m, new_v), res):
        dst.update(_split_small(r, w))

    return (loss, dx[None], *[grads[n] for n in WEIGHTS], *[deltas[n] for n in WEIGHTS],
            *[new_m[n] for n in WEIGHTS], *[new_v[n] for n in WEIGHTS])
```

```python
import functools
import math

import numpy as np
import jax
import jax.numpy as jnp
from jax import lax
from jax.experimental import pallas as pl
from jax.experimental.pallas import tpu as pltpu

F32 = jnp.float32
BF16 = jnp.bfloat16

DEPTH = 4
N_MIXERS = 4
S5_CH_PER_GROUP = 16
S5_STATE = 64
POOL_WINDOWS = (2, 4, 8, 16)
ATT_HEADS = 8
DILATED_PATTERNS = ((128, 1), (512, 4), (2048, 16))
ATT_BLOCK = 128
ATT_MIX_ROWS = 256
ATT_UNROLL = 4
REL_BUCKETS = 32
REL_MAX_DIST = 2048
X_HEAD_DIM = 128
NORM_EPS = 1e-6
NEG_INF = -1e30
ADAM_LR = 0.001
ADAM_B1 = 0.9
ADAM_B2 = 0.999
ADAM_EPS = 1e-08
ADAM_WD = 0.01
ADAM_STEP = 10

LANES = 128
SUBLANES = 8
VMEM_BYTES = 64 * 1024 * 1024
VMEM_LIMIT = (VMEM_BYTES * 3) // 4
VMEM_LIMIT_BIG = (VMEM_BYTES * 7) // 8
STREAM_BLOCK_BYTES = 1024 * 1024
SMALL_ROW_TILE = 512
N_DEV = 8
N_CHIP = 4
MESH_AXES = ("x", "y", "c")

WEIGHTS = ['rel_bias', 'mem_norm_g', 'norm_mix_g', 'w_in', 's5_lam_re', 's5_lam_im', 's5_log_dt', 's5_b_re',
           's5_b_im', 's5_c_re', 's5_c_im', 's5_d', 's5_w_glu', 'pool_w', 'pool_scale', 'conv_w_dw', 'conv_b_dw',
           'conv_ln_g', 'conv_ln_b', 'conv_w_pw', 'grp_norm_g', 'w_out', 'norm_x_g', 'w_xq', 'w_xk', 'w_xv', 'w_xo',
           'norm_mlp_g', 'w_up', 'w_down', 'norm_final_g']
SHARDED = {'w_in': 2, 's5_w_glu': 1, 'conv_w_dw': 2, 'conv_w_pw': 1, 'w_out': 1, 'w_xq': 1, 'w_xk': 1, 'w_xv': 1,
           'w_xo': 2, 'w_up': 2, 'w_down': 1}
GATHER_BF16 = ('w_in', 'w_out', 'w_xq', 'w_xk', 'w_xv', 'w_xo', 'w_up', 'w_down')
SMALL = [n for n in WEIGHTS if n not in SHARDED]
LAYER_WEIGHTS = [n for n in WEIGHTS if n not in ('rel_bias', 'mem_norm_g', 'norm_final_g')]
IN_WEIGHTS = ('norm_mix_g', 'w_in')
MLP_WEIGHTS = ('norm_mlp_g', 'w_up', 'w_down')
MIX_WEIGHTS = [n for n in LAYER_WEIGHTS if n not in IN_WEIGHTS + MLP_WEIGHTS]
MLP_SHARDED = ('w_up', 'w_down')

def _cp(n_axes, vmem=VMEM_LIMIT):
    return pltpu.CompilerParams(dimension_semantics=("arbitrary",) * n_axes, vmem_limit_bytes=vmem)


def _tile(n, prefs):
    for t in prefs:
        if n % t == 0:
            return t
    return n


MM_TILE = 1024
MM_TILE_K = 2048
MM_FULL_K = 4096


def _chip_exchange_copies(srcs, dsts, window, send_sems, recv_sems, local_sems, base=0):
    x_, y_, c_ = _place()
    me = _chip(x_, y_)
    first, count, total = window
    copies = []
    for p, (src, dst) in enumerate(zip(srcs, dsts)):
        unit = src.shape[1] // total
        rows = pl.ds(first * unit, count * unit)
        copies.append(pltpu.make_async_copy(src.at[me, rows], dst.at[me, rows], local_sems.at[base + p]))
        for k in range(1, N_CHIP):
            px = 1 - x_ if k & 2 else x_
            py = 1 - y_ if k & 1 else y_
            s = (base + p) * (N_CHIP - 1) + k - 1
            copies.append(pltpu.make_async_remote_copy(
                src_ref=src.at[_chip(px, py), rows], dst_ref=dst.at[me, rows], send_sem=send_sems.at[s],
                recv_sem=recv_sems.at[s], device_id=(px, py, c_), device_id_type=MESH_ID))
    return copies


class _Riders:
    def __init__(self, rides):
        self.rides = rides or []
        self.srcs = [s for r in self.rides for s in r[0]]
        self.dsts = [d for r in self.rides for d in r[1]]
        self.n = len(self.srcs)

    def operands(self):
        return (*self.srcs, *self.dsts)

    def in_specs(self):
        return [HBM_SPEC] * (2 * self.n)

    def out_specs(self):
        return [HBM_SPEC] * self.n

    def out_shape(self):
        return [jax.ShapeDtypeStruct(d.shape, d.dtype) for d in self.dsts]

    def scratch(self):
        if not self.n:
            return []
        n_sem = self.n * (N_CHIP - 1)
        return [pltpu.SemaphoreType.DMA((n_sem,)), pltpu.SemaphoreType.DMA((n_sem,)), pltpu.SemaphoreType.DMA((self.n,))]

    def aliases(self, first_in, first_out):
        return {first_in + self.n + p: first_out + p for p in range(self.n)}

    def hooks(self, grid, src_refs, dst_refs, sems):
        if not self.n:
            return (lambda: None), (lambda: None)

        def copies():
            out, base = [], 0
            for srcs, _, window in self.rides:
                k = len(srcs)
                out += _chip_exchange_copies(src_refs[base:base + k], dst_refs[base:base + k], window, *sems, base=base)
                base += k
            return out

        ids = [pl.program_id(ax) for ax in range(len(grid))]
        at_start = functools.reduce(jnp.logical_and, [i == 0 for i in ids])
        at_end = functools.reduce(jnp.logical_and, [i == g - 1 for i, g in zip(ids, grid)])

        def start():
            @pl.when(at_start)
            def _():
                for cp in copies():
                    cp.start()

        def wait():
            @pl.when(at_end)
            def _():
                for cp in copies():
                    cp.wait()

        return start, wait


def _mm(a, b, *, ta=False, tb=False, res=None, out_dtype=F32, epilogue=None, pre=None, riders=None, dest_cols=None,
        b_slots=None, name):
    if ta:
        K, M = a.shape
    else:
        M, K = a.shape
    if b_slots is not None:
        layer, n_layers, axis = b_slots
        a_blk, b_blk = b.shape[1] // n_layers, b.shape[2]
        w_shape = (N_DEV * a_blk, b_blk) if axis == 1 else (a_blk, N_DEV * b_blk)
    else:
        w_shape = b.shape
    N, K2 = w_shape if tb else w_shape[::-1]
    assert K == K2, (a.shape, b.shape, ta, tb)
    wide_f32 = K >= MM_TILE_K and F32 in (a.dtype, b.dtype)
    tm = _tile(M, (MM_TILE // 2 if wide_f32 and not ta else MM_TILE, 512, 256, 128))
    tn = _tile(N, (MM_TILE, 512, 256, 128)) if dest_cols is None else dest_cols
    tk = K if K <= MM_FULL_K else _tile(K, (MM_TILE_K, 1024, 512, 256, 128))
    n_b = 1
    if b_slots is not None:
        assert not ta
        sharded_is_k = (axis == 1) != tb
        slot = a_blk if axis == 1 else b_blk
        if sharded_is_k:
            n_b = _tile(N_DEV, [n for n in (8, 4, 2) if n * slot <= MM_TILE_K])
            tk = n_b * slot
        else:
            tn = slot
    nk = K // tk
    a_spec = pl.BlockSpec((tk, tm), lambda i, j, k: (k, i)) if ta else pl.BlockSpec((tm, tk), lambda i, j, k: (i, k))
    if b_slots is None:
        b_specs = [pl.BlockSpec((tn, tk), lambda i, j, k: (j, k)) if tb
                   else pl.BlockSpec((tk, tn), lambda i, j, k: (k, j))]
    elif axis == 1 and not tb:
        b_specs = [pl.BlockSpec((None, slot, tn), lambda i, j, k, q=q: (n_b * k + q, layer, j)) for q in range(n_b)]
    elif axis == 1:
        b_specs = [pl.BlockSpec((None, tn, tk), lambda i, j, k: (j, layer, k))]
    elif not tb:
        b_specs = [pl.BlockSpec((None, tk, tn), lambda i, j, k: (j, layer * (a_blk // tk) + k, 0))]
    else:
        b_specs = [pl.BlockSpec((None, tn, slot), lambda i, j, k, q=q: (n_b * k + q, layer * (a_blk // tn) + j, 0))
                   for q in range(n_b)]
    o_spec = pl.BlockSpec((tm, tn), lambda i, j, k: (i, j))
    dn = (((0 if ta else 1,), (1 if tb else 0,)), ((), ()))
    extra = [x for x in (res, pre) if x is not None]
    assert not (res is not None and pre is not None)
    assert dest_cols is None or (epilogue is None and not extra and tn <= MM_TILE)
    n_out = 2 if epilogue == 'relu_sq' else 1
    ride = _Riders(riders)
    n_pairs = ride.n
    grid = (M // tm, N // tn, nk)

    def body(*refs):
        a_ref, b_refs = refs[0], refs[1:1 + n_b]
        x_ref = refs[1 + n_b] if extra else None
        n_in = 1 + n_b + len(extra) + 2 * n_pairs
        o_refs = refs[n_in:n_in + n_out]
        scratch = refs[n_in + n_out + n_pairs:]
        acc = scratch[0] if nk > 1 else None
        start, wait = ride.hooks(grid, refs[n_in - 2 * n_pairs:n_in - n_pairs],
                                 refs[n_in + n_out:n_in + n_out + n_pairs], scratch[-3:])
        start()

        def finish(r):
            if res is not None:
                r = r + x_ref[...]
            if epilogue == 'relu_sq':
                o_refs[0][...] = r
                o_refs[1][...] = jnp.square(jnp.maximum(r, 0.0)).astype(out_dtype)
            elif epilogue == 'relu_sq_grad':
                o_refs[0][...] = (r * (2.0 * jnp.maximum(x_ref[...], 0.0))).astype(out_dtype)
            else:
                o_refs[0][...] = r.astype(out_dtype)

        dot = lambda x, y: lax.dot_general(x.astype(BF16), y[...].astype(BF16), dn, preferred_element_type=F32)
        if n_b == 1:
            part = dot(a_ref[...], b_refs[0])
        else:
            part = sum(dot(a_ref[:, q * slot:(q + 1) * slot], b_refs[q]) for q in range(n_b))
        if nk == 1:
            finish(part)
        else:
            k = pl.program_id(2)

            @pl.when(k == 0)
            def _():
                acc[...] = part

            @pl.when(k > 0)
            def _():
                acc[...] += part

            @pl.when(k == nk - 1)
            def _():
                finish(acc[...])

        wait()

    out_shape = [jax.ShapeDtypeStruct((M, N), F32 if epilogue == 'relu_sq' else out_dtype)]
    if n_out == 2:
        out_shape.append(jax.ShapeDtypeStruct((M, N), out_dtype))
    in_specs = [a_spec, *b_specs] + [o_spec] * len(extra)
    out_specs = [o_spec] * n_out
    if dest_cols is not None:
        out_shape = [jax.ShapeDtypeStruct((N // tn, M, tn), out_dtype)]
        out_specs = [pl.BlockSpec((None, tm, tn), lambda i, j, k: (j, i, 0))]
    scratch = ([pltpu.VMEM((tm, tn), F32)] if nk > 1 else []) + ride.scratch()
    args = (a, *[b] * n_b, *extra)
    outs = pl.pallas_call(
        body, grid=grid, in_specs=in_specs + ride.in_specs(), out_specs=out_specs + ride.out_specs(),
        out_shape=out_shape + ride.out_shape(), scratch_shapes=scratch,
        input_output_aliases=ride.aliases(len(args), n_out), name=name, compiler_params=_cp(3, VMEM_LIMIT_BIG))(
            *args, *ride.operands())
    return outs[0] if len(outs) == 1 else tuple(outs)


def _linear(name):
    @jax.custom_vjp
    def lin(a, w):
        return _mm(a, w, name=name + "_fwd")

    def fwd(a, w):
        return _mm(a, w, name=name + "_fwd"), (a, w)

    def bwd(r, dy):
        a, w = r
        da = _mm(dy, w, tb=True, name=name + "_dx")
        dw = _mm(a, dy, ta=True, out_dtype=w.dtype, name=name + "_dw")
        return da, dw

    lin.defvjp(fwd, bwd)
    return lin


def _act_linear(name, act, n_in, with_res):
    def run(*a):
        ins, w = a[:n_in], a[n_in]
        x = act.fwd_call(*ins)[0]
        return _mm(x, w, res=a[n_in + 1] if with_res else None, name=name + "_fwd"), (ins, x, w)

    @jax.custom_vjp
    def op(*a):
        return run(*a)[0]

    def bwd(r, dy):
        ins, x, w = r
        dx = _mm(dy, w, tb=True, name=name + "_dx")
        dw = _mm(x, dy, ta=True, out_dtype=w.dtype, name=name + "_dw")
        return (*act.bwd_all(ins, (dx,)), dw) + ((dy,) if with_res else ())

    op.defvjp(run, bwd)
    return op


def _mlp_fwd(name, norm, h, g, w_up, w_down, layer):
    hn = norm.fwd_call(h, g)[0]
    up, down = (layer, DEPTH, SHARDED['w_up']), (layer, DEPTH, SHARDED['w_down'])
    a, r = _mm(hn, w_up, epilogue='relu_sq', out_dtype=BF16, b_slots=up, name=name + "_up_fwd")
    return _mm(r, w_down, res=h, b_slots=down, name=name + "_down_fwd"), (h, g, hn, a, r, w_up, w_down, up, down)


RIDE_UNITS = 16
MLP_RIDE_UNITS = (3, 3, 3, 2)
ATT_RIDE_WINDOW = (sum(MLP_RIDE_UNITS), RIDE_UNITS - sum(MLP_RIDE_UNITS), RIDE_UNITS)


def _mlp_bwd(name, norm, saved, dy, pending, lands):
    h, g, hn, a, r, w_up, w_down, up, down = saved

    def mm(i, *args, **kw):
        nonlocal lands
        if pending is None:
            return _mm(*args, **kw)
        window = (sum(MLP_RIDE_UNITS[:i]), MLP_RIDE_UNITS[i], RIDE_UNITS)
        out, *lands = _mm(*args, riders=[(pending, lands, window)], **kw)
        return out

    da = mm(0, dy, w_down, tb=True, epilogue='relu_sq_grad', pre=a, out_dtype=BF16, b_slots=down,
            name=name + "_down_dx")
    dw_down = mm(1, r, dy, ta=True, out_dtype=w_down.dtype, name=name + "_down_dw")
    dhn = mm(2, da, w_up, tb=True, b_slots=up, name=name + "_up_dx")
    dw_up = mm(3, hn, da, ta=True, out_dtype=w_up.dtype, dest_cols=w_up.shape[2], name=name + "_up_dw")
    dh, dg = norm.bwd_all((h, g), (dhn,), add_to_first=dy)
    return dh, dg, dw_up, dw_down, lands


def _block_op(name, f, grid, ins, outs, vmem=VMEM_LIMIT):
    n_in, n_out = len(ins), len(outs)
    in_specs = [pl.BlockSpec(bs, im) for bs, im, _, _ in ins]
    out_specs = [pl.BlockSpec(bs, im) for _, _, bs, im in outs]
    out_shape = [jax.ShapeDtypeStruct(s, d) for s, d, _, _ in outs]
    didx = [i for i in range(n_in) if ins[i][3]]

    def fwd_call(*args):
        def body(*refs):
            res = f(*[r[...] for r in refs[:n_in]])
            for r, o in zip(refs[n_in:], res):
                r[...] = o.astype(r.dtype)

        return pl.pallas_call(body, grid=grid, in_specs=in_specs, out_specs=out_specs, out_shape=out_shape,
                              name=name + "_fwd", compiler_params=_cp(len(grid), vmem))(*args)

    def bwd_call(args, cts, add_to_first=None):
        n_add = 0 if add_to_first is None else 1

        def body(*refs):
            vals = [r[...] for r in refs[:n_in]]
            ct_refs = refs[n_in:n_in + n_out]
            g_refs = refs[n_in + n_out + n_add:]

            def fd(*dv):
                full = list(vals)
                for i, v in zip(didx, dv):
                    full[i] = v
                return f(*full)

            _, vjp = jax.vjp(fd, *[vals[i] for i in didx])
            grads = list(vjp(tuple(r[...] for r in ct_refs)))
            if n_add:
                grads[0] = grads[0] + refs[n_in + n_out][...]
            for gref, i, g in zip(g_refs, didx, grads):
                acc = ins[i][2]
                if acc:
                    first = functools.reduce(jnp.logical_and, [pl.program_id(ax) == 0 for ax in acc])

                    @pl.when(first)
                    def _(gref=gref):
                        gref[...] = jnp.zeros_like(gref)

                    gref[...] += g.astype(gref.dtype)
                else:
                    gref[...] = g.astype(gref.dtype)

        g_specs = [pl.BlockSpec(ins[i][0], ins[i][1]) for i in didx]
        g_shape = [jax.ShapeDtypeStruct(args[i].shape, args[i].dtype) for i in didx]
        assert not n_add or (didx[0] == 0 and not ins[0][2])
        added = [] if add_to_first is None else [add_to_first]
        return pl.pallas_call(body, grid=grid, in_specs=in_specs + out_specs + in_specs[:n_add], out_specs=g_specs,
                              out_shape=g_shape, name=name + "_bwd", compiler_params=_cp(len(grid), vmem))(
                                  *args, *cts, *added)

    @jax.custom_vjp
    def op(*args):
        return tuple(fwd_call(*args))

    def op_fwd(*args):
        return tuple(fwd_call(*args)), args

    def op_bwd(args, cts, add_to_first=None):
        it = iter(bwd_call(args, cts, add_to_first))
        return tuple(next(it) if ins[i][3] else jnp.zeros_like(args[i]) for i in range(n_in))

    op.defvjp(op_fwd, op_bwd)
    op.fwd_call = fwd_call
    op.bwd_all = op_bwd
    return op


def _row(tr, c):
    return ((tr, c), lambda i: (i, 0), None, True)


def _par(shape):
    nd = len(shape)
    return (shape, lambda i: (0,) * nd, (0,), True)


def _bdot(a, w):
    return jnp.dot(a.astype(BF16), w.astype(BF16), preferred_element_type=F32)


def _rms_f(x, g):
    return (x * lax.rsqrt(jnp.mean(x * x, axis=-1, keepdims=True) + NORM_EPS) * g,)


def _rms_op(name, R, D, out_dtype):
    tr = _tile(R, (256,))
    return _block_op(name, _rms_f, (R // tr,), [_row(tr, D), _par((1, D))],
                     [((R, D), out_dtype, (tr, D), lambda i: (i, 0))])


def rmsnorm(x, g, name):
    R, D = x.shape
    return _rms_op(name, R, D, F32)(x, g.reshape(1, D))[0]


def s5_epilogue(yc, u, d, w_glu, name):
    R, C = yc.shape
    tr = _tile(R, (256,))

    def f(yc, u, d, w):
        g = jax.nn.gelu(yc + d * u)
        return (g * jax.nn.sigmoid(_bdot(g, w)),)

    op = _block_op(name, f, (R // tr,), [_row(tr, C), _row(tr, C), _par((1, C)), _par((C, C))],
                   [((R, C), F32, (tr, C), lambda i: (i, 0))])
    return op(yc, u, d.reshape(1, C), w_glu)[0]


def pool_proj(p, w, scale, name):
    R, C = p.shape
    ng, pc, _ = w.shape
    tr = _tile(R, (256,))

    def f(p, w, s):
        ys = [_bdot(p[:, g * pc:(g + 1) * pc], w[g]) for g in range(ng)]
        return (jnp.concatenate(ys, axis=-1) * s,)

    op = _block_op(name, f, (R // tr,), [_row(tr, C), _par((ng, pc, pc)), _par((1, C))],
                   [((R, C), F32, (tr, C), lambda i: (i, 0))])
    return op(p, w, scale.reshape(1, C))[0]


def conv_post(h, ln_g, ln_b, w_pw, name):
    R, C = h.shape
    tr = _tile(R, (256,))

    def f(h, g, b, w):
        hc = h - jnp.mean(h, axis=-1, keepdims=True)
        y = hc * lax.rsqrt(jnp.mean(hc * hc, axis=-1, keepdims=True) + NORM_EPS) * g + b
        return (_bdot(jax.nn.silu(y), w),)

    op = _block_op(name, f, (R // tr,), [_row(tr, C), _par((1, C)), _par((1, C)), _par((C, C))],
                   [((R, C), F32, (tr, C), lambda i: (i, 0))])
    return op(h, ln_g.reshape(1, C), ln_b.reshape(1, C), w_pw)[0]


def _group_norm_op(name, R, C, n, out_dtype):
    tr = _tile(R, (256,))

    def f(*a):
        g = a[n]
        parts = [y * lax.rsqrt(jnp.mean(y * y, axis=-1, keepdims=True) + NORM_EPS) for y in a[:n]]
        return (jnp.concatenate(parts, axis=-1) * g,)

    return _block_op(name, f, (R // tr,), [_row(tr, C)] * n + [_par((1, n * C))],
                     [((R, n * C), out_dtype, (tr, n * C), lambda i: (i, 0))])


def _cross_attention_op(name, L, W, M, out_dtype):
    E = X_HEAD_DIM
    tq = _tile(L, (512,))

    def f(q, k, v):
        s = lax.dot_general(q.astype(BF16), k.astype(BF16), (((1,), (1,)), ((), ())),
                            preferred_element_type=F32) * (E ** -0.5)
        p = jax.nn.softmax(s, axis=-1)
        return (_bdot(p, v),)

    qspec = ((tq, E), lambda h, i: (i, h), None, True)
    kspec = ((M, E), lambda h, i: (0, h), (1,), True)
    return _block_op(name, f, (W // E, L // tq), [qspec, kspec, kspec],
                     [((L, W), out_dtype, (tq, E), lambda h, i: (i, h))])


def s5_discretise(lam_re, lam_im, log_dt, b_re_t, b_im_t, name):
    G, _, N = lam_re.shape
    C = b_re_t.shape[1]

    def f(lr, li, ldt, br, bi):
        dt = jnp.exp(ldt)
        mag = jnp.exp(lr * dt)
        ab_r, ab_i = mag * jnp.cos(li * dt), mag * jnp.sin(li * dt)
        den = lr * lr + li * li
        nr, ni = ab_r - 1.0, ab_i
        f_r = (nr * lr + ni * li) / den
        f_i = (ni * lr - nr * li) / den
        return ab_r, ab_i, f_r * br - f_i * bi, f_r * bi + f_i * br

    vec = ((G, 1, N), lambda i: (0, 0, 0), None, True)
    mat = ((G, C, N), lambda i: (0, 0, 0), None, True)
    ov = ((G, 1, N), F32, (G, 1, N), lambda i: (0, 0, 0))
    om = ((G, C, N), F32, (G, C, N), lambda i: (0, 0, 0))
    op = _block_op(name, f, (1,), [vec, vec, vec, mat, mat], [ov, ov, om, om])
    return op(lam_re, lam_im, log_dt, b_re_t, b_im_t)


def rel_bias_tables(rel_bias, onehot, name):
    B, H = rel_bias.shape
    P, _, Q = onehot.shape

    def f(rbt, oh):
        return (jnp.dot(rbt, oh, precision=lax.Precision.HIGHEST, preferred_element_type=F32),)

    op = _block_op(name, f, (P,), [((H, B), lambda p: (0, 0), (0,), True), ((None, B, Q), lambda p: (p, 0, 0), None, False)],
                   [((P, H, Q), F32, (None, H, Q), lambda p: (p, 0, 0))])
    return op(rel_bias.T, onehot)[0]


def _shift_down(x, s, row):
    return jnp.where(row >= s, pltpu.roll(x, s, 0), 0.0)


def _shift_up(x, s, row):
    n = x.shape[0]
    return jnp.where(row < n - s, pltpu.roll(x, n - s, 0), 0.0)


def _window_sum(x, w, row, shift):
    span = 1
    while span < w:
        x = x + shift(x, span, row)
        span *= 2
    return x


def _pool_call(u, d_out, name):
    L, C = u.shape
    pc = C // len(POOL_WINDOWS)
    assert pc % LANES == 0

    def body(x_ref, o_ref):
        row = lax.broadcasted_iota(jnp.int32, (L, pc), 0)
        for g, w in enumerate(POOL_WINDOWS):
            sl = slice(g * pc, (g + 1) * pc)
            x = x_ref[:, sl]
            cnt = jnp.minimum(row + 1, w).astype(F32)
            if d_out is None:
                o_ref[:, sl] = _window_sum(x, w, row, _shift_down) / cnt - x
            else:
                o_ref[:, sl] = _window_sum(x / cnt, w, row, _shift_up) - x

    src = u if d_out is None else d_out
    return pl.pallas_call(body, out_shape=jax.ShapeDtypeStruct((L, C), F32), name=name,
                          compiler_params=pltpu.CompilerParams(vmem_limit_bytes=VMEM_LIMIT))(src)


def _pool_mix(name):
    @jax.custom_vjp
    def op(u):
        return _pool_call(u, None, name + "_fwd")

    def fwd(u):
        return _pool_call(u, None, name + "_fwd"), u

    def bwd(u, dp):
        return (_pool_call(u, dp, name + "_bwd"),)

    op.defvjp(fwd, bwd)
    return op


def _conv_fwd(u, w, b, name):
    L, C2 = u.shape
    C = C2 // 2
    K = w.shape[0]
    nb = C // LANES

    def body(val_ref, gate_ref, w_ref, b_ref, o_ref):
        row = lax.broadcasted_iota(jnp.int32, (L, LANES), 0)
        h = val_ref[...] * jax.nn.sigmoid(gate_ref[...])
        acc = jnp.broadcast_to(b_ref[...], (L, LANES))
        for k in range(K):
            acc = acc + w_ref[k:k + 1, :] * _shift_down(h, K - 1 - k, row)
        o_ref[...] = acc

    blk = lambda off: pl.BlockSpec((L, LANES), lambda j: (0, j + off))
    return pl.pallas_call(
        body, grid=(nb,), in_specs=[blk(0), blk(nb), pl.BlockSpec((K, LANES), lambda j: (0, j)),
                                    pl.BlockSpec((1, LANES), lambda j: (0, j))],
        out_specs=blk(0), out_shape=jax.ShapeDtypeStruct((L, C), F32), name=name, compiler_params=_cp(1))(u, u, w, b)


def _conv_bwd(u, w, dh, name):
    L, C2 = u.shape
    C = C2 // 2
    K = w.shape[0]
    nb = C // LANES

    def body(val_ref, gate_ref, w_ref, dh_ref, dval_ref, dgate_ref, dw_ref, db_ref):
        row = lax.broadcasted_iota(jnp.int32, (L, LANES), 0)
        val = val_ref[...]
        sig = jax.nn.sigmoid(gate_ref[...])
        h = val * sig
        d = dh_ref[...]
        dh0 = jnp.zeros((L, LANES), F32)
        for k in range(K):
            s = K - 1 - k
            dh0 = dh0 + w_ref[k:k + 1, :] * _shift_up(d, s, row)
            dw_ref[k:k + 1, :] = jnp.sum(d * _shift_down(h, s, row), axis=0, keepdims=True)
        db_ref[...] = jnp.sum(d, axis=0, keepdims=True)
        dval_ref[...] = dh0 * sig
        dgate_ref[...] = dh0 * val * sig * (1.0 - sig)

    blk = lambda off: pl.BlockSpec((L, LANES), lambda j: (0, j + off))
    return pl.pallas_call(
        body, grid=(nb,), in_specs=[blk(0), blk(nb), pl.BlockSpec((K, LANES), lambda j: (0, j)), blk(0)],
        out_specs=[blk(0), blk(0), pl.BlockSpec((K, LANES), lambda j: (0, j)), pl.BlockSpec((1, LANES), lambda j: (0, j))],
        out_shape=[jax.ShapeDtypeStruct((L, C), F32), jax.ShapeDtypeStruct((L, C), F32),
                   jax.ShapeDtypeStruct((K, C), F32), jax.ShapeDtypeStruct((1, C), F32)],
        name=name, compiler_params=_cp(1))(u, u, w, dh)


def _glu_conv(name):
    @jax.custom_vjp
    def op(u, w, b):
        return _conv_fwd(u, w, b, name + "_fwd")

    def fwd(u, w, b):
        return _conv_fwd(u, w, b, name + "_fwd"), (u, w)

    def bwd(r, dh):
        u, w = r
        dval, dgate, dw, db = _conv_bwd(u, w, dh, name + "_bwd")
        return jnp.concatenate([dval, dgate], axis=-1), dw, db

    op.defvjp(fwd, bwd)
    return op


S5_BLOCK_CH = LANES
S5_BLOCK_ST = S5_BLOCK_CH // S5_CH_PER_GROUP * S5_STATE


def _s5_scan(br_ref, bi_ref, ar, ai, reverse):
    L, C = br_ref.shape
    T = SUBLANES
    row = lax.broadcasted_iota(jnp.int32, (T, C), 0)
    pw = [(ar, ai)]
    for _ in range(T - 1):
        pr, pi = pw[-1]
        pw.append((pr * ar - pi * ai, pr * ai + pi * ar))
    cr = jnp.zeros((T, C), F32)
    ci = jnp.zeros((T, C), F32)
    for r in range(T):
        e = (T - r) if reverse else (r + 1)
        cr = jnp.where(row == r, pw[e - 1][0], cr)
        ci = jnp.where(row == r, pw[e - 1][1], ci)
    steps = []
    s = 1
    while s < T:
        mask = (row < T - s) if reverse else (row >= s)
        steps.append((T - s if reverse else s, mask, pw[s - 1][0], pw[s - 1][1]))
        s *= 2
    nt = L // T
    last = 0 if reverse else T - 1

    def body(i, carry):
        kr, ki = carry
        t = (nt - 1 - i) if reverse else i
        off = pl.multiple_of(t * T, T)
        xr = br_ref[pl.ds(off, T), :]
        xi = bi_ref[pl.ds(off, T), :]
        for sh, mask, mr, mi in steps:
            sr = jnp.where(mask, pltpu.roll(xr, sh, 0), 0.0)
            si = jnp.where(mask, pltpu.roll(xi, sh, 0), 0.0)
            xr, xi = xr + mr * sr - mi * si, xi + mr * si + mi * sr
        xr, xi = xr + cr * kr - ci * ki, xi + cr * ki + ci * kr
        br_ref[pl.ds(off, T), :] = xr
        bi_ref[pl.ds(off, T), :] = xi
        return (jnp.broadcast_to(xr[last:last + 1, :], (T, C)), jnp.broadcast_to(xi[last:last + 1, :], (T, C)))

    z = jnp.zeros((T, C), F32)
    lax.fori_loop(0, nt, body, (z, z))


def _s5_specs(L):
    nb_axis = lambda j: (j, 0, 0)
    u = pl.BlockSpec((L, S5_BLOCK_CH), lambda j: (0, j))
    wb = pl.BlockSpec((None, S5_BLOCK_CH, S5_BLOCK_ST), nb_axis)
    a = pl.BlockSpec((1, S5_BLOCK_ST), lambda j: (0, j))
    wc = pl.BlockSpec((None, S5_BLOCK_ST, S5_BLOCK_CH), nb_axis)
    return u, wb, a, wc


def _s5_fwd(u, wbr, wbi, ar, ai, wcr, wci, name):
    L, C = u.shape
    nb = C // S5_BLOCK_CH
    us, wbs, as_, wcs = _s5_specs(L)

    def body(u_ref, wbr_ref, wbi_ref, ar_ref, ai_ref, wcr_ref, wci_ref, y_ref, xr, xi):
        ub = u_ref[...]
        xr[...] = _bdot(ub, wbr_ref[...])
        xi[...] = _bdot(ub, wbi_ref[...])
        _s5_scan(xr, xi, ar_ref[...], ai_ref[...], False)
        y_ref[...] = _bdot(xr[...], wcr_ref[...]) - _bdot(xi[...], wci_ref[...])

    return pl.pallas_call(
        body, grid=(nb,), in_specs=[us, wbs, wbs, as_, as_, wcs, wcs], out_specs=us,
        out_shape=jax.ShapeDtypeStruct((L, C), F32),
        scratch_shapes=[pltpu.VMEM((L, S5_BLOCK_ST), F32)] * 2, name=name, compiler_params=_cp(1))(
            u, wbr, wbi, ar, ai, wcr, wci)


def _dot_t(a, b):
    return lax.dot_general(a.astype(BF16), b.astype(BF16), (((0,), (0,)), ((), ())), preferred_element_type=F32)


def _dot_nt(a, b):
    return lax.dot_general(a.astype(BF16), b.astype(BF16), (((1,), (1,)), ((), ())), preferred_element_type=F32)


def _s5_bwd(u, wbr, wbi, ar, ai, wcr, wci, dy, name):
    L, C = u.shape
    nb = C // S5_BLOCK_CH
    us, wbs, as_, wcs = _s5_specs(L)
    T = SUBLANES

    def body(u_ref, wbr_ref, wbi_ref, ar_ref, ai_ref, wcr_ref, wci_ref, dy_ref,
             du_ref, dwbr_ref, dwbi_ref, dar_ref, dai_ref, dwcr_ref, dwci_ref, xr, xi, gr, gi):
        ub = u_ref[...]
        a_r, a_i = ar_ref[...], ai_ref[...]
        xr[...] = _bdot(ub, wbr_ref[...])
        xi[...] = _bdot(ub, wbi_ref[...])
        _s5_scan(xr, xi, a_r, a_i, False)
        d = dy_ref[...]
        dwcr_ref[...] = _dot_t(xr[...], d)
        dwci_ref[...] = -_dot_t(xi[...], d)
        gr[...] = _dot_nt(d, wcr_ref[...])
        gi[...] = -_dot_nt(d, wci_ref[...])
        _s5_scan(gr, gi, a_r, -a_i, True)

        row = lax.broadcasted_iota(jnp.int32, (T, S5_BLOCK_ST), 0)

        def da_body(i, carry):
            pr, pi, sr, si = carry
            off = pl.multiple_of(i * T, T)
            xr_t, xi_t = xr[pl.ds(off, T), :], xi[pl.ds(off, T), :]
            lr_t, li_t = gr[pl.ds(off, T), :], gi[pl.ds(off, T), :]
            qr = jnp.where(row == 0, pr, pltpu.roll(xr_t, 1, 0))
            qi = jnp.where(row == 0, pi, pltpu.roll(xi_t, 1, 0))
            sr = sr + qr * lr_t + qi * li_t
            si = si + qr * li_t - qi * lr_t
            return (jnp.broadcast_to(xr_t[T - 1:T, :], (T, S5_BLOCK_ST)),
                    jnp.broadcast_to(xi_t[T - 1:T, :], (T, S5_BLOCK_ST)), sr, si)

        z = jnp.zeros((T, S5_BLOCK_ST), F32)
        _, _, sr, si = lax.fori_loop(0, L // T, da_body, (z, z, z, z))
        dar_ref[...] = jnp.sum(sr, axis=0, keepdims=True)
        dai_ref[...] = jnp.sum(si, axis=0, keepdims=True)
        lr, li = gr[...], gi[...]
        dwbr_ref[...] = _dot_t(ub, lr)
        dwbi_ref[...] = _dot_t(ub, li)
        du_ref[...] = _dot_nt(lr, wbr_ref[...]) + _dot_nt(li, wbi_ref[...])

    sds = jax.ShapeDtypeStruct
    return pl.pallas_call(
        body, grid=(nb,), in_specs=[us, wbs, wbs, as_, as_, wcs, wcs, us],
        out_specs=[us, wbs, wbs, as_, as_, wcs, wcs],
        out_shape=[sds(u.shape, F32), sds(wbr.shape, F32), sds(wbi.shape, F32), sds(ar.shape, F32),
                   sds(ai.shape, F32), sds(wcr.shape, F32), sds(wci.shape, F32)],
        scratch_shapes=[pltpu.VMEM((L, S5_BLOCK_ST), F32)] * 4, name=name,
        compiler_params=_cp(1, VMEM_LIMIT_BIG))(u, wbr, wbi, ar, ai, wcr, wci, dy)


def _s5_core(name):
    @jax.custom_vjp
    def op(u, wbr, wbi, ar, ai, wcr, wci):
        return _s5_fwd(u, wbr, wbi, ar, ai, wcr, wci, name + "_fwd")

    def fwd(*a):
        return _s5_fwd(*a, name + "_fwd"), a

    def bwd(a, dy):
        return tuple(_s5_bwd(*a, dy, name + "_bwd"))

    op.defvjp(fwd, bwd)
    return op


def _att_tile_f(first, q, kp, kc, vp, vc, bias):
    nq = q.shape[0]
    hb = bias.shape[0]
    E = q.shape[1] // hb
    r = lax.broadcasted_iota(jnp.int32, (nq, 2 * nq), 0)
    c = lax.broadcasted_iota(jnp.int32, (nq, 2 * nq), 1)
    prev_ok = jnp.logical_and(jnp.logical_and(c < nq, c >= r), jnp.logical_not(first))
    valid = jnp.logical_or(prev_ok, jnp.logical_and(c >= nq, c - nq <= r))
    lane = lax.broadcasted_iota(jnp.int32, (1, hb * E), 1)
    k = jnp.concatenate([kp, kc], axis=0)
    v = jnp.concatenate([vp, vc], axis=0)
    o = jnp.zeros((nq, hb * E), F32)
    lse = jnp.zeros((nq, hb * E), F32)
    for h in range(hb):
        mine = jnp.logical_and(lane >= h * E, lane < (h + 1) * E)
        s = jnp.where(valid, _dot_nt(jnp.where(mine, q, 0.0), k) * (E ** -0.5) + bias[h], NEG_INF)
        m = jnp.max(s, axis=-1, keepdims=True)
        p = jnp.exp(s - m)
        den = jnp.sum(p, axis=-1, keepdims=True)
        o = jnp.where(mine, _bdot(p, v) / den, o)
        lse = jnp.where(mine, m + jnp.log(den), lse)
    return o, lse


def _att_tile_grad(first, q, kp, kc, vp, vc, bias, o, lse, do, dlse):
    nq = q.shape[0]
    hb = bias.shape[0]
    E = q.shape[1] // hb
    scale = E ** -0.5
    r = lax.broadcasted_iota(jnp.int32, (nq, 2 * nq), 0)
    c = lax.broadcasted_iota(jnp.int32, (nq, 2 * nq), 1)
    prev_ok = jnp.logical_and(jnp.logical_and(c < nq, c >= r), jnp.logical_not(first))
    valid = jnp.logical_or(prev_ok, jnp.logical_and(c >= nq, c - nq <= r))
    lane = lax.broadcasted_iota(jnp.int32, (1, hb * E), 1)
    k = jnp.concatenate([kp, kc], axis=0)
    v = jnp.concatenate([vp, vc], axis=0)
    dq = jnp.zeros((nq, hb * E), F32)
    dk = jnp.zeros((2 * nq, hb * E), F32)
    dv = jnp.zeros((2 * nq, hb * E), F32)
    db = []
    for h in range(hb):
        mine = jnp.logical_and(lane >= h * E, lane < (h + 1) * E)
        qh = jnp.where(mine, q, 0.0)
        doh = jnp.where(mine, do, 0.0)
        s = jnp.where(valid, _dot_nt(qh, k) * scale + bias[h], NEG_INF)
        p = jnp.exp(s - jnp.max(jnp.where(mine, lse, NEG_INF), axis=-1, keepdims=True))
        row = jnp.sum(jnp.where(mine, dlse, 0.0) - doh * o, axis=-1, keepdims=True)
        ds = p * (_dot_nt(doh, v) + row)
        db.append(ds)
        dv = dv + _dot_t(p, doh)
        dk = dk + _dot_t(ds, qh) * scale
        dq = jnp.where(mine, _bdot(ds, k) * scale, dq)
    return dq, dk[:nq], dk[nq:], dv[:nq], dv[nq:], db


def _att_mix_f(*a):
    n = len(a) // 2
    o, l = a[:n], a[n:]
    m = functools.reduce(jnp.maximum, l)
    e = [jnp.exp(li - m) for li in l]
    return sum(ei * oi for ei, oi in zip(e, o)) / sum(e)


def _att_rows(start, dil):
    if dil == 1:
        return pl.ds(pl.multiple_of(start, ATT_BLOCK), ATT_BLOCK)
    return pl.ds(start, ATT_BLOCK, stride=dil)


def _att_blocks(L, dil):
    nb = L // dil // ATT_BLOCK
    return dil * nb, nb


def _att_specs(L, W):
    nblk = W // LANES
    col = lambda off: pl.BlockSpec((L, LANES), lambda j: (0, j + off))
    per_pattern = pl.BlockSpec((len(DILATED_PATTERNS), L, LANES), lambda j: (0, 0, j))
    hb = ATT_HEADS // nblk
    bias = pl.BlockSpec((len(DILATED_PATTERNS), hb, ATT_BLOCK, 2 * ATT_BLOCK), lambda j: (0, j, 0, 0))
    return nblk, col, per_pattern, bias


def _att_fwd(qkv, bias, name):
    L, W3 = qkv.shape
    W = W3 // 3
    nblk, col, per_pattern, bias_spec = _att_specs(L, W)
    P = len(DILATED_PATTERNS)

    def body(q_ref, k_ref, v_ref, b_ref, y_ref, o_ref, l_ref):
        for p, (_, dil) in enumerate(DILATED_PATTERNS):
            n_it, nb = _att_blocks(L, dil)

            def step(i, carry, p=p, dil=dil, nb=nb):
                n = i % nb
                cur = i // nb + n * (ATT_BLOCK * dil)
                prev = i // nb + jnp.maximum(n - 1, 0) * (ATT_BLOCK * dil)
                rc, rp = _att_rows(cur, dil), _att_rows(prev, dil)
                o, l = _att_tile_f(n == 0, q_ref[rc, :], k_ref[rp, :], k_ref[rc, :], v_ref[rp, :], v_ref[rc, :],
                                   b_ref[p])
                o_ref[p, rc, :] = o
                l_ref[p, rc, :] = l
                return carry

            lax.fori_loop(0, n_it, step, 0, unroll=ATT_UNROLL)

        def mix(i, carry):
            rows = pl.ds(pl.multiple_of(i * ATT_MIX_ROWS, ATT_MIX_ROWS), ATT_MIX_ROWS)
            y_ref[rows, :] = _att_mix_f(*[o_ref[p, rows, :] for p in range(P)], *[l_ref[p, rows, :] for p in range(P)])
            return carry

        lax.fori_loop(0, L // ATT_MIX_ROWS, mix, 0)

    sds = jax.ShapeDtypeStruct
    return pl.pallas_call(
        body, grid=(nblk,), in_specs=[col(0), col(nblk), col(2 * nblk), bias_spec],
        out_specs=[col(0), per_pattern, per_pattern],
        out_shape=[sds((L, W), F32), sds((P, L, W), F32), sds((P, L, W), F32)], name=name,
        compiler_params=_cp(1))(qkv, qkv, qkv, bias)


def _att_bwd(qkv, bias, o_all, l_all, dy, name, riders=None):
    L, W3 = qkv.shape
    W = W3 // 3
    nblk, col, per_pattern, bias_spec = _att_specs(L, W)
    P = len(DILATED_PATTERNS)
    ride = _Riders(riders)
    n_in, n_out = 7, 4

    def body(*refs):
        q_ref, k_ref, v_ref, b_ref, o_ref, l_ref, dy_ref = refs[:n_in]
        outs = refs[n_in + 2 * ride.n:]
        dq_ref, dk_ref, dv_ref, db_ref = outs[:n_out]
        do_s, dl_s = outs[n_out + ride.n:n_out + ride.n + 2]
        start, wait = ride.hooks((nblk,), refs[n_in:n_in + ride.n], outs[n_out:n_out + ride.n],
                                 outs[n_out + ride.n + 2:])
        start()

        def mix(i, carry):
            rows = pl.ds(pl.multiple_of(i * ATT_MIX_ROWS, ATT_MIX_ROWS), ATT_MIX_ROWS)
            _, mix_vjp = jax.vjp(_att_mix_f, *[o_ref[p, rows, :] for p in range(P)],
                                 *[l_ref[p, rows, :] for p in range(P)])
            g = mix_vjp(dy_ref[rows, :])
            for p in range(P):
                do_s[p, rows, :] = g[p]
                dl_s[p, rows, :] = g[P + p]
            return carry

        lax.fori_loop(0, L // ATT_MIX_ROWS, mix, 0)
        for ref in (dq_ref, dk_ref, dv_ref, db_ref):
            ref[...] = jnp.zeros_like(ref)

        def add(ref, rows, val):
            ref[rows, :] = ref[rows, :] + val

        for p, (_, dil) in enumerate(DILATED_PATTERNS):
            n_it, nb = _att_blocks(L, dil)

            def step(i, carry, p=p, dil=dil, nb=nb):
                n = i % nb
                first = n == 0
                cur = i // nb + n * (ATT_BLOCK * dil)
                prev = i // nb + jnp.maximum(n - 1, 0) * (ATT_BLOCK * dil)
                rc, rp = _att_rows(cur, dil), _att_rows(prev, dil)
                dq, dkp, dkc, dvp, dvc, db = _att_tile_grad(
                    first, q_ref[rc, :], k_ref[rp, :], k_ref[rc, :], v_ref[rp, :], v_ref[rc, :], b_ref[p],
                    o_ref[p, rc, :], l_ref[p, rc, :], do_s[p, rc, :], dl_s[p, rc, :])
                add(dq_ref, rc, dq)
                add(dk_ref, rc, dkc)
                add(dv_ref, rc, dvc)
                for h, dbh in enumerate(db):
                    db_ref[p, h] = db_ref[p, h] + dbh

                @pl.when(jnp.logical_not(first))
                def _():
                    add(dk_ref, rp, dkp)
                    add(dv_ref, rp, dvp)

                return carry

            lax.fori_loop(0, n_it, step, 0, unroll=ATT_UNROLL)
        wait()

    sds = jax.ShapeDtypeStruct((L, W), F32)
    return pl.pallas_call(
        body, grid=(nblk,),
        in_specs=[col(0), col(nblk), col(2 * nblk), bias_spec, per_pattern, per_pattern, col(0)] + ride.in_specs(),
        out_specs=[col(0), col(0), col(0), bias_spec] + ride.out_specs(),
        out_shape=[sds, sds, sds, jax.ShapeDtypeStruct(bias.shape, F32)] + ride.out_shape(),
        scratch_shapes=[pltpu.VMEM((P, L, LANES), F32)] * 2 + ride.scratch(),
        input_output_aliases=ride.aliases(n_in, n_out), name=name, compiler_params=_cp(1, VMEM_LIMIT_BIG))(
            qkv, qkv, qkv, bias, o_all, l_all, dy, *ride.operands())


def _t5_bucket(dist):
    n = np.maximum(dist, 0)
    max_exact = REL_BUCKETS // 2
    large = max_exact + (np.log(np.maximum(n, 1) / max_exact) / np.log(REL_MAX_DIST / max_exact)
                         * (REL_BUCKETS - max_exact)).astype(np.int64)
    large = np.minimum(large, REL_BUCKETS - 1)
    return np.where(n < max_exact, n, large).astype(np.int32)


def _bucket_onehot():
    a = np.arange(ATT_BLOCK)[:, None]
    b = np.arange(2 * ATT_BLOCK)[None, :]
    sub = a + ATT_BLOCK - b
    bucket = jnp.asarray(np.stack([_t5_bucket(sub * dil).reshape(-1) for _, dil in DILATED_PATTERNS]))
    ids = jnp.arange(REL_BUCKETS, dtype=jnp.int32)
    return (bucket[:, None, :] == ids[None, :, None]).astype(F32)


def loss_head(h, target, g, name):
    R, D = h.shape
    tr = _tile(R, (256,))

    def body(h_ref, t_ref, g_ref, l_ref, dh_ref, dg_ref):
        def lf(hv, gv):
            y = _rms_f(hv, gv)[0]
            return 0.5 * jnp.sum(jnp.mean(jnp.square(y - t_ref[...]), axis=-1))

        l, (dh, dg) = jax.value_and_grad(lf, argnums=(0, 1))(h_ref[...], g_ref[...])

        @pl.when(pl.program_id(0) == 0)
        def _():
            l_ref[...] = jnp.zeros_like(l_ref)
            dg_ref[...] = jnp.zeros_like(dg_ref)

        dh_ref[...] = dh
        dg_ref[...] += dg
        l_ref[...] += l

    rows = pl.BlockSpec((tr, D), lambda i: (i, 0))
    vec = pl.BlockSpec((1, D), lambda i: (0, 0))
    l, dh, dg = pl.pallas_call(
        body, grid=(R // tr,), in_specs=[rows, rows, vec],
        out_specs=[pl.BlockSpec((SUBLANES, LANES), lambda i: (0, 0)), rows, vec],
        out_shape=[jax.ShapeDtypeStruct((SUBLANES, LANES), F32), jax.ShapeDtypeStruct((R, D), F32),
                   jax.ShapeDtypeStruct((1, D), F32)], name=name, compiler_params=_cp(1))(h, target, g.reshape(1, D))
    return l[0, 0], dh, dg.reshape(D)


def _adamw_update(w, g, m, v):
    c1 = 1.0 - ADAM_B1 ** ADAM_STEP
    c2 = 1.0 - ADAM_B2 ** ADAM_STEP
    nm = ADAM_B1 * m + (1.0 - ADAM_B1) * g
    nv = ADAM_B2 * v + (1.0 - ADAM_B2) * jnp.square(g)
    return -ADAM_LR * ((nm / c1) / (jnp.sqrt(nv / c2) + ADAM_EPS) + ADAM_WD * w), nm, nv


def adamw_layers(w, parts, m, v, name, riders=None):
    nl, a, b = w.shape
    n_parts = parts[0].shape[0]
    tr = _row_tile(a, b)
    ride = _Riders(riders)
    grid = (nl, a // tr)
    n_in, n_out = nl + 3, 4

    def body(*refs):
        w_ref, p_refs, m_ref, v_ref = refs[0], refs[1:1 + nl], refs[1 + nl], refs[2 + nl]
        outs = refs[n_in + 2 * ride.n:]
        g_ref, d_ref, nm_ref, nv_ref = outs[:n_out]
        start, wait = ride.hooks(grid, refs[n_in:n_in + ride.n], outs[n_out:n_out + ride.n], outs[n_out + ride.n:])
        start()
        for l in range(nl):
            @pl.when(pl.program_id(0) == l)
            def _(p_ref=p_refs[l]):
                g = p_ref[0].astype(F32)
                for i in range(1, n_parts):
                    g = g + p_ref[i].astype(F32)
                d_ref[...], nm_ref[...], nv_ref[...] = _adamw_update(w_ref[...], g, m_ref[...], v_ref[...])
                g_ref[...] = g
        wait()

    rows = pl.BlockSpec((None, tr, b), lambda l, i: (l, i, 0))
    part = lambda k: pl.BlockSpec((n_parts, tr, b), lambda l, i: (0, jnp.where(l == k, i, 0), 0))
    sds = jax.ShapeDtypeStruct((nl, a, b), F32)
    return pl.pallas_call(
        body, grid=grid, in_specs=[rows] + [part(k) for k in range(nl)] + [rows, rows] + ride.in_specs(),
        out_specs=[rows] * n_out + ride.out_specs(), out_shape=[sds] * n_out + ride.out_shape(),
        scratch_shapes=ride.scratch(), input_output_aliases=ride.aliases(n_in, n_out), name=name,
        compiler_params=_cp(2))(w, *parts, m, v, *ride.operands())


def adamw(w, parts, m, v, name):
    R, C = w.shape
    n_parts = parts.shape[0]
    tr = _row_tile(R, C)

    def body(w_ref, p_ref, m_ref, v_ref, g_ref, d_ref, nm_ref, nv_ref):
        g = p_ref[0].astype(F32)
        for i in range(1, n_parts):
            g = g + p_ref[i].astype(F32)
        d_ref[...], nm_ref[...], nv_ref[...] = _adamw_update(w_ref[...], g, m_ref[...], v_ref[...])
        g_ref[...] = g

    rows = pl.BlockSpec((tr, C), lambda i: (i, 0))
    sds = jax.ShapeDtypeStruct((R, C), F32)
    return pl.pallas_call(body, grid=(R // tr,),
                          in_specs=[rows, pl.BlockSpec((n_parts, tr, C), lambda i: (0, i, 0)), rows, rows],
                          out_specs=[rows] * 4, out_shape=[sds] * 4, name=name, compiler_params=_cp(1))(w, parts, m, v)


HBM_SPEC = pl.BlockSpec(memory_space=pltpu.HBM)
MESH_ID = pl.DeviceIdType.MESH


def _place():
    return lax.axis_index("x"), lax.axis_index("y"), lax.axis_index("c")


def _index(x, y, c):
    return 4 * x + 2 * y + c


AG_COPIES = 9
AG_ROW_UNIT = 32


def all_gather(x, name):
    R, C = x.shape
    assert R % AG_ROW_UNIT == 0, x.shape
    half = R // 2

    def body(x_ref, out_ref, send_sems, recv_sems, local_sem):
        x_, y_, c_ = _place()
        me, sib = (x_, y_, c_), (x_, y_, 1 - c_)
        nx, ny, nd = (1 - x_, y_, c_), (x_, 1 - y_, c_), (1 - x_, 1 - y_, c_)
        upper, lower = pl.ds(0, half), pl.ds(half, half)

        def slot(dev, rows=None):
            ref = out_ref.at[_index(*dev)]
            return ref if rows is None else ref.at[rows]

        def copy(k, block, to, rows=None, src=None):
            return pltpu.make_async_remote_copy(
                src_ref=slot(block, rows) if src is None else src, dst_ref=slot(block, rows),
                send_sem=send_sems.at[k], recv_sem=recv_sems.at[k], device_id=to, device_id_type=MESH_ID)

        def other(dev):
            return (dev[0], dev[1], 1 - c_)

        mine = pltpu.make_async_copy(x_ref, slot(me), local_sem)
        mine.start()
        sent = [copy(0, me, sib, src=x_ref), copy(1, me, nx, src=x_ref), copy(2, me, ny, src=x_ref)]
        for cp in sent:
            cp.start()

        def then(arrival, *forwards):
            arrival.wait_recv()
            for cp in forwards:
                cp.start()
            sent.extend(forwards)

        then(copy(1, nx, me), copy(4, nx, ny, upper), copy(5, nx, sib))
        then(copy(2, ny, me), copy(3, ny, nx, lower), copy(6, ny, sib))
        then(copy(3, nd, me, lower), copy(8, nd, sib, lower))
        then(copy(4, nd, me, upper), copy(7, nd, sib, upper))
        copy(0, sib, me).wait_recv()
        copy(5, other(nx), me).wait_recv()
        copy(6, other(ny), me).wait_recv()
        copy(7, other(nd), me, upper).wait_recv()
        copy(8, other(nd), me, lower).wait_recv()
        for cp in sent:
            cp.wait_send()
        mine.wait()

    return pl.pallas_call(
        body, out_shape=jax.ShapeDtypeStruct((N_DEV,) + x.shape, x.dtype), in_specs=[HBM_SPEC], out_specs=HBM_SPEC,
        scratch_shapes=[pltpu.SemaphoreType.DMA((AG_COPIES,)), pltpu.SemaphoreType.DMA((AG_COPIES,)),
                        pltpu.SemaphoreType.DMA], name=name)(x)


def _chip(x, y):
    return 2 * x + y


def sibling_exchange(xs, name):
    n = len(xs)

    def body(*refs):
        x_refs, out_refs, (send_sems, recv_sems) = refs[:n], refs[n:2 * n], refs[2 * n:]
        x_, y_, c_ = _place()
        copies = [pltpu.make_async_remote_copy(src_ref=x_ref.at[:, 1 - c_], dst_ref=out_ref, send_sem=send_sems.at[p],
                                               recv_sem=recv_sems.at[p], device_id=(x_, y_, 1 - c_),
                                               device_id_type=MESH_ID)
                  for p, (x_ref, out_ref) in enumerate(zip(x_refs, out_refs))]
        for cp in copies:
            cp.start()
        for cp in copies:
            cp.wait()

    return pl.pallas_call(
        body, out_shape=[jax.ShapeDtypeStruct((x.shape[0],) + x.shape[2:], x.dtype) for x in xs],
        in_specs=[HBM_SPEC] * n, out_specs=[HBM_SPEC] * n,
        scratch_shapes=[pltpu.SemaphoreType.DMA((n,)), pltpu.SemaphoreType.DMA((n,))], name=name)(*xs)


def _row_tile(R, C):
    cap = max(SUBLANES, STREAM_BLOCK_BYTES // (4 * C))
    return _tile(R, [t for t in (512, 256, 128, 64, 32, 16, 8) if t <= cap])


def pair_sum(x, recv, name):
    nc, _, R, C = x.shape
    tr = _row_tile(R, C)
    core = lax.axis_index("c").astype(jnp.int32).reshape(1)

    def body(c_ref, a_ref, b_ref, o_ref):
        o_ref[...] = (a_ref[...].astype(F32) + b_ref[...].astype(F32)).astype(o_ref.dtype)

    blk = pl.BlockSpec((None, tr, C), lambda k, i, c_ref: (k, i, 0))
    grid_spec = pltpu.PrefetchScalarGridSpec(
        num_scalar_prefetch=1, grid=(nc, R // tr),
        in_specs=[pl.BlockSpec((None, None, tr, C), lambda k, i, c_ref: (k, c_ref[0], i, 0)), blk], out_specs=blk)
    return pl.pallas_call(body, grid_spec=grid_spec, out_shape=jax.ShapeDtypeStruct((nc, R, C), x.dtype), name=name,
                          compiler_params=_cp(2))(core, x, recv)


def chip_exchange(ss, name):
    n = len(ss)

    def body(*refs):
        copies = _chip_exchange_copies(refs[:n], refs[n:2 * n], (0, 1, 1), *refs[2 * n:])
        for cp in copies:
            cp.start()
        for cp in copies:
            cp.wait()

    n_sem = n * (N_CHIP - 1)
    return pl.pallas_call(
        body, out_shape=[jax.ShapeDtypeStruct(s.shape, s.dtype) for s in ss], in_specs=[HBM_SPEC] * n,
        out_specs=[HBM_SPEC] * n,
        scratch_shapes=[pltpu.SemaphoreType.DMA((n_sem,)), pltpu.SemaphoreType.DMA((n_sem,)),
                        pltpu.SemaphoreType.DMA((n,))], name=name)(*ss)


def _block_diag(w, nb):
    G, a, b = w.shape
    gp = G // nb
    eye = jnp.eye(gp, dtype=w.dtype)
    return jnp.einsum('jgab,gh->jgahb', w.reshape(nb, gp, a, b), eye).reshape(nb, gp * a, gp * b)


def _split_columns(x, cuts):
    edges = (0,) + tuple(cuts) + (x.shape[1],)

    def split(x):
        return tuple(x[:, a:b] for a, b in zip(edges[:-1], edges[1:]))

    op = jax.custom_vjp(split)
    op.defvjp(lambda x: (split(x), None), lambda _, cts: (jnp.concatenate(cts, axis=-1),))
    return op(x)


def _project_in(l, h, P):
    L, D = h.shape
    GW = D // N_MIXERS
    proj = _act_linear(f"l{l}_w_in", _rms_op(f"l{l}_norm_mix", L, D, BF16), 2, False)(
        h, P['norm_mix_g'].reshape(1, D), P['w_in'])
    return _split_columns(proj, (GW, 2 * GW, 4 * GW))


def _mix_and_memory(l, u_a, u_b, u_c, y_d, h, memn, P):
    nm = lambda s: f"l{l}_{s}"
    L, D = h.shape
    GW = D // N_MIXERS
    G = GW // S5_CH_PER_GROUP

    row = lambda g: g.reshape(1, D)

    v3 = lambda a: a.reshape(G, 1, S5_STATE)
    log_dt = jnp.broadcast_to(P['s5_log_dt'][:, None, None], (G, 1, S5_STATE))
    a_r, a_i, bb_r, bb_i = s5_discretise(v3(P['s5_lam_re']), v3(P['s5_lam_im']), log_dt,
                                         P['s5_b_re'].transpose(0, 2, 1), P['s5_b_im'].transpose(0, 2, 1), nm("s5_disc"))
    nblk = GW // S5_BLOCK_CH
    y_s5 = _s5_core(nm("s5_core"))(
        u_a, _block_diag(bb_r, nblk), _block_diag(bb_i, nblk), a_r.reshape(1, G * S5_STATE), a_i.reshape(1, G * S5_STATE),
        _block_diag(P['s5_c_re'].transpose(0, 2, 1), nblk), _block_diag(P['s5_c_im'].transpose(0, 2, 1), nblk))
    y_a = s5_epilogue(y_s5, u_a, P['s5_d'], P['s5_w_glu'], nm("s5_glu"))

    y_b = pool_proj(_pool_mix(nm("pool_mix"))(u_b), P['pool_w'], P['pool_scale'], nm("pool_proj"))

    hc = _glu_conv(nm("conv_dw"))(u_c, P['conv_w_dw'], P['conv_b_dw'].reshape(1, GW))
    y_c = conv_post(hc, P['conv_ln_g'], P['conv_ln_b'], P['conv_w_pw'], nm("conv_post"))

    grp = _group_norm_op(nm("grp_norm"), L, GW, N_MIXERS, BF16)
    h = _act_linear(nm("w_out"), grp, N_MIXERS + 1, True)(y_a, y_b, y_c, y_d, row(P['grp_norm_g']), P['w_out'], h)

    xq = _act_linear(nm("w_xq"), _rms_op(nm("norm_x"), L, D, BF16), 2, False)(h, row(P['norm_x_g']), P['w_xq'])
    xk = _linear(nm("w_xk"))(memn, P['w_xk'])
    xv = _linear(nm("w_xv"))(memn, P['w_xv'])
    xat = _cross_attention_op(nm("xattn"), L, xq.shape[1], memn.shape[0], BF16)
    return _act_linear(nm("w_xo"), xat, 3, True)(xq, xk, xv, P['w_xo'], h)


def _bias_tables(rel_bias):
    tabs = rel_bias_tables(rel_bias, _bucket_onehot(), "rel_bias")
    return tabs.reshape(len(DILATED_PATTERNS), ATT_HEADS, ATT_BLOCK, 2 * ATT_BLOCK)


def _gather_weight(name, w):
    ax = SHARDED[name]
    dt = BF16 if name in GATHER_BF16 else F32
    nl, a, b = w.shape
    rows = nl * a
    flat = jnp.pad(w.astype(dt).reshape(rows, b), ((0, (-rows) % AG_ROW_UNIT), (0, 0)))
    g = all_gather(flat, "ag_" + name)
    if name in MLP_SHARDED:
        assert rows % AG_ROW_UNIT == 0
        return g
    g = g[:, :rows].reshape(N_DEV, nl, a, b)
    if ax == 1:
        return g.transpose(1, 0, 2, 3).reshape(nl, N_DEV * a, b)
    return g.transpose(1, 2, 0, 3).reshape(nl, a, N_DEV * b)


def _scatter_grad(name, g):
    ax = SHARDED[name]
    nl = g.shape[0]
    if ax == 1:
        a, b = g.shape[1] // N_DEV, g.shape[2]
        s = g.reshape(nl, N_DEV, a, b).transpose(1, 0, 2, 3)
    else:
        a, b = g.shape[1], g.shape[2] // N_DEV
        s = g.reshape(nl, a, N_DEV, b).transpose(2, 0, 1, 3)
    s = s.reshape(N_CHIP, N_DEV // N_CHIP, nl * a, b)
    pair = pair_sum(s, sibling_exchange([s], "d2d_" + name)[0], "pairsum_" + name)
    return chip_exchange([pair], "ici_" + name)[0]


def _by_destination(name, g):
    if g.ndim == 2 and SHARDED[name] == 1:
        g = g.reshape(N_DEV, g.shape[0] // N_DEV, g.shape[1])
    elif g.ndim == 2:
        g = g.reshape(g.shape[0], N_DEV, g.shape[1] // N_DEV).transpose(1, 0, 2)
    return g.reshape(N_CHIP, N_DEV // N_CHIP, *g.shape[1:])


def _flatten_small(d):
    flat = jnp.concatenate([d[n].reshape(-1).astype(F32) for n in SMALL])
    pad = (-flat.shape[0]) % (LANES * SMALL_ROW_TILE)
    return jnp.pad(flat, (0, pad)).reshape(-1, LANES)


def _split_small(flat, like):
    flat = flat.reshape(-1)
    out, off = {}, 0
    for n in SMALL:
        sz = math.prod(like[n].shape)
        out[n] = flat[off:off + sz].reshape(like[n].shape)
        off += sz
    return out


def kernel(x, mem, rel_bias, mem_norm_g, norm_mix_g, w_in, s5_lam_re, s5_lam_im, s5_log_dt, s5_b_re, s5_b_im, s5_c_re, s5_c_im, s5_d, s5_w_glu, pool_w, pool_scale, conv_w_dw, conv_b_dw, conv_ln_g, conv_ln_b, conv_w_pw, grp_norm_g, w_out, norm_x_g, w_xq, w_xk, w_xv, w_xo, norm_mlp_g, w_up, w_down, norm_final_g, loss_target, m_rel_bias, m_mem_norm_g, m_norm_mix_g, m_w_in, m_s5_lam_re, m_s5_lam_im, m_s5_log_dt, m_s5_b_re, m_s5_b_im, m_s5_c_re, m_s5_c_im, m_s5_d, m_s5_w_glu, m_pool_w, m_pool_scale, m_conv_w_dw, m_conv_b_dw, m_conv_ln_g, m_conv_ln_b, m_conv_w_pw, m_grp_norm_g, m_w_out, m_norm_x_g, m_w_xq, m_w_xk, m_w_xv, m_w_xo, m_norm_mlp_g, m_w_up, m_w_down, m_norm_final_g, v_rel_bias, v_mem_norm_g, v_norm_mix_g, v_w_in, v_s5_lam_re, v_s5_lam_im, v_s5_log_dt, v_s5_b_re, v_s5_b_im, v_s5_c_re, v_s5_c_im, v_s5_d, v_s5_w_glu, v_pool_w, v_pool_scale, v_conv_w_dw, v_conv_b_dw, v_conv_ln_g, v_conv_ln_b, v_conv_w_pw, v_grp_norm_g, v_w_out, v_norm_x_g, v_w_xq, v_w_xk, v_w_xv, v_w_xo, v_norm_mlp_g, v_w_up, v_w_down, v_norm_final_g):
    w = dict(zip(WEIGHTS, (rel_bias, mem_norm_g, norm_mix_g, w_in, s5_lam_re, s5_lam_im, s5_log_dt, s5_b_re, s5_b_im, s5_c_re, s5_c_im, s5_d, s5_w_glu, pool_w, pool_scale, conv_w_dw, conv_b_dw, conv_ln_g, conv_ln_b, conv_w_pw, grp_norm_g, w_out, norm_x_g, w_xq, w_xk, w_xv, w_xo, norm_mlp_g, w_up, w_down, norm_final_g)))
    m = dict(zip(WEIGHTS, (m_rel_bias, m_mem_norm_g, m_norm_mix_g, m_w_in, m_s5_lam_re, m_s5_lam_im, m_s5_log_dt, m_s5_b_re, m_s5_b_im, m_s5_c_re, m_s5_c_im, m_s5_d, m_s5_w_glu, m_pool_w, m_pool_scale, m_conv_w_dw, m_conv_b_dw, m_conv_ln_g, m_conv_ln_b, m_conv_w_pw, m_grp_norm_g, m_w_out, m_norm_x_g, m_w_xq, m_w_xk, m_w_xv, m_w_xo, m_norm_mlp_g, m_w_up, m_w_down, m_norm_final_g)))
    v = dict(zip(WEIGHTS, (v_rel_bias, v_mem_norm_g, v_norm_mix_g, v_w_in, v_s5_lam_re, v_s5_lam_im, v_s5_log_dt, v_s5_b_re, v_s5_b_im, v_s5_c_re, v_s5_c_im, v_s5_d, v_s5_w_glu, v_pool_w, v_pool_scale, v_conv_w_dw, v_conv_b_dw, v_conv_ln_g, v_conv_ln_b, v_conv_w_pw, v_grp_norm_g, v_w_out, v_norm_x_g, v_w_xq, v_w_xk, v_w_xv, v_w_xo, v_norm_mlp_g, v_w_up, v_w_down, v_norm_final_g)))

    full = {n: (_gather_weight(n, w[n]) if n in SHARDED else w[n]) for n in WEIGHTS if n != 'norm_final_g'}
    L, D = x.shape[1:]

    memn, mem_vjp = jax.vjp(lambda a, g: rmsnorm(a, g, "mem_norm"), mem[0], w['mem_norm_g'])
    tabs, tabs_vjp = jax.vjp(_bias_tables, w['rel_bias'])
    h = x[0]
    stages = []
    for l in range(DEPTH):
        layer = lambda names: {n: full[n][l] for n in names}
        (u_a, u_b, u_c, qkv), in_vjp = jax.vjp(functools.partial(_project_in, l), h, layer(IN_WEIGHTS))
        y_d, o_all, l_all = _att_fwd(qkv, tabs, f"l{l}_att_fwd")
        h, mix_vjp = jax.vjp(functools.partial(_mix_and_memory, l), u_a, u_b, u_c, y_d, h, memn, layer(MIX_WEIGHTS))
        norm = _rms_op(f"l{l}_norm_mlp", L, D, BF16)
        h, saved = _mlp_fwd(f"l{l}_mlp", norm, h, full['norm_mlp_g'][l].reshape(1, D), full['w_up'], full['w_down'], l)
        stages.append((in_vjp, (qkv, tabs, o_all, l_all), mix_vjp, norm, saved))
    loss_local, dh, d_final_g = loss_head(h, loss_target[0], w['norm_final_g'], "loss_head")
    loss = lax.psum(loss_local, MESH_AXES)

    def pair_sums(l, names, grads_l):
        by_dest = [_by_destination(n, grads_l[n]) for n in names]
        theirs = sibling_exchange(by_dest, f"d2d_l{l}_{names[0]}")
        return [pair_sum(s, t, f"pairsum_l{l}_{n}") for n, s, t in zip(names, by_dest, theirs)]

    empties = lambda like: [lax.empty(p.shape, p.dtype) for p in like]
    layer_grads, arrived = [None] * DEPTH, [None] * DEPTH
    pending, dmemn, dtabs = None, 0.0, 0.0
    for l in reversed(range(DEPTH)):
        in_vjp, att_saved, mix_vjp, norm, saved = stages[l]
        lands = None if pending is None else empties(pending)
        dh, dg_mlp, dw_up, dw_down, lands = _mlp_bwd(f"l{l}_mlp", norm, saved, dh, pending, lands)
        mlp_grads = dict(norm_mlp_g=dg_mlp.reshape(D), w_up=dw_up, w_down=dw_down)
        du_a, du_b, du_c, dy_d, dh_res, dmemn_l, d_mix = mix_vjp(dh)
        rides = [] if pending is None else [(pending, lands, ATT_RIDE_WINDOW)]
        if l == 0:
            early = pair_sums(l, MLP_SHARDED, mlp_grads)
            rides.append((early, empties(early), (0, 1, 1)))
        dq, dk, dv, dtabs_l, *landed = _att_bwd(*att_saved, dy_d, f"l{l}_att_bwd", riders=rides)
        if pending is not None:
            arrived[l + 1] = landed[:len(pending)]
        dh_in, d_in = in_vjp((du_a, du_b, du_c, jnp.concatenate([dq, dk, dv], axis=-1)))
        dh = dh_in + dh_res
        dmemn, dtabs = dmemn + dmemn_l, dtabs + dtabs_l
        layer_grads[l] = {**d_in, **d_mix, **mlp_grads}
        if l > 0:
            pending = pair_sums(l, GATHER_BF16, layer_grads[l])
        else:
            late = [n for n in GATHER_BF16 if n not in MLP_SHARDED]
            late_pairs = pair_sums(l, late, layer_grads[l])
            got = dict(zip(MLP_SHARDED, landed[-len(MLP_SHARDED):]))
    dx = dh
    dfull = {n: jnp.concatenate([layer_grads[l][n][None] for l in range(DEPTH)])
             for n in LAYER_WEIGHTS if n not in GATHER_BF16}
    dfull['mem_norm_g'] = mem_vjp(dmemn)[1]
    dfull['rel_bias'] = tabs_vjp(dtabs)[0]
    dfull['norm_final_g'] = d_final_g

    grads, deltas, new_m, new_v = {}, {}, {}, {}
    late_lands = empties(late_pairs)
    for half, n in enumerate(MLP_SHARDED):
        parts = [got[n] if l == 0 else arrived[l][GATHER_BF16.index(n)] for l in range(DEPTH)]
        out = adamw_layers(w[n], parts, m[n], v[n], "adamw_" + n,
                           riders=[(late_pairs, late_lands, (half, 1, len(MLP_SHARDED)))])
        grads[n], deltas[n], new_m[n], new_v[n] = out[:4]
        late_lands = list(out[4:])
    got.update(zip(late, late_lands))
    arrived[0] = [got[n] for n in GATHER_BF16]
    for k, n in enumerate(GATHER_BF16):
        if n in MLP_SHARDED:
            continue
        res = adamw_layers(w[n], [arrived[l][k] for l in range(DEPTH)], m[n], v[n], "adamw_" + n)
        grads[n], deltas[n], new_m[n], new_v[n] = res
    for n in SHARDED:
        if n in GATHER_BF16:
            continue
        parts = _scatter_grad(n, dfull[n])
        shp = w[n].shape
        two_d = lambda a: a.reshape(shp[0] * shp[1], shp[2])
        res = adamw(two_d(w[n]), parts, two_d(m[n]), two_d(v[n]), "adamw_" + n)
        grads[n], deltas[n], new_m[n], new_v[n] = (r.reshape(shp) for r in res)

    parts = all_gather(_flatten_small(dfull), "ag_small_grads")
    res = adamw(_flatten_small(w), parts, _flatten_small(m), _flatten_small(v), "adamw_small")
    for dst, r in zip((grads, deltas, new_m, new_v), res):
        dst.update(_split_small(r, w))

    return (loss, dx[None], *[grads[n] for n in WEIGHTS], *[deltas[n] for n in WEIGHTS],
            *[new_m[n] for n in WEIGHTS], *[new_v[n] for n in WEIGHTS])
```

```python
import functools
import math

import numpy as np
import jax
import jax.numpy as jnp
from jax import lax
from jax.experimental import pallas as pl
from jax.experimental.pallas import tpu as pltpu

F32 = jnp.float32
BF16 = jnp.bfloat16

DEPTH = 4
N_MIXERS = 4
S5_CH_PER_GROUP = 16
S5_STATE = 64
POOL_WINDOWS = (2, 4, 8, 16)
CONV_WIDTH = 31
ATT_HEADS = 8
DILATED_PATTERNS = ((128, 1), (512, 4), (2048, 16))
ATT_BLOCK = 128
ATT_MIX_ROWS = 256
ATT_UNROLL = 4
REL_BUCKETS = 32
REL_MAX_DIST = 2048
X_HEADS = 4
X_HEAD_DIM = 128
NORM_EPS = 1e-6
NEG_INF = -1e30
ADAM_LR = 0.001
ADAM_B1 = 0.9
ADAM_B2 = 0.999
ADAM_EPS = 1e-08
ADAM_WD = 0.01
ADAM_STEP = 10

LANES = 128
SUBLANES = 8
VMEM_BYTES = 64 * 1024 * 1024
VMEM_LIMIT = (VMEM_BYTES * 3) // 4
VMEM_LIMIT_BIG = (VMEM_BYTES * 7) // 8
STREAM_BLOCK_BYTES = 1024 * 1024
PAIR_SUM_BLOCK_BYTES = 4 * STREAM_BLOCK_BYTES
SMALL_ROW_TILE = 512
N_DEV = 8
N_CHIP = 4
MESH_AXES = ("x", "y", "c")

WEIGHTS = ['rel_bias', 'mem_norm_g', 'norm_mix_g', 'w_in', 's5_lam_re', 's5_lam_im', 's5_log_dt', 's5_b_re',
           's5_b_im', 's5_c_re', 's5_c_im', 's5_d', 's5_w_glu', 'pool_w', 'pool_scale', 'conv_w_dw', 'conv_b_dw',
           'conv_ln_g', 'conv_ln_b', 'conv_w_pw', 'grp_norm_g', 'w_out', 'norm_x_g', 'w_xq', 'w_xk', 'w_xv', 'w_xo',
           'norm_mlp_g', 'w_up', 'w_down', 'norm_final_g']
SHARDED = {'w_in': 2, 's5_w_glu': 1, 'conv_w_dw': 2, 'conv_w_pw': 1, 'w_out': 1, 'w_xq': 1, 'w_xk': 1, 'w_xv': 1,
           'w_xo': 2, 'w_up': 2, 'w_down': 1}
GATHER_BF16 = ('w_in', 'w_out', 'w_xq', 'w_xk', 'w_xv', 'w_xo', 'w_up', 'w_down')
SMALL = [n for n in WEIGHTS if n not in SHARDED]
LAYER_WEIGHTS = [n for n in WEIGHTS if n not in ('rel_bias', 'mem_norm_g', 'norm_final_g')]
IN_WEIGHTS = ('norm_mix_g', 'w_in')
MLP_WEIGHTS = ('norm_mlp_g', 'w_up', 'w_down')
MIX_WEIGHTS = [n for n in LAYER_WEIGHTS if n not in IN_WEIGHTS + MLP_WEIGHTS]
MLP_SHARDED = ('w_up', 'w_down')

def _cp(n_axes, vmem=VMEM_LIMIT):
    return pltpu.CompilerParams(dimension_semantics=("arbitrary",) * n_axes, vmem_limit_bytes=vmem)


def _tile(n, prefs):
    for t in prefs:
        if n % t == 0:
            return t
    return n


MM_TILE = 1024
MM_TILE_K = 2048
MM_FULL_K = 4096


def _chip_exchange_copies(srcs, dsts, window, send_sems, recv_sems, local_sems, base=0):
    x_, y_, c_ = _place()
    me = _chip(x_, y_)
    first, count, total = window
    copies = []
    for p, (src, dst) in enumerate(zip(srcs, dsts)):
        unit = src.shape[1] // total
        rows = pl.ds(first * unit, count * unit)
        copies.append(pltpu.make_async_copy(src.at[me, rows], dst.at[me, rows], local_sems.at[base + p]))
        for k in range(1, N_CHIP):
            px = 1 - x_ if k & 2 else x_
            py = 1 - y_ if k & 1 else y_
            s = (base + p) * (N_CHIP - 1) + k - 1
            copies.append(pltpu.make_async_remote_copy(
                src_ref=src.at[_chip(px, py), rows], dst_ref=dst.at[me, rows], send_sem=send_sems.at[s],
                recv_sem=recv_sems.at[s], device_id=(px, py, c_), device_id_type=MESH_ID))
    return copies


class _Riders:
    def __init__(self, rides):
        self.rides = rides or []
        self.srcs = [s for r in self.rides for s in r[0]]
        self.dsts = [d for r in self.rides for d in r[1]]
        self.n = len(self.srcs)

    def operands(self):
        return (*self.srcs, *self.dsts)

    def in_specs(self):
        return [HBM_SPEC] * (2 * self.n)

    def out_specs(self):
        return [HBM_SPEC] * self.n

    def out_shape(self):
        return [jax.ShapeDtypeStruct(d.shape, d.dtype) for d in self.dsts]

    def scratch(self):
        if not self.n:
            return []
        n_sem = self.n * (N_CHIP - 1)
        return [pltpu.SemaphoreType.DMA((n_sem,)), pltpu.SemaphoreType.DMA((n_sem,)), pltpu.SemaphoreType.DMA((self.n,))]

    def aliases(self, first_in, first_out):
        return {first_in + self.n + p: first_out + p for p in range(self.n)}

    def hooks(self, grid, src_refs, dst_refs, sems):
        if not self.n:
            return (lambda: None), (lambda: None)

        def copies():
            out, base = [], 0
            for srcs, _, window in self.rides:
                k = len(srcs)
                out += _chip_exchange_copies(src_refs[base:base + k], dst_refs[base:base + k], window, *sems, base=base)
                base += k
            return out

        ids = [pl.program_id(ax) for ax in range(len(grid))]
        at_start = functools.reduce(jnp.logical_and, [i == 0 for i in ids])
        at_end = functools.reduce(jnp.logical_and, [i == g - 1 for i, g in zip(ids, grid)])

        def start():
            @pl.when(at_start)
            def _():
                for cp in copies():
                    cp.start()

        def wait():
            @pl.when(at_end)
            def _():
                for cp in copies():
                    cp.wait()

        return start, wait


def _mm(a, b, *, ta=False, tb=False, res=None, out_dtype=F32, epilogue=None, pre=None, riders=None, dest_cols=None,
        b_slots=None, name):
    if ta:
        K, M = a.shape
    else:
        M, K = a.shape
    if b_slots is not None:
        layer, n_layers, axis = b_slots
        a_blk, b_blk = b.shape[1] // n_layers, b.shape[2]
        w_shape = (N_DEV * a_blk, b_blk) if axis == 1 else (a_blk, N_DEV * b_blk)
    else:
        w_shape = b.shape
    N, K2 = w_shape if tb else w_shape[::-1]
    assert K == K2, (a.shape, b.shape, ta, tb)
    wide_f32 = K >= MM_TILE_K and F32 in (a.dtype, b.dtype)
    tm = _tile(M, (MM_TILE // 2 if wide_f32 and not ta else MM_TILE, 512, 256, 128))
    tn = _tile(N, (MM_TILE, 512, 256, 128)) if dest_cols is None else dest_cols
    tk = K if K <= MM_FULL_K else _tile(K, (MM_TILE_K, 1024, 512, 256, 128))
    n_b = 1
    if b_slots is not None:
        assert not ta
        sharded_is_k = (axis == 1) != tb
        slot = a_blk if axis == 1 else b_blk
        if sharded_is_k:
            n_b = _tile(N_DEV, [n for n in (8, 4, 2) if n * slot <= MM_TILE_K])
            tk = n_b * slot
        else:
            tn = slot
    nk = K // tk
    a_spec = pl.BlockSpec((tk, tm), lambda i, j, k: (k, i)) if ta else pl.BlockSpec((tm, tk), lambda i, j, k: (i, k))
    if b_slots is None:
        b_specs = [pl.BlockSpec((tn, tk), lambda i, j, k: (j, k)) if tb
                   else pl.BlockSpec((tk, tn), lambda i, j, k: (k, j))]
    elif axis == 1 and not tb:
        b_specs = [pl.BlockSpec((None, slot, tn), lambda i, j, k, q=q: (n_b * k + q, layer, j)) for q in range(n_b)]
    elif axis == 1:
        b_specs = [pl.BlockSpec((None, tn, tk), lambda i, j, k: (j, layer, k))]
    elif not tb:
        b_specs = [pl.BlockSpec((None, tk, tn), lambda i, j, k: (j, layer * (a_blk // tk) + k, 0))]
    else:
        b_specs = [pl.BlockSpec((None, tn, slot), lambda i, j, k, q=q: (n_b * k + q, layer * (a_blk // tn) + j, 0))
                   for q in range(n_b)]
    o_spec = pl.BlockSpec((tm, tn), lambda i, j, k: (i, j))
    dn = (((0 if ta else 1,), (1 if tb else 0,)), ((), ()))
    extra = [x for x in (res, pre) if x is not None]
    assert not (res is not None and pre is not None)
    assert dest_cols is None or (epilogue is None and not extra and tn <= MM_TILE)
    n_out = 2 if epilogue == 'relu_sq' else 1
    ride = _Riders(riders)
    n_pairs = ride.n
    grid = (M // tm, N // tn, nk)

    def body(*refs):
        a_ref, b_refs = refs[0], refs[1:1 + n_b]
        x_ref = refs[1 + n_b] if extra else None
        n_in = 1 + n_b + len(extra) + 2 * n_pairs
        o_refs = refs[n_in:n_in + n_out]
        scratch = refs[n_in + n_out + n_pairs:]
        acc = scratch[0] if nk > 1 else None
        start, wait = ride.hooks(grid, refs[n_in - 2 * n_pairs:n_in - n_pairs],
                                 refs[n_in + n_out:n_in + n_out + n_pairs], scratch[-3:])
        start()

        def finish(r):
            if res is not None:
                r = r + x_ref[...]
            if epilogue == 'relu_sq':
                o_refs[0][...] = r
                o_refs[1][...] = jnp.square(jnp.maximum(r, 0.0)).astype(out_dtype)
            elif epilogue == 'relu_sq_grad':
                o_refs[0][...] = (r * (2.0 * jnp.maximum(x_ref[...], 0.0))).astype(out_dtype)
            else:
                o_refs[0][...] = r.astype(out_dtype)

        dot = lambda x, y: lax.dot_general(x.astype(BF16), y[...].astype(BF16), dn, preferred_element_type=F32)
        if n_b == 1:
            part = dot(a_ref[...], b_refs[0])
        else:
            part = sum(dot(a_ref[:, q * slot:(q + 1) * slot], b_refs[q]) for q in range(n_b))
        if nk == 1:
            finish(part)
        else:
            k = pl.program_id(2)

            @pl.when(k == 0)
            def _():
                acc[...] = part

            @pl.when(k > 0)
            def _():
                acc[...] += part

            @pl.when(k == nk - 1)
            def _():
                finish(acc[...])

        wait()

    out_shape = [jax.ShapeDtypeStruct((M, N), F32 if epilogue == 'relu_sq' else out_dtype)]
    if n_out == 2:
        out_shape.append(jax.ShapeDtypeStruct((M, N), out_dtype))
    in_specs = [a_spec, *b_specs] + [o_spec] * len(extra)
    out_specs = [o_spec] * n_out
    if dest_cols is not None:
        out_shape = [jax.ShapeDtypeStruct((N // tn, M, tn), out_dtype)]
        out_specs = [pl.BlockSpec((None, tm, tn), lambda i, j, k: (j, i, 0))]
    scratch = ([pltpu.VMEM((tm, tn), F32)] if nk > 1 else []) + ride.scratch()
    args = (a, *[b] * n_b, *extra)
    outs = pl.pallas_call(
        body, grid=grid, in_specs=in_specs + ride.in_specs(), out_specs=out_specs + ride.out_specs(),
        out_shape=out_shape + ride.out_shape(), scratch_shapes=scratch,
        input_output_aliases=ride.aliases(len(args), n_out), name=name, compiler_params=_cp(3, VMEM_LIMIT_BIG))(
            *args, *ride.operands())
    return outs[0] if len(outs) == 1 else tuple(outs)


def _linear(name):
    @jax.custom_vjp
    def lin(a, w):
        return _mm(a, w, name=name + "_fwd")

    def fwd(a, w):
        return _mm(a, w, name=name + "_fwd"), (a, w)

    def bwd(r, dy):
        a, w = r
        da = _mm(dy, w, tb=True, name=name + "_dx")
        dw = _mm(a, dy, ta=True, out_dtype=w.dtype, name=name + "_dw")
        return da, dw

    lin.defvjp(fwd, bwd)
    return lin


def _act_linear(name, act, n_in, with_res):
    def run(*a):
        ins, w = a[:n_in], a[n_in]
        x = act.fwd_call(*ins)[0]
        return _mm(x, w, res=a[n_in + 1] if with_res else None, name=name + "_fwd"), (ins, x, w)

    @jax.custom_vjp
    def op(*a):
        return run(*a)[0]

    def bwd(r, dy):
        ins, x, w = r
        dx = _mm(dy, w, tb=True, name=name + "_dx")
        dw = _mm(x, dy, ta=True, out_dtype=w.dtype, name=name + "_dw")
        return (*act.bwd_all(ins, (dx,)), dw) + ((dy,) if with_res else ())

    op.defvjp(run, bwd)
    return op


def _mlp_fwd(name, norm, h, g, w_up, w_down, layer):
    hn = norm.fwd_call(h, g)[0]
    up, down = (layer, DEPTH, SHARDED['w_up']), (layer, DEPTH, SHARDED['w_down'])
    a, r = _mm(hn, w_up, epilogue='relu_sq', out_dtype=BF16, b_slots=up, name=name + "_up_fwd")
    return _mm(r, w_down, res=h, b_slots=down, name=name + "_down_fwd"), (h, g, hn, a, r, w_up, w_down, up, down)


RIDE_UNITS = 16
MLP_RIDE_UNITS = (3, 3, 3, 2)
ATT_RIDE_WINDOW = (sum(MLP_RIDE_UNITS), RIDE_UNITS - sum(MLP_RIDE_UNITS), RIDE_UNITS)


def _mlp_bwd(name, norm, saved, dy, pending, lands):
    h, g, hn, a, r, w_up, w_down, up, down = saved

    def mm(i, *args, **kw):
        nonlocal lands
        if pending is None:
            return _mm(*args, **kw)
        window = (sum(MLP_RIDE_UNITS[:i]), MLP_RIDE_UNITS[i], RIDE_UNITS)
        out, *lands = _mm(*args, riders=[(pending, lands, window)], **kw)
        return out

    da = mm(0, dy, w_down, tb=True, epilogue='relu_sq_grad', pre=a, out_dtype=BF16, b_slots=down,
            name=name + "_down_dx")
    dw_down = mm(1, r, dy, ta=True, out_dtype=w_down.dtype, name=name + "_down_dw")
    dhn = mm(2, da, w_up, tb=True, b_slots=up, name=name + "_up_dx")
    dw_up = mm(3, hn, da, ta=True, out_dtype=w_up.dtype, dest_cols=w_up.shape[2], name=name + "_up_dw")
    dh, dg = norm.bwd_all((h, g), (dhn,), add_to_first=dy)
    return dh, dg, dw_up, dw_down, lands


def _block_op(name, f, grid, ins, outs, vmem=VMEM_LIMIT):
    n_in, n_out = len(ins), len(outs)
    in_specs = [pl.BlockSpec(bs, im) for bs, im, _, _ in ins]
    out_specs = [pl.BlockSpec(bs, im) for _, _, bs, im in outs]
    out_shape = [jax.ShapeDtypeStruct(s, d) for s, d, _, _ in outs]
    didx = [i for i in range(n_in) if ins[i][3]]

    def fwd_call(*args):
        def body(*refs):
            res = f(*[r[...] for r in refs[:n_in]])
            for r, o in zip(refs[n_in:], res):
                r[...] = o.astype(r.dtype)

        return pl.pallas_call(body, grid=grid, in_specs=in_specs, out_specs=out_specs, out_shape=out_shape,
                              name=name + "_fwd", compiler_params=_cp(len(grid), vmem))(*args)

    def bwd_call(args, cts, add_to_first=None):
        n_add = 0 if add_to_first is None else 1

        def body(*refs):
            vals = [r[...] for r in refs[:n_in]]
            ct_refs = refs[n_in:n_in + n_out]
            g_refs = refs[n_in + n_out + n_add:]

            def fd(*dv):
                full = list(vals)
                for i, v in zip(didx, dv):
                    full[i] = v
                return f(*full)

            _, vjp = jax.vjp(fd, *[vals[i] for i in didx])
            grads = list(vjp(tuple(r[...] for r in ct_refs)))
            if n_add:
                grads[0] = grads[0] + refs[n_in + n_out][...]
            for gref, i, g in zip(g_refs, didx, grads):
                acc = ins[i][2]
                if acc:
                    first = functools.reduce(jnp.logical_and, [pl.program_id(ax) == 0 for ax in acc])

                    @pl.when(first)
                    def _(gref=gref):
                        gref[...] = jnp.zeros_like(gref)

                    gref[...] += g.astype(gref.dtype)
                else:
                    gref[...] = g.astype(gref.dtype)

        g_specs = [pl.BlockSpec(ins[i][0], ins[i][1]) for i in didx]
        g_shape = [jax.ShapeDtypeStruct(args[i].shape, args[i].dtype) for i in didx]
        assert not n_add or (didx[0] == 0 and not ins[0][2])
        added = [] if add_to_first is None else [add_to_first]
        return pl.pallas_call(body, grid=grid, in_specs=in_specs + out_specs + in_specs[:n_add], out_specs=g_specs,
                              out_shape=g_shape, name=name + "_bwd", compiler_params=_cp(len(grid), vmem))(
                                  *args, *cts, *added)

    @jax.custom_vjp
    def op(*args):
        return tuple(fwd_call(*args))

    def op_fwd(*args):
        return tuple(fwd_call(*args)), args

    def op_bwd(args, cts, add_to_first=None):
        it = iter(bwd_call(args, cts, add_to_first))
        return tuple(next(it) if ins[i][3] else jnp.zeros_like(args[i]) for i in range(n_in))

    op.defvjp(op_fwd, op_bwd)
    op.fwd_call = fwd_call
    op.bwd_all = op_bwd
    return op


def _row(tr, c):
    return ((tr, c), lambda i: (i, 0), None, True)


def _par(shape):
    nd = len(shape)
    return (shape, lambda i: (0,) * nd, (0,), True)


def _bdot(a, w):
    return jnp.dot(a.astype(BF16), w.astype(BF16), preferred_element_type=F32)


def _rms_f(x, g):
    return (x * lax.rsqrt(jnp.mean(x * x, axis=-1, keepdims=True) + NORM_EPS) * g,)


def _rms_op(name, R, D, out_dtype):
    tr = _tile(R, (256,))
    return _block_op(name, _rms_f, (R // tr,), [_row(tr, D), _par((1, D))],
                     [((R, D), out_dtype, (tr, D), lambda i: (i, 0))])


def rmsnorm(x, g, name):
    R, D = x.shape
    return _rms_op(name, R, D, F32)(x, g.reshape(1, D))[0]


def s5_epilogue(yc, u, d, w_glu, name):
    R, C = yc.shape
    tr = _tile(R, (256,))

    def f(yc, u, d, w):
        g = jax.nn.gelu(yc + d * u)
        return (g * jax.nn.sigmoid(_bdot(g, w)),)

    op = _block_op(name, f, (R // tr,), [_row(tr, C), _row(tr, C), _par((1, C)), _par((C, C))],
                   [((R, C), F32, (tr, C), lambda i: (i, 0))])
    return op(yc, u, d.reshape(1, C), w_glu)[0]


def pool_proj(p, w, scale, name):
    R, C = p.shape
    ng, pc, _ = w.shape
    tr = _tile(R, (256,))

    def f(p, w, s):
        ys = [_bdot(p[:, g * pc:(g + 1) * pc], w[g]) for g in range(ng)]
        return (jnp.concatenate(ys, axis=-1) * s,)

    op = _block_op(name, f, (R // tr,), [_row(tr, C), _par((ng, pc, pc)), _par((1, C))],
                   [((R, C), F32, (tr, C), lambda i: (i, 0))])
    return op(p, w, scale.reshape(1, C))[0]


def conv_post(h, ln_g, ln_b, w_pw, name):
    R, C = h.shape
    tr = _tile(R, (256,))

    def f(h, g, b, w):
        hc = h - jnp.mean(h, axis=-1, keepdims=True)
        y = hc * lax.rsqrt(jnp.mean(hc * hc, axis=-1, keepdims=True) + NORM_EPS) * g + b
        return (_bdot(jax.nn.silu(y), w),)

    op = _block_op(name, f, (R // tr,), [_row(tr, C), _par((1, C)), _par((1, C)), _par((C, C))],
                   [((R, C), F32, (tr, C), lambda i: (i, 0))])
    return op(h, ln_g.reshape(1, C), ln_b.reshape(1, C), w_pw)[0]


def _group_norm_op(name, R, C, n, out_dtype):
    tr = _tile(R, (256,))

    def f(*a):
        g = a[n]
        parts = [y * lax.rsqrt(jnp.mean(y * y, axis=-1, keepdims=True) + NORM_EPS) for y in a[:n]]
        return (jnp.concatenate(parts, axis=-1) * g,)

    return _block_op(name, f, (R // tr,), [_row(tr, C)] * n + [_par((1, n * C))],
                     [((R, n * C), out_dtype, (tr, n * C), lambda i: (i, 0))])


def _cross_attention_op(name, L, W, M, out_dtype):
    E = X_HEAD_DIM
    tq = _tile(L, (512,))

    def f(q, k, v):
        s = lax.dot_general(q.astype(BF16), k.astype(BF16), (((1,), (1,)), ((), ())),
                            preferred_element_type=F32) * (E ** -0.5)
        p = jax.nn.softmax(s, axis=-1)
        return (_bdot(p, v),)

    qspec = ((tq, E), lambda h, i: (i, h), None, True)
    kspec = ((M, E), lambda h, i: (0, h), (1,), True)
    return _block_op(name, f, (W // E, L // tq), [qspec, kspec, kspec],
                     [((L, W), out_dtype, (tq, E), lambda h, i: (i, h))])


def cross_attention(q, k, v, name):
    return _cross_attention_op(name, q.shape[0], q.shape[1], k.shape[0], F32)(q, k, v)[0]


def s5_discretise(lam_re, lam_im, log_dt, b_re_t, b_im_t, name):
    G, _, N = lam_re.shape
    C = b_re_t.shape[1]

    def f(lr, li, ldt, br, bi):
        dt = jnp.exp(ldt)
        mag = jnp.exp(lr * dt)
        ab_r, ab_i = mag * jnp.cos(li * dt), mag * jnp.sin(li * dt)
        den = lr * lr + li * li
        nr, ni = ab_r - 1.0, ab_i
        f_r = (nr * lr + ni * li) / den
        f_i = (ni * lr - nr * li) / den
        return ab_r, ab_i, f_r * br - f_i * bi, f_r * bi + f_i * br

    vec = ((G, 1, N), lambda i: (0, 0, 0), None, True)
    mat = ((G, C, N), lambda i: (0, 0, 0), None, True)
    ov = ((G, 1, N), F32, (G, 1, N), lambda i: (0, 0, 0))
    om = ((G, C, N), F32, (G, C, N), lambda i: (0, 0, 0))
    op = _block_op(name, f, (1,), [vec, vec, vec, mat, mat], [ov, ov, om, om])
    return op(lam_re, lam_im, log_dt, b_re_t, b_im_t)


def rel_bias_tables(rel_bias, onehot, name):
    B, H = rel_bias.shape
    P, _, Q = onehot.shape

    def f(rbt, oh):
        return (jnp.dot(rbt, oh, precision=lax.Precision.HIGHEST, preferred_element_type=F32),)

    op = _block_op(name, f, (P,), [((H, B), lambda p: (0, 0), (0,), True), ((None, B, Q), lambda p: (p, 0, 0), None, False)],
                   [((P, H, Q), F32, (None, H, Q), lambda p: (p, 0, 0))])
    return op(rel_bias.T, onehot)[0]


def _shift_down(x, s, row):
    return jnp.where(row >= s, pltpu.roll(x, s, 0), 0.0)


def _shift_up(x, s, row):
    n = x.shape[0]
    return jnp.where(row < n - s, pltpu.roll(x, n - s, 0), 0.0)


def _window_sum(x, w, row, shift):
    span = 1
    while span < w:
        x = x + shift(x, span, row)
        span *= 2
    return x


def _pool_call(u, d_out, name):
    L, C = u.shape
    pc = C // len(POOL_WINDOWS)
    assert pc % LANES == 0

    def body(x_ref, o_ref):
        row = lax.broadcasted_iota(jnp.int32, (L, pc), 0)
        for g, w in enumerate(POOL_WINDOWS):
            sl = slice(g * pc, (g + 1) * pc)
            x = x_ref[:, sl]
            cnt = jnp.minimum(row + 1, w).astype(F32)
            if d_out is None:
                o_ref[:, sl] = _window_sum(x, w, row, _shift_down) / cnt - x
            else:
                o_ref[:, sl] = _window_sum(x / cnt, w, row, _shift_up) - x

    src = u if d_out is None else d_out
    return pl.pallas_call(body, out_shape=jax.ShapeDtypeStruct((L, C), F32), name=name,
                          compiler_params=pltpu.CompilerParams(vmem_limit_bytes=VMEM_LIMIT))(src)


def _pool_mix(name):
    @jax.custom_vjp
    def op(u):
        return _pool_call(u, None, name + "_fwd")

    def fwd(u):
        return _pool_call(u, None, name + "_fwd"), u

    def bwd(u, dp):
        return (_pool_call(u, dp, name + "_bwd"),)

    op.defvjp(fwd, bwd)
    return op


def _conv_fwd(u, w, b, name):
    L, C2 = u.shape
    C = C2 // 2
    K = w.shape[0]
    nb = C // LANES

    def body(val_ref, gate_ref, w_ref, b_ref, o_ref):
        row = lax.broadcasted_iota(jnp.int32, (L, LANES), 0)
        h = val_ref[...] * jax.nn.sigmoid(gate_ref[...])
        acc = jnp.broadcast_to(b_ref[...], (L, LANES))
        for k in range(K):
            acc = acc + w_ref[k:k + 1, :] * _shift_down(h, K - 1 - k, row)
        o_ref[...] = acc

    blk = lambda off: pl.BlockSpec((L, LANES), lambda j: (0, j + off))
    return pl.pallas_call(
        body, grid=(nb,), in_specs=[blk(0), blk(nb), pl.BlockSpec((K, LANES), lambda j: (0, j)),
                                    pl.BlockSpec((1, LANES), lambda j: (0, j))],
        out_specs=blk(0), out_shape=jax.ShapeDtypeStruct((L, C), F32), name=name, compiler_params=_cp(1))(u, u, w, b)


def _conv_bwd(u, w, dh, name):
    L, C2 = u.shape
    C = C2 // 2
    K = w.shape[0]
    nb = C // LANES

    def body(val_ref, gate_ref, w_ref, dh_ref, dval_ref, dgate_ref, dw_ref, db_ref):
        row = lax.broadcasted_iota(jnp.int32, (L, LANES), 0)
        val = val_ref[...]
        sig = jax.nn.sigmoid(gate_ref[...])
        h = val * sig
        d = dh_ref[...]
        dh0 = jnp.zeros((L, LANES), F32)
        for k in range(K):
            s = K - 1 - k
            dh0 = dh0 + w_ref[k:k + 1, :] * _shift_up(d, s, row)
            dw_ref[k:k + 1, :] = jnp.sum(d * _shift_down(h, s, row), axis=0, keepdims=True)
        db_ref[...] = jnp.sum(d, axis=0, keepdims=True)
        dval_ref[...] = dh0 * sig
        dgate_ref[...] = dh0 * val * sig * (1.0 - sig)

    blk = lambda off: pl.BlockSpec((L, LANES), lambda j: (0, j + off))
    return pl.pallas_call(
        body, grid=(nb,), in_specs=[blk(0), blk(nb), pl.BlockSpec((K, LANES), lambda j: (0, j)), blk(0)],
        out_specs=[blk(0), blk(0), pl.BlockSpec((K, LANES), lambda j: (0, j)), pl.BlockSpec((1, LANES), lambda j: (0, j))],
        out_shape=[jax.ShapeDtypeStruct((L, C), F32), jax.ShapeDtypeStruct((L, C), F32),
                   jax.ShapeDtypeStruct((K, C), F32), jax.ShapeDtypeStruct((1, C), F32)],
        name=name, compiler_params=_cp(1))(u, u, w, dh)


def _glu_conv(name):
    @jax.custom_vjp
    def op(u, w, b):
        return _conv_fwd(u, w, b, name + "_fwd")

    def fwd(u, w, b):
        return _conv_fwd(u, w, b, name + "_fwd"), (u, w)

    def bwd(r, dh):
        u, w = r
        dval, dgate, dw, db = _conv_bwd(u, w, dh, name + "_bwd")
        return jnp.concatenate([dval, dgate], axis=-1), dw, db

    op.defvjp(fwd, bwd)
    return op


S5_BLOCK_CH = LANES
S5_BLOCK_ST = S5_BLOCK_CH // S5_CH_PER_GROUP * S5_STATE


def _s5_scan(br_ref, bi_ref, ar, ai, reverse):
    L, C = br_ref.shape
    T = SUBLANES
    row = lax.broadcasted_iota(jnp.int32, (T, C), 0)
    pw = [(ar, ai)]
    for _ in range(T - 1):
        pr, pi = pw[-1]
        pw.append((pr * ar - pi * ai, pr * ai + pi * ar))
    cr = jnp.zeros((T, C), F32)
    ci = jnp.zeros((T, C), F32)
    for r in range(T):
        e = (T - r) if reverse else (r + 1)
        cr = jnp.where(row == r, pw[e - 1][0], cr)
        ci = jnp.where(row == r, pw[e - 1][1], ci)
    steps = []
    s = 1
    while s < T:
        mask = (row < T - s) if reverse else (row >= s)
        steps.append((T - s if reverse else s, mask, pw[s - 1][0], pw[s - 1][1]))
        s *= 2
    nt = L // T
    last = 0 if reverse else T - 1

    def body(i, carry):
        kr, ki = carry
        t = (nt - 1 - i) if reverse else i
        off = pl.multiple_of(t * T, T)
        xr = br_ref[pl.ds(off, T), :]
        xi = bi_ref[pl.ds(off, T), :]
        for sh, mask, mr, mi in steps:
            sr = jnp.where(mask, pltpu.roll(xr, sh, 0), 0.0)
            si = jnp.where(mask, pltpu.roll(xi, sh, 0), 0.0)
            xr, xi = xr + mr * sr - mi * si, xi + mr * si + mi * sr
        xr, xi = xr + cr * kr - ci * ki, xi + cr * ki + ci * kr
        br_ref[pl.ds(off, T), :] = xr
        bi_ref[pl.ds(off, T), :] = xi
        return (jnp.broadcast_to(xr[last:last + 1, :], (T, C)), jnp.broadcast_to(xi[last:last + 1, :], (T, C)))

    z = jnp.zeros((T, C), F32)
    lax.fori_loop(0, nt, body, (z, z))


def _s5_specs(L):
    nb_axis = lambda j: (j, 0, 0)
    u = pl.BlockSpec((L, S5_BLOCK_CH), lambda j: (0, j))
    wb = pl.BlockSpec((None, S5_BLOCK_CH, S5_BLOCK_ST), nb_axis)
    a = pl.BlockSpec((1, S5_BLOCK_ST), lambda j: (0, j))
    wc = pl.BlockSpec((None, S5_BLOCK_ST, S5_BLOCK_CH), nb_axis)
    return u, wb, a, wc


def _s5_fwd(u, wbr, wbi, ar, ai, wcr, wci, name):
    L, C = u.shape
    nb = C // S5_BLOCK_CH
    us, wbs, as_, wcs = _s5_specs(L)

    def body(u_ref, wbr_ref, wbi_ref, ar_ref, ai_ref, wcr_ref, wci_ref, y_ref, xr, xi):
        ub = u_ref[...]
        xr[...] = _bdot(ub, wbr_ref[...])
        xi[...] = _bdot(ub, wbi_ref[...])
        _s5_scan(xr, xi, ar_ref[...], ai_ref[...], False)
        y_ref[...] = _bdot(xr[...], wcr_ref[...]) - _bdot(xi[...], wci_ref[...])

    return pl.pallas_call(
        body, grid=(nb,), in_specs=[us, wbs, wbs, as_, as_, wcs, wcs], out_specs=us,
        out_shape=jax.ShapeDtypeStruct((L, C), F32),
        scratch_shapes=[pltpu.VMEM((L, S5_BLOCK_ST), F32)] * 2, name=name, compiler_params=_cp(1))(
            u, wbr, wbi, ar, ai, wcr, wci)


def _dot_t(a, b):
    return lax.dot_general(a.astype(BF16), b.astype(BF16), (((0,), (0,)), ((), ())), preferred_element_type=F32)


def _dot_nt(a, b):
    return lax.dot_general(a.astype(BF16), b.astype(BF16), (((1,), (1,)), ((), ())), preferred_element_type=F32)


def _s5_bwd(u, wbr, wbi, ar, ai, wcr, wci, dy, name):
    L, C = u.shape
    nb = C // S5_BLOCK_CH
    us, wbs, as_, wcs = _s5_specs(L)
    T = SUBLANES

    def body(u_ref, wbr_ref, wbi_ref, ar_ref, ai_ref, wcr_ref, wci_ref, dy_ref,
             du_ref, dwbr_ref, dwbi_ref, dar_ref, dai_ref, dwcr_ref, dwci_ref, xr, xi, gr, gi):
        ub = u_ref[...]
        a_r, a_i = ar_ref[...], ai_ref[...]
        xr[...] = _bdot(ub, wbr_ref[...])
        xi[...] = _bdot(ub, wbi_ref[...])
        _s5_scan(xr, xi, a_r, a_i, False)
        d = dy_ref[...]
        dwcr_ref[...] = _dot_t(xr[...], d)
        dwci_ref[...] = -_dot_t(xi[...], d)
        gr[...] = _dot_nt(d, wcr_ref[...])
        gi[...] = -_dot_nt(d, wci_ref[...])
        _s5_scan(gr, gi, a_r, -a_i, True)

        row = lax.broadcasted_iota(jnp.int32, (T, S5_BLOCK_ST), 0)

        def da_body(i, carry):
            pr, pi, sr, si = carry
            off = pl.multiple_of(i * T, T)
            xr_t, xi_t = xr[pl.ds(off, T), :], xi[pl.ds(off, T), :]
            lr_t, li_t = gr[pl.ds(off, T), :], gi[pl.ds(off, T), :]
            qr = jnp.where(row == 0, pr, pltpu.roll(xr_t, 1, 0))
            qi = jnp.where(row == 0, pi, pltpu.roll(xi_t, 1, 0))
            sr = sr + qr * lr_t + qi * li_t
            si = si + qr * li_t - qi * lr_t
            return (jnp.broadcast_to(xr_t[T - 1:T, :], (T, S5_BLOCK_ST)),
                    jnp.broadcast_to(xi_t[T - 1:T, :], (T, S5_BLOCK_ST)), sr, si)

        z = jnp.zeros((T, S5_BLOCK_ST), F32)
        _, _, sr, si = lax.fori_loop(0, L // T, da_body, (z, z, z, z))
        dar_ref[...] = jnp.sum(sr, axis=0, keepdims=True)
        dai_ref[...] = jnp.sum(si, axis=0, keepdims=True)
        lr, li = gr[...], gi[...]
        dwbr_ref[...] = _dot_t(ub, lr)
        dwbi_ref[...] = _dot_t(ub, li)
        du_ref[...] = _dot_nt(lr, wbr_ref[...]) + _dot_nt(li, wbi_ref[...])

    sds = jax.ShapeDtypeStruct
    return pl.pallas_call(
        body, grid=(nb,), in_specs=[us, wbs, wbs, as_, as_, wcs, wcs, us],
        out_specs=[us, wbs, wbs, as_, as_, wcs, wcs],
        out_shape=[sds(u.shape, F32), sds(wbr.shape, F32), sds(wbi.shape, F32), sds(ar.shape, F32),
                   sds(ai.shape, F32), sds(wcr.shape, F32), sds(wci.shape, F32)],
        scratch_shapes=[pltpu.VMEM((L, S5_BLOCK_ST), F32)] * 4, name=name,
        compiler_params=_cp(1, VMEM_LIMIT_BIG))(u, wbr, wbi, ar, ai, wcr, wci, dy)


def _s5_core(name):
    @jax.custom_vjp
    def op(u, wbr, wbi, ar, ai, wcr, wci):
        return _s5_fwd(u, wbr, wbi, ar, ai, wcr, wci, name + "_fwd")

    def fwd(*a):
        return _s5_fwd(*a, name + "_fwd"), a

    def bwd(a, dy):
        return tuple(_s5_bwd(*a, dy, name + "_bwd"))

    op.defvjp(fwd, bwd)
    return op


def _att_tile_f(first, q, kp, kc, vp, vc, bias):
    nq = q.shape[0]
    hb = bias.shape[0]
    E = q.shape[1] // hb
    r = lax.broadcasted_iota(jnp.int32, (nq, 2 * nq), 0)
    c = lax.broadcasted_iota(jnp.int32, (nq, 2 * nq), 1)
    prev_ok = jnp.logical_and(jnp.logical_and(c < nq, c >= r), jnp.logical_not(first))
    valid = jnp.logical_or(prev_ok, jnp.logical_and(c >= nq, c - nq <= r))
    lane = lax.broadcasted_iota(jnp.int32, (1, hb * E), 1)
    k = jnp.concatenate([kp, kc], axis=0)
    v = jnp.concatenate([vp, vc], axis=0)
    o = jnp.zeros((nq, hb * E), F32)
    lse = jnp.zeros((nq, hb * E), F32)
    for h in range(hb):
        mine = jnp.logical_and(lane >= h * E, lane < (h + 1) * E)
        s = jnp.where(valid, _dot_nt(jnp.where(mine, q, 0.0), k) * (E ** -0.5) + bias[h], NEG_INF)
        m = jnp.max(s, axis=-1, keepdims=True)
        p = jnp.exp(s - m)
        den = jnp.sum(p, axis=-1, keepdims=True)
        o = jnp.where(mine, _bdot(p, v) / den, o)
        lse = jnp.where(mine, m + jnp.log(den), lse)
    return o, lse


def _att_tile_grad(first, q, kp, kc, vp, vc, bias, o, lse, do, dlse):
    nq = q.shape[0]
    hb = bias.shape[0]
    E = q.shape[1] // hb
    scale = E ** -0.5
    r = lax.broadcasted_iota(jnp.int32, (nq, 2 * nq), 0)
    c = lax.broadcasted_iota(jnp.int32, (nq, 2 * nq), 1)
    prev_ok = jnp.logical_and(jnp.logical_and(c < nq, c >= r), jnp.logical_not(first))
    valid = jnp.logical_or(prev_ok, jnp.logical_and(c >= nq, c - nq <= r))
    lane = lax.broadcasted_iota(jnp.int32, (1, hb * E), 1)
    k = jnp.concatenate([kp, kc], axis=0)
    v = jnp.concatenate([vp, vc], axis=0)
    dq = jnp.zeros((nq, hb * E), F32)
    dk = jnp.zeros((2 * nq, hb * E), F32)
    dv = jnp.zeros((2 * nq, hb * E), F32)
    db = []
    for h in range(hb):
        mine = jnp.logical_and(lane >= h * E, lane < (h + 1) * E)
        qh = jnp.where(mine, q, 0.0)
        doh = jnp.where(mine, do, 0.0)
        s = jnp.where(valid, _dot_nt(qh, k) * scale + bias[h], NEG_INF)
        p = jnp.exp(s - jnp.max(jnp.where(mine, lse, NEG_INF), axis=-1, keepdims=True))
        row = jnp.sum(jnp.where(mine, dlse, 0.0) - doh * o, axis=-1, keepdims=True)
        ds = p * (_dot_nt(doh, v) + row)
        db.append(ds)
        dv = dv + _dot_t(p, doh)
        dk = dk + _dot_t(ds, qh) * scale
        dq = jnp.where(mine, _bdot(ds, k) * scale, dq)
    return dq, dk[:nq], dk[nq:], dv[:nq], dv[nq:], db


def _att_mix_f(*a):
    n = len(a) // 2
    o, l = a[:n], a[n:]
    m = functools.reduce(jnp.maximum, l)
    e = [jnp.exp(li - m) for li in l]
    return sum(ei * oi for ei, oi in zip(e, o)) / sum(e)


def _att_rows(start, dil):
    if dil == 1:
        return pl.ds(pl.multiple_of(start, ATT_BLOCK), ATT_BLOCK)
    return pl.ds(start, ATT_BLOCK, stride=dil)


def _att_blocks(L, dil):
    nb = L // dil // ATT_BLOCK
    return dil * nb, nb


def _att_specs(L, W):
    nblk = W // LANES
    col = lambda off: pl.BlockSpec((L, LANES), lambda j: (0, j + off))
    per_pattern = pl.BlockSpec((len(DILATED_PATTERNS), L, LANES), lambda j: (0, 0, j))
    hb = ATT_HEADS // nblk
    bias = pl.BlockSpec((len(DILATED_PATTERNS), hb, ATT_BLOCK, 2 * ATT_BLOCK), lambda j: (0, j, 0, 0))
    return nblk, col, per_pattern, bias


def _att_fwd(qkv, bias, name):
    L, W3 = qkv.shape
    W = W3 // 3
    nblk, col, per_pattern, bias_spec = _att_specs(L, W)
    P = len(DILATED_PATTERNS)

    def body(q_ref, k_ref, v_ref, b_ref, y_ref, o_ref, l_ref):
        for p, (_, dil) in enumerate(DILATED_PATTERNS):
            n_it, nb = _att_blocks(L, dil)

            def step(i, carry, p=p, dil=dil, nb=nb):
                n = i % nb
                cur = i // nb + n * (ATT_BLOCK * dil)
                prev = i // nb + jnp.maximum(n - 1, 0) * (ATT_BLOCK * dil)
                rc, rp = _att_rows(cur, dil), _att_rows(prev, dil)
                o, l = _att_tile_f(n == 0, q_ref[rc, :], k_ref[rp, :], k_ref[rc, :], v_ref[rp, :], v_ref[rc, :],
                                   b_ref[p])
                o_ref[p, rc, :] = o
                l_ref[p, rc, :] = l
                return carry

            lax.fori_loop(0, n_it, step, 0, unroll=ATT_UNROLL)

        def mix(i, carry):
            rows = pl.ds(pl.multiple_of(i * ATT_MIX_ROWS, ATT_MIX_ROWS), ATT_MIX_ROWS)
            y_ref[rows, :] = _att_mix_f(*[o_ref[p, rows, :] for p in range(P)], *[l_ref[p, rows, :] for p in range(P)])
            return carry

        lax.fori_loop(0, L // ATT_MIX_ROWS, mix, 0)

    sds = jax.ShapeDtypeStruct
    return pl.pallas_call(
        body, grid=(nblk,), in_specs=[col(0), col(nblk), col(2 * nblk), bias_spec],
        out_specs=[col(0), per_pattern, per_pattern],
        out_shape=[sds((L, W), F32), sds((P, L, W), F32), sds((P, L, W), F32)], name=name,
        compiler_params=_cp(1))(qkv, qkv, qkv, bias)


def _att_bwd(qkv, bias, o_all, l_all, dy, name, riders=None):
    L, W3 = qkv.shape
    W = W3 // 3
    nblk, col, per_pattern, bias_spec = _att_specs(L, W)
    P = len(DILATED_PATTERNS)
    ride = _Riders(riders)
    n_in, n_out = 7, 4

    def body(*refs):
        q_ref, k_ref, v_ref, b_ref, o_ref, l_ref, dy_ref = refs[:n_in]
        outs = refs[n_in + 2 * ride.n:]
        dq_ref, dk_ref, dv_ref, db_ref = outs[:n_out]
        do_s, dl_s = outs[n_out + ride.n:n_out + ride.n + 2]
        start, wait = ride.hooks((nblk,), refs[n_in:n_in + ride.n], outs[n_out:n_out + ride.n],
                                 outs[n_out + ride.n + 2:])
        start()

        def mix(i, carry):
            rows = pl.ds(pl.multiple_of(i * ATT_MIX_ROWS, ATT_MIX_ROWS), ATT_MIX_ROWS)
            _, mix_vjp = jax.vjp(_att_mix_f, *[o_ref[p, rows, :] for p in range(P)],
                                 *[l_ref[p, rows, :] for p in range(P)])
            g = mix_vjp(dy_ref[rows, :])
            for p in range(P):
                do_s[p, rows, :] = g[p]
                dl_s[p, rows, :] = g[P + p]
            return carry

        lax.fori_loop(0, L // ATT_MIX_ROWS, mix, 0)
        for ref in (dq_ref, dk_ref, dv_ref, db_ref):
            ref[...] = jnp.zeros_like(ref)

        def add(ref, rows, val):
            ref[rows, :] = ref[rows, :] + val

        for p, (_, dil) in enumerate(DILATED_PATTERNS):
            n_it, nb = _att_blocks(L, dil)

            def step(i, carry, p=p, dil=dil, nb=nb):
                n = i % nb
                first = n == 0
                cur = i // nb + n * (ATT_BLOCK * dil)
                prev = i // nb + jnp.maximum(n - 1, 0) * (ATT_BLOCK * dil)
                rc, rp = _att_rows(cur, dil), _att_rows(prev, dil)
                dq, dkp, dkc, dvp, dvc, db = _att_tile_grad(
                    first, q_ref[rc, :], k_ref[rp, :], k_ref[rc, :], v_ref[rp, :], v_ref[rc, :], b_ref[p],
                    o_ref[p, rc, :], l_ref[p, rc, :], do_s[p, rc, :], dl_s[p, rc, :])
                add(dq_ref, rc, dq)
                add(dk_ref, rc, dkc)
                add(dv_ref, rc, dvc)
                for h, dbh in enumerate(db):
                    db_ref[p, h] = db_ref[p, h] + dbh

                @pl.when(jnp.logical_not(first))
                def _():
                    add(dk_ref, rp, dkp)
                    add(dv_ref, rp, dvp)

                return carry

            lax.fori_loop(0, n_it, step, 0, unroll=ATT_UNROLL)
        wait()

    sds = jax.ShapeDtypeStruct((L, W), F32)
    return pl.pallas_call(
        body, grid=(nblk,),
        in_specs=[col(0), col(nblk), col(2 * nblk), bias_spec, per_pattern, per_pattern, col(0)] + ride.in_specs(),
        out_specs=[col(0), col(0), col(0), bias_spec] + ride.out_specs(),
        out_shape=[sds, sds, sds, jax.ShapeDtypeStruct(bias.shape, F32)] + ride.out_shape(),
        scratch_shapes=[pltpu.VMEM((P, L, LANES), F32)] * 2 + ride.scratch(),
        input_output_aliases=ride.aliases(n_in, n_out), name=name, compiler_params=_cp(1, VMEM_LIMIT_BIG))(
            qkv, qkv, qkv, bias, o_all, l_all, dy, *ride.operands())


def _t5_bucket(dist):
    n = np.maximum(dist, 0)
    max_exact = REL_BUCKETS // 2
    large = max_exact + (np.log(np.maximum(n, 1) / max_exact) / np.log(REL_MAX_DIST / max_exact)
                         * (REL_BUCKETS - max_exact)).astype(np.int64)
    large = np.minimum(large, REL_BUCKETS - 1)
    return np.where(n < max_exact, n, large).astype(np.int32)


def _bucket_onehot():
    a = np.arange(ATT_BLOCK)[:, None]
    b = np.arange(2 * ATT_BLOCK)[None, :]
    sub = a + ATT_BLOCK - b
    bucket = jnp.asarray(np.stack([_t5_bucket(sub * dil).reshape(-1) for _, dil in DILATED_PATTERNS]))
    ids = jnp.arange(REL_BUCKETS, dtype=jnp.int32)
    return (bucket[:, None, :] == ids[None, :, None]).astype(F32)


def loss_head(h, target, g, name):
    R, D = h.shape
    tr = _tile(R, (256,))

    def body(h_ref, t_ref, g_ref, l_ref, dh_ref, dg_ref):
        def lf(hv, gv):
            y = _rms_f(hv, gv)[0]
            return 0.5 * jnp.sum(jnp.mean(jnp.square(y - t_ref[...]), axis=-1))

        l, (dh, dg) = jax.value_and_grad(lf, argnums=(0, 1))(h_ref[...], g_ref[...])

        @pl.when(pl.program_id(0) == 0)
        def _():
            l_ref[...] = jnp.zeros_like(l_ref)
            dg_ref[...] = jnp.zeros_like(dg_ref)

        dh_ref[...] = dh
        dg_ref[...] += dg
        l_ref[...] += l

    rows = pl.BlockSpec((tr, D), lambda i: (i, 0))
    vec = pl.BlockSpec((1, D), lambda i: (0, 0))
    l, dh, dg = pl.pallas_call(
        body, grid=(R // tr,), in_specs=[rows, rows, vec],
        out_specs=[pl.BlockSpec((SUBLANES, LANES), lambda i: (0, 0)), rows, vec],
        out_shape=[jax.ShapeDtypeStruct((SUBLANES, LANES), F32), jax.ShapeDtypeStruct((R, D), F32),
                   jax.ShapeDtypeStruct((1, D), F32)], name=name, compiler_params=_cp(1))(h, target, g.reshape(1, D))
    return l[0, 0], dh, dg.reshape(D)


def _adamw_update(w, g, m, v):
    c1 = 1.0 - ADAM_B1 ** ADAM_STEP
    c2 = 1.0 - ADAM_B2 ** ADAM_STEP
    nm = ADAM_B1 * m + (1.0 - ADAM_B1) * g
    nv = ADAM_B2 * v + (1.0 - ADAM_B2) * jnp.square(g)
    return -ADAM_LR * ((nm / c1) / (jnp.sqrt(nv / c2) + ADAM_EPS) + ADAM_WD * w), nm, nv


def adamw_layers(w, parts, m, v, name):
    nl, a, b = w.shape
    n_parts = parts[0].shape[0]
    tr = _row_tile(a, b)

    def body(*refs):
        w_ref, p_refs, (m_ref, v_ref, g_ref, d_ref, nm_ref, nv_ref) = refs[0], refs[1:1 + nl], refs[1 + nl:]
        for l in range(nl):
            @pl.when(pl.program_id(0) == l)
            def _(p_ref=p_refs[l]):
                g = p_ref[0].astype(F32)
                for i in range(1, n_parts):
                    g = g + p_ref[i].astype(F32)
                d_ref[...], nm_ref[...], nv_ref[...] = _adamw_update(w_ref[...], g, m_ref[...], v_ref[...])
                g_ref[...] = g

    rows = pl.BlockSpec((None, tr, b), lambda l, i: (l, i, 0))
    part = lambda k: pl.BlockSpec((n_parts, tr, b), lambda l, i: (0, jnp.where(l == k, i, 0), 0))
    sds = jax.ShapeDtypeStruct((nl, a, b), F32)
    return pl.pallas_call(body, grid=(nl, a // tr), in_specs=[rows] + [part(k) for k in range(nl)] + [rows, rows],
                          out_specs=[rows] * 4, out_shape=[sds] * 4, name=name, compiler_params=_cp(2))(
                              w, *parts, m, v)


def adamw(w, parts, m, v, name):
    R, C = w.shape
    n_parts = parts.shape[0]
    tr = _row_tile(R, C)

    def body(w_ref, p_ref, m_ref, v_ref, g_ref, d_ref, nm_ref, nv_ref):
        g = p_ref[0].astype(F32)
        for i in range(1, n_parts):
            g = g + p_ref[i].astype(F32)
        d_ref[...], nm_ref[...], nv_ref[...] = _adamw_update(w_ref[...], g, m_ref[...], v_ref[...])
        g_ref[...] = g

    rows = pl.BlockSpec((tr, C), lambda i: (i, 0))
    sds = jax.ShapeDtypeStruct((R, C), F32)
    return pl.pallas_call(body, grid=(R // tr,),
                          in_specs=[rows, pl.BlockSpec((n_parts, tr, C), lambda i: (0, i, 0)), rows, rows],
                          out_specs=[rows] * 4, out_shape=[sds] * 4, name=name, compiler_params=_cp(1))(w, parts, m, v)


HBM_SPEC = pl.BlockSpec(memory_space=pltpu.HBM)
MESH_ID = pl.DeviceIdType.MESH


def _place():
    return lax.axis_index("x"), lax.axis_index("y"), lax.axis_index("c")


def _index(x, y, c):
    return 4 * x + 2 * y + c


AG_COPIES = 9
AG_ROW_UNIT = 32


def all_gather(x, name):
    R, C = x.shape
    assert R % AG_ROW_UNIT == 0, x.shape
    half = R // 2

    def body(x_ref, out_ref, send_sems, recv_sems, local_sem):
        x_, y_, c_ = _place()
        me, sib = (x_, y_, c_), (x_, y_, 1 - c_)
        nx, ny, nd = (1 - x_, y_, c_), (x_, 1 - y_, c_), (1 - x_, 1 - y_, c_)
        upper, lower = pl.ds(0, half), pl.ds(half, half)

        def slot(dev, rows=None):
            ref = out_ref.at[_index(*dev)]
            return ref if rows is None else ref.at[rows]

        def copy(k, block, to, rows=None, src=None):
            return pltpu.make_async_remote_copy(
                src_ref=slot(block, rows) if src is None else src, dst_ref=slot(block, rows),
                send_sem=send_sems.at[k], recv_sem=recv_sems.at[k], device_id=to, device_id_type=MESH_ID)

        def other(dev):
            return (dev[0], dev[1], 1 - c_)

        mine = pltpu.make_async_copy(x_ref, slot(me), local_sem)
        mine.start()
        sent = [copy(0, me, sib, src=x_ref), copy(1, me, nx, src=x_ref), copy(2, me, ny, src=x_ref)]
        for cp in sent:
            cp.start()

        def then(arrival, *forwards):
            arrival.wait_recv()
            for cp in forwards:
                cp.start()
            sent.extend(forwards)

        then(copy(1, nx, me), copy(4, nx, ny, upper), copy(5, nx, sib))
        then(copy(2, ny, me), copy(3, ny, nx, lower), copy(6, ny, sib))
        then(copy(3, nd, me, lower), copy(8, nd, sib, lower))
        then(copy(4, nd, me, upper), copy(7, nd, sib, upper))
        copy(0, sib, me).wait_recv()
        copy(5, other(nx), me).wait_recv()
        copy(6, other(ny), me).wait_recv()
        copy(7, other(nd), me, upper).wait_recv()
        copy(8, other(nd), me, lower).wait_recv()
        for cp in sent:
            cp.wait_send()
        mine.wait()

    return pl.pallas_call(
        body, out_shape=jax.ShapeDtypeStruct((N_DEV,) + x.shape, x.dtype), in_specs=[HBM_SPEC], out_specs=HBM_SPEC,
        scratch_shapes=[pltpu.SemaphoreType.DMA((AG_COPIES,)), pltpu.SemaphoreType.DMA((AG_COPIES,)),
                        pltpu.SemaphoreType.DMA], name=name)(x)


def _chip(x, y):
    return 2 * x + y


def sibling_exchange(xs, name):
    n = len(xs)

    def body(*refs):
        x_refs, out_refs, (send_sems, recv_sems) = refs[:n], refs[n:2 * n], refs[2 * n:]
        x_, y_, c_ = _place()
        copies = [pltpu.make_async_remote_copy(src_ref=x_ref.at[:, 1 - c_], dst_ref=out_ref, send_sem=send_sems.at[p],
                                               recv_sem=recv_sems.at[p], device_id=(x_, y_, 1 - c_),
                                               device_id_type=MESH_ID)
                  for p, (x_ref, out_ref) in enumerate(zip(x_refs, out_refs))]
        for cp in copies:
            cp.start()
        for cp in copies:
            cp.wait()

    return pl.pallas_call(
        body, out_shape=[jax.ShapeDtypeStruct((x.shape[0],) + x.shape[2:], x.dtype) for x in xs],
        in_specs=[HBM_SPEC] * n, out_specs=[HBM_SPEC] * n,
        scratch_shapes=[pltpu.SemaphoreType.DMA((n,)), pltpu.SemaphoreType.DMA((n,))], name=name)(*xs)


def _row_tile(R, C, block_bytes=STREAM_BLOCK_BYTES):
    cap = max(SUBLANES, block_bytes // (4 * C))
    return _tile(R, [t for t in (2048, 1024, 512, 256, 128, 64, 32, 16, 8) if t <= cap])


def pair_sum(x, recv, name):
    nc, _, R, C = x.shape
    tr = _row_tile(R, C, PAIR_SUM_BLOCK_BYTES)
    core = lax.axis_index("c").astype(jnp.int32).reshape(1)

    def body(c_ref, a_ref, b_ref, o_ref):
        o_ref[...] = (a_ref[...].astype(F32) + b_ref[...].astype(F32)).astype(o_ref.dtype)

    blk = pl.BlockSpec((None, tr, C), lambda k, i, c_ref: (k, i, 0))
    grid_spec = pltpu.PrefetchScalarGridSpec(
        num_scalar_prefetch=1, grid=(nc, R // tr),
        in_specs=[pl.BlockSpec((None, None, tr, C), lambda k, i, c_ref: (k, c_ref[0], i, 0)), blk], out_specs=blk)
    return pl.pallas_call(body, grid_spec=grid_spec, out_shape=jax.ShapeDtypeStruct((nc, R, C), x.dtype), name=name,
                          compiler_params=_cp(2))(core, x, recv)


def chip_exchange(ss, name):
    n = len(ss)

    def body(*refs):
        copies = _chip_exchange_copies(refs[:n], refs[n:2 * n], (0, 1, 1), *refs[2 * n:])
        for cp in copies:
            cp.start()
        for cp in copies:
            cp.wait()

    n_sem = n * (N_CHIP - 1)
    return pl.pallas_call(
        body, out_shape=[jax.ShapeDtypeStruct(s.shape, s.dtype) for s in ss], in_specs=[HBM_SPEC] * n,
        out_specs=[HBM_SPEC] * n,
        scratch_shapes=[pltpu.SemaphoreType.DMA((n_sem,)), pltpu.SemaphoreType.DMA((n_sem,)),
                        pltpu.SemaphoreType.DMA((n,))], name=name)(*ss)


def _block_diag(w, nb):
    G, a, b = w.shape
    gp = G // nb
    eye = jnp.eye(gp, dtype=w.dtype)
    return jnp.einsum('jgab,gh->jgahb', w.reshape(nb, gp, a, b), eye).reshape(nb, gp * a, gp * b)


def _split_columns(x, cuts):
    edges = (0,) + tuple(cuts) + (x.shape[1],)

    def split(x):
        return tuple(x[:, a:b] for a, b in zip(edges[:-1], edges[1:]))

    op = jax.custom_vjp(split)
    op.defvjp(lambda x: (split(x), None), lambda _, cts: (jnp.concatenate(cts, axis=-1),))
    return op(x)


def _project_in(l, h, P):
    L, D = h.shape
    GW = D // N_MIXERS
    proj = _act_linear(f"l{l}_w_in", _rms_op(f"l{l}_norm_mix", L, D, BF16), 2, False)(
        h, P['norm_mix_g'].reshape(1, D), P['w_in'])
    return _split_columns(proj, (GW, 2 * GW, 4 * GW))


def _mix_and_memory(l, u_a, u_b, u_c, y_d, h, memn, P):
    nm = lambda s: f"l{l}_{s}"
    L, D = h.shape
    GW = D // N_MIXERS
    G = GW // S5_CH_PER_GROUP

    row = lambda g: g.reshape(1, D)

    v3 = lambda a: a.reshape(G, 1, S5_STATE)
    log_dt = jnp.broadcast_to(P['s5_log_dt'][:, None, None], (G, 1, S5_STATE))
    a_r, a_i, bb_r, bb_i = s5_discretise(v3(P['s5_lam_re']), v3(P['s5_lam_im']), log_dt,
                                         P['s5_b_re'].transpose(0, 2, 1), P['s5_b_im'].transpose(0, 2, 1), nm("s5_disc"))
    nblk = GW // S5_BLOCK_CH
    y_s5 = _s5_core(nm("s5_core"))(
        u_a, _block_diag(bb_r, nblk), _block_diag(bb_i, nblk), a_r.reshape(1, G * S5_STATE), a_i.reshape(1, G * S5_STATE),
        _block_diag(P['s5_c_re'].transpose(0, 2, 1), nblk), _block_diag(P['s5_c_im'].transpose(0, 2, 1), nblk))
    y_a = s5_epilogue(y_s5, u_a, P['s5_d'], P['s5_w_glu'], nm("s5_glu"))

    y_b = pool_proj(_pool_mix(nm("pool_mix"))(u_b), P['pool_w'], P['pool_scale'], nm("pool_proj"))

    hc = _glu_conv(nm("conv_dw"))(u_c, P['conv_w_dw'], P['conv_b_dw'].reshape(1, GW))
    y_c = conv_post(hc, P['conv_ln_g'], P['conv_ln_b'], P['conv_w_pw'], nm("conv_post"))

    grp = _group_norm_op(nm("grp_norm"), L, GW, N_MIXERS, BF16)
    h = _act_linear(nm("w_out"), grp, N_MIXERS + 1, True)(y_a, y_b, y_c, y_d, row(P['grp_norm_g']), P['w_out'], h)

    xq = _act_linear(nm("w_xq"), _rms_op(nm("norm_x"), L, D, BF16), 2, False)(h, row(P['norm_x_g']), P['w_xq'])
    xk = _linear(nm("w_xk"))(memn, P['w_xk'])
    xv = _linear(nm("w_xv"))(memn, P['w_xv'])
    xat = _cross_attention_op(nm("xattn"), L, xq.shape[1], memn.shape[0], BF16)
    return _act_linear(nm("w_xo"), xat, 3, True)(xq, xk, xv, P['w_xo'], h)


def _bias_tables(rel_bias):
    tabs = rel_bias_tables(rel_bias, _bucket_onehot(), "rel_bias")
    return tabs.reshape(len(DILATED_PATTERNS), ATT_HEADS, ATT_BLOCK, 2 * ATT_BLOCK)


def _gather_weight(name, w):
    ax = SHARDED[name]
    dt = BF16 if name in GATHER_BF16 else F32
    nl, a, b = w.shape
    rows = nl * a
    flat = jnp.pad(w.astype(dt).reshape(rows, b), ((0, (-rows) % AG_ROW_UNIT), (0, 0)))
    g = all_gather(flat, "ag_" + name)
    if name in MLP_SHARDED:
        assert rows % AG_ROW_UNIT == 0
        return g
    g = g[:, :rows].reshape(N_DEV, nl, a, b)
    if ax == 1:
        return g.transpose(1, 0, 2, 3).reshape(nl, N_DEV * a, b)
    return g.transpose(1, 2, 0, 3).reshape(nl, a, N_DEV * b)


def _scatter_grad(name, g):
    ax = SHARDED[name]
    nl = g.shape[0]
    if ax == 1:
        a, b = g.shape[1] // N_DEV, g.shape[2]
        s = g.reshape(nl, N_DEV, a, b).transpose(1, 0, 2, 3)
    else:
        a, b = g.shape[1], g.shape[2] // N_DEV
        s = g.reshape(nl, a, N_DEV, b).transpose(2, 0, 1, 3)
    s = s.reshape(N_CHIP, N_DEV // N_CHIP, nl * a, b)
    pair = pair_sum(s, sibling_exchange([s], "d2d_" + name)[0], "pairsum_" + name)
    return chip_exchange([pair], "ici_" + name)[0]


def _by_destination(name, g):
    if g.ndim == 2 and SHARDED[name] == 1:
        g = g.reshape(N_DEV, g.shape[0] // N_DEV, g.shape[1])
    elif g.ndim == 2:
        g = g.reshape(g.shape[0], N_DEV, g.shape[1] // N_DEV).transpose(1, 0, 2)
    return g.reshape(N_CHIP, N_DEV // N_CHIP, *g.shape[1:])


def _flatten_small(d):
    flat = jnp.concatenate([d[n].reshape(-1).astype(F32) for n in SMALL])
    pad = (-flat.shape[0]) % (LANES * SMALL_ROW_TILE)
    return jnp.pad(flat, (0, pad)).reshape(-1, LANES)


def _split_small(flat, like):
    flat = flat.reshape(-1)
    out, off = {}, 0
    for n in SMALL:
        sz = math.prod(like[n].shape)
        out[n] = flat[off:off + sz].reshape(like[n].shape)
        off += sz
    return out


def kernel(x, mem, rel_bias, mem_norm_g, norm_mix_g, w_in, s5_lam_re, s5_lam_im, s5_log_dt, s5_b_re, s5_b_im, s5_c_re, s5_c_im, s5_d, s5_w_glu, pool_w, pool_scale, conv_w_dw, conv_b_dw, conv_ln_g, conv_ln_b, conv_w_pw, grp_norm_g, w_out, norm_x_g, w_xq, w_xk, w_xv, w_xo, norm_mlp_g, w_up, w_down, norm_final_g, loss_target, m_rel_bias, m_mem_norm_g, m_norm_mix_g, m_w_in, m_s5_lam_re, m_s5_lam_im, m_s5_log_dt, m_s5_b_re, m_s5_b_im, m_s5_c_re, m_s5_c_im, m_s5_d, m_s5_w_glu, m_pool_w, m_pool_scale, m_conv_w_dw, m_conv_b_dw, m_conv_ln_g, m_conv_ln_b, m_conv_w_pw, m_grp_norm_g, m_w_out, m_norm_x_g, m_w_xq, m_w_xk, m_w_xv, m_w_xo, m_norm_mlp_g, m_w_up, m_w_down, m_norm_final_g, v_rel_bias, v_mem_norm_g, v_norm_mix_g, v_w_in, v_s5_lam_re, v_s5_lam_im, v_s5_log_dt, v_s5_b_re, v_s5_b_im, v_s5_c_re, v_s5_c_im, v_s5_d, v_s5_w_glu, v_pool_w, v_pool_scale, v_conv_w_dw, v_conv_b_dw, v_conv_ln_g, v_conv_ln_b, v_conv_w_pw, v_grp_norm_g, v_w_out, v_norm_x_g, v_w_xq, v_w_xk, v_w_xv, v_w_xo, v_norm_mlp_g, v_w_up, v_w_down, v_norm_final_g):
    w = dict(zip(WEIGHTS, (rel_bias, mem_norm_g, norm_mix_g, w_in, s5_lam_re, s5_lam_im, s5_log_dt, s5_b_re, s5_b_im, s5_c_re, s5_c_im, s5_d, s5_w_glu, pool_w, pool_scale, conv_w_dw, conv_b_dw, conv_ln_g, conv_ln_b, conv_w_pw, grp_norm_g, w_out, norm_x_g, w_xq, w_xk, w_xv, w_xo, norm_mlp_g, w_up, w_down, norm_final_g)))
    m = dict(zip(WEIGHTS, (m_rel_bias, m_mem_norm_g, m_norm_mix_g, m_w_in, m_s5_lam_re, m_s5_lam_im, m_s5_log_dt, m_s5_b_re, m_s5_b_im, m_s5_c_re, m_s5_c_im, m_s5_d, m_s5_w_glu, m_pool_w, m_pool_scale, m_conv_w_dw, m_conv_b_dw, m_conv_ln_g, m_conv_ln_b, m_conv_w_pw, m_grp_norm_g, m_w_out, m_norm_x_g, m_w_xq, m_w_xk, m_w_xv, m_w_xo, m_norm_mlp_g, m_w_up, m_w_down, m_norm_final_g)))
    v = dict(zip(WEIGHTS, (v_rel_bias, v_mem_norm_g, v_norm_mix_g, v_w_in, v_s5_lam_re, v_s5_lam_im, v_s5_log_dt, v_s5_b_re, v_s5_b_im, v_s5_c_re, v_s5_c_im, v_s5_d, v_s5_w_glu, v_pool_w, v_pool_scale, v_conv_w_dw, v_conv_b_dw, v_conv_ln_g, v_conv_ln_b, v_conv_w_pw, v_grp_norm_g, v_w_out, v_norm_x_g, v_w_xq, v_w_xk, v_w_xv, v_w_xo, v_norm_mlp_g, v_w_up, v_w_down, v_norm_final_g)))

    full = {n: (_gather_weight(n, w[n]) if n in SHARDED else w[n]) for n in WEIGHTS if n != 'norm_final_g'}
    L, D = x.shape[1:]

    memn, mem_vjp = jax.vjp(lambda a, g: rmsnorm(a, g, "mem_norm"), mem[0], w['mem_norm_g'])
    tabs, tabs_vjp = jax.vjp(_bias_tables, w['rel_bias'])
    h = x[0]
    stages = []
    for l in range(DEPTH):
        layer = lambda names: {n: full[n][l] for n in names}
        (u_a, u_b, u_c, qkv), in_vjp = jax.vjp(functools.partial(_project_in, l), h, layer(IN_WEIGHTS))
        y_d, o_all, l_all = _att_fwd(qkv, tabs, f"l{l}_att_fwd")
        h, mix_vjp = jax.vjp(functools.partial(_mix_and_memory, l), u_a, u_b, u_c, y_d, h, memn, layer(MIX_WEIGHTS))
        norm = _rms_op(f"l{l}_norm_mlp", L, D, BF16)
        h, saved = _mlp_fwd(f"l{l}_mlp", norm, h, full['norm_mlp_g'][l].reshape(1, D), full['w_up'], full['w_down'], l)
        stages.append((in_vjp, (qkv, tabs, o_all, l_all), mix_vjp, norm, saved))
    loss_local, dh, d_final_g = loss_head(h, loss_target[0], w['norm_final_g'], "loss_head")
    loss = lax.psum(loss_local, MESH_AXES)

    def pair_sums(l, names, grads_l):
        by_dest = [_by_destination(n, grads_l[n]) for n in names]
        theirs = sibling_exchange(by_dest, f"d2d_l{l}_{names[0]}")
        return [pair_sum(s, t, f"pairsum_l{l}_{n}") for n, s, t in zip(names, by_dest, theirs)]

    empties = lambda like: [lax.empty(p.shape, p.dtype) for p in like]
    layer_grads, arrived = [None] * DEPTH, [None] * DEPTH
    pending, dmemn, dtabs = None, 0.0, 0.0
    for l in reversed(range(DEPTH)):
        in_vjp, att_saved, mix_vjp, norm, saved = stages[l]
        lands = None if pending is None else empties(pending)
        dh, dg_mlp, dw_up, dw_down, lands = _mlp_bwd(f"l{l}_mlp", norm, saved, dh, pending, lands)
        mlp_grads = dict(norm_mlp_g=dg_mlp.reshape(D), w_up=dw_up, w_down=dw_down)
        du_a, du_b, du_c, dy_d, dh_res, dmemn_l, d_mix = mix_vjp(dh)
        rides = [] if pending is None else [(pending, lands, ATT_RIDE_WINDOW)]
        if l == 0:
            early = pair_sums(l, MLP_SHARDED, mlp_grads)
            rides.append((early, empties(early), (0, 1, 1)))
        dq, dk, dv, dtabs_l, *landed = _att_bwd(*att_saved, dy_d, f"l{l}_att_bwd", riders=rides)
        if pending is not None:
            arrived[l + 1] = landed[:len(pending)]
        dh_in, d_in = in_vjp((du_a, du_b, du_c, jnp.concatenate([dq, dk, dv], axis=-1)))
        dh = dh_in + dh_res
        dmemn, dtabs = dmemn + dmemn_l, dtabs + dtabs_l
        layer_grads[l] = {**d_in, **d_mix, **mlp_grads}
        if l > 0:
            pending = pair_sums(l, GATHER_BF16, layer_grads[l])
        else:
            late = [n for n in GATHER_BF16 if n not in MLP_SHARDED]
            got = dict(zip(late, chip_exchange(pair_sums(l, late, layer_grads[l]), "ici_l0")))
            got.update(zip(MLP_SHARDED, landed[-len(MLP_SHARDED):]))
            arrived[0] = [got[n] for n in GATHER_BF16]
    dx = dh
    dfull = {n: jnp.concatenate([layer_grads[l][n][None] for l in range(DEPTH)])
             for n in LAYER_WEIGHTS if n not in GATHER_BF16}
    dfull['mem_norm_g'] = mem_vjp(dmemn)[1]
    dfull['rel_bias'] = tabs_vjp(dtabs)[0]
    dfull['norm_final_g'] = d_final_g

    grads, deltas, new_m, new_v = {}, {}, {}, {}
    for k, n in enumerate(GATHER_BF16):
        res = adamw_layers(w[n], [arrived[l][k] for l in range(DEPTH)], m[n], v[n], "adamw_" + n)
        grads[n], deltas[n], new_m[n], new_v[n] = res
    for n in SHARDED:
        if n in GATHER_BF16:
            continue
        parts = _scatter_grad(n, dfull[n])
        shp = w[n].shape
        two_d = lambda a: a.reshape(shp[0] * shp[1], shp[2])
        res = adamw(two_d(w[n]), parts, two_d(m[n]), two_d(v[n]), "adamw_" + n)
        grads[n], deltas[n], new_m[n], new_v[n] = (r.reshape(shp) for r in res)

    parts = all_gather(_flatten_small(dfull), "ag_small_grads")
    res = adamw(_flatten_small(w), parts, _flatten_small(m), _flatten_small(v), "adamw_small")
    for dst, r in zip((grads, deltas, new_m, new_v), res):
        dst.update(_split_small(r, w))

    return (loss, dx[None], *[grads[n] for n in WEIGHTS], *[deltas[n] for n in WEIGHTS],
            *[new_m[n] for n in WEIGHTS], *[new_v[n] for n in WEIGHTS])
```
